```python
import math
import numpy as np
import jax
import jax.numpy as jnp
from jax import lax

D_MODEL = 1024
BATCH = 2
SEQ = 8192
DEPTH = 2

GRID_W = 64
CTX_LEN = 256
HEAD_DIM = 64
ATTN_SCALE = HEAD_DIM ** -0.5
ROPE_THETA = 10000.0
Q_BLOCK = 128
NA_HEADS = 8
NA_KH = 8
NA_KW = 16
NA_QCB = 8
NA_KCB = NA_QCB + NA_KW
DIFF_HEADS = 4
GQA_HEADS = 8
GQA_KV_HEADS = 2
SWA_HEADS = 8
SWA_KV_HEADS = 2
SWA_WINDOW = 128
N_EXPERTS = 32
TOP_K = 4
D_EXPERT = D_MODEL
SWIGLU_LIMIT = 7.0
SWIGLU_ALPHA = 1.702
MOE_BLOCK = 256
N_EVEN = (DEPTH + 1) // 2
N_ODD = DEPTH // 2
AB_COLS = 3 * NA_HEADS * HEAD_DIM + 3 * DIFF_HEADS * 2 * HEAD_DIM
CD_COLS = (GQA_HEADS + 2 * GQA_KV_HEADS + SWA_HEADS + 2 * SWA_KV_HEADS) * HEAD_DIM
DEEPNORM_ALPHA = (2.0 * DEPTH) ** 0.25
DEEPNORM_BETA = (8.0 * DEPTH) ** -0.25
LN_EPS = 1e-5
RMS_EPS = 1e-6
NEG_INF = -1e30

kernel_name = 'hybrid_natten_diff_gqa_swa_moe_dit'


def _layernorm(x, g, b):
    xf = x.astype(jnp.float32)
    mu = jnp.mean(xf, axis=-1, keepdims=True)
    var = jnp.mean(jnp.square(xf - mu), axis=-1, keepdims=True)
    return ((xf - mu) * lax.rsqrt(var + LN_EPS) * g + b).astype(x.dtype)


def _rmsnorm(x, g):
    xf = x.astype(jnp.float32)
    return (xf * lax.rsqrt(jnp.mean(xf * xf, axis=-1, keepdims=True) + RMS_EPS) * g).astype(x.dtype)


def _axial_rope_tables(n_tokens):
    t = jnp.arange(n_tokens, dtype=jnp.int32)
    row = (t // GRID_W).astype(jnp.float32)
    col = (t % GRID_W).astype(jnp.float32)
    axis_dim = HEAD_DIM // 2
    freqs = ROPE_THETA ** (-jnp.arange(0, axis_dim, 2, dtype=jnp.float32) / axis_dim)
    ang = jnp.concatenate([row[:, None] * freqs, col[:, None] * freqs], axis=-1)
    return jnp.cos(ang), jnp.sin(ang)


def _apply_rope(x, cos, sin):
    xf = x.astype(jnp.float32).reshape(x.shape[:-1] + (-1, 2))
    x0, x1 = xf[..., 0], xf[..., 1]
    out = jnp.stack([x0 * cos - x1 * sin, x0 * sin + x1 * cos], axis=-1)
    return out.reshape(x.shape).astype(x.dtype)


def _to_heads(t, n_heads):
    return t.reshape(t.shape[0], t.shape[1], n_heads, -1).transpose(0, 2, 1, 3)


def _to_gqa_q(t, n_kv, group):
    return t.reshape(t.shape[0], t.shape[1], n_kv, group, -1).transpose(0, 2, 3, 1, 4)


def _merge_heads(t):
    return t.transpose(0, 2, 1, 3).reshape(t.shape[0], t.shape[2], -1)


def _merge_gqa(t):
    return t.transpose(0, 3, 1, 2, 4).reshape(t.shape[0], t.shape[3], -1)


def _sweep_query_blocks(fn, *qs):
    s = qs[0].shape[-2]
    nb = s // Q_BLOCK
    blocks = tuple(jnp.moveaxis(q.reshape(q.shape[:-2] + (nb, Q_BLOCK, q.shape[-1])), -3, 0) for q in qs)
    out = lax.map(lambda bs: fn(*bs), blocks)
    out = jnp.moveaxis(out, 0, -3)
    return out.reshape(out.shape[:-3] + (s, out.shape[-1]))


def _attend(q, k, v, sink=None):
    s = jnp.einsum('bhgqd,bhkd->bhgqk', q, k).astype(jnp.float32) * ATTN_SCALE
    if sink is not None:
        sk = jnp.broadcast_to(sink[None, :, :, None, None].astype(jnp.float32), s.shape[:-1] + (1,))
        p = jax.nn.softmax(jnp.concatenate([s, sk], axis=-1), axis=-1)[..., :-1]
    else:
        p = jax.nn.softmax(s, axis=-1)
    return jnp.einsum('bhgqk,bhkd->bhgqd', p.astype(v.dtype), v)


def _diff_attend(q1, q2, k1, k2, v, lam):
    s1 = jnp.einsum('bhqd,bhkd->bhqk', q1, k1).astype(jnp.float32) * ATTN_SCALE
    s2 = jnp.einsum('bhqd,bhkd->bhqk', q2, k2).astype(jnp.float32) * ATTN_SCALE
    w = jax.nn.softmax(s1, axis=-1) - lam * jax.nn.softmax(s2, axis=-1)
    return jnp.einsum('bhqk,bhkv->bhqv', w.astype(v.dtype), v)


def _na_tables(rows):
    kh = min(NA_KH, rows)
    ncb = GRID_W // NA_QCB
    blk_start = np.clip(np.arange(ncb) * NA_QCB - NA_KW // 2, 0, GRID_W - NA_KCB)
    kcol = blk_start[:, None] + np.arange(NA_KCB)[None, :]
    qcol = np.arange(ncb)[:, None] * NA_QCB + np.arange(NA_QCB)[None, :]
    ws = np.clip(qcol - NA_KW // 2, 0, GRID_W - NA_KW)
    valid = (kcol[:, None, :] >= ws[:, :, None]) & (kcol[:, None, :] < ws[:, :, None] + NA_KW)
    dc_idx = np.clip(kcol[:, None, :] - qcol[:, :, None] + NA_KW - 1, 0, 2 * NA_KW - 2)
    mask = np.broadcast_to(valid[:, :, None, :], (ncb, NA_QCB, kh, NA_KCB)).reshape(ncb, NA_QCB, kh * NA_KCB)
    return kh, ncb, kcol, mask, dc_idx


def _neighbourhood_attend(q, k, v, kc, vc, rpb):
    b, nh, s, d = q.shape
    rows = s // GRID_W
    kh, ncb, kcol, mask, dc_idx = _na_tables(rows)
    n_ctx = kc.shape[2]
    qg = q.reshape(b, nh, rows, GRID_W, d)
    kg = k.reshape(b, nh, rows, GRID_W, d)
    vg = v.reshape(b, nh, rows, GRID_W, d)

    def row_fn(r):
        rs = jnp.clip(r - kh // 2, 0, rows - kh)
        q_r = lax.dynamic_index_in_dim(qg, r, axis=2, keepdims=False).reshape(b, nh, ncb, NA_QCB, d)

        def gather(t):
            t_rows = lax.dynamic_slice_in_dim(t, rs, kh, axis=2)
            t_blk = t_rows[:, :, :, kcol]
            return t_blk.transpose(0, 1, 3, 2, 4, 5).reshape(b, nh, ncb, kh * NA_KCB, d)

        kb, vb = gather(kg), gather(vg)
        dr_idx = rs + jnp.arange(kh) - r + NA_KH - 1
        bias = jnp.take(rpb, dr_idx, axis=1)[:, :, dc_idx]
        bias = bias.transpose(0, 2, 3, 1, 4).reshape(nh, ncb, NA_QCB, kh * NA_KCB)
        s_win = jnp.einsum('bhjqd,bhjkd->bhjqk', q_r, kb).astype(jnp.float32) * ATTN_SCALE + bias.astype(jnp.float32)
        s_win = jnp.where(mask, s_win, NEG_INF)
        s_ctx = jnp.einsum('bhjqd,bhkd->bhjqk', q_r, kc).astype(jnp.float32) * ATTN_SCALE
        p = jax.nn.softmax(jnp.concatenate([s_ctx, s_win], axis=-1), axis=-1)
        out = (jnp.einsum('bhjqk,bhkd->bhjqd', p[..., :n_ctx].astype(vc.dtype), vc)
               + jnp.einsum('bhjqk,bhjkd->bhjqd', p[..., n_ctx:].astype(vb.dtype), vb))
        return out.reshape(b, nh, GRID_W, d)

    out = lax.map(row_fn, jnp.arange(rows, dtype=jnp.int32))
    return out.transpose(1, 2, 0, 3, 4).reshape(b, nh, s, d)


def _window_attend(q, k, v, kc, vc, sink):
    b, nkv, grp, s, d = q.shape
    w = SWA_WINDOW
    nb = s // w
    n_ctx = kc.shape[2]
    qb = q.reshape(b, nkv, grp, nb, w, d)

    def band(t):
        tp = jnp.pad(t, ((0, 0), (0, 0), (w, w), (0, 0))).reshape(b, nkv, nb + 2, w, t.shape[-1])
        return jnp.concatenate([tp[:, :, :-2], tp[:, :, 1:-1], tp[:, :, 2:]], axis=3)

    kb, vb = band(k), band(v)
    qi = np.arange(w)[:, None]
    kj = np.arange(3 * w)[None, :]
    kpos = np.arange(nb)[:, None] * w - w + np.arange(3 * w)[None, :]
    mask = (np.abs(qi - kj + w) <= w)[None] & ((kpos >= 0) & (kpos < s))[:, None, :]
    s_band = jnp.einsum('bhgnqd,bhnkd->bhgnqk', qb, kb).astype(jnp.float32) * ATTN_SCALE
    s_band = jnp.where(mask, s_band, NEG_INF)
    s_ctx = jnp.einsum('bhgnqd,bhkd->bhgnqk', qb, kc).astype(jnp.float32) * ATTN_SCALE
    s_sink = jnp.broadcast_to(sink[None, :, :, None, None, None].astype(jnp.float32), s_band.shape[:-1] + (1,))
    p = jax.nn.softmax(jnp.concatenate([s_ctx, s_band, s_sink], axis=-1), axis=-1)
    out = (jnp.einsum('bhgnqk,bhkd->bhgnqd', p[..., :n_ctx].astype(vc.dtype), vc)
           + jnp.einsum('bhgnqk,bhnkd->bhgnqd', p[..., n_ctx:n_ctx + 3 * w].astype(vb.dtype), vb))
    return out.reshape(b, nkv, grp, s, d)


def _mix_ab(h, hc, cos, sin, w_in, w_out, rpb, lq1, lk1, lq2, lk2, subln, lam_init, ctx_out):
    na_w = NA_HEADS * HEAD_DIM
    df_w = DIFF_HEADS * 2 * HEAD_DIM
    cuts = [na_w, 2 * na_w, 3 * na_w, 3 * na_w + df_w, 3 * na_w + 2 * df_w]
    nq, nk, nv, dq, dk, dv = jnp.split(h @ w_in, cuts, axis=-1)
    nqc, nkc, nvc, dqc, dkc, dvc = jnp.split(hc @ w_in, cuts, axis=-1)
    nkc_h, nvc_h = _to_heads(nkc, NA_HEADS), _to_heads(nvc, NA_HEADS)
    y_na = _neighbourhood_attend(_to_heads(nq, NA_HEADS), _to_heads(nk, NA_HEADS), _to_heads(nv, NA_HEADS), nkc_h, nvc_h, rpb)

    def pair(t):
        t = _to_heads(t, 2 * DIFF_HEADS)
        t = t.reshape(t.shape[0], DIFF_HEADS, 2, t.shape[2], t.shape[3])
        return t[:, :, 0], t[:, :, 1]

    lam = (jnp.exp(jnp.sum(lq1.astype(jnp.float32) * lk1.astype(jnp.float32)))
           - jnp.exp(jnp.sum(lq2.astype(jnp.float32) * lk2.astype(jnp.float32))) + lam_init)
    q1, q2 = pair(dq)
    k1, k2 = pair(dk)
    q1, q2 = _apply_rope(q1, cos, sin), _apply_rope(q2, cos, sin)
    k1, k2 = _apply_rope(k1, cos, sin), _apply_rope(k2, cos, sin)
    k1c, k2c = pair(dkc)
    dvc_h = _to_heads(dvc, DIFF_HEADS)
    k1_all = jnp.concatenate([k1c, k1], axis=2)
    k2_all = jnp.concatenate([k2c, k2], axis=2)
    v_all = jnp.concatenate([dvc_h, _to_heads(dv, DIFF_HEADS)], axis=2)
    y_df = _sweep_query_blocks(lambda a, bq: _diff_attend(a, bq, k1_all, k2_all, v_all, lam), q1, q2)
    y_df = _rmsnorm(y_df, subln) * (1.0 - lam_init)
    y = jnp.concatenate([_merge_heads(y_na), _merge_heads(y_df)], axis=-1) @ w_out
    if not ctx_out:
        return y, None
    y_na_c = _attend(_to_heads(nqc, NA_HEADS)[:, :, None], nkc_h, nvc_h)[:, :, 0]
    q1c, q2c = pair(dqc)
    y_df_c = _rmsnorm(_diff_attend(q1c, q2c, k1c, k2c, dvc_h, lam), subln) * (1.0 - lam_init)
    yc = jnp.concatenate([_merge_heads(y_na_c), _merge_heads(y_df_c)], axis=-1) @ w_out
    return y, yc


def _mix_cd(h, hc, cos, sin, w_in, w_out, q_norm, k_norm, sink, ctx_out):
    sizes = [GQA_HEADS * HEAD_DIM, GQA_KV_HEADS * HEAD_DIM, GQA_KV_HEADS * HEAD_DIM,
             SWA_HEADS * HEAD_DIM, SWA_KV_HEADS * HEAD_DIM, SWA_KV_HEADS * HEAD_DIM]
    cuts = [int(v) for v in np.cumsum(sizes)[:-1]]
    gq, gk, gv, wq, wk, wv = jnp.split(h @ w_in, cuts, axis=-1)
    gqc, gkc, gvc, wqc, wkc, wvc = jnp.split(hc @ w_in, cuts, axis=-1)
    g_c = GQA_HEADS // GQA_KV_HEADS
    g_d = SWA_HEADS // SWA_KV_HEADS
    q = _apply_rope(_rmsnorm(_to_gqa_q(gq, GQA_KV_HEADS, g_c), q_norm), cos, sin)
    k = _apply_rope(_rmsnorm(_to_heads(gk, GQA_KV_HEADS), k_norm), cos, sin)
    kc = _rmsnorm(_to_heads(gkc, GQA_KV_HEADS), k_norm)
    vc = _to_heads(gvc, GQA_KV_HEADS)
    k_all = jnp.concatenate([kc, k], axis=2)
    v_all = jnp.concatenate([vc, _to_heads(gv, GQA_KV_HEADS)], axis=2)
    y_c = _sweep_query_blocks(lambda qb: _attend(qb, k_all, v_all), q)
    sink_d = sink.reshape(SWA_KV_HEADS, g_d)
    qd = _apply_rope(_to_gqa_q(wq, SWA_KV_HEADS, g_d), cos, sin)
    kd = _apply_rope(_to_heads(wk, SWA_KV_HEADS), cos, sin)
    kdc, vdc = _to_heads(wkc, SWA_KV_HEADS), _to_heads(wvc, SWA_KV_HEADS)
    y_d = _window_attend(qd, kd, _to_heads(wv, SWA_KV_HEADS), kdc, vdc, sink_d)
    y = jnp.concatenate([_merge_gqa(y_c), _merge_gqa(y_d)], axis=-1) @ w_out
    if not ctx_out:
        return y, None
    y_cc = _attend(_rmsnorm(_to_gqa_q(gqc, GQA_KV_HEADS, g_c), q_norm), kc, vc)
    y_dc = _attend(_to_gqa_q(wqc, SWA_KV_HEADS, g_d), kdc, vdc, sink_d)
    yc = jnp.concatenate([_merge_gqa(y_cc), _merge_gqa(y_dc)], axis=-1) @ w_out
    return y, yc


def _moe(x, router_w, router_b, w_in, b_in, w_out, b_out):
    n, d = x.shape
    logits = (x @ router_w).astype(jnp.float32) + router_b.astype(jnp.float32)
    top_val, top_idx = lax.top_k(logits, TOP_K)
    gates = jax.nn.softmax(top_val, axis=-1)
    nk = n * TOP_K
    flat_e = top_idx.reshape(-1).astype(jnp.int32)
    flat_tok = jnp.arange(nk, dtype=jnp.int32) // TOP_K
    order = jnp.argsort(flat_e)
    e_sorted = flat_e[order]
    tok_sorted = flat_tok[order]
    gate_sorted = gates.reshape(-1)[order]
    counts = jnp.zeros((N_EXPERTS,), jnp.int32).at[flat_e].add(1)
    padded = (counts + MOE_BLOCK - 1) // MOE_BLOCK * MOE_BLOCK
    start = jnp.cumsum(counts) - counts
    pend = jnp.cumsum(padded)
    pstart = pend - padded
    dest = pstart[e_sorted] + jnp.arange(nk, dtype=jnp.int32) - start[e_sorted]
    n_blocks = -(-nk // MOE_BLOCK) + N_EXPERTS
    slot_tok = jnp.full((n_blocks * MOE_BLOCK,), n, jnp.int32).at[dest].set(tok_sorted)
    block_e = jnp.clip(jnp.searchsorted(pend, jnp.arange(n_blocks, dtype=jnp.int32) * MOE_BLOCK, side='right'), 0, N_EXPERTS - 1)
    x_pad = jnp.concatenate([x, jnp.zeros((1, d), x.dtype)], axis=0)
    xb = x_pad[slot_tok].reshape(n_blocks, MOE_BLOCK, d)

    def expert_block(args):
        xs, e = args
        hh = xs @ w_in[e] + b_in[e]
        gate = jnp.minimum(hh[:, :D_EXPERT], SWIGLU_LIMIT)
        up = jnp.clip(hh[:, D_EXPERT:], -SWIGLU_LIMIT, SWIGLU_LIMIT)
        act = gate * jax.nn.sigmoid(SWIGLU_ALPHA * gate) * (up + 1.0)
        return act @ w_out[e] + b_out[e]

    yb = lax.map(expert_block, (xb, block_e)).reshape(-1, d)
    return jax.ops.segment_sum(yb[dest] * gate_sorted[:, None].astype(yb.dtype), tok_sorted, num_segments=n)


def _modulation(cvec, w_mod, b_mod):
    m = jax.nn.silu(cvec) @ w_mod + b_mod
    return jnp.split(m, 6, axis=-1)


def setup_inputs(seed: int = 0) -> dict:
    key = jax.random.key(seed)
    ks = jax.random.split(key, 32)
    f32 = jnp.float32
    D = D_MODEL

    def nrm(k, shape, scale):
        return jax.random.normal(k, shape, f32) * scale

    return {
        'x': nrm(ks[0], (BATCH, SEQ, D), 1.0),
        'c': nrm(ks[1], (BATCH, D), 1.0),
        'ctx': nrm(ks[2], (BATCH, CTX_LEN, D), 1.0),
        'c_ctx': nrm(ks[3], (D,), 1.0),
        'mod_w': nrm(ks[4], (DEPTH, D, 6 * D), 0.5 * D ** -0.5),
        'mod_b': nrm(ks[5], (DEPTH, 6 * D), 0.02),
        'ln1_g': 1.0 + nrm(ks[6], (DEPTH, D), 0.02),
        'ln1_b': nrm(ks[7], (DEPTH, D), 0.02),
        'ln2_g': 1.0 + nrm(ks[8], (DEPTH, D), 0.02),
        'ln2_b': nrm(ks[9], (DEPTH, D), 0.02),
        'router_w': nrm(ks[10], (DEPTH, D, N_EXPERTS), D ** -0.5),
        'router_b': nrm(ks[11], (DEPTH, N_EXPERTS), 0.01),
        'moe_w_in': nrm(ks[12], (DEPTH, N_EXPERTS, D, 2 * D_EXPERT), D ** -0.5),
        'moe_b_in': nrm(ks[13], (DEPTH, N_EXPERTS, 2 * D_EXPERT), 0.02),
        'moe_w_out': nrm(ks[14], (DEPTH, N_EXPERTS, D_EXPERT, D), D_EXPERT ** -0.5 * DEEPNORM_BETA),
        'moe_b_out': nrm(ks[15], (DEPTH, N_EXPERTS, D), 0.02),
        'ab_w_in': nrm(ks[16], (N_EVEN, D, AB_COLS), D ** -0.5),
        'ab_w_out': nrm(ks[17], (N_EVEN, D, D), D ** -0.5 * DEEPNORM_BETA),
        'na_rpb': nrm(ks[18], (N_EVEN, NA_HEADS, 2 * NA_KH - 1, 2 * NA_KW - 1), 0.1),
        'diff_lq1': nrm(ks[19], (N_EVEN, HEAD_DIM), 0.1),
        'diff_lk1': nrm(ks[20], (N_EVEN, HEAD_DIM), 0.1),
        'diff_lq2': nrm(ks[21], (N_EVEN, HEAD_DIM), 0.1),
        'diff_lk2': nrm(ks[22], (N_EVEN, HEAD_DIM), 0.1),
        'diff_subln': 1.0 + nrm(ks[23], (N_EVEN, 2 * HEAD_DIM), 0.02),
        'cd_w_in': nrm(ks[24], (N_ODD, D, CD_COLS), D ** -0.5),
        'cd_w_out': nrm(ks[25], (N_ODD, D, D), D ** -0.5 * DEEPNORM_BETA),
        'gqa_q_norm': 1.0 + nrm(ks[26], (N_ODD, HEAD_DIM), 0.02),
        'gqa_k_norm': 1.0 + nrm(ks[27], (N_ODD, HEAD_DIM), 0.02),
        'swa_sink': nrm(ks[28], (N_ODD, SWA_HEADS), 1.0),
    }


def reference(x, c, ctx, c_ctx, mod_w, mod_b, ln1_g, ln1_b, ln2_g, ln2_b, router_w, router_b,
              moe_w_in, moe_b_in, moe_w_out, moe_b_out, ab_w_in, ab_w_out, na_rpb,
              diff_lq1, diff_lk1, diff_lq2, diff_lk2, diff_subln, cd_w_in, cd_w_out,
              gqa_q_norm, gqa_k_norm, swa_sink):
    b, s, d = x.shape
    n_lat = b * s
    cos, sin = _axial_rope_tables(s)
    xc = ctx
    for i in range(DEPTH):
        last = i == DEPTH - 1
        j = i // 2
        sh1, sc1, g1, sh2, sc2, g2 = _modulation(c, mod_w[i], mod_b[i])
        ch1, cs1, cg1, ch2, cs2, cg2 = _modulation(c_ctx, mod_w[i], mod_b[i])
        h = x * (1.0 + sc1[:, None]) + sh1[:, None]
        hc = xc * (1.0 + cs1) + ch1
        if i % 2 == 0:
            y, yc = _mix_ab(h, hc, cos, sin, ab_w_in[j], ab_w_out[j], na_rpb[j], diff_lq1[j], diff_lk1[j],
                            diff_lq2[j], diff_lk2[j], diff_subln[j], 0.8 - 0.6 * math.exp(-0.3 * i), not last)
        else:
            y, yc = _mix_cd(h, hc, cos, sin, cd_w_in[j], cd_w_out[j], gqa_q_norm[j], gqa_k_norm[j],
                            swa_sink[j], not last)
        x = _layernorm(DEEPNORM_ALPHA * x + g1[:, None] * y, ln1_g[i], ln1_b[i])
        tokens = (x * (1.0 + sc2[:, None]) + sh2[:, None]).reshape(n_lat, d)
        if not last:
            xc = _layernorm(DEEPNORM_ALPHA * xc + cg1 * yc, ln1_g[i], ln1_b[i])
            tokens = jnp.concatenate([tokens, (xc * (1.0 + cs2) + ch2).reshape(-1, d)], axis=0)
        f = _moe(tokens, router_w[i], router_b[i], moe_w_in[i], moe_b_in[i], moe_w_out[i], moe_b_out[i])
        x = _layernorm(DEEPNORM_ALPHA * x + g2[:, None] * f[:n_lat].reshape(b, s, d), ln2_g[i], ln2_b[i])
        if not last:
            xc = _layernorm(DEEPNORM_ALPHA * xc + cg2 * f[n_lat:].reshape(xc.shape), ln2_g[i], ln2_b[i])
    return x
```

```python
import functools
import math

import numpy as np
import jax
import jax.numpy as jnp
from jax import lax
from jax.experimental import pallas as pl
from jax.experimental.pallas import tpu as pltpu

F32 = jnp.float32
_MXU = jnp.bfloat16

HEAD_DIM = 64
GRID_W = 64
ATTN_SCALE = HEAD_DIM ** -0.5
ROPE_THETA = 10000.0
NA_HEADS = 8
NA_KH = 8
NA_KW = 16
DIFF_HEADS = 4
GQA_HEADS = 8
GQA_KV_HEADS = 2
SWA_HEADS = 8
SWA_KV_HEADS = 2
SWA_WINDOW = 128
TOP_K = 4
SWIGLU_LIMIT = 7.0
SWIGLU_ALPHA = 1.702
LN_EPS = 1e-5
RMS_EPS = 1e-6
NEG_INF = -1e30

LANES = 128
ROW_TILE = 256
MOE_ROWS = 256
NA_Q_ROWS = 4
VMEM_LIMIT = 52 * 1024 * 1024


def _cparams(sem, vmem=None):
    return pltpu.CompilerParams(dimension_semantics=sem, vmem_limit_bytes=vmem)


def _mod_kernel(c_ref, w_ref, b_ref, o_ref):
    c = c_ref[...]
    a = (c / (1.0 + jnp.exp(-c))).astype(_MXU)
    o_ref[...] = jnp.dot(a, w_ref[...].astype(_MXU), preferred_element_type=F32) + b_ref[...]


def _modulation(cc, mod_w, mod_b):
    depth, d, d6 = mod_w.shape
    tn = d6 // 4
    return pl.pallas_call(
        _mod_kernel,
        grid=(depth, d6 // tn),
        in_specs=[pl.BlockSpec((8, d), lambda l, j: (0, 0)),
                  pl.BlockSpec((None, d, tn), lambda l, j: (l, 0, j)),
                  pl.BlockSpec((None, 1, tn), lambda l, j: (l, 0, j))],
        out_specs=pl.BlockSpec((None, 8, tn), lambda l, j: (l, 0, j)),
        out_shape=jax.ShapeDtypeStruct((depth, 8, d6), F32),
        compiler_params=_cparams(("arbitrary", "arbitrary"), VMEM_LIMIT),
        name="modulation",
    )(cc, mod_w, mod_b.reshape(depth, 1, d6))


def _in_kernel(x_ref, mod_ref, w_ref, cos_ref, sin_ref, nw_ref, o_ref, *, plan):
    x = x_ref[...]
    h = (x * (1.0 + mod_ref[1:2, :]) + mod_ref[0:1, :]).astype(_MXU)
    tm = x.shape[0]
    lane = lax.broadcasted_iota(jnp.int32, (tm, LANES), 1)
    first_half = (lane & (HEAD_DIM // 2)) == 0
    gi = lax.broadcasted_iota(jnp.int32, (LANES, LANES), 0) // HEAD_DIM
    gj = lax.broadcasted_iota(jnp.int32, (LANES, LANES), 1) // HEAD_DIM
    seg = jnp.where(gi == gj, 1.0, 0.0).astype(_MXU)
    cos = cos_ref[...]
    sin = sin_ref[...]
    for (c0, width, norm_row, rope, scale) in plan:
        acc = jnp.dot(h, w_ref[:, c0:c0 + width], preferred_element_type=F32)
        for j in range(width // LANES):
            a = acc[:, j * LANES:(j + 1) * LANES]
            if norm_row is not None:
                a2 = a * a
                hi = a2.astype(_MXU)
                lo = (a2 - hi.astype(F32)).astype(_MXU)
                ss = (jnp.dot(hi, seg, preferred_element_type=F32)
                      + jnp.dot(lo, seg, preferred_element_type=F32))
                a = a * lax.rsqrt(ss * (1.0 / HEAD_DIM) + RMS_EPS) * nw_ref[norm_row:norm_row + 1, :]
            if rope:
                partner = jnp.where(first_half, pltpu.roll(a, LANES - HEAD_DIM // 2, 1),
                                    pltpu.roll(a, HEAD_DIM // 2, 1))
                a = a * cos + partner * sin
            if scale != 1.0:
                a = a * scale
            o_ref[:, c0 + j * LANES:c0 + (j + 1) * LANES] = a.astype(o_ref.dtype)


def _in_proj(x2d, mods, w, cos_t, sin_t, nw, plan, *, batch, n_ctx):
    rows, d = x2d.shape
    ncols = w.shape[1]
    tm = ROW_TILE
    nbt = rows // batch // tm
    nctx = n_ctx // tm

    def mod_map(i):
        return (jnp.where(i % nbt < nctx, batch, i // nbt), 0, 0)

    return pl.pallas_call(
        functools.partial(_in_kernel, plan=plan),
        grid=(rows // tm,),
        in_specs=[pl.BlockSpec((tm, d), lambda i: (i, 0)),
                  pl.BlockSpec((None, 6, d), mod_map),
                  pl.BlockSpec((d, ncols), lambda i: (0, 0)),
                  pl.BlockSpec((tm, LANES), lambda i: (i % nbt, 0)),
                  pl.BlockSpec((tm, LANES), lambda i: (i % nbt, 0)),
                  pl.BlockSpec((8, LANES), lambda i: (0, 0))],
        out_specs=pl.BlockSpec((tm, ncols), lambda i: (i, 0)),
        out_shape=jax.ShapeDtypeStruct((rows, ncols), _MXU),
        compiler_params=_cparams(("parallel",), VMEM_LIMIT),
        name="in_proj",
    )(x2d, mods, w, cos_t, sin_t, nw)


def _attn_kernel(*refs, n_pairs, nseg, mode, bias_segs, band, nkv, kv_axis, q_axis,
                 has_init, lam_init, s_len):
    it = iter(refs)
    q_ref = next(it)
    k_refs = [next(it) for _ in range(nseg)]
    v_refs = [next(it) for _ in range(nseg)]
    bias_ref = next(it) if bias_segs else None
    init_ref = next(it) if has_init else None
    par_ref = next(it) if mode == "diff" else None
    o_ref = next(it)
    m_scr, l_scr, acc_scr = (next(it), next(it), next(it)) if nkv > 1 else (None, None, None)

    tq = q_ref.shape[0]
    nheads = 2 * n_pairs
    lane_q = lax.broadcasted_iota(jnp.int32, (tq, LANES), 1)
    is_lo = lane_q < HEAD_DIM
    lane_1 = lax.broadcasted_iota(jnp.int32, (1, LANES), 1)
    lo_mask = jnp.where(lane_1 < HEAD_DIM, 1.0, 0.0).astype(q_ref.dtype)
    hi_mask = jnp.where(lane_1 < HEAD_DIM, 0.0, 1.0).astype(q_ref.dtype)

    masks = [None] * nseg
    if band:
        qi = pl.program_id(q_axis)
        qpos = qi * tq + lax.broadcasted_iota(jnp.int32, (tq, tq), 0)
        for j in range(1, nseg):
            kpos = (qi + j - 2) * tq + lax.broadcasted_iota(jnp.int32, (tq, tq), 1)
            dist = jnp.abs(qpos - kpos)
            ok = jnp.where(kpos >= 0, jnp.where(kpos < s_len, dist, SWA_WINDOW + 1), SWA_WINDOW + 1)
            masks[j] = ok <= SWA_WINDOW

    def init_state(hh):
        if has_init:
            m0 = jnp.broadcast_to(init_ref[hh:hh + 1, 0:1], (tq, 1))
            l0 = jnp.ones((tq, 1), F32)
        else:
            m0 = jnp.full((tq, 1), NEG_INF, F32)
            l0 = jnp.zeros((tq, 1), F32)
        return m0, l0, jnp.zeros((tq, LANES), F32)

    def head_step(hh, state):
        p_idx, half = hh // 2, hh % 2
        q2 = q_ref[:, p_idx * LANES:(p_idx + 1) * LANES]
        qm = q2 * (lo_mask if half == 0 else hi_mask)
        m_prev, l_prev, acc_prev = state
        ss = []
        boff = 0
        for j in range(nseg):
            s = lax.dot_general(qm, k_refs[j][...], (((1,), (1,)), ((), ())),
                                preferred_element_type=F32)
            if j in bias_segs:
                n = k_refs[j].shape[0]
                s = s + bias_ref[half, :, boff:boff + n]
                boff += n
            if masks[j] is not None:
                s = jnp.where(masks[j], s, NEG_INF)
            ss.append(s)
        s_max = functools.reduce(jnp.maximum, [jnp.max(s, axis=1, keepdims=True) for s in ss])
        m_new = jnp.maximum(m_prev, s_max)
        alpha = jnp.exp(m_prev - m_new)
        l_new = alpha * l_prev
        acc = alpha * acc_prev
        for j in range(nseg):
            p = jnp.exp(ss[j] - m_new)
            l_new = l_new + jnp.sum(p, axis=1, keepdims=True)
            acc = acc + jnp.dot(p.astype(_MXU), v_refs[j][...], preferred_element_type=F32)
        return m_new, l_new, acc

    def finalize(states):
        if mode == "pair":
            for p_idx in range(n_pairs):
                _, l_lo, a_lo = states[2 * p_idx]
                _, l_hi, a_hi = states[2 * p_idx + 1]
                out = jnp.where(is_lo, a_lo / l_lo, a_hi / l_hi)
                o_ref[:, p_idx * LANES:(p_idx + 1) * LANES] = out.astype(o_ref.dtype)
        else:
            lam = (jnp.exp(jnp.sum(par_ref[0:1, :] * par_ref[1:2, :], axis=1, keepdims=True))
                   - jnp.exp(jnp.sum(par_ref[2:3, :] * par_ref[3:4, :], axis=1, keepdims=True))
                   + lam_init)
            _, l_lo, a_lo = states[0]
            _, l_hi, a_hi = states[1]
            out = a_lo / l_lo - lam * (a_hi / l_hi)
            ms = jnp.mean(out * out, axis=1, keepdims=True)
            out = out * lax.rsqrt(ms + RMS_EPS) * par_ref[4:5, :] * (1.0 - lam_init)
            o_ref[...] = out.astype(o_ref.dtype)

    if nkv == 1:
        finalize([head_step(hh, init_state(hh)) for hh in range(nheads)])
        return

    ki = pl.program_id(kv_axis)

    @pl.when(ki == 0)
    def _():
        for hh in range(nheads):
            m0, l0, a0 = init_state(hh)
            m_scr[hh] = m0
            l_scr[hh] = l0
            acc_scr[hh] = a0

    for hh in range(nheads):
        m_new, l_new, acc = head_step(hh, (m_scr[hh], l_scr[hh], acc_scr[hh]))
        m_scr[hh] = m_new
        l_scr[hh] = l_new
        acc_scr[hh] = acc

    @pl.when(ki == nkv - 1)
    def _():
        finalize([(m_scr[hh], l_scr[hh], acc_scr[hh]) for hh in range(nheads)])


def _attention(q_arr, kv_arr, *, grid, q_spec, k_specs, v_specs, out_spec, out_shape, n_pairs, mode="pair",
               bias=None, bias_spec=None, bias_segs=(), band=False, nkv=1, kv_axis=0, q_axis=0,
               init=None, par=None, lam_init=0.0, s_len=0, sem=None, name="attention"):
    nseg = len(k_specs)
    args = [q_arr] + [kv_arr] * (2 * nseg)
    specs = [q_spec] + list(k_specs) + list(v_specs)
    if bias is not None:
        args.append(bias)
        specs.append(bias_spec)
    if init is not None:
        args.append(init)
        specs.append(pl.BlockSpec(init.shape, lambda *_: (0, 0)))
    if par is not None:
        args.append(par)
        specs.append(pl.BlockSpec(par.shape, lambda *_: (0, 0)))
    tq = q_spec.block_shape[-2]
    scratch = []
    if nkv > 1:
        scratch = [pltpu.VMEM((2 * n_pairs, tq, 1), F32), pltpu.VMEM((2 * n_pairs, tq, 1), F32),
                   pltpu.VMEM((2 * n_pairs, tq, LANES), F32)]
    kern = functools.partial(_attn_kernel, n_pairs=n_pairs, nseg=nseg, mode=mode, bias_segs=tuple(bias_segs),
                             band=band, nkv=nkv, kv_axis=kv_axis, q_axis=q_axis, has_init=init is not None,
                             lam_init=lam_init, s_len=s_len)
    return pl.pallas_call(
        kern, grid=grid, in_specs=specs, out_specs=out_spec, out_shape=out_shape, scratch_shapes=scratch,
        compiler_params=_cparams(sem, VMEM_LIMIT), name=name,
    )(*args)


def _pick_tile(n, candidates):
    for c in candidates:
        if n % c == 0:
            return c
    raise ValueError(f"no tile for {n}")


def _flash(qkv, *, batch, n_groups, n_pairs, q_col, k_col, v_col, group_stride, q_rows, q_off, kv_rows,
           mode, par=None, lam_init=0.0, name):
    tq = ROW_TILE
    tk = _pick_tile(kv_rows, (768, 512, 384, 256, 128))
    nq, nkv = q_rows // tq, kv_rows // tk
    qw = n_pairs * LANES
    q_spec = pl.BlockSpec((None, tq, qw), lambda b, g, i, k: (b, q_off // tq + i, q_col // n_pairs + g * group_stride))
    k_spec = pl.BlockSpec((None, tk, LANES), lambda b, g, i, k: (b, k, k_col + g * group_stride))
    v_spec = pl.BlockSpec((None, tk, LANES), lambda b, g, i, k: (b, k, v_col + g * group_stride))
    out_spec = pl.BlockSpec((None, tq, qw), lambda b, g, i, k: (b, i, g))
    return _attention(
        qkv, qkv, grid=(batch, n_groups, nq, nkv), q_spec=q_spec, k_specs=[k_spec], v_specs=[v_spec],
        out_spec=out_spec, out_shape=jax.ShapeDtypeStruct((batch, q_rows, n_groups * qw), _MXU),
        n_pairs=n_pairs, mode=mode, nkv=nkv, kv_axis=3, par=par, lam_init=lam_init,
        sem=("parallel", "parallel", "parallel", "arbitrary"), name=name)


def _na_bias_tables(rpb, rows):
    nh = rpb.shape[0]
    nkr = 3 * NA_Q_ROWS
    qr = np.arange(NA_Q_ROWS)[:, None, None, None]
    qc = np.arange(GRID_W)[None, :, None, None]
    kr = np.arange(nkr)[None, None, :, None]
    kc = np.arange(GRID_W)[None, None, None, :]
    idxs, valids = [], []
    for variant in range(3):
        if variant == 0:
            r0, k0, nrows = 0, 0, rows
        elif variant == 1:
            r0, k0, nrows = 2 * NA_Q_ROWS, NA_Q_ROWS, 8 * NA_Q_ROWS
        else:
            r0, k0, nrows = rows - NA_Q_ROWS, rows - nkr, rows
        r = r0 + qr
        rp = k0 + kr
        rs = np.clip(r - NA_KH // 2, 0, nrows - NA_KH)
        rvalid = (rp >= rs) & (rp < rs + NA_KH)
        dr = np.clip(rp - r + NA_KH - 1, 0, 2 * NA_KH - 2)
        ws = np.clip(qc - NA_KW // 2, 0, GRID_W - NA_KW)
        cvalid = (kc >= ws) & (kc < ws + NA_KW)
        dc = np.clip(kc - qc + NA_KW - 1, 0, 2 * NA_KW - 2)
        shape = (NA_Q_ROWS, GRID_W, nkr, GRID_W)
        idxs.append(np.broadcast_to(dr * (2 * NA_KW - 1) + dc, shape).reshape(NA_Q_ROWS * GRID_W, nkr * GRID_W))
        valids.append(np.broadcast_to(rvalid & cvalid, shape).reshape(NA_Q_ROWS * GRID_W, nkr * GRID_W))
    idx = np.stack(idxs).astype(np.int32)
    valid = np.stack(valids)
    flat = rpb.reshape(nh, -1).astype(F32)
    tbl = jnp.where(valid[:, None], jnp.take(flat, idx, axis=1).transpose(1, 0, 2, 3), NEG_INF)
    return tbl.reshape(3, nh // 2, 2, NA_Q_ROWS * GRID_W, nkr * GRID_W)


def _neighbourhood(qkv, bias, *, batch, s_len, n_ctx, q_col, k_col, v_col):
    tq = NA_Q_ROWS * GRID_W
    nq = s_len // tq
    n_pairs = NA_HEADS // 2
    cb = n_ctx // tq
    q_spec = pl.BlockSpec((None, tq, LANES), lambda p, i, b: (b, cb + i, q_col + p))

    def seg_specs(col):
        specs = [pl.BlockSpec((None, n_ctx, LANES), lambda p, i, b: (b, 0, col + p))]
        for j in range(3):
            specs.append(pl.BlockSpec(
                (None, tq, LANES),
                lambda p, i, b, j=j: (b, cb + jnp.clip(i - 1, 0, nq - 3) + j, col + p)))
        return specs

    bias_spec = pl.BlockSpec((None, None, 2, tq, 3 * tq),
                             lambda p, i, b: (jnp.where(i == 0, 0, jnp.where(i == nq - 1, 2, 1)), p, 0, 0, 0))
    out_spec = pl.BlockSpec((None, tq, LANES), lambda p, i, b: (b, i, p))
    return _attention(
        qkv, qkv, grid=(n_pairs, nq, batch), q_spec=q_spec, k_specs=seg_specs(k_col), v_specs=seg_specs(v_col),
        out_spec=out_spec, out_shape=jax.ShapeDtypeStruct((batch, s_len, n_pairs * LANES), _MXU),
        n_pairs=1, bias=bias, bias_spec=bias_spec, bias_segs=(1, 2, 3),
        sem=("parallel", "parallel", "parallel"), name="neighbourhood_attn")


def _windowed(qkv, init, *, batch, s_len, n_ctx, q_col, k_col, v_col):
    tq = SWA_WINDOW
    nb = s_len // tq
    n_pairs = SWA_HEADS // 2
    cb = n_ctx // tq
    qw = n_pairs * LANES
    q_spec = pl.BlockSpec((None, tq, qw), lambda b, i: (b, cb + i, q_col // n_pairs))

    def seg_specs(col):
        specs = [pl.BlockSpec((None, n_ctx, LANES), lambda b, i: (b, 0, col))]
        for j in range(3):
            specs.append(pl.BlockSpec((None, tq, LANES),
                                      lambda b, i, j=j: (b, cb + jnp.clip(i - 1 + j, 0, nb - 1), col)))
        return specs

    out_spec = pl.BlockSpec((None, tq, qw), lambda b, i: (b, i, 0))
    return _attention(
        qkv, qkv, grid=(batch, nb), q_spec=q_spec, k_specs=seg_specs(k_col), v_specs=seg_specs(v_col),
        out_spec=out_spec, out_shape=jax.ShapeDtypeStruct((batch, s_len, qw), _MXU),
        n_pairs=n_pairs, band=True, q_axis=1, init=init, s_len=s_len,
        sem=("parallel", "parallel"), name="windowed_attn")


def _layernorm(z, g, b):
    mu = jnp.mean(z, axis=1, keepdims=True)
    zc = z - mu
    var = jnp.mean(zc * zc, axis=1, keepdims=True)
    return zc * lax.rsqrt(var + LN_EPS) * g + b


def _post_kernel(x_ref, ya_ref, yb_ref, wo_ref, mod_ref, g_ref, b_ref, rw_ref, rb_ref,
                 x1_ref, tok_ref, route_ref, *, alpha):
    half = ya_ref.shape[1]
    y = (jnp.dot(ya_ref[...], wo_ref[:half, :], preferred_element_type=F32)
         + jnp.dot(yb_ref[...], wo_ref[half:, :], preferred_element_type=F32))
    x1 = _layernorm(alpha * x_ref[...] + mod_ref[2:3, :] * y, g_ref[...], b_ref[...])
    x1_ref[...] = x1
    tok = (x1 * (1.0 + mod_ref[4:5, :]) + mod_ref[3:4, :]).astype(tok_ref.dtype)
    tok_ref[...] = tok
    logits = jnp.dot(tok, rw_ref[...], preferred_element_type=F32) + rb_ref[...]
    tm = logits.shape[0]
    lane = lax.broadcasted_iota(jnp.int32, (tm, LANES), 1).astype(F32)
    vals, idxs = [], []
    for _ in range(TOP_K):
        mx = jnp.max(logits, axis=1, keepdims=True)
        ix = jnp.min(jnp.where(logits == mx, lane, float(LANES)), axis=1, keepdims=True)
        vals.append(mx)
        idxs.append(ix)
        logits = jnp.where(lane == ix, -3.0e38, logits)
    es = [jnp.exp(v - vals[0]) for v in vals]
    den = functools.reduce(lambda a, c: a + c, es)
    route = jnp.zeros((tm, LANES), F32)
    for k in range(TOP_K):
        route = jnp.where(lane == float(k), idxs[k], route)
        route = jnp.where(lane == float(TOP_K + k), es[k] / den, route)
    route_ref[...] = route


def _post(x2d, ya, yb, wo, mods, g, b, rw, rb, *, batch, nbt_in, off, nblk, n_ctx_blk, alpha):
    d = x2d.shape[1]
    tm = ROW_TILE
    half = ya.shape[1]

    def mod_map(bi, t):
        return (jnp.where(t < n_ctx_blk, batch, bi), 0, 0)

    rows_out = batch * nblk * tm
    o_map = lambda bi, t: (bi * nblk + t, 0)
    return pl.pallas_call(
        functools.partial(_post_kernel, alpha=alpha),
        grid=(batch, nblk),
        in_specs=[pl.BlockSpec((tm, d), lambda bi, t: (bi * nbt_in + off + t, 0)),
                  pl.BlockSpec((tm, half), o_map),
                  pl.BlockSpec((tm, half), o_map),
                  pl.BlockSpec((d, d), lambda bi, t: (0, 0)),
                  pl.BlockSpec((None, 6, d), mod_map),
                  pl.BlockSpec((1, d), lambda bi, t: (0, 0)),
                  pl.BlockSpec((1, d), lambda bi, t: (0, 0)),
                  pl.BlockSpec((d, LANES), lambda bi, t: (0, 0)),
                  pl.BlockSpec((1, LANES), lambda bi, t: (0, 0))],
        out_specs=[pl.BlockSpec((tm, d), o_map), pl.BlockSpec((tm, d), o_map), pl.BlockSpec((tm, LANES), o_map)],
        out_shape=[jax.ShapeDtypeStruct((rows_out, d), F32), jax.ShapeDtypeStruct((rows_out, d), _MXU),
                   jax.ShapeDtypeStruct((rows_out, LANES), F32)],
        compiler_params=_cparams(("parallel", "parallel"), VMEM_LIMIT),
        name="post_attn",
    )(x2d, ya, yb, wo, mods, g, b, rw, rb)


def _moe_kernel(be_ref, nu_ref, x_ref, wi_ref, bi_ref, wo_ref, bo_ref, o_ref, wi_s, wo_s):
    i = pl.program_id(0)
    f = wo_s.shape[0]
    e = be_ref[i]
    changed = jnp.logical_or(i == 0, be_ref[jnp.maximum(i - 1, 0)] != e)

    @pl.when(i < nu_ref[0])
    def _():
        @pl.when(changed)
        def _():
            wi_s[...] = wi_ref[...].astype(wi_s.dtype)
            wo_s[...] = wo_ref[...].astype(wo_s.dtype)

        hh = jnp.dot(x_ref[...], wi_s[...], preferred_element_type=F32) + bi_ref[...]
        gate = jnp.minimum(hh[:, :f], SWIGLU_LIMIT)
        up = jnp.clip(hh[:, f:], -SWIGLU_LIMIT, SWIGLU_LIMIT)
        act = gate * (1.0 / (1.0 + jnp.exp(-SWIGLU_ALPHA * gate))) * (up + 1.0)
        o_ref[...] = jnp.dot(act.astype(wo_s.dtype), wo_s[...], preferred_element_type=F32) + bo_ref[...]


def _experts(xs, block_e, n_used, w_in, b_in, w_out, b_out):
    n_e, d, f2 = w_in.shape
    f = f2 // 2
    n_blocks = xs.shape[0] // MOE_ROWS

    def blk(i, be, nu):
        return jnp.minimum(i, nu[0] - 1)

    grid_spec = pltpu.PrefetchScalarGridSpec(
        num_scalar_prefetch=2,
        grid=(n_blocks,),
        in_specs=[pl.BlockSpec((MOE_ROWS, d), lambda i, be, nu: (blk(i, be, nu), 0)),
                  pl.BlockSpec((None, d, f2), lambda i, be, nu: (be[blk(i, be, nu)], 0, 0)),
                  pl.BlockSpec((None, 1, f2), lambda i, be, nu: (be[blk(i, be, nu)], 0, 0)),
                  pl.BlockSpec((None, f, d), lambda i, be, nu: (be[blk(i, be, nu)], 0, 0)),
                  pl.BlockSpec((None, 1, d), lambda i, be, nu: (be[blk(i, be, nu)], 0, 0))],
        out_specs=pl.BlockSpec((MOE_ROWS, d), lambda i, be, nu: (blk(i, be, nu), 0)),
        scratch_shapes=[pltpu.VMEM((d, f2), _MXU), pltpu.VMEM((f, d), _MXU)],
    )
    return pl.pallas_call(
        _moe_kernel, grid_spec=grid_spec,
        out_shape=jax.ShapeDtypeStruct((n_blocks * MOE_ROWS, d), F32),
        compiler_params=_cparams(("arbitrary",), VMEM_LIMIT),
        name="experts",
    )(block_e, n_used, xs, w_in, b_in.reshape(n_e, 1, f2), w_out, b_out.reshape(n_e, 1, d))


def _route_plan(route, n_experts):
    n = route.shape[0]
    top_idx = route[:, :TOP_K].astype(jnp.int32)
    nk = n * TOP_K
    flat_e = top_idx.reshape(-1)
    flat_tok = jnp.arange(nk, dtype=jnp.int32) // TOP_K
    order = jnp.argsort(flat_e)
    e_sorted = flat_e[order]
    tok_sorted = flat_tok[order]
    counts = jnp.zeros((n_experts,), jnp.int32).at[flat_e].add(1)
    padded = (counts + MOE_ROWS - 1) // MOE_ROWS * MOE_ROWS
    start = jnp.cumsum(counts) - counts
    pend = jnp.cumsum(padded)
    pstart = pend - padded
    dest = pstart[e_sorted] + jnp.arange(nk, dtype=jnp.int32) - start[e_sorted]
    n_blocks = -(-nk // MOE_ROWS) + n_experts
    slot_tok = jnp.zeros((n_blocks * MOE_ROWS,), jnp.int32).at[dest].set(tok_sorted)
    block_e = jnp.clip(jnp.searchsorted(pend, jnp.arange(n_blocks, dtype=jnp.int32) * MOE_ROWS, side="right"),
                       0, n_experts - 1).astype(jnp.int32)
    n_used = (pend[-1:] // MOE_ROWS).astype(jnp.int32)
    pos = jnp.zeros((nk,), jnp.int32).at[order].set(dest).reshape(n, TOP_K)
    return slot_tok, block_e, n_used, pos


def _ln2_kernel(x_ref, y0_ref, y1_ref, y2_ref, y3_ref, route_ref, mod_ref, g_ref, b_ref, o_ref, *, alpha):
    ys = (y0_ref, y1_ref, y2_ref, y3_ref)
    f = route_ref[:, TOP_K:TOP_K + 1] * ys[0][...]
    for k in range(1, TOP_K):
        f = f + route_ref[:, TOP_K + k:TOP_K + k + 1] * ys[k][...]
    o_ref[...] = _layernorm(alpha * x_ref[...] + mod_ref[5:6, :] * f, g_ref[...], b_ref[...])


def _ln2(x1, ys, route, mods, g, b, *, batch, nblk, n_ctx_blk, alpha):
    d = x1.shape[1]
    tm = ROW_TILE
    r_map = lambda bi, t: (bi * nblk + t, 0)

    def mod_map(bi, t):
        return (jnp.where(t < n_ctx_blk, batch, bi), 0, 0)

    return pl.pallas_call(
        functools.partial(_ln2_kernel, alpha=alpha),
        grid=(batch, nblk),
        in_specs=[pl.BlockSpec((tm, d), r_map)] + [pl.BlockSpec((tm, d), r_map)] * TOP_K
                 + [pl.BlockSpec((tm, LANES), r_map), pl.BlockSpec((None, 6, d), mod_map),
                    pl.BlockSpec((1, d), lambda bi, t: (0, 0)), pl.BlockSpec((1, d), lambda bi, t: (0, 0))],
        out_specs=pl.BlockSpec((tm, d), r_map),
        out_shape=jax.ShapeDtypeStruct((batch * nblk * tm, d), F32),
        compiler_params=_cparams(("parallel", "parallel"), VMEM_LIMIT),
        name="combine_ln2",
    )(x1, *ys, route, mods, g, b)


def _rope_perm(n_heads):
    within = np.concatenate([np.arange(0, HEAD_DIM, 2), np.arange(1, HEAD_DIM, 2)])
    return (np.arange(n_heads)[:, None] * HEAD_DIM + within[None, :]).reshape(-1)


def _pair_heads(n_heads):
    half = n_heads // 2
    return np.stack([np.arange(half), np.arange(half) + half], axis=1).reshape(-1)


def _head_cols(head_order, within=None):
    within = np.arange(HEAD_DIM) if within is None else within
    return (np.asarray(head_order)[:, None] * HEAD_DIM + within[None, :]).reshape(-1)


def _rope_tables(s_len, n_ctx):
    t = np.arange(s_len)
    row = (t // GRID_W).astype(np.float32)
    col = (t % GRID_W).astype(np.float32)
    axis_dim = HEAD_DIM // 2
    freqs = jnp.asarray(ROPE_THETA, F32) ** (-jnp.arange(0, axis_dim, 2, dtype=F32) / axis_dim)
    ang = jnp.concatenate([jnp.asarray(row)[:, None] * freqs, jnp.asarray(col)[:, None] * freqs], axis=-1)
    cos, sin = jnp.cos(ang), jnp.sin(ang)
    cos_h = jnp.concatenate([cos, cos], axis=-1)
    sin_h = jnp.concatenate([-sin, sin], axis=-1)
    cos_t = jnp.concatenate([jnp.ones((n_ctx, HEAD_DIM), F32), cos_h], axis=0)
    sin_t = jnp.concatenate([jnp.zeros((n_ctx, HEAD_DIM), F32), sin_h], axis=0)
    return jnp.tile(cos_t, (1, 2)), jnp.tile(sin_t, (1, 2))


def _moe_and_ln2(x1, tok, route, mods, router_n, w_in, b_in, w_out, b_out, g, b, *, batch, nblk, n_ctx_blk, alpha):
    slot_tok, block_e, n_used, pos = _route_plan(route, router_n)
    xs = jnp.take(tok, slot_tok, axis=0)
    yb = _experts(xs, block_e, n_used, w_in, b_in, w_out, b_out)
    ys = [jnp.take(yb, pos[:, k], axis=0) for k in range(TOP_K)]
    return _ln2(x1, ys, route, mods, g, b, batch=batch, nblk=nblk, n_ctx_blk=n_ctx_blk, alpha=alpha)


def kernel(x, c, ctx, c_ctx, mod_w, mod_b, ln1_g, ln1_b, ln2_g, ln2_b, router_w, router_b, moe_w_in, moe_b_in,
           moe_w_out, moe_b_out, ab_w_in, ab_w_out, na_rpb, diff_lq1, diff_lk1, diff_lq2, diff_lk2, diff_subln,
           cd_w_in, cd_w_out, gqa_q_norm, gqa_k_norm, swa_sink):
    batch, s_len, d = x.shape
    n_ctx = ctx.shape[1]
    t_len = n_ctx + s_len
    depth = mod_w.shape[0]
    n_experts = router_w.shape[2]
    alpha = (2.0 * depth) ** 0.25
    tm = ROW_TILE
    nbt = t_len // tm
    nbs = s_len // tm
    assert depth == 2 and n_ctx % tm == 0 and s_len % tm == 0 and batch + 1 <= 8

    cc = jnp.zeros((8, d), F32).at[:batch].set(c).at[batch].set(c_ctx)
    mod_all = _modulation(cc, mod_w, mod_b)
    mods = [mod_all[l, :batch + 1].reshape(batch + 1, 6, d) for l in range(depth)]
    cos_t, sin_t = _rope_tables(s_len, n_ctx)
    rw = [jnp.zeros((d, LANES), F32).at[:, :n_experts].set(router_w[l]).astype(_MXU) for l in range(depth)]
    rb = [jnp.full((1, LANES), NEG_INF, F32).at[0, :n_experts].set(router_b[l]) for l in range(depth)]

    stream = jnp.concatenate([ctx, x], axis=1).reshape(batch * t_len, d)

    na_w = NA_HEADS * HEAD_DIM
    df_w = DIFF_HEADS * 2 * HEAD_DIM
    rp = _rope_perm(2 * DIFF_HEADS)
    cols0 = np.concatenate([np.arange(3 * na_w), 3 * na_w + rp, 3 * na_w + df_w + rp,
                            3 * na_w + 2 * df_w + np.arange(df_w)])
    w0 = jnp.take(ab_w_in[0], cols0, axis=1).astype(_MXU)
    plan0 = ((0, na_w, None, False, ATTN_SCALE), (na_w, 2 * na_w, None, False, 1.0),
             (3 * na_w, df_w, None, True, ATTN_SCALE), (3 * na_w + df_w, df_w, None, True, 1.0),
             (3 * na_w + 2 * df_w, df_w, None, False, 1.0))
    nw0 = jnp.zeros((8, LANES), F32)
    qkv0 = _in_proj(stream, mods[0], w0, cos_t, sin_t, nw0, plan0, batch=batch, n_ctx=n_ctx)
    qkv0 = qkv0.reshape(batch, t_len, -1)
    cb = lambda width: width // LANES
    lam_init0 = 0.8 - 0.6 * math.exp(-0.3 * 0)
    par = jnp.zeros((8, LANES), F32)
    par = par.at[0, :HEAD_DIM].set(diff_lq1[0]).at[1, :HEAD_DIM].set(diff_lk1[0])
    par = par.at[2, :HEAD_DIM].set(diff_lq2[0]).at[3, :HEAD_DIM].set(diff_lk2[0]).at[4].set(diff_subln[0])
    bias = _na_bias_tables(na_rpb[0], s_len // GRID_W)
    y_na = _neighbourhood(qkv0, bias, batch=batch, s_len=s_len, n_ctx=n_ctx,
                          q_col=0, k_col=cb(na_w), v_col=cb(2 * na_w))
    y_na_c = _flash(qkv0, batch=batch, n_groups=NA_HEADS // 2, n_pairs=1, q_col=0, k_col=cb(na_w),
                    v_col=cb(2 * na_w), group_stride=1, q_rows=n_ctx, q_off=0, kv_rows=n_ctx, mode="pair",
                    name="neighbourhood_attn_ctx")
    dq, dk, dv = cb(3 * na_w), cb(3 * na_w + df_w), cb(3 * na_w + 2 * df_w)
    y_df = _flash(qkv0, batch=batch, n_groups=DIFF_HEADS, n_pairs=1, q_col=dq, k_col=dk, v_col=dv, group_stride=1,
                  q_rows=s_len, q_off=n_ctx, kv_rows=t_len, mode="diff", par=par, lam_init=lam_init0,
                  name="diff_attn")
    y_df_c = _flash(qkv0, batch=batch, n_groups=DIFF_HEADS, n_pairs=1, q_col=dq, k_col=dk, v_col=dv,
                    group_stride=1, q_rows=n_ctx, q_off=0, kv_rows=n_ctx, mode="diff", par=par,
                    lam_init=lam_init0, name="diff_attn_ctx")
    ya = jnp.concatenate([y_na_c, y_na], axis=1).reshape(batch * t_len, -1)
    yb = jnp.concatenate([y_df_c, y_df], axis=1).reshape(batch * t_len, -1)
    x1, tok, route = _post(stream, ya, yb, ab_w_out[0].astype(_MXU), mods[0], ln1_g[0][None], ln1_b[0][None],
                           rw[0], rb[0], batch=batch, nbt_in=nbt, off=0, nblk=nbt, n_ctx_blk=n_ctx // tm, alpha=alpha)
    stream = _moe_and_ln2(x1, tok, route, mods[0], n_experts, moe_w_in[0], moe_b_in[0], moe_w_out[0], moe_b_out[0],
                          ln2_g[0][None], ln2_b[0][None], batch=batch, nblk=nbt, n_ctx_blk=n_ctx // tm, alpha=alpha)

    hw = GQA_HEADS * HEAD_DIM
    kw = GQA_KV_HEADS * HEAD_DIM
    within = np.concatenate([np.arange(0, HEAD_DIM, 2), np.arange(1, HEAD_DIM, 2)])
    o_gq, o_gk, o_gv = 0, hw, hw + kw
    o_wq, o_wk, o_wv = hw + 2 * kw, 2 * hw + 2 * kw, 2 * hw + 3 * kw
    cols1 = np.concatenate([
        o_gq + _head_cols(_pair_heads(GQA_HEADS), within), o_wq + _head_cols(_pair_heads(SWA_HEADS), within),
        o_gk + _head_cols(np.arange(GQA_KV_HEADS), within), o_gv + np.arange(kw),
        o_wk + _head_cols(np.arange(SWA_KV_HEADS), within), o_wv + np.arange(kw)])
    w1 = jnp.take(cd_w_in[0], cols1, axis=1).astype(_MXU)
    plan1 = ((0, hw, 0, True, ATTN_SCALE), (hw, hw, None, True, ATTN_SCALE), (2 * hw, kw, 1, True, 1.0),
             (2 * hw + kw, kw, None, False, 1.0), (2 * hw + 2 * kw, kw, None, True, 1.0),
             (2 * hw + 3 * kw, kw, None, False, 1.0))
    nw1 = jnp.zeros((8, LANES), F32).at[0].set(jnp.tile(gqa_q_norm[0][within], 2))
    nw1 = nw1.at[1].set(jnp.tile(gqa_k_norm[0][within], 2))
    qkv1 = _in_proj(stream, mods[1], w1, cos_t, sin_t, nw1, plan1, batch=batch, n_ctx=n_ctx)
    qkv1 = qkv1.reshape(batch, t_len, -1)
    kcol = cb(2 * hw)
    y_c = _flash(qkv1, batch=batch, n_groups=1, n_pairs=GQA_HEADS // 2, q_col=0, k_col=kcol, v_col=kcol + 1,
                 group_stride=0, q_rows=s_len, q_off=n_ctx, kv_rows=t_len, mode="pair", name="gqa_attn")
    sink = swa_sink[0][_pair_heads(SWA_HEADS)]
    init = jnp.broadcast_to(sink[:, None], (SWA_HEADS, LANES)).astype(F32)
    y_d = _windowed(qkv1, init, batch=batch, s_len=s_len, n_ctx=n_ctx, q_col=cb(hw), k_col=kcol + 2,
                    v_col=kcol + 3)
    out_rows = np.concatenate([_head_cols(_pair_heads(GQA_HEADS)), hw + _head_cols(_pair_heads(SWA_HEADS))])
    wo1 = jnp.take(cd_w_out[0], out_rows, axis=0).astype(_MXU)
    x1, tok, route = _post(stream, y_c.reshape(batch * s_len, -1), y_d.reshape(batch * s_len, -1), wo1, mods[1],
                           ln1_g[1][None], ln1_b[1][None], rw[1], rb[1], batch=batch, nbt_in=nbt,
                           off=n_ctx // tm, nblk=nbs, n_ctx_blk=0, alpha=alpha)
    out = _moe_and_ln2(x1, tok, route, mods[1], n_experts, moe_w_in[1], moe_b_in[1], moe_w_out[1], moe_b_out[1],
                       ln2_g[1][None], ln2_b[1][None], batch=batch, nblk=nbs, n_ctx_blk=0, alpha=alpha)
    return out.reshape(batch, s_len, d)
```

```python
import functools
import math

import numpy as np
import jax
import jax.numpy as jnp
from jax import lax
from jax.experimental import pallas as pl
from jax.experimental.pallas import tpu as pltpu

F32 = jnp.float32
_MXU = jnp.bfloat16

HEAD_DIM = 64
GRID_W = 64
ATTN_SCALE = HEAD_DIM ** -0.5
ROPE_THETA = 10000.0
NA_HEADS = 8
NA_KH = 8
NA_KW = 16
DIFF_HEADS = 4
GQA_HEADS = 8
GQA_KV_HEADS = 2
SWA_HEADS = 8
SWA_KV_HEADS = 2
SWA_WINDOW = 128
TOP_K = 4
SWIGLU_LIMIT = 7.0
SWIGLU_ALPHA = 1.702
LN_EPS = 1e-5
RMS_EPS = 1e-6
NEG_INF = -1e30

LANES = 128
ROW_TILE = 256
MOE_ROWS = 256
NA_Q_ROWS = 4
VMEM_LIMIT = 52 * 1024 * 1024


def _cparams(sem, vmem=None):
    return pltpu.CompilerParams(dimension_semantics=sem, vmem_limit_bytes=vmem)


def _mod_kernel(c_ref, w_ref, b_ref, o_ref):
    c = c_ref[...]
    a = (c / (1.0 + jnp.exp(-c))).astype(_MXU)
    o_ref[...] = jnp.dot(a, w_ref[...].astype(_MXU), preferred_element_type=F32) + b_ref[...]


def _modulation(cc, mod_w, mod_b):
    depth, d, d6 = mod_w.shape
    tn = d6 // 4
    return pl.pallas_call(
        _mod_kernel,
        grid=(depth, d6 // tn),
        in_specs=[pl.BlockSpec((8, d), lambda l, j: (0, 0)),
                  pl.BlockSpec((None, d, tn), lambda l, j: (l, 0, j)),
                  pl.BlockSpec((None, 1, tn), lambda l, j: (l, 0, j))],
        out_specs=pl.BlockSpec((None, 8, tn), lambda l, j: (l, 0, j)),
        out_shape=jax.ShapeDtypeStruct((depth, 8, d6), F32),
        compiler_params=_cparams(("arbitrary", "arbitrary"), VMEM_LIMIT),
        name="modulation",
    )(cc, mod_w, mod_b.reshape(depth, 1, d6))


def _in_kernel(x_ref, mod_ref, w_ref, cos_ref, sin_ref, nw_ref, o_ref, *, plan):
    x = x_ref[...]
    h = (x * (1.0 + mod_ref[1:2, :]) + mod_ref[0:1, :]).astype(_MXU)
    tm = x.shape[0]
    lane = lax.broadcasted_iota(jnp.int32, (tm, LANES), 1)
    first_half = (lane & (HEAD_DIM // 2)) == 0
    gi = lax.broadcasted_iota(jnp.int32, (LANES, LANES), 0) // HEAD_DIM
    gj = lax.broadcasted_iota(jnp.int32, (LANES, LANES), 1) // HEAD_DIM
    seg = jnp.where(gi == gj, 1.0, 0.0).astype(_MXU)
    cos = cos_ref[...]
    sin = sin_ref[...]
    for (c0, width, norm_row, rope, scale) in plan:
        acc = jnp.dot(h, w_ref[:, c0:c0 + width], preferred_element_type=F32)
        for j in range(width // LANES):
            a = acc[:, j * LANES:(j + 1) * LANES]
            if norm_row is not None:
                a2 = a * a
                hi = a2.astype(_MXU)
                lo = (a2 - hi.astype(F32)).astype(_MXU)
                ss = (jnp.dot(hi, seg, preferred_element_type=F32)
                      + jnp.dot(lo, seg, preferred_element_type=F32))
                a = a * lax.rsqrt(ss * (1.0 / HEAD_DIM) + RMS_EPS) * nw_ref[norm_row:norm_row + 1, :]
            if rope:
                partner = jnp.where(first_half, pltpu.roll(a, LANES - HEAD_DIM // 2, 1),
                                    pltpu.roll(a, HEAD_DIM // 2, 1))
                a = a * cos + partner * sin
            if scale != 1.0:
                a = a * scale
            o_ref[:, c0 + j * LANES:c0 + (j + 1) * LANES] = a.astype(o_ref.dtype)


def _in_proj(x2d, mods, w, cos_t, sin_t, nw, plan, *, batch, n_ctx):
    rows, d = x2d.shape
    ncols = w.shape[1]
    tm = ROW_TILE
    nbt = rows // batch // tm
    nctx = n_ctx // tm

    def mod_map(i):
        return (jnp.where(i % nbt < nctx, batch, i // nbt), 0, 0)

    return pl.pallas_call(
        functools.partial(_in_kernel, plan=plan),
        grid=(rows // tm,),
        in_specs=[pl.BlockSpec((tm, d), lambda i: (i, 0)),
                  pl.BlockSpec((None, 6, d), mod_map),
                  pl.BlockSpec((d, ncols), lambda i: (0, 0)),
                  pl.BlockSpec((tm, LANES), lambda i: (i % nbt, 0)),
                  pl.BlockSpec((tm, LANES), lambda i: (i % nbt, 0)),
                  pl.BlockSpec((8, LANES), lambda i: (0, 0))],
        out_specs=pl.BlockSpec((tm, ncols), lambda i: (i, 0)),
        out_shape=jax.ShapeDtypeStruct((rows, ncols), _MXU),
        compiler_params=_cparams(("parallel",), VMEM_LIMIT),
        name="in_proj",
    )(x2d, mods, w, cos_t, sin_t, nw)


def _attn_kernel(*refs, n_pairs, nseg, mode, bias_segs, band, nkv, kv_axis, q_axis,
                 has_init, lam_init, s_len):
    it = iter(refs)
    q_ref = next(it)
    k_refs = [next(it) for _ in range(nseg)]
    v_refs = [next(it) for _ in range(nseg)]
    bias_ref = next(it) if bias_segs else None
    init_ref = next(it) if has_init else None
    par_ref = next(it) if mode == "diff" else None
    o_ref = next(it)
    m_scr, l_scr, acc_scr = (next(it), next(it), next(it)) if nkv > 1 else (None, None, None)

    tq = q_ref.shape[0]
    nheads = 2 * n_pairs
    lane_q = lax.broadcasted_iota(jnp.int32, (tq, LANES), 1)
    is_lo = lane_q < HEAD_DIM
    lane_1 = lax.broadcasted_iota(jnp.int32, (1, LANES), 1)
    lo_mask = jnp.where(lane_1 < HEAD_DIM, 1.0, 0.0).astype(q_ref.dtype)
    hi_mask = jnp.where(lane_1 < HEAD_DIM, 0.0, 1.0).astype(q_ref.dtype)

    masks = [None] * nseg
    if band:
        qi = pl.program_id(q_axis)
        qpos = qi * tq + lax.broadcasted_iota(jnp.int32, (tq, tq), 0)
        for j in range(1, nseg):
            kpos = (qi + j - 2) * tq + lax.broadcasted_iota(jnp.int32, (tq, tq), 1)
            dist = jnp.abs(qpos - kpos)
            ok = jnp.where(kpos >= 0, jnp.where(kpos < s_len, dist, SWA_WINDOW + 1), SWA_WINDOW + 1)
            masks[j] = ok <= SWA_WINDOW

    def init_state(hh):
        if has_init:
            m0 = jnp.broadcast_to(init_ref[hh:hh + 1, 0:1], (tq, 1))
            l0 = jnp.ones((tq, 1), F32)
        else:
            m0 = jnp.full((tq, 1), NEG_INF, F32)
            l0 = jnp.zeros((tq, 1), F32)
        return m0, l0, jnp.zeros((tq, LANES), F32)

    def head_step(hh, state):
        p_idx, half = hh // 2, hh % 2
        q2 = q_ref[:, p_idx * LANES:(p_idx + 1) * LANES]
        qm = q2 * (lo_mask if half == 0 else hi_mask)
        m_prev, l_prev, acc_prev = state
        ss = []
        boff = 0
        for j in range(nseg):
            s = lax.dot_general(qm, k_refs[j][...], (((1,), (1,)), ((), ())),
                                preferred_element_type=F32)
            if j in bias_segs:
                n = k_refs[j].shape[0]
                s = s + bias_ref[half, :, boff:boff + n]
                boff += n
            if masks[j] is not None:
                s = jnp.where(masks[j], s, NEG_INF)
            ss.append(s)
        s_max = functools.reduce(jnp.maximum, [jnp.max(s, axis=1, keepdims=True) for s in ss])
        m_new = jnp.maximum(m_prev, s_max)
        alpha = jnp.exp(m_prev - m_new)
        l_new = alpha * l_prev
        acc = alpha * acc_prev
        for j in range(nseg):
            p = jnp.exp(ss[j] - m_new)
            l_new = l_new + jnp.sum(p, axis=1, keepdims=True)
            acc = acc + jnp.dot(p.astype(_MXU), v_refs[j][...], preferred_element_type=F32)
        return m_new, l_new, acc

    def finalize(states):
        if mode == "pair":
            for p_idx in range(n_pairs):
                _, l_lo, a_lo = states[2 * p_idx]
                _, l_hi, a_hi = states[2 * p_idx + 1]
                out = jnp.where(is_lo, a_lo / l_lo, a_hi / l_hi)
                o_ref[:, p_idx * LANES:(p_idx + 1) * LANES] = out.astype(o_ref.dtype)
        else:
            lam = (jnp.exp(jnp.sum(par_ref[0:1, :] * par_ref[1:2, :], axis=1, keepdims=True))
                   - jnp.exp(jnp.sum(par_ref[2:3, :] * par_ref[3:4, :], axis=1, keepdims=True))
                   + lam_init)
            _, l_lo, a_lo = states[0]
            _, l_hi, a_hi = states[1]
            out = a_lo / l_lo - lam * (a_hi / l_hi)
            ms = jnp.mean(out * out, axis=1, keepdims=True)
            out = out * lax.rsqrt(ms + RMS_EPS) * par_ref[4:5, :] * (1.0 - lam_init)
            o_ref[...] = out.astype(o_ref.dtype)

    if nkv == 1:
        finalize([head_step(hh, init_state(hh)) for hh in range(nheads)])
        return

    ki = pl.program_id(kv_axis)

    @pl.when(ki == 0)
    def _():
        for hh in range(nheads):
            m0, l0, a0 = init_state(hh)
            m_scr[hh] = m0
            l_scr[hh] = l0
            acc_scr[hh] = a0

    for hh in range(nheads):
        m_new, l_new, acc = head_step(hh, (m_scr[hh], l_scr[hh], acc_scr[hh]))
        m_scr[hh] = m_new
        l_scr[hh] = l_new
        acc_scr[hh] = acc

    @pl.when(ki == nkv - 1)
    def _():
        finalize([(m_scr[hh], l_scr[hh], acc_scr[hh]) for hh in range(nheads)])


def _attention(q_arr, kv_arr, *, grid, q_spec, k_specs, v_specs, out_spec, out_shape, n_pairs, mode="pair",
               bias=None, bias_spec=None, bias_segs=(), band=False, nkv=1, kv_axis=0, q_axis=0,
               init=None, par=None, lam_init=0.0, s_len=0, sem=None, name="attention"):
    nseg = len(k_specs)
    args = [q_arr] + [kv_arr] * (2 * nseg)
    specs = [q_spec] + list(k_specs) + list(v_specs)
    if bias is not None:
        args.append(bias)
        specs.append(bias_spec)
    if init is not None:
        args.append(init)
        specs.append(pl.BlockSpec(init.shape, lambda *_: (0, 0)))
    if par is not None:
        args.append(par)
        specs.append(pl.BlockSpec(par.shape, lambda *_: (0, 0)))
    tq = q_spec.block_shape[-2]
    scratch = []
    if nkv > 1:
        scratch = [pltpu.VMEM((2 * n_pairs, tq, 1), F32), pltpu.VMEM((2 * n_pairs, tq, 1), F32),
                   pltpu.VMEM((2 * n_pairs, tq, LANES), F32)]
    kern = functools.partial(_attn_kernel, n_pairs=n_pairs, nseg=nseg, mode=mode, bias_segs=tuple(bias_segs),
                             band=band, nkv=nkv, kv_axis=kv_axis, q_axis=q_axis, has_init=init is not None,
                             lam_init=lam_init, s_len=s_len)
    return pl.pallas_call(
        kern, grid=grid, in_specs=specs, out_specs=out_spec, out_shape=out_shape, scratch_shapes=scratch,
        compiler_params=_cparams(sem, VMEM_LIMIT), name=name,
    )(*args)


def _pick_tile(n, candidates):
    for c in candidates:
        if n % c == 0:
            return c
    raise ValueError(f"no tile for {n}")


def _flash(qkv, *, batch, n_groups, n_pairs, q_col, k_col, v_col, group_stride, q_rows, q_off, kv_rows,
           mode, par=None, lam_init=0.0, name):
    tq = ROW_TILE
    tk = _pick_tile(kv_rows, (768, 512, 384, 256, 128))
    nq, nkv = q_rows // tq, kv_rows // tk
    qw = n_pairs * LANES
    q_spec = pl.BlockSpec((None, tq, qw), lambda b, g, i, k: (b, q_off // tq + i, q_col // n_pairs + g * group_stride))
    k_spec = pl.BlockSpec((None, tk, LANES), lambda b, g, i, k: (b, k, k_col + g * group_stride))
    v_spec = pl.BlockSpec((None, tk, LANES), lambda b, g, i, k: (b, k, v_col + g * group_stride))
    out_spec = pl.BlockSpec((None, tq, qw), lambda b, g, i, k: (b, i, g))
    return _attention(
        qkv, qkv, grid=(batch, n_groups, nq, nkv), q_spec=q_spec, k_specs=[k_spec], v_specs=[v_spec],
        out_spec=out_spec, out_shape=jax.ShapeDtypeStruct((batch, q_rows, n_groups * qw), _MXU),
        n_pairs=n_pairs, mode=mode, nkv=nkv, kv_axis=3, par=par, lam_init=lam_init,
        sem=("parallel", "parallel", "parallel", "arbitrary"), name=name)


def _na_bias_tables(rpb, rows):
    nh = rpb.shape[0]
    nkr = 3 * NA_Q_ROWS
    qc = np.arange(GRID_W)[:, None]
    kc = np.arange(GRID_W)[None, :]
    ws = np.clip(qc - NA_KW // 2, 0, GRID_W - NA_KW)
    cvalid = ((kc >= ws) & (kc < ws + NA_KW)).reshape(-1)
    dc = (kc - qc + NA_KW - 1).reshape(-1)
    onehot = ((np.arange(2 * NA_KW - 1)[:, None] == dc[None, :]) & cvalid[None, :]).astype(np.float32)
    tiles = jnp.einsum("hrd,dx->hrx", rpb.astype(F32), jnp.asarray(onehot), precision=lax.Precision.HIGHEST)
    tiles = jnp.where(cvalid[None, None, :], tiles, NEG_INF)
    qr = np.arange(NA_Q_ROWS)[:, None]
    kr = np.arange(nkr)[None, :]
    tables = []
    for variant in range(3):
        if variant == 0:
            r0, k0, nrows = 0, 0, rows
        elif variant == 1:
            r0, k0, nrows = 2 * NA_Q_ROWS, NA_Q_ROWS, 8 * NA_Q_ROWS
        else:
            r0, k0, nrows = rows - NA_Q_ROWS, rows - nkr, rows
        r = r0 + qr
        rp = k0 + kr
        rs = np.clip(r - NA_KH // 2, 0, nrows - NA_KH)
        rvalid = (rp >= rs) & (rp < rs + NA_KH)
        dr = np.clip(rp - r + NA_KH - 1, 0, 2 * NA_KH - 2)
        t = jnp.take(tiles, jnp.asarray(dr.reshape(-1).astype(np.int32)), axis=1)
        t = jnp.where(rvalid.reshape(-1)[None, :, None], t, NEG_INF)
        t = t.reshape(nh, NA_Q_ROWS, nkr, GRID_W, GRID_W).transpose(0, 1, 3, 2, 4)
        tables.append(t.reshape(nh // 2, 2, NA_Q_ROWS * GRID_W, nkr * GRID_W))
    return jnp.stack(tables)


def _neighbourhood(qkv, bias, *, batch, s_len, n_ctx, q_col, k_col, v_col):
    tq = NA_Q_ROWS * GRID_W
    nq = s_len // tq
    n_pairs = NA_HEADS // 2
    cb = n_ctx // tq
    q_spec = pl.BlockSpec((None, tq, LANES), lambda p, i, b: (b, cb + i, q_col + p))

    def seg_specs(col):
        specs = [pl.BlockSpec((None, n_ctx, LANES), lambda p, i, b: (b, 0, col + p))]
        for j in range(3):
            specs.append(pl.BlockSpec(
                (None, tq, LANES),
                lambda p, i, b, j=j: (b, cb + jnp.clip(i - 1, 0, nq - 3) + j, col + p)))
        return specs

    bias_spec = pl.BlockSpec((None, None, 2, tq, 3 * tq),
                             lambda p, i, b: (jnp.where(i == 0, 0, jnp.where(i == nq - 1, 2, 1)), p, 0, 0, 0))
    out_spec = pl.BlockSpec((None, tq, LANES), lambda p, i, b: (b, i, p))
    return _attention(
        qkv, qkv, grid=(n_pairs, nq, batch), q_spec=q_spec, k_specs=seg_specs(k_col), v_specs=seg_specs(v_col),
        out_spec=out_spec, out_shape=jax.ShapeDtypeStruct((batch, s_len, n_pairs * LANES), _MXU),
        n_pairs=1, bias=bias, bias_spec=bias_spec, bias_segs=(1, 2, 3),
        sem=("parallel", "parallel", "parallel"), name="neighbourhood_attn")


def _windowed(qkv, init, *, batch, s_len, n_ctx, q_col, k_col, v_col):
    tq = SWA_WINDOW
    nb = s_len // tq
    n_pairs = SWA_HEADS // 2
    cb = n_ctx // tq
    qw = n_pairs * LANES
    q_spec = pl.BlockSpec((None, tq, qw), lambda b, i: (b, cb + i, q_col // n_pairs))

    def seg_specs(col):
        specs = [pl.BlockSpec((None, n_ctx, LANES), lambda b, i: (b, 0, col))]
        for j in range(3):
            specs.append(pl.BlockSpec((None, tq, LANES),
                                      lambda b, i, j=j: (b, cb + jnp.clip(i - 1 + j, 0, nb - 1), col)))
        return specs

    out_spec = pl.BlockSpec((None, tq, qw), lambda b, i: (b, i, 0))
    return _attention(
        qkv, qkv, grid=(batch, nb), q_spec=q_spec, k_specs=seg_specs(k_col), v_specs=seg_specs(v_col),
        out_spec=out_spec, out_shape=jax.ShapeDtypeStruct((batch, s_len, qw), _MXU),
        n_pairs=n_pairs, band=True, q_axis=1, init=init, s_len=s_len,
        sem=("parallel", "parallel"), name="windowed_attn")


def _layernorm(z, g, b):
    mu = jnp.mean(z, axis=1, keepdims=True)
    zc = z - mu
    var = jnp.mean(zc * zc, axis=1, keepdims=True)
    return zc * lax.rsqrt(var + LN_EPS) * g + b


def _post_kernel(x_ref, ya_ref, yb_ref, wo_ref, mod_ref, g_ref, b_ref, rw_ref, rb_ref,
                 x1_ref, tok_ref, route_ref, *, alpha):
    half = ya_ref.shape[1]
    y = (jnp.dot(ya_ref[...], wo_ref[:half, :], preferred_element_type=F32)
         + jnp.dot(yb_ref[...], wo_ref[half:, :], preferred_element_type=F32))
    x1 = _layernorm(alpha * x_ref[...] + mod_ref[2:3, :] * y, g_ref[...], b_ref[...])
    x1_ref[...] = x1
    tok = (x1 * (1.0 + mod_ref[4:5, :]) + mod_ref[3:4, :]).astype(tok_ref.dtype)
    tok_ref[...] = tok
    logits = jnp.dot(tok, rw_ref[...], preferred_element_type=F32) + rb_ref[...]
    tm = logits.shape[0]
    lane = lax.broadcasted_iota(jnp.int32, (tm, LANES), 1).astype(F32)
    vals, idxs = [], []
    for _ in range(TOP_K):
        mx = jnp.max(logits, axis=1, keepdims=True)
        ix = jnp.min(jnp.where(logits == mx, lane, float(LANES)), axis=1, keepdims=True)
        vals.append(mx)
        idxs.append(ix)
        logits = jnp.where(lane == ix, -3.0e38, logits)
    es = [jnp.exp(v - vals[0]) for v in vals]
    den = functools.reduce(lambda a, c: a + c, es)
    route = jnp.zeros((tm, LANES), F32)
    for k in range(TOP_K):
        route = jnp.where(lane == float(k), idxs[k], route)
        route = jnp.where(lane == float(TOP_K + k), es[k] / den, route)
    route_ref[...] = route


def _post(x2d, ya, yb, wo, mods, g, b, rw, rb, *, batch, nbt_in, off, nblk, n_ctx_blk, alpha):
    d = x2d.shape[1]
    tm = ROW_TILE
    half = ya.shape[1]

    def mod_map(bi, t):
        return (jnp.where(t < n_ctx_blk, batch, bi), 0, 0)

    rows_out = batch * nblk * tm
    o_map = lambda bi, t: (bi * nblk + t, 0)
    return pl.pallas_call(
        functools.partial(_post_kernel, alpha=alpha),
        grid=(batch, nblk),
        in_specs=[pl.BlockSpec((tm, d), lambda bi, t: (bi * nbt_in + off + t, 0)),
                  pl.BlockSpec((tm, half), o_map),
                  pl.BlockSpec((tm, half), o_map),
                  pl.BlockSpec((d, d), lambda bi, t: (0, 0)),
                  pl.BlockSpec((None, 6, d), mod_map),
                  pl.BlockSpec((1, d), lambda bi, t: (0, 0)),
                  pl.BlockSpec((1, d), lambda bi, t: (0, 0)),
                  pl.BlockSpec((d, LANES), lambda bi, t: (0, 0)),
                  pl.BlockSpec((1, LANES), lambda bi, t: (0, 0))],
        out_specs=[pl.BlockSpec((tm, d), o_map), pl.BlockSpec((tm, d), o_map), pl.BlockSpec((tm, LANES), o_map)],
        out_shape=[jax.ShapeDtypeStruct((rows_out, d), F32), jax.ShapeDtypeStruct((rows_out, d), _MXU),
                   jax.ShapeDtypeStruct((rows_out, LANES), F32)],
        compiler_params=_cparams(("parallel", "parallel"), VMEM_LIMIT),
        name="post_attn",
    )(x2d, ya, yb, wo, mods, g, b, rw, rb)


def _moe_kernel(be_ref, nu_ref, x_ref, wi_ref, bi_ref, wo_ref, bo_ref, o_ref, wi_s, wo_s):
    i = pl.program_id(0)
    f = wo_s.shape[0]
    e = be_ref[i]
    changed = jnp.logical_or(i == 0, be_ref[jnp.maximum(i - 1, 0)] != e)

    @pl.when(i < nu_ref[0])
    def _():
        @pl.when(changed)
        def _():
            wi_s[...] = wi_ref[...].astype(wi_s.dtype)
            wo_s[...] = wo_ref[...].astype(wo_s.dtype)

        hh = jnp.dot(x_ref[...], wi_s[...], preferred_element_type=F32) + bi_ref[...]
        gate = jnp.minimum(hh[:, :f], SWIGLU_LIMIT)
        up = jnp.clip(hh[:, f:], -SWIGLU_LIMIT, SWIGLU_LIMIT)
        act = gate * (1.0 / (1.0 + jnp.exp(-SWIGLU_ALPHA * gate))) * (up + 1.0)
        o_ref[...] = jnp.dot(act.astype(wo_s.dtype), wo_s[...], preferred_element_type=F32) + bo_ref[...]


def _experts(xs, block_e, n_used, w_in, b_in, w_out, b_out):
    n_e, d, f2 = w_in.shape
    f = f2 // 2
    n_blocks = xs.shape[0] // MOE_ROWS

    def blk(i, be, nu):
        return jnp.minimum(i, nu[0] - 1)

    grid_spec = pltpu.PrefetchScalarGridSpec(
        num_scalar_prefetch=2,
        grid=(n_blocks,),
        in_specs=[pl.BlockSpec((MOE_ROWS, d), lambda i, be, nu: (blk(i, be, nu), 0)),
                  pl.BlockSpec((None, d, f2), lambda i, be, nu: (be[blk(i, be, nu)], 0, 0)),
                  pl.BlockSpec((None, 1, f2), lambda i, be, nu: (be[blk(i, be, nu)], 0, 0)),
                  pl.BlockSpec((None, f, d), lambda i, be, nu: (be[blk(i, be, nu)], 0, 0)),
                  pl.BlockSpec((None, 1, d), lambda i, be, nu: (be[blk(i, be, nu)], 0, 0))],
        out_specs=pl.BlockSpec((MOE_ROWS, d), lambda i, be, nu: (blk(i, be, nu), 0)),
        scratch_shapes=[pltpu.VMEM((d, f2), _MXU), pltpu.VMEM((f, d), _MXU)],
    )
    return pl.pallas_call(
        _moe_kernel, grid_spec=grid_spec,
        out_shape=jax.ShapeDtypeStruct((n_blocks * MOE_ROWS, d), F32),
        compiler_params=_cparams(("arbitrary",), VMEM_LIMIT),
        name="experts",
    )(block_e, n_used, xs, w_in, b_in.reshape(n_e, 1, f2), w_out, b_out.reshape(n_e, 1, d))


def _route_plan(route, n_experts):
    n = route.shape[0]
    top_idx = route[:, :TOP_K].astype(jnp.int32)
    nk = n * TOP_K
    flat_e = top_idx.reshape(-1)
    order = jnp.argsort(flat_e).astype(jnp.int32)
    e_sorted = flat_e[order]
    experts = jnp.arange(n_experts, dtype=jnp.int32)
    counts = jnp.sum(flat_e[:, None] == experts[None, :], axis=0, dtype=jnp.int32)
    padded = (counts + MOE_ROWS - 1) // MOE_ROWS * MOE_ROWS
    start = jnp.cumsum(counts) - counts
    pend = jnp.cumsum(padded)
    pstart = pend - padded
    n_blocks = -(-nk // MOE_ROWS) + n_experts
    first = jnp.arange(n_blocks, dtype=jnp.int32) * MOE_ROWS
    block_e = jnp.minimum(jnp.sum(pend[None, :] <= first[:, None], axis=1, dtype=jnp.int32), n_experts - 1)
    n_used = (pend[-1:] // MOE_ROWS).astype(jnp.int32)
    slot = jnp.arange(n_blocks * MOE_ROWS, dtype=jnp.int32)
    slot_e = jnp.repeat(block_e, MOE_ROWS)
    rank = slot - pstart[slot_e]
    src = jnp.clip(start[slot_e] + rank, 0, nk - 1)
    slot_tok = jnp.where(rank < counts[slot_e], order[src] // TOP_K, 0)
    dest = pstart[e_sorted] + jnp.arange(nk, dtype=jnp.int32) - start[e_sorted]
    pos = jnp.zeros((nk,), jnp.int32).at[order].set(dest, unique_indices=True).reshape(n, TOP_K)
    return slot_tok, block_e, n_used, pos


def _ln2_kernel(x_ref, y0_ref, y1_ref, y2_ref, y3_ref, route_ref, mod_ref, g_ref, b_ref, o_ref, *, alpha):
    ys = (y0_ref, y1_ref, y2_ref, y3_ref)
    f = route_ref[:, TOP_K:TOP_K + 1] * ys[0][...]
    for k in range(1, TOP_K):
        f = f + route_ref[:, TOP_K + k:TOP_K + k + 1] * ys[k][...]
    o_ref[...] = _layernorm(alpha * x_ref[...] + mod_ref[5:6, :] * f, g_ref[...], b_ref[...])


def _ln2(x1, ys, route, mods, g, b, *, batch, nblk, n_ctx_blk, alpha):
    d = x1.shape[1]
    tm = ROW_TILE
    r_map = lambda bi, t: (bi * nblk + t, 0)

    def mod_map(bi, t):
        return (jnp.where(t < n_ctx_blk, batch, bi), 0, 0)

    return pl.pallas_call(
        functools.partial(_ln2_kernel, alpha=alpha),
        grid=(batch, nblk),
        in_specs=[pl.BlockSpec((tm, d), r_map)] + [pl.BlockSpec((tm, d), r_map)] * TOP_K
                 + [pl.BlockSpec((tm, LANES), r_map), pl.BlockSpec((None, 6, d), mod_map),
                    pl.BlockSpec((1, d), lambda bi, t: (0, 0)), pl.BlockSpec((1, d), lambda bi, t: (0, 0))],
        out_specs=pl.BlockSpec((tm, d), r_map),
        out_shape=jax.ShapeDtypeStruct((batch * nblk * tm, d), F32),
        compiler_params=_cparams(("parallel", "parallel"), VMEM_LIMIT),
        name="combine_ln2",
    )(x1, *ys, route, mods, g, b)


def _rope_layout(w, n_heads):
    lead = w.shape[:-1]
    return w.reshape(lead + (n_heads, HEAD_DIM // 2, 2)).swapaxes(-1, -2).reshape(lead + (n_heads * HEAD_DIM,))


def _pair_layout(w, n_heads, axis=-1):
    axis = axis % w.ndim
    shape = w.shape
    w = w.reshape(shape[:axis] + (2, n_heads // 2, HEAD_DIM) + shape[axis + 1:])
    return w.swapaxes(axis, axis + 1).reshape(shape)


def _rope_tables(s_len, n_ctx):
    t = np.arange(s_len)
    row = (t // GRID_W).astype(np.float32)
    col = (t % GRID_W).astype(np.float32)
    axis_dim = HEAD_DIM // 2
    freqs = jnp.asarray(ROPE_THETA, F32) ** (-jnp.arange(0, axis_dim, 2, dtype=F32) / axis_dim)
    ang = jnp.concatenate([jnp.asarray(row)[:, None] * freqs, jnp.asarray(col)[:, None] * freqs], axis=-1)
    cos, sin = jnp.cos(ang), jnp.sin(ang)
    cos_h = jnp.concatenate([cos, cos], axis=-1)
    sin_h = jnp.concatenate([-sin, sin], axis=-1)
    cos_t = jnp.concatenate([jnp.ones((n_ctx, HEAD_DIM), F32), cos_h], axis=0)
    sin_t = jnp.concatenate([jnp.zeros((n_ctx, HEAD_DIM), F32), sin_h], axis=0)
    return jnp.tile(cos_t, (1, 2)), jnp.tile(sin_t, (1, 2))


def _moe_and_ln2(x1, tok, route, mods, router_n, w_in, b_in, w_out, b_out, g, b, *, batch, nblk, n_ctx_blk, alpha):
    slot_tok, block_e, n_used, pos = _route_plan(route, router_n)
    xs = jnp.take(tok, slot_tok, axis=0)
    yb = _experts(xs, block_e, n_used, w_in, b_in, w_out, b_out)
    ys = [jnp.take(yb, pos[:, k], axis=0) for k in range(TOP_K)]
    return _ln2(x1, ys, route, mods, g, b, batch=batch, nblk=nblk, n_ctx_blk=n_ctx_blk, alpha=alpha)


def kernel(x, c, ctx, c_ctx, mod_w, mod_b, ln1_g, ln1_b, ln2_g, ln2_b, router_w, router_b, moe_w_in, moe_b_in,
           moe_w_out, moe_b_out, ab_w_in, ab_w_out, na_rpb, diff_lq1, diff_lk1, diff_lq2, diff_lk2, diff_subln,
           cd_w_in, cd_w_out, gqa_q_norm, gqa_k_norm, swa_sink):
    batch, s_len, d = x.shape
    n_ctx = ctx.shape[1]
    t_len = n_ctx + s_len
    depth = mod_w.shape[0]
    n_experts = router_w.shape[2]
    alpha = (2.0 * depth) ** 0.25
    tm = ROW_TILE
    nbt = t_len // tm
    nbs = s_len // tm
    assert depth == 2 and n_ctx % tm == 0 and s_len % tm == 0 and batch + 1 <= 8

    cc = jnp.zeros((8, d), F32).at[:batch].set(c).at[batch].set(c_ctx)
    mod_all = _modulation(cc, mod_w, mod_b)
    mods = [mod_all[l, :batch + 1].reshape(batch + 1, 6, d) for l in range(depth)]
    cos_t, sin_t = _rope_tables(s_len, n_ctx)
    rw = [jnp.zeros((d, LANES), F32).at[:, :n_experts].set(router_w[l]).astype(_MXU) for l in range(depth)]
    rb = [jnp.full((1, LANES), NEG_INF, F32).at[0, :n_experts].set(router_b[l]) for l in range(depth)]

    stream = jnp.concatenate([ctx, x], axis=1).reshape(batch * t_len, d)

    na_w = NA_HEADS * HEAD_DIM
    df_w = DIFF_HEADS * 2 * HEAD_DIM
    wab = ab_w_in[0]
    w0 = jnp.concatenate([wab[:, :3 * na_w], _rope_layout(wab[:, 3 * na_w:3 * na_w + df_w], 2 * DIFF_HEADS),
                          _rope_layout(wab[:, 3 * na_w + df_w:3 * na_w + 2 * df_w], 2 * DIFF_HEADS),
                          wab[:, 3 * na_w + 2 * df_w:]], axis=1).astype(_MXU)
    plan0 = ((0, na_w, None, False, ATTN_SCALE), (na_w, 2 * na_w, None, False, 1.0),
             (3 * na_w, df_w, None, True, ATTN_SCALE), (3 * na_w + df_w, df_w, None, True, 1.0),
             (3 * na_w + 2 * df_w, df_w, None, False, 1.0))
    nw0 = jnp.zeros((8, LANES), F32)
    qkv0 = _in_proj(stream, mods[0], w0, cos_t, sin_t, nw0, plan0, batch=batch, n_ctx=n_ctx)
    qkv0 = qkv0.reshape(batch, t_len, -1)
    cb = lambda width: width // LANES
    lam_init0 = 0.8 - 0.6 * math.exp(-0.3 * 0)
    par = jnp.zeros((8, LANES), F32)
    par = par.at[0, :HEAD_DIM].set(diff_lq1[0]).at[1, :HEAD_DIM].set(diff_lk1[0])
    par = par.at[2, :HEAD_DIM].set(diff_lq2[0]).at[3, :HEAD_DIM].set(diff_lk2[0]).at[4].set(diff_subln[0])
    bias = _na_bias_tables(na_rpb[0], s_len // GRID_W)
    y_na = _neighbourhood(qkv0, bias, batch=batch, s_len=s_len, n_ctx=n_ctx,
                          q_col=0, k_col=cb(na_w), v_col=cb(2 * na_w))
    y_na_c = _flash(qkv0, batch=batch, n_groups=NA_HEADS // 2, n_pairs=1, q_col=0, k_col=cb(na_w),
                    v_col=cb(2 * na_w), group_stride=1, q_rows=n_ctx, q_off=0, kv_rows=n_ctx, mode="pair",
                    name="neighbourhood_attn_ctx")
    dq, dk, dv = cb(3 * na_w), cb(3 * na_w + df_w), cb(3 * na_w + 2 * df_w)
    y_df = _flash(qkv0, batch=batch, n_groups=DIFF_HEADS, n_pairs=1, q_col=dq, k_col=dk, v_col=dv, group_stride=1,
                  q_rows=s_len, q_off=n_ctx, kv_rows=t_len, mode="diff", par=par, lam_init=lam_init0,
                  name="diff_attn")
    y_df_c = _flash(qkv0, batch=batch, n_groups=DIFF_HEADS, n_pairs=1, q_col=dq, k_col=dk, v_col=dv,
                    group_stride=1, q_rows=n_ctx, q_off=0, kv_rows=n_ctx, mode="diff", par=par,
                    lam_init=lam_init0, name="diff_attn_ctx")
    ya = jnp.concatenate([y_na_c, y_na], axis=1).reshape(batch * t_len, -1)
    yb = jnp.concatenate([y_df_c, y_df], axis=1).reshape(batch * t_len, -1)
    x1, tok, route = _post(stream, ya, yb, ab_w_out[0].astype(_MXU), mods[0], ln1_g[0][None], ln1_b[0][None],
                           rw[0], rb[0], batch=batch, nbt_in=nbt, off=0, nblk=nbt, n_ctx_blk=n_ctx // tm, alpha=alpha)
    stream = _moe_and_ln2(x1, tok, route, mods[0], n_experts, moe_w_in[0], moe_b_in[0], moe_w_out[0], moe_b_out[0],
                          ln2_g[0][None], ln2_b[0][None], batch=batch, nblk=nbt, n_ctx_blk=n_ctx // tm, alpha=alpha)

    hw = GQA_HEADS * HEAD_DIM
    kw = GQA_KV_HEADS * HEAD_DIM
    wcd = cd_w_in[0]
    gq, gk, gv, wq, wk, wv = jnp.split(wcd, [hw, hw + kw, hw + 2 * kw, 2 * hw + 2 * kw, 2 * hw + 3 * kw], axis=1)
    w1 = jnp.concatenate([_pair_layout(_rope_layout(gq, GQA_HEADS), GQA_HEADS),
                          _pair_layout(_rope_layout(wq, SWA_HEADS), SWA_HEADS),
                          _rope_layout(gk, GQA_KV_HEADS), gv, _rope_layout(wk, SWA_KV_HEADS), wv],
                         axis=1).astype(_MXU)
    plan1 = ((0, hw, 0, True, ATTN_SCALE), (hw, hw, None, True, ATTN_SCALE), (2 * hw, kw, 1, True, 1.0),
             (2 * hw + kw, kw, None, False, 1.0), (2 * hw + 2 * kw, kw, None, True, 1.0),
             (2 * hw + 3 * kw, kw, None, False, 1.0))
    nw1 = jnp.zeros((8, LANES), F32).at[0].set(jnp.tile(_rope_layout(gqa_q_norm[0], 1), 2))
    nw1 = nw1.at[1].set(jnp.tile(_rope_layout(gqa_k_norm[0], 1), 2))
    qkv1 = _in_proj(stream, mods[1], w1, cos_t, sin_t, nw1, plan1, batch=batch, n_ctx=n_ctx)
    qkv1 = qkv1.reshape(batch, t_len, -1)
    kcol = cb(2 * hw)
    y_c = _flash(qkv1, batch=batch, n_groups=1, n_pairs=GQA_HEADS // 2, q_col=0, k_col=kcol, v_col=kcol + 1,
                 group_stride=0, q_rows=s_len, q_off=n_ctx, kv_rows=t_len, mode="pair", name="gqa_attn")
    sink = swa_sink[0].reshape(2, SWA_HEADS // 2).T.reshape(SWA_HEADS)
    init = jnp.broadcast_to(sink[:, None], (SWA_HEADS, LANES)).astype(F32)
    y_d = _windowed(qkv1, init, batch=batch, s_len=s_len, n_ctx=n_ctx, q_col=cb(hw), k_col=kcol + 2,
                    v_col=kcol + 3)
    wo1 = jnp.concatenate([_pair_layout(cd_w_out[0][:hw], GQA_HEADS, axis=0),
                           _pair_layout(cd_w_out[0][hw:], SWA_HEADS, axis=0)], axis=0).astype(_MXU)
    x1, tok, route = _post(stream, y_c.reshape(batch * s_len, -1), y_d.reshape(batch * s_len, -1), wo1, mods[1],
                           ln1_g[1][None], ln1_b[1][None], rw[1], rb[1], batch=batch, nbt_in=nbt,
                           off=n_ctx // tm, nblk=nbs, n_ctx_blk=0, alpha=alpha)
    out = _moe_and_ln2(x1, tok, route, mods[1], n_experts, moe_w_in[1], moe_b_in[1], moe_w_out[1], moe_b_out[1],
                       ln2_g[1][None], ln2_b[1][None], batch=batch, nblk=nbs, n_ctx_blk=0, alpha=alpha)
    return out.reshape(batch, s_len, d)
```

```python
import functools
import math

import numpy as np
import jax
import jax.numpy as jnp
from jax import lax
from jax.experimental import pallas as pl
from jax.experimental.pallas import tpu as pltpu

F32 = jnp.float32
_MXU = jnp.bfloat16

HEAD_DIM = 64
GRID_W = 64
LOG2E = math.log2(math.e)
QSCALE = HEAD_DIM ** -0.5 * LOG2E
ROPE_THETA = 10000.0
NA_HEADS = 8
NA_KH = 8
NA_KW = 16
DIFF_HEADS = 4
GQA_HEADS = 8
GQA_KV_HEADS = 2
SWA_HEADS = 8
SWA_KV_HEADS = 2
SWA_WINDOW = 128
TOP_K = 4
SWIGLU_LIMIT = 7.0
SWIGLU_ALPHA = 1.702
LN_EPS = 1e-5
RMS_EPS = 1e-6
NEG_INF = -1e30

LANES = 128
ROW_TILE = 256
MOE_ROWS = 256
NA_Q_ROWS = 4
VMEM_LIMIT = 52 * 1024 * 1024


def _cparams(sem, vmem=VMEM_LIMIT):
    return pltpu.CompilerParams(dimension_semantics=sem, vmem_limit_bytes=vmem)


def _mod_kernel(c_ref, w_ref, b_ref, o_ref):
    c = c_ref[...]
    a = (c / (1.0 + jnp.exp(-c))).astype(_MXU)
    o_ref[...] = jnp.dot(a, w_ref[...].astype(_MXU), preferred_element_type=F32) + b_ref[...]


def _modulation(cc, mod_w, mod_b):
    depth, d, d6 = mod_w.shape
    tn = d6 // 4
    return pl.pallas_call(
        _mod_kernel,
        grid=(depth, d6 // tn),
        in_specs=[pl.BlockSpec((8, d), lambda l, j: (0, 0)),
                  pl.BlockSpec((None, d, tn), lambda l, j: (l, 0, j)),
                  pl.BlockSpec((None, 1, tn), lambda l, j: (l, 0, j))],
        out_specs=pl.BlockSpec((None, 8, tn), lambda l, j: (l, 0, j)),
        out_shape=jax.ShapeDtypeStruct((depth, 8, d6), F32),
        compiler_params=_cparams(("arbitrary", "arbitrary")),
        name="modulation",
    )(cc, mod_w, mod_b.reshape(depth, 1, d6))


def _in_kernel(x_ref, mod_ref, w_ref, cos_ref, sin_ref, nw_ref, *o_refs, plan):
    x = x_ref[...]
    h = (x * (1.0 + mod_ref[1:2, :]) + mod_ref[0:1, :]).astype(_MXU)
    tm = x.shape[0]
    lane = lax.broadcasted_iota(jnp.int32, (tm, LANES), 1)
    first_half = (lane & (HEAD_DIM // 2)) == 0
    gi = lax.broadcasted_iota(jnp.int32, (LANES, LANES), 0) // HEAD_DIM
    gj = lax.broadcasted_iota(jnp.int32, (LANES, LANES), 1) // HEAD_DIM
    seg = jnp.where(gi == gj, 1.0, 0.0).astype(_MXU)
    cos = cos_ref[...]
    sin = sin_ref[...]
    for o_ref, (c0, width, norm_row, rope, scale) in zip(o_refs, plan):
        acc = jnp.dot(h, w_ref[:, c0:c0 + width], preferred_element_type=F32)
        for j in range(width // LANES):
            a = acc[:, j * LANES:(j + 1) * LANES]
            if norm_row is not None:
                a2 = a * a
                hi = a2.astype(_MXU)
                lo = (a2 - hi.astype(F32)).astype(_MXU)
                ss = (jnp.dot(hi, seg, preferred_element_type=F32)
                      + jnp.dot(lo, seg, preferred_element_type=F32))
                a = a * lax.rsqrt(ss * (1.0 / HEAD_DIM) + RMS_EPS) * nw_ref[norm_row:norm_row + 1, :]
            if rope:
                partner = jnp.where(first_half, pltpu.roll(a, LANES - HEAD_DIM // 2, 1),
                                    pltpu.roll(a, HEAD_DIM // 2, 1))
                a = a * cos + partner * sin
            if scale != 1.0:
                a = a * scale
            o_ref[:, j * LANES:(j + 1) * LANES] = a.astype(o_ref.dtype)


def _in_proj(x2d, mods, w, cos_t, sin_t, nw, plan, *, batch, n_ctx):
    rows, d = x2d.shape
    ncols = w.shape[1]
    tm = ROW_TILE
    nbt = rows // batch // tm
    n_lat_blk = nbt - n_ctx // tm

    def mod_map(i):
        return (jnp.where(i % nbt >= n_lat_blk, batch, i // nbt), 0, 0)

    return pl.pallas_call(
        functools.partial(_in_kernel, plan=plan),
        grid=(rows // tm,),
        in_specs=[pl.BlockSpec((tm, d), lambda i: (i, 0)),
                  pl.BlockSpec((None, 6, d), mod_map),
                  pl.BlockSpec((d, ncols), lambda i: (0, 0)),
                  pl.BlockSpec((tm, LANES), lambda i: (i % nbt, 0)),
                  pl.BlockSpec((tm, LANES), lambda i: (i % nbt, 0)),
                  pl.BlockSpec((8, LANES), lambda i: (0, 0))],
        out_specs=[pl.BlockSpec((tm, p[1]), lambda i: (i, 0)) for p in plan],
        out_shape=[jax.ShapeDtypeStruct((rows, p[1]), _MXU) for p in plan],
        compiler_params=_cparams(("parallel",)),
        name="in_proj",
    )(x2d, mods, w, cos_t, sin_t, nw)


def _head_masks(dtype):
    lane = lax.broadcasted_iota(jnp.int32, (1, LANES), 1)
    lo = jnp.where(lane < HEAD_DIM, 1.0, 0.0).astype(dtype)
    return lo, (1.0 - lo).astype(dtype)


def _flash_kernel(*refs, mode, tk, n_chunks, lam_init):
    if mode == "diff":
        q_ref, k_ref, v_ref, par_ref, o_ref, qt_scr, vxt_scr, m_scr, acc_scr, st_scr = refs
    else:
        q_ref, k_ref, v_ref, o_ref, qt_scr, vxt_scr, m_scr, acc_scr, st_scr = refs
    sub = lax.broadcasted_iota(jnp.int32, (LANES, 1), 0)
    lo_col = jnp.where(sub < HEAD_DIM, 1.0, 0.0)

    @pl.when(pl.program_id(2) == 0)
    def _():
        for c in range(n_chunks):
            vt = v_ref[c * tk:(c + 1) * tk, :].astype(F32).T
            if mode == "diff":
                vxt_scr[c, :LANES, :] = vt.astype(vxt_scr.dtype)
                vxt_scr[c, LANES:, :] = jnp.ones((LANES, tk), vxt_scr.dtype)
            else:
                vxt_scr[0, c] = (vt * lo_col + (1.0 - lo_col)).astype(vxt_scr.dtype)
                vxt_scr[1, c] = (vt * (1.0 - lo_col) + lo_col).astype(vxt_scr.dtype)

    qt = q_ref[...].astype(F32).T
    qt_scr[0] = (qt * lo_col).astype(qt_scr.dtype)
    qt_scr[1] = (qt * (1.0 - lo_col)).astype(qt_scr.dtype)
    m_scr[...] = jnp.full(m_scr.shape, NEG_INF, F32)
    acc_scr[...] = jnp.zeros(acc_scr.shape, F32)

    def qk(c, slot):
        off = pl.multiple_of(c * tk, tk)
        k = k_ref[pl.ds(off, tk), :]
        for h in range(2):
            st_scr[slot, h] = jnp.dot(k, qt_scr[h], preferred_element_type=F32)

    def softmax_pv(c, slot):
        for h in range(2):
            st = st_scr[slot, h]
            m_prev = m_scr[h]
            m_new = jnp.maximum(m_prev, jnp.max(st, axis=0, keepdims=True))
            alpha = jnp.exp2(m_prev - m_new)
            pt = jnp.exp2(st - m_new).astype(vxt_scr.dtype)
            vxt = vxt_scr[c] if mode == "diff" else vxt_scr[h, c]
            acc_scr[h] = alpha * acc_scr[h] + jnp.dot(vxt, pt, preferred_element_type=F32)
            m_scr[h] = m_new

    qk(0, 0)

    def body(j, carry):
        c = 2 * j
        qk(c + 1, 1)
        softmax_pv(c, 0)
        qk(c + 2, 0)
        softmax_pv(c + 1, 1)
        return carry

    lax.fori_loop(0, (n_chunks - 1) // 2, body, 0)
    if n_chunks % 2 == 1:
        softmax_pv(n_chunks - 1, 0)
    else:
        qk(n_chunks - 1, 1)
        softmax_pv(n_chunks - 2, 0)
        softmax_pv(n_chunks - 1, 1)

    a_lo = acc_scr[0]
    a_hi = acc_scr[1]
    if mode == "diff":
        lam = (jnp.exp(jnp.sum(par_ref[0:1, :] * par_ref[1:2, :], axis=1, keepdims=True))
               - jnp.exp(jnp.sum(par_ref[2:3, :] * par_ref[3:4, :], axis=1, keepdims=True))
               + lam_init)
        out_t = a_lo[:LANES] / a_lo[LANES:] - lam * (a_hi[:LANES] / a_hi[LANES:])
        ms = jnp.mean(out_t * out_t, axis=0, keepdims=True)
        out = (out_t * lax.rsqrt(ms + RMS_EPS)).T * par_ref[4:5, :] * (1.0 - lam_init)
    else:
        out = jnp.concatenate([a_lo[:HEAD_DIM] / a_lo[HEAD_DIM:], a_hi[HEAD_DIM:] / a_hi[:HEAD_DIM]], axis=0).T
    o_ref[...] = out.astype(o_ref.dtype)


def _pick_tile(n, candidates):
    for c in candidates:
        if n % c == 0:
            return c
    raise ValueError(f"no tile for {n}")


def _flash(q_arr, k_arr, v_arr, *, n_groups, q_rows, q_off, kv_rows, kv_off, k_col, v_col, mode, par=None,
           lam_init=0.0, name):
    batch = q_arr.shape[0]
    tq = _pick_tile(q_rows, (512, 256))
    tk = _pick_tile(kv_rows, (768, 512, 384, 256, 128))
    nq, n_chunks = q_rows // tq, kv_rows // tk
    assert q_off % tq == 0 and kv_off % kv_rows == 0
    vw = 2 * LANES if mode == "diff" else LANES
    in_specs = [pl.BlockSpec((None, tq, LANES), lambda b, g, i: (b, q_off // tq + i, g)),
                pl.BlockSpec((None, kv_rows, LANES), lambda b, g, i: (b, kv_off // kv_rows, k_col + g)),
                pl.BlockSpec((None, kv_rows, LANES), lambda b, g, i: (b, kv_off // kv_rows, v_col + g))]
    args = [q_arr, k_arr, v_arr]
    if mode == "diff":
        in_specs.append(pl.BlockSpec(par.shape, lambda b, g, i: (0, 0)))
        args.append(par)
    return pl.pallas_call(
        functools.partial(_flash_kernel, mode=mode, tk=tk, n_chunks=n_chunks, lam_init=lam_init),
        grid=(batch, n_groups, nq),
        in_specs=in_specs,
        out_specs=pl.BlockSpec((None, tq, LANES), lambda b, g, i: (b, i, g)),
        out_shape=jax.ShapeDtypeStruct((batch, q_rows, n_groups * LANES), _MXU),
        scratch_shapes=[pltpu.VMEM((2, LANES, tq), _MXU),
                        pltpu.VMEM((n_chunks, vw, tk) if mode == "diff" else (2, n_chunks, vw, tk), _MXU),
                        pltpu.VMEM((2, 1, tq), F32),
                        pltpu.VMEM((2, vw, tq), F32),
                        pltpu.VMEM((2, 2, tk, tq), F32)],
        compiler_params=_cparams(("parallel", "parallel", "arbitrary")),
        name=name,
    )(*args)


def _seg_attn_kernel(*refs, n_pairs, nseg, bias_segs, band, q_axis, has_init, s_len):
    it = iter(refs)
    q_ref = next(it)
    k_refs = [next(it) for _ in range(nseg)]
    v_refs = [next(it) for _ in range(nseg)]
    bias_ref = next(it) if bias_segs else None
    init_ref = next(it) if has_init else None
    o_ref = next(it)

    tq = q_ref.shape[0]
    is_lo = lax.broadcasted_iota(jnp.int32, (tq, LANES), 1) < HEAD_DIM
    lo, hi = _head_masks(q_ref.dtype)

    masks = [None] * nseg
    if band:
        qi = pl.program_id(q_axis)
        qpos = qi * tq + lax.broadcasted_iota(jnp.int32, (tq, tq), 0)
        for j in range(1, nseg):
            kpos = (qi + j - 2) * tq + lax.broadcasted_iota(jnp.int32, (tq, tq), 1)
            dist = jnp.abs(qpos - kpos)
            ok = jnp.where(kpos >= 0, jnp.where(kpos < s_len, dist, SWA_WINDOW + 1), SWA_WINDOW + 1)
            masks[j] = ok <= SWA_WINDOW

    def one_head(hh):
        p_idx, half = hh // 2, hh % 2
        qm = q_ref[:, p_idx * LANES:(p_idx + 1) * LANES] * (lo if half == 0 else hi)
        ss = []
        boff = 0
        for j in range(nseg):
            s = lax.dot_general(qm, k_refs[j][...], (((1,), (1,)), ((), ())), preferred_element_type=F32)
            if j in bias_segs:
                n = k_refs[j].shape[0]
                s = s + bias_ref[half, :, boff:boff + n]
                boff += n
            if masks[j] is not None:
                s = jnp.where(masks[j], s, NEG_INF)
            ss.append(s)
        m = functools.reduce(jnp.maximum, [jnp.max(s, axis=1, keepdims=True) for s in ss])
        if has_init:
            sink = init_ref[hh:hh + 1, 0:1]
            m = jnp.maximum(m, sink)
            l = jnp.exp2(sink - m)
        else:
            l = jnp.zeros((tq, 1), F32)
        acc = jnp.zeros((tq, LANES), F32)
        for j in range(nseg):
            p = jnp.exp2(ss[j] - m)
            l = l + jnp.sum(p, axis=1, keepdims=True)
            acc = acc + jnp.dot(p.astype(_MXU), v_refs[j][...], preferred_element_type=F32)
        return acc / l

    for p_idx in range(n_pairs):
        out = jnp.where(is_lo, one_head(2 * p_idx), one_head(2 * p_idx + 1))
        o_ref[:, p_idx * LANES:(p_idx + 1) * LANES] = out.astype(o_ref.dtype)


def _seg_attention(q_arr, k_arr, v_arr, *, grid, q_spec, k_specs, v_specs, out_spec, out_shape, n_pairs,
                   bias=None, bias_spec=None, bias_segs=(), band=False, q_axis=0, init=None, s_len=0, sem, name):
    nseg = len(k_specs)
    args = [q_arr] + [k_arr] * nseg + [v_arr] * nseg
    specs = [q_spec] + list(k_specs) + list(v_specs)
    if bias is not None:
        args.append(bias)
        specs.append(bias_spec)
    if init is not None:
        args.append(init)
        specs.append(pl.BlockSpec(init.shape, lambda *_: (0, 0)))
    kern = functools.partial(_seg_attn_kernel, n_pairs=n_pairs, nseg=nseg, bias_segs=tuple(bias_segs), band=band,
                             q_axis=q_axis, has_init=init is not None, s_len=s_len)
    return pl.pallas_call(kern, grid=grid, in_specs=specs, out_specs=out_spec, out_shape=out_shape,
                          compiler_params=_cparams(sem), name=name)(*args)


def _na_bias_tables(rpb, rows):
    nh = rpb.shape[0]
    nkr = 3 * NA_Q_ROWS
    qc = np.arange(GRID_W)[:, None]
    kc = np.arange(GRID_W)[None, :]
    ws = np.clip(qc - NA_KW // 2, 0, GRID_W - NA_KW)
    cvalid = ((kc >= ws) & (kc < ws + NA_KW)).reshape(-1)
    dc = (kc - qc + NA_KW - 1).reshape(-1)
    onehot = ((np.arange(2 * NA_KW - 1)[:, None] == dc[None, :]) & cvalid[None, :]).astype(np.float32)
    tiles = jnp.einsum("hrd,dx->hrx", rpb.astype(F32) * LOG2E, jnp.asarray(onehot),
                       precision=lax.Precision.HIGHEST)
    tiles = jnp.where(cvalid[None, None, :], tiles, NEG_INF)
    qr = np.arange(NA_Q_ROWS)[:, None]
    kr = np.arange(nkr)[None, :]
    tables = []
    for variant in range(3):
        if variant == 0:
            r0, k0, nrows = 0, 0, rows
        elif variant == 1:
            r0, k0, nrows = 2 * NA_Q_ROWS, NA_Q_ROWS, 8 * NA_Q_ROWS
        else:
            r0, k0, nrows = rows - NA_Q_ROWS, rows - nkr, rows
        r = r0 + qr
        rp = k0 + kr
        rs = np.clip(r - NA_KH // 2, 0, nrows - NA_KH)
        rvalid = (rp >= rs) & (rp < rs + NA_KH)
        dr = np.clip(rp - r + NA_KH - 1, 0, 2 * NA_KH - 2)
        t = jnp.take(tiles, jnp.asarray(dr.reshape(-1).astype(np.int32)), axis=1)
        t = jnp.where(rvalid.reshape(-1)[None, :, None], t, NEG_INF)
        t = t.reshape(nh, NA_Q_ROWS, nkr, GRID_W, GRID_W).transpose(0, 1, 3, 2, 4)
        tables.append(t.reshape(nh // 2, 2, NA_Q_ROWS * GRID_W, nkr * GRID_W))
    return jnp.stack(tables)


def _neighbourhood(q_arr, kv_arr, bias, *, s_len, n_ctx):
    batch = q_arr.shape[0]
    tq = NA_Q_ROWS * GRID_W
    nq = s_len // tq
    n_pairs = NA_HEADS // 2
    cb = s_len // n_ctx
    q_spec = pl.BlockSpec((None, tq, LANES), lambda p, i, b: (b, i, p))

    def seg_specs(col):
        specs = [pl.BlockSpec((None, n_ctx, LANES), lambda p, i, b: (b, cb, col + p))]
        for j in range(3):
            specs.append(pl.BlockSpec(
                (None, tq, LANES),
                lambda p, i, b, j=j: (b, jnp.clip(i - 1, 0, nq - 3) + j, col + p)))
        return specs

    bias_spec = pl.BlockSpec((None, None, 2, tq, 3 * tq),
                             lambda p, i, b: (jnp.where(i == 0, 0, jnp.where(i == nq - 1, 2, 1)), p, 0, 0, 0))
    out_spec = pl.BlockSpec((None, tq, LANES), lambda p, i, b: (b, i, p))
    return _seg_attention(
        q_arr, kv_arr, kv_arr, grid=(n_pairs, nq, batch), q_spec=q_spec, k_specs=seg_specs(0),
        v_specs=seg_specs(n_pairs), out_spec=out_spec,
        out_shape=jax.ShapeDtypeStruct((batch, s_len, n_pairs * LANES), _MXU),
        n_pairs=1, bias=bias, bias_spec=bias_spec, bias_segs=(1, 2, 3),
        sem=("parallel", "parallel", "parallel"), name="neighbourhood_attn")


def _windowed(q_arr, k_arr, v_arr, init, *, s_len, n_ctx):
    batch = q_arr.shape[0]
    tq = SWA_WINDOW
    nb = s_len // tq
    n_pairs = SWA_HEADS // 2
    cb = s_len // n_ctx
    qw = n_pairs * LANES
    q_spec = pl.BlockSpec((None, tq, qw), lambda b, i: (b, i, 0))
    specs = [pl.BlockSpec((None, n_ctx, LANES), lambda b, i: (b, cb, 0))]
    for j in range(3):
        specs.append(pl.BlockSpec((None, tq, LANES),
                                  lambda b, i, j=j: (b, jnp.clip(i - 1 + j, 0, nb - 1), 0)))
    out_spec = pl.BlockSpec((None, tq, qw), lambda b, i: (b, i, 0))
    return _seg_attention(
        q_arr, k_arr, v_arr, grid=(batch, nb), q_spec=q_spec, k_specs=specs, v_specs=specs,
        out_spec=out_spec, out_shape=jax.ShapeDtypeStruct((batch, s_len, qw), _MXU),
        n_pairs=n_pairs, band=True, q_axis=1, init=init, s_len=s_len,
        sem=("parallel", "parallel"), name="windowed_attn")


def _layernorm(z, g, b):
    mu = jnp.mean(z, axis=1, keepdims=True)
    zc = z - mu
    var = jnp.mean(zc * zc, axis=1, keepdims=True)
    return zc * lax.rsqrt(var + LN_EPS) * g + b


def _post_kernel(x_ref, ya_ref, yb_ref, wo_ref, mod_ref, g_ref, b_ref, rw_ref, rb_ref,
                 x1_ref, tok_ref, route_ref, *, alpha):
    half = ya_ref.shape[1]
    y = (jnp.dot(ya_ref[...], wo_ref[:half, :], preferred_element_type=F32)
         + jnp.dot(yb_ref[...], wo_ref[half:, :], preferred_element_type=F32))
    x1 = _layernorm(alpha * x_ref[...] + mod_ref[2:3, :] * y, g_ref[...], b_ref[...])
    x1_ref[...] = x1
    tok = x1 * (1.0 + mod_ref[4:5, :]) + mod_ref[3:4, :]
    tok_ref[...] = tok
    logits = jnp.dot(tok.astype(_MXU), rw_ref[...], preferred_element_type=F32) + rb_ref[...]
    tm = logits.shape[0]
    lane = lax.broadcasted_iota(jnp.int32, (tm, LANES), 1).astype(F32)
    vals, idxs = [], []
    for _ in range(TOP_K):
        mx = jnp.max(logits, axis=1, keepdims=True)
        ix = jnp.min(jnp.where(logits == mx, lane, float(LANES)), axis=1, keepdims=True)
        vals.append(mx)
        idxs.append(ix)
        logits = jnp.where(lane == ix, -3.0e38, logits)
    es = [jnp.exp(v - vals[0]) for v in vals]
    den = functools.reduce(lambda a, c: a + c, es)
    route = jnp.zeros((tm, LANES), F32)
    for k in range(TOP_K):
        route = jnp.where(lane == float(k), idxs[k], route)
        route = jnp.where(lane == float(TOP_K + k), es[k] / den, route)
    route_ref[...] = route


def _post(x2d, ya, yb, wo, mods, g, b, rw, rb, *, batch, nbt_in, nblk, n_lat_blk, alpha):
    d = x2d.shape[1]
    tm = ROW_TILE
    half = ya.shape[1]

    def mod_map(bi, t):
        return (jnp.where(t >= n_lat_blk, batch, bi), 0, 0)

    rows_out = batch * nblk * tm
    o_map = lambda bi, t: (bi * nblk + t, 0)
    return pl.pallas_call(
        functools.partial(_post_kernel, alpha=alpha),
        grid=(batch, nblk),
        in_specs=[pl.BlockSpec((tm, d), lambda bi, t: (bi * nbt_in + t, 0)),
                  pl.BlockSpec((tm, half), o_map),
                  pl.BlockSpec((tm, half), o_map),
                  pl.BlockSpec((d, d), lambda bi, t: (0, 0)),
                  pl.BlockSpec((None, 6, d), mod_map),
                  pl.BlockSpec((1, d), lambda bi, t: (0, 0)),
                  pl.BlockSpec((1, d), lambda bi, t: (0, 0)),
                  pl.BlockSpec((d, LANES), lambda bi, t: (0, 0)),
                  pl.BlockSpec((1, LANES), lambda bi, t: (0, 0))],
        out_specs=[pl.BlockSpec((tm, d), o_map), pl.BlockSpec((tm, d), o_map), pl.BlockSpec((tm, LANES), o_map)],
        out_shape=[jax.ShapeDtypeStruct((rows_out, d), F32), jax.ShapeDtypeStruct((rows_out, d), F32),
                   jax.ShapeDtypeStruct((rows_out, LANES), F32)],
        compiler_params=_cparams(("parallel", "parallel")),
        name="post_attn",
    )(x2d, ya, yb, wo, mods, g, b, rw, rb)


def _moe_kernel(be_ref, nu_ref, idx_ref, idx_next_ref, tok_hbm, wi_ref, bi_ref, wo_ref, bo_ref, o_ref,
                xbuf, sem, wi_s, wo_s):
    i = pl.program_id(0)
    n_used = nu_ref[0]
    f = wo_s.shape[0]
    slot = i % 2

    def gather(rows_ref, buf_slot):
        def issue(r, carry):
            pltpu.make_async_copy(tok_hbm.at[pl.ds(rows_ref[0, r], 1), :],
                                  xbuf.at[buf_slot, pl.ds(r, 1), :], sem.at[buf_slot]).start()
            return carry
        lax.fori_loop(0, MOE_ROWS, issue, 0, unroll=8)

    @pl.when(i == 0)
    def _():
        gather(idx_ref, 0)

    @pl.when(i + 1 < n_used)
    def _():
        gather(idx_next_ref, 1 - slot)

    @pl.when(i < n_used)
    def _():
        for _ in range(MOE_ROWS):
            pltpu.make_async_copy(tok_hbm.at[pl.ds(0, 1), :], xbuf.at[slot, pl.ds(0, 1), :], sem.at[slot]).wait()

        @pl.when(jnp.logical_or(i == 0, be_ref[jnp.maximum(i - 1, 0)] != be_ref[i]))
        def _():
            wi_s[...] = wi_ref[...].astype(wi_s.dtype)
            wo_s[...] = wo_ref[...].astype(wo_s.dtype)

        x = xbuf[slot].astype(wi_s.dtype)
        hh = jnp.dot(x, wi_s[...], preferred_element_type=F32) + bi_ref[...]
        gate = jnp.minimum(hh[:, :f], SWIGLU_LIMIT)
        up = jnp.clip(hh[:, f:], -SWIGLU_LIMIT, SWIGLU_LIMIT)
        act = gate * (1.0 / (1.0 + jnp.exp(-SWIGLU_ALPHA * gate))) * (up + 1.0)
        o_ref[...] = jnp.dot(act.astype(wo_s.dtype), wo_s[...], preferred_element_type=F32) + bo_ref[...]

    @pl.when(i >= n_used)
    def _():
        o_ref[...] = jnp.zeros(o_ref.shape, o_ref.dtype)


def _experts(tok, slot_tok, block_e, n_used, w_in, b_in, w_out, b_out, layer):
    _, n_e, d, f2 = w_in.shape
    f = f2 // 2
    n_blocks = slot_tok.shape[0]

    def blk(i, be, nu):
        return jnp.minimum(i, nu[0] - 1)

    def e_map(i, be, nu):
        return (layer, be[blk(i, be, nu)], 0, 0)

    grid_spec = pltpu.PrefetchScalarGridSpec(
        num_scalar_prefetch=2,
        grid=(n_blocks,),
        in_specs=[pl.BlockSpec((None, 1, MOE_ROWS), lambda i, be, nu: (blk(i, be, nu), 0, 0),
                               memory_space=pltpu.SMEM),
                  pl.BlockSpec((None, 1, MOE_ROWS), lambda i, be, nu: (blk(i + 1, be, nu), 0, 0),
                               memory_space=pltpu.SMEM),
                  pl.BlockSpec(memory_space=pl.ANY),
                  pl.BlockSpec((None, None, d, f2), e_map),
                  pl.BlockSpec((None, None, 1, f2), e_map),
                  pl.BlockSpec((None, None, f, d), e_map),
                  pl.BlockSpec((None, None, 1, d), e_map)],
        out_specs=pl.BlockSpec((MOE_ROWS, d), lambda i, be, nu: (i, 0)),
        scratch_shapes=[pltpu.VMEM((2, MOE_ROWS, d), F32), pltpu.SemaphoreType.DMA((2,)),
                        pltpu.VMEM((d, f2), _MXU), pltpu.VMEM((f, d), _MXU)],
    )
    depth = w_in.shape[0]
    return pl.pallas_call(
        _moe_kernel, grid_spec=grid_spec,
        out_shape=jax.ShapeDtypeStruct((n_blocks * MOE_ROWS, d), F32),
        compiler_params=_cparams(("arbitrary",)),
        name="experts",
    )(block_e, n_used, slot_tok, slot_tok, tok, w_in, b_in.reshape(depth, n_e, 1, f2), w_out,
      b_out.reshape(depth, n_e, 1, d))


def _route_plan(route, n_experts):
    n = route.shape[0]
    top_idx = route[:, :TOP_K].astype(jnp.int32)
    nk = n * TOP_K
    flat_e = top_idx.reshape(-1)
    order = jnp.argsort(flat_e).astype(jnp.int32)
    e_sorted = jnp.take(flat_e, order, mode="clip")
    experts = jnp.arange(n_experts, dtype=jnp.int32)
    counts = jnp.sum(flat_e[:, None] == experts[None, :], axis=0, dtype=jnp.int32)
    padded = (counts + MOE_ROWS - 1) // MOE_ROWS * MOE_ROWS
    start = jnp.cumsum(counts) - counts
    pend = jnp.cumsum(padded)
    pstart = pend - padded
    n_blocks = -(-nk // MOE_ROWS) + n_experts
    first = jnp.arange(n_blocks, dtype=jnp.int32) * MOE_ROWS
    block_e = jnp.minimum(jnp.sum(pend[None, :] <= first[:, None], axis=1, dtype=jnp.int32), n_experts - 1)
    n_used = (pend[-1:] // MOE_ROWS).astype(jnp.int32)
    slot = jnp.arange(n_blocks * MOE_ROWS, dtype=jnp.int32)
    slot_e = jnp.repeat(block_e, MOE_ROWS)
    rank = slot - jnp.take(pstart, slot_e, mode="clip")
    src = jnp.clip(jnp.take(start, slot_e, mode="clip") + rank, 0, nk - 1)
    slot_tok = jnp.where(rank < jnp.take(counts, slot_e, mode="clip"),
                         jnp.take(order, src, mode="clip") // TOP_K, 0)
    dest = (jnp.take(pstart, e_sorted, mode="clip") + jnp.arange(nk, dtype=jnp.int32)
            - jnp.take(start, e_sorted, mode="clip"))
    pos = jnp.zeros((nk,), jnp.int32).at[order].set(dest, unique_indices=True, mode="promise_in_bounds")
    return slot_tok.reshape(n_blocks, 1, MOE_ROWS), block_e, n_used, pos.reshape(n, TOP_K)


def _ln2_kernel(x_ref, y0_ref, y1_ref, y2_ref, y3_ref, route_ref, mod_ref, g_ref, b_ref, o_ref, *, alpha):
    ys = (y0_ref, y1_ref, y2_ref, y3_ref)
    f = route_ref[:, TOP_K:TOP_K + 1] * ys[0][...]
    for k in range(1, TOP_K):
        f = f + route_ref[:, TOP_K + k:TOP_K + k + 1] * ys[k][...]
    o_ref[...] = _layernorm(alpha * x_ref[...] + mod_ref[5:6, :] * f, g_ref[...], b_ref[...])


def _ln2(x1, ys, route, mods, g, b, *, batch, nblk, n_lat_blk, alpha):
    d = x1.shape[1]
    tm = ROW_TILE
    r_map = lambda bi, t: (bi * nblk + t, 0)

    def mod_map(bi, t):
        return (jnp.where(t >= n_lat_blk, batch, bi), 0, 0)

    return pl.pallas_call(
        functools.partial(_ln2_kernel, alpha=alpha),
        grid=(batch, nblk),
        in_specs=[pl.BlockSpec((tm, d), r_map)] + [pl.BlockSpec((tm, d), r_map)] * TOP_K
                 + [pl.BlockSpec((tm, LANES), r_map), pl.BlockSpec((None, 6, d), mod_map),
                    pl.BlockSpec((1, d), lambda bi, t: (0, 0)), pl.BlockSpec((1, d), lambda bi, t: (0, 0))],
        out_specs=pl.BlockSpec((tm, d), r_map),
        out_shape=jax.ShapeDtypeStruct((batch * nblk * tm, d), F32),
        compiler_params=_cparams(("parallel", "parallel")),
        name="combine_ln2",
    )(x1, *ys, route, mods, g, b)


def _rope_layout(w, n_heads):
    lead = w.shape[:-1]
    return w.reshape(lead + (n_heads, HEAD_DIM // 2, 2)).swapaxes(-1, -2).reshape(lead + (n_heads * HEAD_DIM,))


def _pair_layout(w, n_heads, axis=-1):
    axis = axis % w.ndim
    shape = w.shape
    w = w.reshape(shape[:axis] + (2, n_heads // 2, HEAD_DIM) + shape[axis + 1:])
    return w.swapaxes(axis, axis + 1).reshape(shape)


def _rope_tables(s_len, n_ctx):
    t = np.arange(s_len)
    row = (t // GRID_W).astype(np.float32)
    col = (t % GRID_W).astype(np.float32)
    axis_dim = HEAD_DIM // 2
    freqs = jnp.asarray(ROPE_THETA, F32) ** (-jnp.arange(0, axis_dim, 2, dtype=F32) / axis_dim)
    ang = jnp.concatenate([jnp.asarray(row)[:, None] * freqs, jnp.asarray(col)[:, None] * freqs], axis=-1)
    cos, sin = jnp.cos(ang), jnp.sin(ang)
    cos_h = jnp.concatenate([cos, cos], axis=-1)
    sin_h = jnp.concatenate([-sin, sin], axis=-1)
    cos_t = jnp.concatenate([cos_h, jnp.ones((n_ctx, HEAD_DIM), F32)], axis=0)
    sin_t = jnp.concatenate([sin_h, jnp.zeros((n_ctx, HEAD_DIM), F32)], axis=0)
    return jnp.tile(cos_t, (1, 2)), jnp.tile(sin_t, (1, 2))


def _moe_and_ln2(x1, tok, route, mods, n_experts, w_in, b_in, w_out, b_out, layer, g, b, *,
                 batch, nblk, n_lat_blk, alpha):
    slot_tok, block_e, n_used, pos = _route_plan(route, n_experts)
    yb = _experts(tok, slot_tok, block_e, n_used, w_in, b_in, w_out, b_out, layer)
    ys = [jnp.take(yb, pos[:, k], axis=0, mode="clip") for k in range(TOP_K)]
    return _ln2(x1, ys, route, mods, g, b, batch=batch, nblk=nblk, n_lat_blk=n_lat_blk, alpha=alpha)


def kernel(x, c, ctx, c_ctx, mod_w, mod_b, ln1_g, ln1_b, ln2_g, ln2_b, router_w, router_b, moe_w_in, moe_b_in,
           moe_w_out, moe_b_out, ab_w_in, ab_w_out, na_rpb, diff_lq1, diff_lk1, diff_lq2, diff_lk2, diff_subln,
           cd_w_in, cd_w_out, gqa_q_norm, gqa_k_norm, swa_sink):
    batch, s_len, d = x.shape
    n_ctx = ctx.shape[1]
    t_len = n_ctx + s_len
    depth = mod_w.shape[0]
    n_experts = router_w.shape[2]
    alpha = (2.0 * depth) ** 0.25
    tm = ROW_TILE
    nbt = t_len // tm
    nbs = s_len // tm
    assert depth == 2 and n_ctx % tm == 0 and s_len % tm == 0 and batch + 1 <= 8

    cc = jnp.zeros((8, d), F32).at[:batch].set(c).at[batch].set(c_ctx)
    mod_all = _modulation(cc, mod_w, mod_b)
    mods = [mod_all[l, :batch + 1].reshape(batch + 1, 6, d) for l in range(depth)]
    cos_t, sin_t = _rope_tables(s_len, n_ctx)
    rw = [jnp.zeros((d, LANES), F32).at[:, :n_experts].set(router_w[l]).astype(_MXU) for l in range(depth)]
    rb = [jnp.full((1, LANES), NEG_INF, F32).at[0, :n_experts].set(router_b[l]) for l in range(depth)]

    stream = jnp.concatenate([x, ctx], axis=1).reshape(batch * t_len, d)
    by_batch = lambda a: a.reshape(batch, t_len, a.shape[-1])

    na_w = NA_HEADS * HEAD_DIM
    df_w = DIFF_HEADS * 2 * HEAD_DIM
    wab = ab_w_in[0]
    w0 = jnp.concatenate([wab[:, :3 * na_w], _rope_layout(wab[:, 3 * na_w:3 * na_w + df_w], 2 * DIFF_HEADS),
                          _rope_layout(wab[:, 3 * na_w + df_w:3 * na_w + 2 * df_w], 2 * DIFF_HEADS),
                          wab[:, 3 * na_w + 2 * df_w:]], axis=1).astype(_MXU)
    plan0 = ((0, na_w, None, False, QSCALE), (na_w, 2 * na_w, None, False, 1.0),
             (3 * na_w, df_w, None, True, QSCALE), (3 * na_w + df_w, df_w, None, True, 1.0),
             (3 * na_w + 2 * df_w, df_w, None, False, 1.0))
    nw0 = jnp.zeros((8, LANES), F32)
    nq, nkv, dq, dk, dv = map(by_batch, _in_proj(stream, mods[0], w0, cos_t, sin_t, nw0, plan0,
                                                 batch=batch, n_ctx=n_ctx))
    lam_init0 = 0.8 - 0.6 * math.exp(-0.3 * 0)
    par = jnp.zeros((8, LANES), F32)
    par = par.at[0, :HEAD_DIM].set(diff_lq1[0]).at[1, :HEAD_DIM].set(diff_lk1[0])
    par = par.at[2, :HEAD_DIM].set(diff_lq2[0]).at[3, :HEAD_DIM].set(diff_lk2[0]).at[4].set(diff_subln[0])
    bias = _na_bias_tables(na_rpb[0], s_len // GRID_W)
    y_na = _neighbourhood(nq, nkv, bias, s_len=s_len, n_ctx=n_ctx)
    y_na_c = _flash(nq, nkv, nkv, n_groups=NA_HEADS // 2, q_rows=n_ctx, q_off=s_len, kv_rows=n_ctx,
                    kv_off=s_len, k_col=0,
                    v_col=NA_HEADS // 2, mode="pair", name="neighbourhood_attn_ctx")
    y_df = _flash(dq, dk, dv, n_groups=DIFF_HEADS, q_rows=s_len, q_off=0, kv_rows=t_len, kv_off=0, k_col=0, v_col=0,
                  mode="diff", par=par, lam_init=lam_init0, name="diff_attn")
    y_df_c = _flash(dq, dk, dv, n_groups=DIFF_HEADS, q_rows=n_ctx, q_off=s_len, kv_rows=n_ctx, kv_off=s_len,
                    k_col=0, v_col=0,
                    mode="diff", par=par, lam_init=lam_init0, name="diff_attn_ctx")
    ya = jnp.concatenate([y_na, y_na_c], axis=1).reshape(batch * t_len, -1)
    yb = jnp.concatenate([y_df, y_df_c], axis=1).reshape(batch * t_len, -1)
    x1, tok, route = _post(stream, ya, yb, ab_w_out[0].astype(_MXU), mods[0], ln1_g[0][None], ln1_b[0][None],
                           rw[0], rb[0], batch=batch, nbt_in=nbt, nblk=nbt, n_lat_blk=nbs, alpha=alpha)
    stream = _moe_and_ln2(x1, tok, route, mods[0], n_experts, moe_w_in, moe_b_in, moe_w_out, moe_b_out, 0,
                          ln2_g[0][None], ln2_b[0][None], batch=batch, nblk=nbt, n_lat_blk=nbs, alpha=alpha)

    hw = GQA_HEADS * HEAD_DIM
    kw = GQA_KV_HEADS * HEAD_DIM
    wcd = cd_w_in[0]
    gq, gk, gv, wq, wk, wv = jnp.split(wcd, [hw, hw + kw, hw + 2 * kw, 2 * hw + 2 * kw, 2 * hw + 3 * kw], axis=1)
    w1 = jnp.concatenate([_pair_layout(_rope_layout(gq, GQA_HEADS), GQA_HEADS),
                          _pair_layout(_rope_layout(wq, SWA_HEADS), SWA_HEADS),
                          _rope_layout(gk, GQA_KV_HEADS), gv, _rope_layout(wk, SWA_KV_HEADS), wv],
                         axis=1).astype(_MXU)
    plan1 = ((0, hw, 0, True, QSCALE), (hw, hw, None, True, QSCALE), (2 * hw, kw, 1, True, 1.0),
             (2 * hw + kw, kw, None, False, 1.0), (2 * hw + 2 * kw, kw, None, True, 1.0),
             (2 * hw + 3 * kw, kw, None, False, 1.0))
    nw1 = jnp.zeros((8, LANES), F32).at[0].set(jnp.tile(_rope_layout(gqa_q_norm[0], 1), 2))
    nw1 = nw1.at[1].set(jnp.tile(_rope_layout(gqa_k_norm[0], 1), 2))
    gq, wq, gk, gv, wk, wv = map(by_batch, _in_proj(stream, mods[1], w1, cos_t, sin_t, nw1, plan1,
                                                    batch=batch, n_ctx=n_ctx))
    n_qp = GQA_HEADS // 2
    y_c = _flash(gq.reshape(batch, t_len * n_qp, LANES), gk, gv, n_groups=1, q_rows=s_len * n_qp,
                 q_off=0, kv_rows=t_len, kv_off=0, k_col=0, v_col=0, mode="pair", name="gqa_attn")
    sink = swa_sink[0].reshape(2, SWA_HEADS // 2).T.reshape(SWA_HEADS) * LOG2E
    init = jnp.broadcast_to(sink[:, None], (SWA_HEADS, LANES)).astype(F32)
    y_d = _windowed(wq, wk, wv, init, s_len=s_len, n_ctx=n_ctx)
    wo1 = jnp.concatenate([_pair_layout(cd_w_out[0][:hw], GQA_HEADS, axis=0),
                           _pair_layout(cd_w_out[0][hw:], SWA_HEADS, axis=0)], axis=0).astype(_MXU)
    x1, tok, route = _post(stream, y_c.reshape(batch * s_len, -1), y_d.reshape(batch * s_len, -1), wo1, mods[1],
                           ln1_g[1][None], ln1_b[1][None], rw[1], rb[1], batch=batch, nbt_in=nbt,
                           nblk=nbs, n_lat_blk=nbs, alpha=alpha)
    out = _moe_and_ln2(x1, tok, route, mods[1], n_experts, moe_w_in, moe_b_in, moe_w_out, moe_b_out, 1,
                       ln2_g[1][None], ln2_b[1][None], batch=batch, nblk=nbs, n_lat_blk=nbs, alpha=alpha)
    return out.reshape(batch, s_len, d)
```

```python
import functools
import math

import numpy as np
import jax
import jax.numpy as jnp
from jax import lax
from jax.experimental import pallas as pl
from jax.experimental.pallas import tpu as pltpu

F32 = jnp.float32
_MXU = jnp.bfloat16

HEAD_DIM = 64
GRID_W = 64
LOG2E = math.log2(math.e)
QSCALE = HEAD_DIM ** -0.5 * LOG2E
ROPE_THETA = 10000.0
NA_HEADS = 8
NA_KH = 8
NA_KW = 16
DIFF_HEADS = 4
GQA_HEADS = 8
GQA_KV_HEADS = 2
SWA_HEADS = 8
SWA_KV_HEADS = 2
SWA_WINDOW = 128
TOP_K = 4
SWIGLU_LIMIT = 7.0
SWIGLU_ALPHA = 1.702
LN_EPS = 1e-5
RMS_EPS = 1e-6
NEG_INF = -1e30

LANES = 128
ROW_TILE = 256
MOE_ROWS = 256
NA_Q_ROWS = 4
VMEM_LIMIT = 52 * 1024 * 1024


def _cparams(sem, vmem=VMEM_LIMIT):
    return pltpu.CompilerParams(dimension_semantics=sem, vmem_limit_bytes=vmem)


def _mod_kernel(c_ref, w_ref, b_ref, o_ref):
    c = c_ref[...]
    a = (c / (1.0 + jnp.exp(-c))).astype(_MXU)
    o_ref[...] = jnp.dot(a, w_ref[...].astype(_MXU), preferred_element_type=F32) + b_ref[...]


def _modulation(cc, mod_w, mod_b):
    depth, d, d6 = mod_w.shape
    tn = d6 // 4
    return pl.pallas_call(
        _mod_kernel,
        grid=(depth, d6 // tn),
        in_specs=[pl.BlockSpec((8, d), lambda l, j: (0, 0)),
                  pl.BlockSpec((None, d, tn), lambda l, j: (l, 0, j)),
                  pl.BlockSpec((None, 1, tn), lambda l, j: (l, 0, j))],
        out_specs=pl.BlockSpec((None, 8, tn), lambda l, j: (l, 0, j)),
        out_shape=jax.ShapeDtypeStruct((depth, 8, d6), F32),
        compiler_params=_cparams(("arbitrary", "arbitrary")),
        name="modulation",
    )(cc, mod_w, mod_b.reshape(depth, 1, d6))


def _in_kernel(x_ref, mod_ref, w_ref, cos_ref, sin_ref, nw_ref, *o_refs, plan):
    x = x_ref[...]
    h = (x * (1.0 + mod_ref[1:2, :]) + mod_ref[0:1, :]).astype(_MXU)
    tm = x.shape[0]
    lane = lax.broadcasted_iota(jnp.int32, (tm, LANES), 1)
    first_half = (lane & (HEAD_DIM // 2)) == 0
    gi = lax.broadcasted_iota(jnp.int32, (LANES, LANES), 0) // HEAD_DIM
    gj = lax.broadcasted_iota(jnp.int32, (LANES, LANES), 1) // HEAD_DIM
    seg = jnp.where(gi == gj, 1.0, 0.0).astype(_MXU)
    cos = cos_ref[...]
    sin = sin_ref[...]
    for o_ref, (c0, width, norm_row, rope, scale) in zip(o_refs, plan):
        acc = jnp.dot(h, w_ref[:, c0:c0 + width], preferred_element_type=F32)
        for j in range(width // LANES):
            a = acc[:, j * LANES:(j + 1) * LANES]
            if norm_row is not None:
                a2 = a * a
                hi = a2.astype(_MXU)
                lo = (a2 - hi.astype(F32)).astype(_MXU)
                ss = (jnp.dot(hi, seg, preferred_element_type=F32)
                      + jnp.dot(lo, seg, preferred_element_type=F32))
                a = a * lax.rsqrt(ss * (1.0 / HEAD_DIM) + RMS_EPS) * nw_ref[norm_row:norm_row + 1, :]
            if rope:
                partner = jnp.where(first_half, pltpu.roll(a, LANES - HEAD_DIM // 2, 1),
                                    pltpu.roll(a, HEAD_DIM // 2, 1))
                a = a * cos + partner * sin
            if scale != 1.0:
                a = a * scale
            o_ref[:, j * LANES:(j + 1) * LANES] = a.astype(o_ref.dtype)


def _in_proj(x2d, mods, w, cos_t, sin_t, nw, plan, *, batch, n_ctx):
    rows, d = x2d.shape
    ncols = w.shape[1]
    tm = ROW_TILE
    nbt = rows // batch // tm
    n_lat_blk = nbt - n_ctx // tm

    def mod_map(i):
        return (jnp.where(i % nbt >= n_lat_blk, batch, i // nbt), 0, 0)

    return pl.pallas_call(
        functools.partial(_in_kernel, plan=plan),
        grid=(rows // tm,),
        in_specs=[pl.BlockSpec((tm, d), lambda i: (i, 0)),
                  pl.BlockSpec((None, 6, d), mod_map),
                  pl.BlockSpec((d, ncols), lambda i: (0, 0)),
                  pl.BlockSpec((tm, LANES), lambda i: (i % nbt, 0)),
                  pl.BlockSpec((tm, LANES), lambda i: (i % nbt, 0)),
                  pl.BlockSpec((8, LANES), lambda i: (0, 0))],
        out_specs=[pl.BlockSpec((tm, p[1]), lambda i: (i, 0)) for p in plan],
        out_shape=[jax.ShapeDtypeStruct((rows, p[1]), _MXU) for p in plan],
        compiler_params=_cparams(("parallel",)),
        name="in_proj",
    )(x2d, mods, w, cos_t, sin_t, nw)


def _head_masks(dtype):
    lane = lax.broadcasted_iota(jnp.int32, (1, LANES), 1)
    lo = jnp.where(lane < HEAD_DIM, 1.0, 0.0).astype(dtype)
    return lo, (1.0 - lo).astype(dtype)


def _flash_kernel(*refs, mode, tk, n_chunks, lam_init):
    if mode == "diff":
        q_ref, k_ref, v_ref, par_ref, o_ref, qt_scr, vxt_scr, m_scr, acc_scr, st_scr = refs
    else:
        q_ref, k_ref, v_ref, o_ref, qt_scr, vxt_scr, m_scr, acc_scr, st_scr = refs
    sub = lax.broadcasted_iota(jnp.int32, (LANES, 1), 0)
    lo_col = jnp.where(sub < HEAD_DIM, 1.0, 0.0)

    @pl.when(pl.program_id(2) == 0)
    def _():
        for c in range(n_chunks):
            vt = v_ref[c * tk:(c + 1) * tk, :].astype(F32).T
            if mode == "diff":
                vxt_scr[c, :LANES, :] = vt.astype(vxt_scr.dtype)
                vxt_scr[c, LANES:, :] = jnp.ones((LANES, tk), vxt_scr.dtype)
            else:
                vxt_scr[0, c] = (vt * lo_col + (1.0 - lo_col)).astype(vxt_scr.dtype)
                vxt_scr[1, c] = (vt * (1.0 - lo_col) + lo_col).astype(vxt_scr.dtype)

    qt = q_ref[...].astype(F32).T
    qt_scr[0] = (qt * lo_col).astype(qt_scr.dtype)
    qt_scr[1] = (qt * (1.0 - lo_col)).astype(qt_scr.dtype)
    m_scr[...] = jnp.full(m_scr.shape, NEG_INF, F32)
    acc_scr[...] = jnp.zeros(acc_scr.shape, F32)

    def qk(c, slot):
        off = pl.multiple_of(c * tk, tk)
        k = k_ref[pl.ds(off, tk), :]
        for h in range(2):
            st_scr[slot, h] = jnp.dot(k, qt_scr[h], preferred_element_type=F32)

    def softmax_pv(c, slot):
        for h in range(2):
            st = st_scr[slot, h]
            m_prev = m_scr[h]
            m_new = jnp.maximum(m_prev, jnp.max(st, axis=0, keepdims=True))
            alpha = jnp.exp2(m_prev - m_new)
            pt = jnp.exp2(st - m_new).astype(vxt_scr.dtype)
            vxt = vxt_scr[c] if mode == "diff" else vxt_scr[h, c]
            acc_scr[h] = alpha * acc_scr[h] + jnp.dot(vxt, pt, preferred_element_type=F32)
            m_scr[h] = m_new

    qk(0, 0)

    def body(j, carry):
        c = 2 * j
        qk(c + 1, 1)
        softmax_pv(c, 0)
        qk(c + 2, 0)
        softmax_pv(c + 1, 1)
        return carry

    lax.fori_loop(0, (n_chunks - 1) // 2, body, 0)
    if n_chunks % 2 == 1:
        softmax_pv(n_chunks - 1, 0)
    else:
        qk(n_chunks - 1, 1)
        softmax_pv(n_chunks - 2, 0)
        softmax_pv(n_chunks - 1, 1)

    a_lo = acc_scr[0]
    a_hi = acc_scr[1]
    if mode == "diff":
        lam = (jnp.exp(jnp.sum(par_ref[0:1, :] * par_ref[1:2, :], axis=1, keepdims=True))
               - jnp.exp(jnp.sum(par_ref[2:3, :] * par_ref[3:4, :], axis=1, keepdims=True))
               + lam_init)
        out_t = a_lo[:LANES] / a_lo[LANES:] - lam * (a_hi[:LANES] / a_hi[LANES:])
        ms = jnp.mean(out_t * out_t, axis=0, keepdims=True)
        out = (out_t * lax.rsqrt(ms + RMS_EPS)).T * par_ref[4:5, :] * (1.0 - lam_init)
    else:
        out = jnp.concatenate([a_lo[:HEAD_DIM] / a_lo[HEAD_DIM:], a_hi[HEAD_DIM:] / a_hi[:HEAD_DIM]], axis=0).T
    o_ref[...] = out.astype(o_ref.dtype)


def _pick_tile(n, candidates):
    for c in candidates:
        if n % c == 0:
            return c
    raise ValueError(f"no tile for {n}")


def _flash(q_arr, k_arr, v_arr, *, n_groups, q_rows, q_off, kv_rows, kv_off, k_col, v_col, mode, par=None,
           lam_init=0.0, name):
    batch = q_arr.shape[0]
    tq = _pick_tile(q_rows, (512, 256))
    tk = _pick_tile(kv_rows, (768, 512, 384, 256, 128))
    nq, n_chunks = q_rows // tq, kv_rows // tk
    assert q_off % tq == 0 and kv_off % kv_rows == 0
    vw = 2 * LANES if mode == "diff" else LANES
    in_specs = [pl.BlockSpec((None, tq, LANES), lambda b, g, i: (b, q_off // tq + i, g)),
                pl.BlockSpec((None, kv_rows, LANES), lambda b, g, i: (b, kv_off // kv_rows, k_col + g)),
                pl.BlockSpec((None, kv_rows, LANES), lambda b, g, i: (b, kv_off // kv_rows, v_col + g))]
    args = [q_arr, k_arr, v_arr]
    if mode == "diff":
        in_specs.append(pl.BlockSpec(par.shape, lambda b, g, i: (0, 0)))
        args.append(par)
    return pl.pallas_call(
        functools.partial(_flash_kernel, mode=mode, tk=tk, n_chunks=n_chunks, lam_init=lam_init),
        grid=(batch, n_groups, nq),
        in_specs=in_specs,
        out_specs=pl.BlockSpec((None, tq, LANES), lambda b, g, i: (b, i, g)),
        out_shape=jax.ShapeDtypeStruct((batch, q_rows, n_groups * LANES), _MXU),
        scratch_shapes=[pltpu.VMEM((2, LANES, tq), _MXU),
                        pltpu.VMEM((n_chunks, vw, tk) if mode == "diff" else (2, n_chunks, vw, tk), _MXU),
                        pltpu.VMEM((2, 1, tq), F32),
                        pltpu.VMEM((2, vw, tq), F32),
                        pltpu.VMEM((2, 2, tk, tq), F32)],
        compiler_params=_cparams(("parallel", "parallel", "arbitrary")),
        name=name,
    )(*args)


def _seg_attn_kernel(*refs, n_pairs, nseg, bias_segs, band, q_axis, has_init, s_len):
    it = iter(refs)
    q_ref = next(it)
    k_refs = [next(it) for _ in range(nseg)]
    v_refs = [next(it) for _ in range(nseg)]
    bias_ref = next(it) if bias_segs else None
    init_ref = next(it) if has_init else None
    o_ref = next(it)

    tq = q_ref.shape[0]
    is_lo = lax.broadcasted_iota(jnp.int32, (tq, LANES), 1) < HEAD_DIM
    lo, hi = _head_masks(q_ref.dtype)

    masks = [None] * nseg
    if band:
        qi = pl.program_id(q_axis)
        qpos = qi * tq + lax.broadcasted_iota(jnp.int32, (tq, tq), 0)
        for j in range(1, nseg):
            kpos = (qi + j - 2) * tq + lax.broadcasted_iota(jnp.int32, (tq, tq), 1)
            dist = jnp.abs(qpos - kpos)
            ok = jnp.where(kpos >= 0, jnp.where(kpos < s_len, dist, SWA_WINDOW + 1), SWA_WINDOW + 1)
            masks[j] = ok <= SWA_WINDOW

    def one_head(hh):
        p_idx, half = hh // 2, hh % 2
        qm = q_ref[:, p_idx * LANES:(p_idx + 1) * LANES] * (lo if half == 0 else hi)
        ss = []
        boff = 0
        for j in range(nseg):
            s = lax.dot_general(qm, k_refs[j][...], (((1,), (1,)), ((), ())), preferred_element_type=F32)
            if j in bias_segs:
                n = k_refs[j].shape[0]
                s = s + bias_ref[half, :, boff:boff + n]
                boff += n
            if masks[j] is not None:
                s = jnp.where(masks[j], s, NEG_INF)
            ss.append(s)
        m = functools.reduce(jnp.maximum, [jnp.max(s, axis=1, keepdims=True) for s in ss])
        if has_init:
            sink = init_ref[hh:hh + 1, 0:1]
            m = jnp.maximum(m, sink)
            l = jnp.exp2(sink - m)
        else:
            l = jnp.zeros((tq, 1), F32)
        acc = jnp.zeros((tq, LANES), F32)
        for j in range(nseg):
            p = jnp.exp2(ss[j] - m)
            l = l + jnp.sum(p, axis=1, keepdims=True)
            acc = acc + jnp.dot(p.astype(_MXU), v_refs[j][...], preferred_element_type=F32)
        return acc / l

    for p_idx in range(n_pairs):
        out = jnp.where(is_lo, one_head(2 * p_idx), one_head(2 * p_idx + 1))
        o_ref[:, p_idx * LANES:(p_idx + 1) * LANES] = out.astype(o_ref.dtype)


def _seg_attention(q_arr, k_arr, v_arr, *, grid, q_spec, k_specs, v_specs, out_spec, out_shape, n_pairs,
                   bias=None, bias_spec=None, bias_segs=(), band=False, q_axis=0, init=None, s_len=0, sem, name):
    nseg = len(k_specs)
    args = [q_arr] + [k_arr] * nseg + [v_arr] * nseg
    specs = [q_spec] + list(k_specs) + list(v_specs)
    if bias is not None:
        args.append(bias)
        specs.append(bias_spec)
    if init is not None:
        args.append(init)
        specs.append(pl.BlockSpec(init.shape, lambda *_: (0, 0)))
    kern = functools.partial(_seg_attn_kernel, n_pairs=n_pairs, nseg=nseg, bias_segs=tuple(bias_segs), band=band,
                             q_axis=q_axis, has_init=init is not None, s_len=s_len)
    return pl.pallas_call(kern, grid=grid, in_specs=specs, out_specs=out_spec, out_shape=out_shape,
                          compiler_params=_cparams(sem), name=name)(*args)


def _na_bias_tables(rpb, rows):
    nh = rpb.shape[0]
    nkr = 3 * NA_Q_ROWS
    qc = np.arange(GRID_W)[:, None]
    kc = np.arange(GRID_W)[None, :]
    ws = np.clip(qc - NA_KW // 2, 0, GRID_W - NA_KW)
    cvalid = ((kc >= ws) & (kc < ws + NA_KW)).reshape(-1)
    dc = (kc - qc + NA_KW - 1).reshape(-1)
    onehot = ((np.arange(2 * NA_KW - 1)[:, None] == dc[None, :]) & cvalid[None, :]).astype(np.float32)
    tiles = jnp.einsum("hrd,dx->hrx", rpb.astype(F32) * LOG2E, jnp.asarray(onehot),
                       precision=lax.Precision.HIGHEST)
    tiles = jnp.where(cvalid[None, None, :], tiles, NEG_INF)
    qr = np.arange(NA_Q_ROWS)[:, None]
    kr = np.arange(nkr)[None, :]
    tables = []
    for variant in range(3):
        if variant == 0:
            r0, k0, nrows = 0, 0, rows
        elif variant == 1:
            r0, k0, nrows = 2 * NA_Q_ROWS, NA_Q_ROWS, 8 * NA_Q_ROWS
        else:
            r0, k0, nrows = rows - NA_Q_ROWS, rows - nkr, rows
        r = r0 + qr
        rp = k0 + kr
        rs = np.clip(r - NA_KH // 2, 0, nrows - NA_KH)
        rvalid = (rp >= rs) & (rp < rs + NA_KH)
        dr = np.clip(rp - r + NA_KH - 1, 0, 2 * NA_KH - 2)
        t = jnp.take(tiles, jnp.asarray(dr.reshape(-1).astype(np.int32)), axis=1)
        t = jnp.where(rvalid.reshape(-1)[None, :, None], t, NEG_INF)
        t = t.reshape(nh, NA_Q_ROWS, nkr, GRID_W, GRID_W).transpose(0, 1, 3, 2, 4)
        tables.append(t.reshape(nh // 2, 2, NA_Q_ROWS * GRID_W, nkr * GRID_W))
    return jnp.stack(tables)


def _neighbourhood(q_arr, kv_arr, bias, *, s_len, n_ctx):
    batch = q_arr.shape[0]
    tq = NA_Q_ROWS * GRID_W
    nq = s_len // tq
    n_pairs = NA_HEADS // 2
    cb = s_len // n_ctx
    q_spec = pl.BlockSpec((None, tq, LANES), lambda p, i, b: (b, i, p))

    def seg_specs(col):
        specs = [pl.BlockSpec((None, n_ctx, LANES), lambda p, i, b: (b, cb, col + p))]
        for j in range(3):
            specs.append(pl.BlockSpec(
                (None, tq, LANES),
                lambda p, i, b, j=j: (b, jnp.clip(i - 1, 0, nq - 3) + j, col + p)))
        return specs

    bias_spec = pl.BlockSpec((None, None, 2, tq, 3 * tq),
                             lambda p, i, b: (jnp.where(i == 0, 0, jnp.where(i == nq - 1, 2, 1)), p, 0, 0, 0))
    out_spec = pl.BlockSpec((None, tq, LANES), lambda p, i, b: (b, i, p))
    return _seg_attention(
        q_arr, kv_arr, kv_arr, grid=(n_pairs, nq, batch), q_spec=q_spec, k_specs=seg_specs(0),
        v_specs=seg_specs(n_pairs), out_spec=out_spec,
        out_shape=jax.ShapeDtypeStruct((batch, s_len, n_pairs * LANES), _MXU),
        n_pairs=1, bias=bias, bias_spec=bias_spec, bias_segs=(1, 2, 3),
        sem=("parallel", "parallel", "parallel"), name="neighbourhood_attn")


def _windowed(q_arr, k_arr, v_arr, init, *, s_len, n_ctx):
    batch = q_arr.shape[0]
    tq = SWA_WINDOW
    nb = s_len // tq
    n_pairs = SWA_HEADS // 2
    cb = s_len // n_ctx
    qw = n_pairs * LANES
    q_spec = pl.BlockSpec((None, tq, qw), lambda b, i: (b, i, 0))
    specs = [pl.BlockSpec((None, n_ctx, LANES), lambda b, i: (b, cb, 0))]
    for j in range(3):
        specs.append(pl.BlockSpec((None, tq, LANES),
                                  lambda b, i, j=j: (b, jnp.clip(i - 1 + j, 0, nb - 1), 0)))
    out_spec = pl.BlockSpec((None, tq, qw), lambda b, i: (b, i, 0))
    return _seg_attention(
        q_arr, k_arr, v_arr, grid=(batch, nb), q_spec=q_spec, k_specs=specs, v_specs=specs,
        out_spec=out_spec, out_shape=jax.ShapeDtypeStruct((batch, s_len, qw), _MXU),
        n_pairs=n_pairs, band=True, q_axis=1, init=init, s_len=s_len,
        sem=("parallel", "parallel"), name="windowed_attn")


def _layernorm(z, g, b):
    mu = jnp.mean(z, axis=1, keepdims=True)
    zc = z - mu
    var = jnp.mean(zc * zc, axis=1, keepdims=True)
    return zc * lax.rsqrt(var + LN_EPS) * g + b


def _post_kernel(x_ref, ya_ref, yb_ref, wo_ref, mod_ref, g_ref, b_ref, rw_ref, rb_ref,
                 x1_ref, tok_ref, route_ref, *, alpha):
    half = ya_ref.shape[1]
    y = (jnp.dot(ya_ref[...], wo_ref[:half, :], preferred_element_type=F32)
         + jnp.dot(yb_ref[...], wo_ref[half:, :], preferred_element_type=F32))
    x1 = _layernorm(alpha * x_ref[...] + mod_ref[2:3, :] * y, g_ref[...], b_ref[...])
    x1_ref[...] = x1
    tok = x1 * (1.0 + mod_ref[4:5, :]) + mod_ref[3:4, :]
    tok_ref[...] = tok
    logits = jnp.dot(tok.astype(_MXU), rw_ref[...], preferred_element_type=F32) + rb_ref[...]
    tm = logits.shape[0]
    lane = lax.broadcasted_iota(jnp.int32, (tm, LANES), 1).astype(F32)
    vals, idxs = [], []
    for _ in range(TOP_K):
        mx = jnp.max(logits, axis=1, keepdims=True)
        ix = jnp.min(jnp.where(logits == mx, lane, float(LANES)), axis=1, keepdims=True)
        vals.append(mx)
        idxs.append(ix)
        logits = jnp.where(lane == ix, -3.0e38, logits)
    es = [jnp.exp(v - vals[0]) for v in vals]
    den = functools.reduce(lambda a, c: a + c, es)
    route = jnp.zeros((tm, LANES), F32)
    for k in range(TOP_K):
        route = jnp.where(lane == float(k), idxs[k], route)
        route = jnp.where(lane == float(TOP_K + k), es[k] / den, route)
    route_ref[...] = route


def _post(x2d, ya, yb, wo, mods, g, b, rw, rb, *, batch, nbt_in, nblk, n_lat_blk, alpha):
    d = x2d.shape[1]
    tm = ROW_TILE
    half = ya.shape[1]

    def mod_map(bi, t):
        return (jnp.where(t >= n_lat_blk, batch, bi), 0, 0)

    rows_out = batch * nblk * tm
    o_map = lambda bi, t: (bi * nblk + t, 0)
    return pl.pallas_call(
        functools.partial(_post_kernel, alpha=alpha),
        grid=(batch, nblk),
        in_specs=[pl.BlockSpec((tm, d), lambda bi, t: (bi * nbt_in + t, 0)),
                  pl.BlockSpec((tm, half), o_map),
                  pl.BlockSpec((tm, half), o_map),
                  pl.BlockSpec((d, d), lambda bi, t: (0, 0)),
                  pl.BlockSpec((None, 6, d), mod_map),
                  pl.BlockSpec((1, d), lambda bi, t: (0, 0)),
                  pl.BlockSpec((1, d), lambda bi, t: (0, 0)),
                  pl.BlockSpec((d, LANES), lambda bi, t: (0, 0)),
                  pl.BlockSpec((1, LANES), lambda bi, t: (0, 0))],
        out_specs=[pl.BlockSpec((tm, d), o_map), pl.BlockSpec((tm, d), o_map), pl.BlockSpec((tm, LANES), o_map)],
        out_shape=[jax.ShapeDtypeStruct((rows_out, d), F32), jax.ShapeDtypeStruct((rows_out, d), F32),
                   jax.ShapeDtypeStruct((rows_out, LANES), F32)],
        compiler_params=_cparams(("parallel", "parallel")),
        name="post_attn",
    )(x2d, ya, yb, wo, mods, g, b, rw, rb)


def _moe_kernel(be_ref, nu_ref, idx_ref, idx_next_ref, tok_hbm, wi_ref, bi_ref, wo_ref, bo_ref, y_hbm,
                xbuf0, xbuf1, obuf0, obuf1, gsem, ssem, wi_s, wo_s, *, n_tok):
    i = pl.program_id(0)
    n_used = nu_ref[0]
    f = wo_s.shape[0]
    xbufs = (xbuf0, xbuf1)
    obufs = (obuf0, obuf1)
    dump0 = n_tok * TOP_K
    tok_shift = TOP_K.bit_length() - 1

    def gather_copy(row, r, slot):
        return pltpu.make_async_copy(tok_hbm.at[pl.ds(row, 1), :], xbufs[slot].at[pl.ds(r, 1), :], gsem.at[slot])

    def scatter_copy(p, r, slot):
        return pltpu.make_async_copy(obufs[slot].at[pl.ds(r, 1), :], y_hbm.at[pl.ds(p, 1), :], ssem.at[slot])

    def start_gather(rows_ref, slot):
        for r in range(MOE_ROWS):
            row = jnp.minimum(rows_ref[0, r] >> tok_shift, n_tok - 1)
            gather_copy(row, r, slot).start(priority=r % 2)

    def wait_gather(slot):
        for _ in range(MOE_ROWS):
            gather_copy(0, 0, slot).wait()

    def wait_scatter(slot):
        for _ in range(MOE_ROWS):
            scatter_copy(0, 0, slot).wait()

    @pl.when(i == 0)
    def _():
        for slot in range(2):
            obufs[slot][...] = jnp.zeros(obufs[slot].shape, F32)
            for r in range(MOE_ROWS):
                scatter_copy(dump0 + slot * MOE_ROWS + r, r, slot).start(priority=r % 2)
        start_gather(idx_ref, 0)

    last_used = n_used - 1
    changed = jnp.logical_or(i == 0, be_ref[jnp.clip(i - 1, 0, last_used)] != be_ref[jnp.minimum(i, last_used)])

    for slot in range(2):
        @pl.when(jnp.logical_and(i < n_used, i % 2 == slot))
        def _(slot=slot):
            wait_scatter(slot)
            wait_gather(slot)

            @pl.when(changed)
            def _():
                wi_s[...] = wi_ref[...].astype(wi_s.dtype)
                wo_s[...] = wo_ref[...].astype(wo_s.dtype)

            x = xbufs[slot][...].astype(wi_s.dtype)
            start_gather(idx_next_ref, 1 - slot)
            hh = jnp.dot(x, wi_s[...], preferred_element_type=F32) + bi_ref[...]
            gate = jnp.minimum(hh[:, :f], SWIGLU_LIMIT)
            up = jnp.clip(hh[:, f:], -SWIGLU_LIMIT, SWIGLU_LIMIT)
            act = gate * (1.0 / (1.0 + jnp.exp(-SWIGLU_ALPHA * gate))) * (up + 1.0)
            obufs[slot][...] = jnp.dot(act.astype(wo_s.dtype), wo_s[...], preferred_element_type=F32) + bo_ref[...]
            for r in range(MOE_ROWS):
                scatter_copy(idx_ref[0, r], r, slot).start(priority=r % 2)

        @pl.when(jnp.logical_and(i >= n_used, i % 2 == slot))
        def _(slot=slot):
            @pl.when(i == n_used)
            def _():
                wait_gather(slot)

            @pl.when(i < n_used + 2)
            def _():
                wait_scatter(slot)


def _experts(tok, slot_pair, block_e, n_used, w_in, b_in, w_out, b_out, layer):
    n_tok, d = tok.shape
    _, n_e, _, f2 = w_in.shape
    f = f2 // 2
    n_blocks = slot_pair.shape[0]

    def blk(i, be, nu):
        return jnp.minimum(i, nu[0] - 1)

    def e_map(i, be, nu):
        return (layer, be[blk(i, be, nu)], 0, 0)

    row_buf = pltpu.VMEM((MOE_ROWS, d), F32)
    grid_spec = pltpu.PrefetchScalarGridSpec(
        num_scalar_prefetch=2,
        grid=(n_blocks + 2,),
        in_specs=[pl.BlockSpec((None, 1, MOE_ROWS), lambda i, be, nu: (blk(i, be, nu), 0, 0),
                               memory_space=pltpu.SMEM),
                  pl.BlockSpec((None, 1, MOE_ROWS), lambda i, be, nu: (blk(i + 1, be, nu), 0, 0),
                               memory_space=pltpu.SMEM),
                  pl.BlockSpec(memory_space=pl.ANY),
                  pl.BlockSpec((None, None, d, f2), e_map),
                  pl.BlockSpec((None, None, 1, f2), e_map),
                  pl.BlockSpec((None, None, f, d), e_map),
                  pl.BlockSpec((None, None, 1, d), e_map)],
        out_specs=pl.BlockSpec(memory_space=pl.ANY),
        scratch_shapes=[row_buf, row_buf, row_buf, row_buf, pltpu.SemaphoreType.DMA((2,)),
                        pltpu.SemaphoreType.DMA((2,)), pltpu.VMEM((d, f2), _MXU), pltpu.VMEM((f, d), _MXU)],
    )
    depth = w_in.shape[0]
    return pl.pallas_call(
        functools.partial(_moe_kernel, n_tok=n_tok), grid_spec=grid_spec,
        out_shape=jax.ShapeDtypeStruct((n_tok * TOP_K + 2 * MOE_ROWS, d), F32),
        compiler_params=_cparams(("arbitrary",)),
        name="experts",
    )(block_e, n_used, slot_pair, slot_pair, tok, w_in, b_in.reshape(depth, n_e, 1, f2), w_out,
      b_out.reshape(depth, n_e, 1, d))


def _route_plan(route, n_experts):
    n = route.shape[0]
    top_idx = route[:, :TOP_K].astype(jnp.int32)
    nk = n * TOP_K
    flat_e = top_idx.reshape(-1)
    order = jnp.argsort(flat_e).astype(jnp.int32)
    experts = jnp.arange(n_experts, dtype=jnp.int32)
    counts = jnp.sum(flat_e[:, None] == experts[None, :], axis=0, dtype=jnp.int32)
    padded = (counts + MOE_ROWS - 1) // MOE_ROWS * MOE_ROWS
    start = jnp.cumsum(counts) - counts
    pend = jnp.cumsum(padded)
    pstart = pend - padded
    n_blocks = -(-nk // MOE_ROWS) + n_experts
    first = jnp.arange(n_blocks, dtype=jnp.int32) * MOE_ROWS
    block_e = jnp.minimum(jnp.sum(pend[None, :] <= first[:, None], axis=1, dtype=jnp.int32), n_experts - 1)
    n_used = (pend[-1:] // MOE_ROWS).astype(jnp.int32)
    blk = jnp.arange(n_blocks, dtype=jnp.int32)[:, None]
    row = jnp.arange(MOE_ROWS, dtype=jnp.int32)[None, :]
    rank = blk * MOE_ROWS + row - jnp.take(pstart, block_e, mode="clip")[:, None]
    src = jnp.clip(jnp.take(start, block_e, mode="clip")[:, None] + rank, 0, nk - 1)
    slot_pair = jnp.where(rank < jnp.take(counts, block_e, mode="clip")[:, None],
                          jnp.take(order, src, mode="clip"), nk + (blk % 2) * MOE_ROWS + row)
    return slot_pair.reshape(n_blocks, 1, MOE_ROWS), block_e, n_used


def _ln2_kernel(x_ref, y_ref, route_ref, mod_ref, g_ref, b_ref, o_ref, *, alpha):
    d = x_ref.shape[1]
    f = route_ref[:, TOP_K:TOP_K + 1] * y_ref[:, :d]
    for k in range(1, TOP_K):
        f = f + route_ref[:, TOP_K + k:TOP_K + k + 1] * y_ref[:, k * d:(k + 1) * d]
    o_ref[...] = _layernorm(alpha * x_ref[...] + mod_ref[5:6, :] * f, g_ref[...], b_ref[...])


def _ln2(x1, y4, route, mods, g, b, *, batch, nblk, n_lat_blk, alpha):
    d = x1.shape[1]
    tm = ROW_TILE
    r_map = lambda bi, t: (bi * nblk + t, 0)

    def mod_map(bi, t):
        return (jnp.where(t >= n_lat_blk, batch, bi), 0, 0)

    return pl.pallas_call(
        functools.partial(_ln2_kernel, alpha=alpha),
        grid=(batch, nblk),
        in_specs=[pl.BlockSpec((tm, d), r_map), pl.BlockSpec((tm, TOP_K * d), r_map),
                  pl.BlockSpec((tm, LANES), r_map), pl.BlockSpec((None, 6, d), mod_map),
                  pl.BlockSpec((1, d), lambda bi, t: (0, 0)), pl.BlockSpec((1, d), lambda bi, t: (0, 0))],
        out_specs=pl.BlockSpec((tm, d), r_map),
        out_shape=jax.ShapeDtypeStruct((batch * nblk * tm, d), F32),
        compiler_params=_cparams(("parallel", "parallel")),
        name="combine_ln2",
    )(x1, y4, route, mods, g, b)


def _rope_layout(w, n_heads):
    lead = w.shape[:-1]
    return w.reshape(lead + (n_heads, HEAD_DIM // 2, 2)).swapaxes(-1, -2).reshape(lead + (n_heads * HEAD_DIM,))


def _pair_layout(w, n_heads, axis=-1):
    axis = axis % w.ndim
    shape = w.shape
    w = w.reshape(shape[:axis] + (2, n_heads // 2, HEAD_DIM) + shape[axis + 1:])
    return w.swapaxes(axis, axis + 1).reshape(shape)


def _rope_tables(s_len, n_ctx):
    t = np.arange(s_len)
    row = (t // GRID_W).astype(np.float32)
    col = (t % GRID_W).astype(np.float32)
    axis_dim = HEAD_DIM // 2
    freqs = jnp.asarray(ROPE_THETA, F32) ** (-jnp.arange(0, axis_dim, 2, dtype=F32) / axis_dim)
    ang = jnp.concatenate([jnp.asarray(row)[:, None] * freqs, jnp.asarray(col)[:, None] * freqs], axis=-1)
    cos, sin = jnp.cos(ang), jnp.sin(ang)
    cos_h = jnp.concatenate([cos, cos], axis=-1)
    sin_h = jnp.concatenate([-sin, sin], axis=-1)
    cos_t = jnp.concatenate([cos_h, jnp.ones((n_ctx, HEAD_DIM), F32)], axis=0)
    sin_t = jnp.concatenate([sin_h, jnp.zeros((n_ctx, HEAD_DIM), F32)], axis=0)
    return jnp.tile(cos_t, (1, 2)), jnp.tile(sin_t, (1, 2))


def _moe_and_ln2(x1, tok, route, mods, n_experts, w_in, b_in, w_out, b_out, layer, g, b, *,
                 batch, nblk, n_lat_blk, alpha):
    slot_pair, block_e, n_used = _route_plan(route, n_experts)
    y = _experts(tok, slot_pair, block_e, n_used, w_in, b_in, w_out, b_out, layer)
    y4 = y.reshape(y.shape[0] // TOP_K, TOP_K * y.shape[1])
    return _ln2(x1, y4, route, mods, g, b, batch=batch, nblk=nblk, n_lat_blk=n_lat_blk, alpha=alpha)


def kernel(x, c, ctx, c_ctx, mod_w, mod_b, ln1_g, ln1_b, ln2_g, ln2_b, router_w, router_b, moe_w_in, moe_b_in,
           moe_w_out, moe_b_out, ab_w_in, ab_w_out, na_rpb, diff_lq1, diff_lk1, diff_lq2, diff_lk2, diff_subln,
           cd_w_in, cd_w_out, gqa_q_norm, gqa_k_norm, swa_sink):
    batch, s_len, d = x.shape
    n_ctx = ctx.shape[1]
    t_len = n_ctx + s_len
    depth = mod_w.shape[0]
    n_experts = router_w.shape[2]
    alpha = (2.0 * depth) ** 0.25
    tm = ROW_TILE
    nbt = t_len // tm
    nbs = s_len // tm
    assert depth == 2 and n_ctx % tm == 0 and s_len % tm == 0 and batch + 1 <= 8

    cc = jnp.zeros((8, d), F32).at[:batch].set(c).at[batch].set(c_ctx)
    mod_all = _modulation(cc, mod_w, mod_b)
    mods = [mod_all[l, :batch + 1].reshape(batch + 1, 6, d) for l in range(depth)]
    cos_t, sin_t = _rope_tables(s_len, n_ctx)
    rw = [jnp.zeros((d, LANES), F32).at[:, :n_experts].set(router_w[l]).astype(_MXU) for l in range(depth)]
    rb = [jnp.full((1, LANES), NEG_INF, F32).at[0, :n_experts].set(router_b[l]) for l in range(depth)]

    stream = jnp.concatenate([x, ctx], axis=1).reshape(batch * t_len, d)
    by_batch = lambda a: a.reshape(batch, t_len, a.shape[-1])

    na_w = NA_HEADS * HEAD_DIM
    df_w = DIFF_HEADS * 2 * HEAD_DIM
    wab = ab_w_in[0]
    w0 = jnp.concatenate([wab[:, :3 * na_w], _rope_layout(wab[:, 3 * na_w:3 * na_w + df_w], 2 * DIFF_HEADS),
                          _rope_layout(wab[:, 3 * na_w + df_w:3 * na_w + 2 * df_w], 2 * DIFF_HEADS),
                          wab[:, 3 * na_w + 2 * df_w:]], axis=1).astype(_MXU)
    plan0 = ((0, na_w, None, False, QSCALE), (na_w, 2 * na_w, None, False, 1.0),
             (3 * na_w, df_w, None, True, QSCALE), (3 * na_w + df_w, df_w, None, True, 1.0),
             (3 * na_w + 2 * df_w, df_w, None, False, 1.0))
    nw0 = jnp.zeros((8, LANES), F32)
    nq, nkv, dq, dk, dv = map(by_batch, _in_proj(stream, mods[0], w0, cos_t, sin_t, nw0, plan0,
                                                 batch=batch, n_ctx=n_ctx))
    lam_init0 = 0.8 - 0.6 * math.exp(-0.3 * 0)
    par = jnp.zeros((8, LANES), F32)
    par = par.at[0, :HEAD_DIM].set(diff_lq1[0]).at[1, :HEAD_DIM].set(diff_lk1[0])
    par = par.at[2, :HEAD_DIM].set(diff_lq2[0]).at[3, :HEAD_DIM].set(diff_lk2[0]).at[4].set(diff_subln[0])
    bias = _na_bias_tables(na_rpb[0], s_len // GRID_W)
    y_na = _neighbourhood(nq, nkv, bias, s_len=s_len, n_ctx=n_ctx)
    y_na_c = _flash(nq, nkv, nkv, n_groups=NA_HEADS // 2, q_rows=n_ctx, q_off=s_len, kv_rows=n_ctx,
                    kv_off=s_len, k_col=0,
                    v_col=NA_HEADS // 2, mode="pair", name="neighbourhood_attn_ctx")
    y_df = _flash(dq, dk, dv, n_groups=DIFF_HEADS, q_rows=s_len, q_off=0, kv_rows=t_len, kv_off=0, k_col=0, v_col=0,
                  mode="diff", par=par, lam_init=lam_init0, name="diff_attn")
    y_df_c = _flash(dq, dk, dv, n_groups=DIFF_HEADS, q_rows=n_ctx, q_off=s_len, kv_rows=n_ctx, kv_off=s_len,
                    k_col=0, v_col=0,
                    mode="diff", par=par, lam_init=lam_init0, name="diff_attn_ctx")
    ya = jnp.concatenate([y_na, y_na_c], axis=1).reshape(batch * t_len, -1)
    yb = jnp.concatenate([y_df, y_df_c], axis=1).reshape(batch * t_len, -1)
    x1, tok, route = _post(stream, ya, yb, ab_w_out[0].astype(_MXU), mods[0], ln1_g[0][None], ln1_b[0][None],
                           rw[0], rb[0], batch=batch, nbt_in=nbt, nblk=nbt, n_lat_blk=nbs, alpha=alpha)
    stream = _moe_and_ln2(x1, tok, route, mods[0], n_experts, moe_w_in, moe_b_in, moe_w_out, moe_b_out, 0,
                          ln2_g[0][None], ln2_b[0][None], batch=batch, nblk=nbt, n_lat_blk=nbs, alpha=alpha)

    hw = GQA_HEADS * HEAD_DIM
    kw = GQA_KV_HEADS * HEAD_DIM
    wcd = cd_w_in[0]
    gq, gk, gv, wq, wk, wv = jnp.split(wcd, [hw, hw + kw, hw + 2 * kw, 2 * hw + 2 * kw, 2 * hw + 3 * kw], axis=1)
    w1 = jnp.concatenate([_pair_layout(_rope_layout(gq, GQA_HEADS), GQA_HEADS),
                          _pair_layout(_rope_layout(wq, SWA_HEADS), SWA_HEADS),
                          _rope_layout(gk, GQA_KV_HEADS), gv, _rope_layout(wk, SWA_KV_HEADS), wv],
                         axis=1).astype(_MXU)
    plan1 = ((0, hw, 0, True, QSCALE), (hw, hw, None, True, QSCALE), (2 * hw, kw, 1, True, 1.0),
             (2 * hw + kw, kw, None, False, 1.0), (2 * hw + 2 * kw, kw, None, True, 1.0),
             (2 * hw + 3 * kw, kw, None, False, 1.0))
    nw1 = jnp.zeros((8, LANES), F32).at[0].set(jnp.tile(_rope_layout(gqa_q_norm[0], 1), 2))
    nw1 = nw1.at[1].set(jnp.tile(_rope_layout(gqa_k_norm[0], 1), 2))
    gq, wq, gk, gv, wk, wv = map(by_batch, _in_proj(stream, mods[1], w1, cos_t, sin_t, nw1, plan1,
                                                    batch=batch, n_ctx=n_ctx))
    n_qp = GQA_HEADS // 2
    y_c = _flash(gq.reshape(batch, t_len * n_qp, LANES), gk, gv, n_groups=1, q_rows=s_len * n_qp,
                 q_off=0, kv_rows=t_len, kv_off=0, k_col=0, v_col=0, mode="pair", name="gqa_attn")
    sink = swa_sink[0].reshape(2, SWA_HEADS // 2).T.reshape(SWA_HEADS) * LOG2E
    init = jnp.broadcast_to(sink[:, None], (SWA_HEADS, LANES)).astype(F32)
    y_d = _windowed(wq, wk, wv, init, s_len=s_len, n_ctx=n_ctx)
    wo1 = jnp.concatenate([_pair_layout(cd_w_out[0][:hw], GQA_HEADS, axis=0),
                           _pair_layout(cd_w_out[0][hw:], SWA_HEADS, axis=0)], axis=0).astype(_MXU)
    x1, tok, route = _post(stream, y_c.reshape(batch * s_len, -1), y_d.reshape(batch * s_len, -1), wo1, mods[1],
                           ln1_g[1][None], ln1_b[1][None], rw[1], rb[1], batch=batch, nbt_in=nbt,
                           nblk=nbs, n_lat_blk=nbs, alpha=alpha)
    out = _moe_and_ln2(x1, tok, route, mods[1], n_experts, moe_w_in, moe_b_in, moe_w_out, moe_b_out, 1,
                       ln2_g[1][None], ln2_b[1][None], batch=batch, nblk=nbs, n_lat_blk=nbs, alpha=alpha)
    return out.reshape(batch, s_len, d)
```

```python
import functools
import math

import numpy as np
import jax
import jax.numpy as jnp
from jax import lax
from jax.experimental import pallas as pl
from jax.experimental.pallas import tpu as pltpu

F32 = jnp.float32
_MXU = jnp.bfloat16

HEAD_DIM = 64
GRID_W = 64
LOG2E = math.log2(math.e)
QSCALE = HEAD_DIM ** -0.5 * LOG2E
ROPE_THETA = 10000.0
NA_HEADS = 8
NA_KH = 8
NA_KW = 16
DIFF_HEADS = 4
GQA_HEADS = 8
GQA_KV_HEADS = 2
SWA_HEADS = 8
SWA_KV_HEADS = 2
SWA_WINDOW = 128
TOP_K = 4
SWIGLU_LIMIT = 7.0
SWIGLU_ALPHA = 1.702
LN_EPS = 1e-5
RMS_EPS = 1e-6
NEG_INF = -1e30

LANES = 128
ROW_TILE = 256
MOE_ROWS = 256
MOE_RING = 3
NA_Q_ROWS = 4
VMEM_LIMIT = 52 * 1024 * 1024


def _cparams(sem, vmem=VMEM_LIMIT):
    return pltpu.CompilerParams(dimension_semantics=sem, vmem_limit_bytes=vmem)


def _mod_kernel(c_ref, w_ref, b_ref, o_ref):
    c = c_ref[...]
    a = (c / (1.0 + jnp.exp(-c))).astype(_MXU)
    o_ref[...] = jnp.dot(a, w_ref[...].astype(_MXU), preferred_element_type=F32) + b_ref[...]


def _modulation(cc, mod_w, mod_b):
    depth, d, d6 = mod_w.shape
    tn = d6 // 4
    return pl.pallas_call(
        _mod_kernel,
        grid=(depth, d6 // tn),
        in_specs=[pl.BlockSpec((8, d), lambda l, j: (0, 0)),
                  pl.BlockSpec((None, d, tn), lambda l, j: (l, 0, j)),
                  pl.BlockSpec((None, 1, tn), lambda l, j: (l, 0, j))],
        out_specs=pl.BlockSpec((None, 8, tn), lambda l, j: (l, 0, j)),
        out_shape=jax.ShapeDtypeStruct((depth, 8, d6), F32),
        compiler_params=_cparams(("arbitrary", "arbitrary")),
        name="modulation",
    )(cc, mod_w, mod_b.reshape(depth, 1, d6))


def _in_kernel(x_ref, mod_ref, w_ref, cos_ref, sin_ref, nw_ref, *o_refs, plan):
    x = x_ref[...]
    h = (x * (1.0 + mod_ref[1:2, :]) + mod_ref[0:1, :]).astype(_MXU)
    tm = x.shape[0]
    lane = lax.broadcasted_iota(jnp.int32, (tm, LANES), 1)
    first_half = (lane & (HEAD_DIM // 2)) == 0
    gi = lax.broadcasted_iota(jnp.int32, (LANES, LANES), 0) // HEAD_DIM
    gj = lax.broadcasted_iota(jnp.int32, (LANES, LANES), 1) // HEAD_DIM
    seg = jnp.where(gi == gj, 1.0, 0.0).astype(_MXU)
    cos = cos_ref[...]
    sin = sin_ref[...]
    for o_ref, (c0, width, norm_row, rope, scale) in zip(o_refs, plan):
        acc = jnp.dot(h, w_ref[:, c0:c0 + width], preferred_element_type=F32)
        for j in range(width // LANES):
            a = acc[:, j * LANES:(j + 1) * LANES]
            if norm_row is not None:
                a2 = a * a
                hi = a2.astype(_MXU)
                lo = (a2 - hi.astype(F32)).astype(_MXU)
                ss = (jnp.dot(hi, seg, preferred_element_type=F32)
                      + jnp.dot(lo, seg, preferred_element_type=F32))
                a = a * lax.rsqrt(ss * (1.0 / HEAD_DIM) + RMS_EPS) * nw_ref[norm_row:norm_row + 1, :]
            if rope:
                partner = jnp.where(first_half, pltpu.roll(a, LANES - HEAD_DIM // 2, 1),
                                    pltpu.roll(a, HEAD_DIM // 2, 1))
                a = a * cos + partner * sin
            if scale != 1.0:
                a = a * scale
            o_ref[:, j * LANES:(j + 1) * LANES] = a.astype(o_ref.dtype)


def _in_proj(x2d, mods, w, cos_t, sin_t, nw, plan, *, batch, n_ctx):
    rows, d = x2d.shape
    ncols = w.shape[1]
    tm = ROW_TILE
    nbt = rows // batch // tm
    n_lat_blk = nbt - n_ctx // tm

    def mod_map(i):
        return (jnp.where(i % nbt >= n_lat_blk, batch, i // nbt), 0, 0)

    return pl.pallas_call(
        functools.partial(_in_kernel, plan=plan),
        grid=(rows // tm,),
        in_specs=[pl.BlockSpec((tm, d), lambda i: (i, 0)),
                  pl.BlockSpec((None, 6, d), mod_map),
                  pl.BlockSpec((d, ncols), lambda i: (0, 0)),
                  pl.BlockSpec((tm, LANES), lambda i: (i % nbt, 0)),
                  pl.BlockSpec((tm, LANES), lambda i: (i % nbt, 0)),
                  pl.BlockSpec((8, LANES), lambda i: (0, 0))],
        out_specs=[pl.BlockSpec((tm, p[1]), lambda i: (i, 0)) for p in plan],
        out_shape=[jax.ShapeDtypeStruct((rows, p[1]), _MXU) for p in plan],
        compiler_params=_cparams(("parallel",)),
        name="in_proj",
    )(x2d, mods, w, cos_t, sin_t, nw)


def _head_masks(dtype):
    lane = lax.broadcasted_iota(jnp.int32, (1, LANES), 1)
    lo = jnp.where(lane < HEAD_DIM, 1.0, 0.0).astype(dtype)
    return lo, (1.0 - lo).astype(dtype)


def _flash_kernel(*refs, mode, tk, n_chunks, lam_init):
    if mode == "diff":
        q_ref, k_ref, v_ref, par_ref, o_ref, qt_scr, vxt_scr, m_scr, acc_scr, st_scr = refs
    else:
        q_ref, k_ref, v_ref, o_ref, qt_scr, vxt_scr, m_scr, acc_scr, st_scr = refs
    sub = lax.broadcasted_iota(jnp.int32, (LANES, 1), 0)
    lo_col = jnp.where(sub < HEAD_DIM, 1.0, 0.0)

    @pl.when(pl.program_id(2) == 0)
    def _():
        for c in range(n_chunks):
            vt = v_ref[c * tk:(c + 1) * tk, :].astype(F32).T
            if mode == "diff":
                vxt_scr[c, :LANES, :] = vt.astype(vxt_scr.dtype)
                vxt_scr[c, LANES:, :] = jnp.ones((LANES, tk), vxt_scr.dtype)
            else:
                vxt_scr[0, c] = (vt * lo_col + (1.0 - lo_col)).astype(vxt_scr.dtype)
                vxt_scr[1, c] = (vt * (1.0 - lo_col) + lo_col).astype(vxt_scr.dtype)

    qt = q_ref[...].astype(F32).T
    qt_scr[0] = (qt * lo_col).astype(qt_scr.dtype)
    qt_scr[1] = (qt * (1.0 - lo_col)).astype(qt_scr.dtype)
    m_scr[...] = jnp.full(m_scr.shape, NEG_INF, F32)
    acc_scr[...] = jnp.zeros(acc_scr.shape, F32)

    def qk(c, slot):
        off = pl.multiple_of(c * tk, tk)
        k = k_ref[pl.ds(off, tk), :]
        for h in range(2):
            st_scr[slot, h] = jnp.dot(k, qt_scr[h], preferred_element_type=F32)

    def softmax_pv(c, slot):
        for h in range(2):
            st = st_scr[slot, h]
            m_prev = m_scr[h]
            m_new = jnp.maximum(m_prev, jnp.max(st, axis=0, keepdims=True))
            alpha = jnp.exp2(m_prev - m_new)
            pt = jnp.exp2(st - m_new).astype(vxt_scr.dtype)
            vxt = vxt_scr[c] if mode == "diff" else vxt_scr[h, c]
            acc_scr[h] = alpha * acc_scr[h] + jnp.dot(vxt, pt, preferred_element_type=F32)
            m_scr[h] = m_new

    qk(0, 0)

    def body(j, carry):
        c = 2 * j
        qk(c + 1, 1)
        softmax_pv(c, 0)
        qk(c + 2, 0)
        softmax_pv(c + 1, 1)
        return carry

    lax.fori_loop(0, (n_chunks - 1) // 2, body, 0)
    if n_chunks % 2 == 1:
        softmax_pv(n_chunks - 1, 0)
    else:
        qk(n_chunks - 1, 1)
        softmax_pv(n_chunks - 2, 0)
        softmax_pv(n_chunks - 1, 1)

    a_lo = acc_scr[0]
    a_hi = acc_scr[1]
    if mode == "diff":
        lam = (jnp.exp(jnp.sum(par_ref[0:1, :] * par_ref[1:2, :], axis=1, keepdims=True))
               - jnp.exp(jnp.sum(par_ref[2:3, :] * par_ref[3:4, :], axis=1, keepdims=True))
               + lam_init)
        out_t = a_lo[:LANES] / a_lo[LANES:] - lam * (a_hi[:LANES] / a_hi[LANES:])
        ms = jnp.mean(out_t * out_t, axis=0, keepdims=True)
        out = (out_t * lax.rsqrt(ms + RMS_EPS)).T * par_ref[4:5, :] * (1.0 - lam_init)
    else:
        out = jnp.concatenate([a_lo[:HEAD_DIM] / a_lo[HEAD_DIM:], a_hi[HEAD_DIM:] / a_hi[:HEAD_DIM]], axis=0).T
    o_ref[...] = out.astype(o_ref.dtype)


def _pick_tile(n, candidates):
    for c in candidates:
        if n % c == 0:
            return c
    raise ValueError(f"no tile for {n}")


def _flash(q_arr, k_arr, v_arr, *, n_groups, q_rows, q_off, kv_rows, kv_off, k_col, v_col, mode, par=None,
           lam_init=0.0, name):
    batch = q_arr.shape[0]
    tq = _pick_tile(q_rows, (512, 256))
    tk = _pick_tile(kv_rows, (768, 512, 384, 256, 128))
    nq, n_chunks = q_rows // tq, kv_rows // tk
    assert q_off % tq == 0 and kv_off % kv_rows == 0
    vw = 2 * LANES if mode == "diff" else LANES
    in_specs = [pl.BlockSpec((None, tq, LANES), lambda b, g, i: (b, q_off // tq + i, g)),
                pl.BlockSpec((None, kv_rows, LANES), lambda b, g, i: (b, kv_off // kv_rows, k_col + g)),
                pl.BlockSpec((None, kv_rows, LANES), lambda b, g, i: (b, kv_off // kv_rows, v_col + g))]
    args = [q_arr, k_arr, v_arr]
    if mode == "diff":
        in_specs.append(pl.BlockSpec(par.shape, lambda b, g, i: (0, 0)))
        args.append(par)
    return pl.pallas_call(
        functools.partial(_flash_kernel, mode=mode, tk=tk, n_chunks=n_chunks, lam_init=lam_init),
        grid=(batch, n_groups, nq),
        in_specs=in_specs,
        out_specs=pl.BlockSpec((None, tq, LANES), lambda b, g, i: (b, i, g)),
        out_shape=jax.ShapeDtypeStruct((batch, q_rows, n_groups * LANES), _MXU),
        scratch_shapes=[pltpu.VMEM((2, LANES, tq), _MXU),
                        pltpu.VMEM((n_chunks, vw, tk) if mode == "diff" else (2, n_chunks, vw, tk), _MXU),
                        pltpu.VMEM((2, 1, tq), F32),
                        pltpu.VMEM((2, vw, tq), F32),
                        pltpu.VMEM((2, 2, tk, tq), F32)],
        compiler_params=_cparams(("parallel", "parallel", "arbitrary")),
        name=name,
    )(*args)


def _seg_attn_kernel(*refs, n_pairs, nseg, bias_segs, band, q_axis, has_init, s_len):
    it = iter(refs)
    q_ref = next(it)
    k_refs = [next(it) for _ in range(nseg)]
    v_refs = [next(it) for _ in range(nseg)]
    bias_ref = next(it) if bias_segs else None
    init_ref = next(it) if has_init else None
    o_ref = next(it)

    tq = q_ref.shape[0]
    is_lo = lax.broadcasted_iota(jnp.int32, (tq, LANES), 1) < HEAD_DIM
    lo, hi = _head_masks(q_ref.dtype)

    masks = [None] * nseg
    if band:
        qi = pl.program_id(q_axis)
        qpos = qi * tq + lax.broadcasted_iota(jnp.int32, (tq, tq), 0)
        for j in range(1, nseg):
            kpos = (qi + j - 2) * tq + lax.broadcasted_iota(jnp.int32, (tq, tq), 1)
            dist = jnp.abs(qpos - kpos)
            ok = jnp.where(kpos >= 0, jnp.where(kpos < s_len, dist, SWA_WINDOW + 1), SWA_WINDOW + 1)
            masks[j] = ok <= SWA_WINDOW

    def one_head(hh):
        p_idx, half = hh // 2, hh % 2
        qm = q_ref[:, p_idx * LANES:(p_idx + 1) * LANES] * (lo if half == 0 else hi)
        ss = []
        boff = 0
        for j in range(nseg):
            s = lax.dot_general(qm, k_refs[j][...], (((1,), (1,)), ((), ())), preferred_element_type=F32)
            if j in bias_segs:
                n = k_refs[j].shape[0]
                s = s + bias_ref[half, :, boff:boff + n]
                boff += n
            if masks[j] is not None:
                s = jnp.where(masks[j], s, NEG_INF)
            ss.append(s)
        m = functools.reduce(jnp.maximum, [jnp.max(s, axis=1, keepdims=True) for s in ss])
        if has_init:
            sink = init_ref[hh:hh + 1, 0:1]
            m = jnp.maximum(m, sink)
            l = jnp.exp2(sink - m)
        else:
            l = jnp.zeros((tq, 1), F32)
        acc = jnp.zeros((tq, LANES), F32)
        for j in range(nseg):
            p = jnp.exp2(ss[j] - m)
            l = l + jnp.sum(p, axis=1, keepdims=True)
            acc = acc + jnp.dot(p.astype(_MXU), v_refs[j][...], preferred_element_type=F32)
        return acc / l

    for p_idx in range(n_pairs):
        out = jnp.where(is_lo, one_head(2 * p_idx), one_head(2 * p_idx + 1))
        o_ref[:, p_idx * LANES:(p_idx + 1) * LANES] = out.astype(o_ref.dtype)


def _seg_attention(q_arr, k_arr, v_arr, *, grid, q_spec, k_specs, v_specs, out_spec, out_shape, n_pairs,
                   bias=None, bias_spec=None, bias_segs=(), band=False, q_axis=0, init=None, s_len=0, sem, name):
    nseg = len(k_specs)
    args = [q_arr] + [k_arr] * nseg + [v_arr] * nseg
    specs = [q_spec] + list(k_specs) + list(v_specs)
    if bias is not None:
        args.append(bias)
        specs.append(bias_spec)
    if init is not None:
        args.append(init)
        specs.append(pl.BlockSpec(init.shape, lambda *_: (0, 0)))
    kern = functools.partial(_seg_attn_kernel, n_pairs=n_pairs, nseg=nseg, bias_segs=tuple(bias_segs), band=band,
                             q_axis=q_axis, has_init=init is not None, s_len=s_len)
    return pl.pallas_call(kern, grid=grid, in_specs=specs, out_specs=out_spec, out_shape=out_shape,
                          compiler_params=_cparams(sem), name=name)(*args)


def _na_bias_tables(rpb, rows):
    nh = rpb.shape[0]
    nkr = 3 * NA_Q_ROWS
    qc = np.arange(GRID_W)[:, None]
    kc = np.arange(GRID_W)[None, :]
    ws = np.clip(qc - NA_KW // 2, 0, GRID_W - NA_KW)
    cvalid = ((kc >= ws) & (kc < ws + NA_KW)).reshape(-1)
    dc = (kc - qc + NA_KW - 1).reshape(-1)
    onehot = ((np.arange(2 * NA_KW - 1)[:, None] == dc[None, :]) & cvalid[None, :]).astype(np.float32)
    tiles = jnp.einsum("hrd,dx->hrx", rpb.astype(F32) * LOG2E, jnp.asarray(onehot),
                       precision=lax.Precision.HIGHEST)
    tiles = jnp.where(cvalid[None, None, :], tiles, NEG_INF)
    qr = np.arange(NA_Q_ROWS)[:, None]
    kr = np.arange(nkr)[None, :]
    tables = []
    for variant in range(3):
        if variant == 0:
            r0, k0, nrows = 0, 0, rows
        elif variant == 1:
            r0, k0, nrows = 2 * NA_Q_ROWS, NA_Q_ROWS, 8 * NA_Q_ROWS
        else:
            r0, k0, nrows = rows - NA_Q_ROWS, rows - nkr, rows
        r = r0 + qr
        rp = k0 + kr
        rs = np.clip(r - NA_KH // 2, 0, nrows - NA_KH)
        rvalid = (rp >= rs) & (rp < rs + NA_KH)
        dr = np.clip(rp - r + NA_KH - 1, 0, 2 * NA_KH - 2)
        t = jnp.take(tiles, jnp.asarray(dr.reshape(-1).astype(np.int32)), axis=1)
        t = jnp.where(rvalid.reshape(-1)[None, :, None], t, NEG_INF)
        t = t.reshape(nh, NA_Q_ROWS, nkr, GRID_W, GRID_W).transpose(0, 1, 3, 2, 4)
        tables.append(t.reshape(nh // 2, 2, NA_Q_ROWS * GRID_W, nkr * GRID_W))
    return jnp.stack(tables)


def _neighbourhood(q_arr, kv_arr, bias, *, s_len, n_ctx):
    batch = q_arr.shape[0]
    tq = NA_Q_ROWS * GRID_W
    nq = s_len // tq
    n_pairs = NA_HEADS // 2
    cb = s_len // n_ctx
    q_spec = pl.BlockSpec((None, tq, LANES), lambda p, i, b: (b, i, p))

    def seg_specs(col):
        specs = [pl.BlockSpec((None, n_ctx, LANES), lambda p, i, b: (b, cb, col + p))]
        for j in range(3):
            specs.append(pl.BlockSpec(
                (None, tq, LANES),
                lambda p, i, b, j=j: (b, jnp.clip(i - 1, 0, nq - 3) + j, col + p)))
        return specs

    bias_spec = pl.BlockSpec((None, None, 2, tq, 3 * tq),
                             lambda p, i, b: (jnp.where(i == 0, 0, jnp.where(i == nq - 1, 2, 1)), p, 0, 0, 0))
    out_spec = pl.BlockSpec((None, tq, LANES), lambda p, i, b: (b, i, p))
    return _seg_attention(
        q_arr, kv_arr, kv_arr, grid=(n_pairs, nq, batch), q_spec=q_spec, k_specs=seg_specs(0),
        v_specs=seg_specs(n_pairs), out_spec=out_spec,
        out_shape=jax.ShapeDtypeStruct((batch, s_len, n_pairs * LANES), _MXU),
        n_pairs=1, bias=bias, bias_spec=bias_spec, bias_segs=(1, 2, 3),
        sem=("parallel", "parallel", "parallel"), name="neighbourhood_attn")


def _windowed(q_arr, k_arr, v_arr, init, *, s_len, n_ctx):
    batch = q_arr.shape[0]
    tq = SWA_WINDOW
    nb = s_len // tq
    n_pairs = SWA_HEADS // 2
    cb = s_len // n_ctx
    qw = n_pairs * LANES
    q_spec = pl.BlockSpec((None, tq, qw), lambda b, i: (b, i, 0))
    specs = [pl.BlockSpec((None, n_ctx, LANES), lambda b, i: (b, cb, 0))]
    for j in range(3):
        specs.append(pl.BlockSpec((None, tq, LANES),
                                  lambda b, i, j=j: (b, jnp.clip(i - 1 + j, 0, nb - 1), 0)))
    out_spec = pl.BlockSpec((None, tq, qw), lambda b, i: (b, i, 0))
    return _seg_attention(
        q_arr, k_arr, v_arr, grid=(batch, nb), q_spec=q_spec, k_specs=specs, v_specs=specs,
        out_spec=out_spec, out_shape=jax.ShapeDtypeStruct((batch, s_len, qw), _MXU),
        n_pairs=n_pairs, band=True, q_axis=1, init=init, s_len=s_len,
        sem=("parallel", "parallel"), name="windowed_attn")


def _layernorm(z, g, b):
    mu = jnp.mean(z, axis=1, keepdims=True)
    zc = z - mu
    var = jnp.mean(zc * zc, axis=1, keepdims=True)
    return zc * lax.rsqrt(var + LN_EPS) * g + b


def _post_kernel(x_ref, ya_ref, yb_ref, wo_ref, mod_ref, g_ref, b_ref, rw_ref, rb_ref,
                 x1_ref, tok_ref, route_ref, *, alpha):
    half = ya_ref.shape[1]
    y = (jnp.dot(ya_ref[...], wo_ref[:half, :], preferred_element_type=F32)
         + jnp.dot(yb_ref[...], wo_ref[half:, :], preferred_element_type=F32))
    x1 = _layernorm(alpha * x_ref[...] + mod_ref[2:3, :] * y, g_ref[...], b_ref[...])
    x1_ref[...] = x1
    tok = x1 * (1.0 + mod_ref[4:5, :]) + mod_ref[3:4, :]
    tok_ref[...] = tok
    logits = jnp.dot(tok.astype(_MXU), rw_ref[...], preferred_element_type=F32) + rb_ref[...]
    tm = logits.shape[0]
    lane = lax.broadcasted_iota(jnp.int32, (tm, LANES), 1).astype(F32)
    vals, idxs = [], []
    for _ in range(TOP_K):
        mx = jnp.max(logits, axis=1, keepdims=True)
        ix = jnp.min(jnp.where(logits == mx, lane, float(LANES)), axis=1, keepdims=True)
        vals.append(mx)
        idxs.append(ix)
        logits = jnp.where(lane == ix, -3.0e38, logits)
    es = [jnp.exp(v - vals[0]) for v in vals]
    den = functools.reduce(lambda a, c: a + c, es)
    route = jnp.zeros((tm, LANES), F32)
    for k in range(TOP_K):
        route = jnp.where(lane == float(k), idxs[k], route)
        route = jnp.where(lane == float(TOP_K + k), es[k] / den, route)
    route_ref[...] = route


def _post(x2d, ya, yb, wo, mods, g, b, rw, rb, *, batch, nbt_in, nblk, n_lat_blk, alpha):
    d = x2d.shape[1]
    tm = ROW_TILE
    half = ya.shape[1]

    def mod_map(bi, t):
        return (jnp.where(t >= n_lat_blk, batch, bi), 0, 0)

    rows_out = batch * nblk * tm
    o_map = lambda bi, t: (bi * nblk + t, 0)
    return pl.pallas_call(
        functools.partial(_post_kernel, alpha=alpha),
        grid=(batch, nblk),
        in_specs=[pl.BlockSpec((tm, d), lambda bi, t: (bi * nbt_in + t, 0)),
                  pl.BlockSpec((tm, half), o_map),
                  pl.BlockSpec((tm, half), o_map),
                  pl.BlockSpec((d, d), lambda bi, t: (0, 0)),
                  pl.BlockSpec((None, 6, d), mod_map),
                  pl.BlockSpec((1, d), lambda bi, t: (0, 0)),
                  pl.BlockSpec((1, d), lambda bi, t: (0, 0)),
                  pl.BlockSpec((d, LANES), lambda bi, t: (0, 0)),
                  pl.BlockSpec((1, LANES), lambda bi, t: (0, 0))],
        out_specs=[pl.BlockSpec((tm, d), o_map), pl.BlockSpec((tm, d), o_map), pl.BlockSpec((tm, LANES), o_map)],
        out_shape=[jax.ShapeDtypeStruct((rows_out, d), F32), jax.ShapeDtypeStruct((rows_out, d), F32),
                   jax.ShapeDtypeStruct((rows_out, LANES), F32)],
        compiler_params=_cparams(("parallel", "parallel")),
        name="post_attn",
    )(x2d, ya, yb, wo, mods, g, b, rw, rb)


def _moe_kernel(be_ref, nu_ref, dst_ref, src0_ref, src1_ref, src2_ref, tok_hbm, wi_ref, bi_ref, wo_ref, bo_ref,
                y_hbm, xbuf0, xbuf1, xbuf2, obuf0, obuf1, obuf2, gsem, ssem, wi_s, wo_s, *, dump0):
    i = pl.program_id(0)
    n_used = nu_ref[0]
    f = wo_s.shape[0]
    xbufs = (xbuf0, xbuf1, xbuf2)
    obufs = (obuf0, obuf1, obuf2)

    def gather_copy(row, r, slot):
        return pltpu.make_async_copy(tok_hbm.at[pl.ds(row, 1), :], xbufs[slot].at[pl.ds(r, 1), :], gsem.at[slot])

    def scatter_copy(row, r, slot):
        return pltpu.make_async_copy(obufs[slot].at[pl.ds(r, 1), :], y_hbm.at[pl.ds(row, 1), :], ssem.at[slot])

    def start_gather(rows_ref, slot):
        for r in range(MOE_ROWS):
            gather_copy(rows_ref[0, r], r, slot).start(priority=r % 2)

    def wait_gather(slot):
        for _ in range(MOE_ROWS):
            gather_copy(0, 0, slot).wait()

    def wait_scatter(slot):
        for _ in range(MOE_ROWS):
            scatter_copy(0, 0, slot).wait()

    @pl.when(i == 0)
    def _():
        for slot in range(MOE_RING):
            obufs[slot][...] = jnp.zeros(obufs[slot].shape, F32)
            for r in range(MOE_ROWS):
                scatter_copy(dump0 + slot * MOE_ROWS + r, r, slot).start(priority=r % 2)
        start_gather(src0_ref, 0)
        start_gather(src1_ref, 1)

    last_used = n_used - 1
    changed = jnp.logical_or(i == 0, be_ref[jnp.clip(i - 1, 0, last_used)] != be_ref[jnp.minimum(i, last_used)])

    for slot in range(MOE_RING):
        @pl.when(jnp.logical_and(i < n_used, i % MOE_RING == slot))
        def _(slot=slot):
            wait_scatter(slot)
            wait_gather(slot)

            @pl.when(changed)
            def _():
                wi_s[...] = wi_ref[...].astype(wi_s.dtype)
                wo_s[...] = wo_ref[...].astype(wo_s.dtype)

            x = xbufs[slot][...].astype(wi_s.dtype)
            start_gather(src2_ref, (slot + 2) % MOE_RING)
            hh = jnp.dot(x, wi_s[...], preferred_element_type=F32) + bi_ref[...]
            gate = jnp.minimum(hh[:, :f], SWIGLU_LIMIT)
            up = jnp.clip(hh[:, f:], -SWIGLU_LIMIT, SWIGLU_LIMIT)
            act = gate * (1.0 / (1.0 + jnp.exp(-SWIGLU_ALPHA * gate))) * (up + 1.0)
            obufs[slot][...] = jnp.dot(act.astype(wo_s.dtype), wo_s[...], preferred_element_type=F32) + bo_ref[...]
            for r in range(MOE_ROWS):
                scatter_copy(dst_ref[0, r], r, slot).start(priority=r % 2)

        @pl.when(jnp.logical_and(i >= n_used, i % MOE_RING == slot))
        def _(slot=slot):
            @pl.when(i < n_used + 2)
            def _():
                wait_gather(slot)

            @pl.when(i < n_used + MOE_RING)
            def _():
                wait_scatter(slot)


def _experts(tok, src_rows, dst_rows, block_e, n_used, w_in, b_in, w_out, b_out, layer):
    n_tok, d = tok.shape
    _, n_e, _, f2 = w_in.shape
    f = f2 // 2
    n_blocks = src_rows.shape[0]

    def blk(i, be, nu):
        return jnp.minimum(i, nu[0] - 1)

    def e_map(i, be, nu):
        return (layer, be[blk(i, be, nu)], 0, 0)

    def rows_spec(ahead):
        return pl.BlockSpec((None, 1, MOE_ROWS), lambda i, be, nu: (blk(i + ahead, be, nu), 0, 0),
                            memory_space=pltpu.SMEM)

    row_buf = pltpu.VMEM((MOE_ROWS, d), F32)
    grid_spec = pltpu.PrefetchScalarGridSpec(
        num_scalar_prefetch=2,
        grid=(n_blocks + MOE_RING,),
        in_specs=[rows_spec(0), rows_spec(0), rows_spec(1), rows_spec(2),
                  pl.BlockSpec(memory_space=pl.ANY),
                  pl.BlockSpec((None, None, d, f2), e_map),
                  pl.BlockSpec((None, None, 1, f2), e_map),
                  pl.BlockSpec((None, None, f, d), e_map),
                  pl.BlockSpec((None, None, 1, d), e_map)],
        out_specs=pl.BlockSpec(memory_space=pl.ANY),
        scratch_shapes=[row_buf] * (2 * MOE_RING)
                       + [pltpu.SemaphoreType.DMA((MOE_RING,)), pltpu.SemaphoreType.DMA((MOE_RING,)),
                          pltpu.VMEM((d, f2), _MXU), pltpu.VMEM((f, d), _MXU)],
    )
    depth = w_in.shape[0]
    return pl.pallas_call(
        functools.partial(_moe_kernel, dump0=TOP_K * n_tok), grid_spec=grid_spec,
        out_shape=jax.ShapeDtypeStruct((TOP_K * n_tok + MOE_RING * MOE_ROWS, d), F32),
        compiler_params=_cparams(("arbitrary",)),
        name="experts",
    )(block_e, n_used, dst_rows, src_rows, src_rows, src_rows, tok, w_in, b_in.reshape(depth, n_e, 1, f2), w_out,
      b_out.reshape(depth, n_e, 1, d))


def _route_plan(route, n_experts):
    n = route.shape[0]
    top_idx = route[:, :TOP_K].astype(jnp.int32)
    nk = n * TOP_K
    flat_e = top_idx.reshape(-1)
    order = jnp.argsort(flat_e).astype(jnp.int32)
    experts = jnp.arange(n_experts, dtype=jnp.int32)
    counts = jnp.sum(flat_e[:, None] == experts[None, :], axis=0, dtype=jnp.int32)
    padded = (counts + MOE_ROWS - 1) // MOE_ROWS * MOE_ROWS
    start = jnp.cumsum(counts) - counts
    pend = jnp.cumsum(padded)
    pstart = pend - padded
    n_blocks = -(-nk // MOE_ROWS) + n_experts
    first = jnp.arange(n_blocks, dtype=jnp.int32) * MOE_ROWS
    block_e = jnp.minimum(jnp.sum(pend[None, :] <= first[:, None], axis=1, dtype=jnp.int32), n_experts - 1)
    n_used = (pend[-1:] // MOE_ROWS).astype(jnp.int32)
    blk = jnp.arange(n_blocks, dtype=jnp.int32)[:, None]
    row = jnp.arange(MOE_ROWS, dtype=jnp.int32)[None, :]
    rank = blk * MOE_ROWS + row - jnp.take(pstart, block_e, mode="clip")[:, None]
    src = jnp.clip(jnp.take(start, block_e, mode="clip")[:, None] + rank, 0, nk - 1)
    valid = rank < jnp.take(counts, block_e, mode="clip")[:, None]
    pair = jnp.take(order, src, mode="clip")
    token = pair // TOP_K
    src_rows = jnp.where(valid, token, 0)
    dst_rows = jnp.where(valid, (pair % TOP_K) * n + token, nk + (blk % MOE_RING) * MOE_ROWS + row)
    return (src_rows.reshape(n_blocks, 1, MOE_ROWS), dst_rows.reshape(n_blocks, 1, MOE_ROWS), block_e, n_used)


def _ln2_kernel(x_ref, y0_ref, y1_ref, y2_ref, y3_ref, route_ref, mod_ref, g_ref, b_ref, o_ref, *, alpha):
    ys = (y0_ref, y1_ref, y2_ref, y3_ref)
    f = route_ref[:, TOP_K:TOP_K + 1] * ys[0][...]
    for k in range(1, TOP_K):
        f = f + route_ref[:, TOP_K + k:TOP_K + k + 1] * ys[k][...]
    o_ref[...] = _layernorm(alpha * x_ref[...] + mod_ref[5:6, :] * f, g_ref[...], b_ref[...])


def _ln2(x1, y, route, mods, g, b, *, batch, nblk, n_lat_blk, alpha):
    rows, d = x1.shape
    tm = ROW_TILE
    r_map = lambda bi, t: (bi * nblk + t, 0)

    def y_spec(k):
        return pl.BlockSpec((tm, d), lambda bi, t: (k * (rows // tm) + bi * nblk + t, 0))

    def mod_map(bi, t):
        return (jnp.where(t >= n_lat_blk, batch, bi), 0, 0)

    return pl.pallas_call(
        functools.partial(_ln2_kernel, alpha=alpha),
        grid=(batch, nblk),
        in_specs=[pl.BlockSpec((tm, d), r_map)] + [y_spec(k) for k in range(TOP_K)]
                 + [pl.BlockSpec((tm, LANES), r_map), pl.BlockSpec((None, 6, d), mod_map),
                    pl.BlockSpec((1, d), lambda bi, t: (0, 0)), pl.BlockSpec((1, d), lambda bi, t: (0, 0))],
        out_specs=pl.BlockSpec((tm, d), r_map),
        out_shape=jax.ShapeDtypeStruct((rows, d), F32),
        compiler_params=_cparams(("parallel", "parallel")),
        name="combine_ln2",
    )(x1, y, y, y, y, route, mods, g, b)


def _rope_layout(w, n_heads):
    lead = w.shape[:-1]
    return w.reshape(lead + (n_heads, HEAD_DIM // 2, 2)).swapaxes(-1, -2).reshape(lead + (n_heads * HEAD_DIM,))


def _pair_layout(w, n_heads, axis=-1):
    axis = axis % w.ndim
    shape = w.shape
    w = w.reshape(shape[:axis] + (2, n_heads // 2, HEAD_DIM) + shape[axis + 1:])
    return w.swapaxes(axis, axis + 1).reshape(shape)


def _rope_tables(s_len, n_ctx):
    t = np.arange(s_len)
    row = (t // GRID_W).astype(np.float32)
    col = (t % GRID_W).astype(np.float32)
    axis_dim = HEAD_DIM // 2
    freqs = jnp.asarray(ROPE_THETA, F32) ** (-jnp.arange(0, axis_dim, 2, dtype=F32) / axis_dim)
    ang = jnp.concatenate([jnp.asarray(row)[:, None] * freqs, jnp.asarray(col)[:, None] * freqs], axis=-1)
    cos, sin = jnp.cos(ang), jnp.sin(ang)
    cos_h = jnp.concatenate([cos, cos], axis=-1)
    sin_h = jnp.concatenate([-sin, sin], axis=-1)
    cos_t = jnp.concatenate([cos_h, jnp.ones((n_ctx, HEAD_DIM), F32)], axis=0)
    sin_t = jnp.concatenate([sin_h, jnp.zeros((n_ctx, HEAD_DIM), F32)], axis=0)
    return jnp.tile(cos_t, (1, 2)), jnp.tile(sin_t, (1, 2))


def _moe_and_ln2(x1, tok, route, mods, n_experts, w_in, b_in, w_out, b_out, layer, g, b, *,
                 batch, nblk, n_lat_blk, alpha):
    src_rows, dst_rows, block_e, n_used = _route_plan(route, n_experts)
    y = _experts(tok, src_rows, dst_rows, block_e, n_used, w_in, b_in, w_out, b_out, layer)
    return _ln2(x1, y, route, mods, g, b, batch=batch, nblk=nblk, n_lat_blk=n_lat_blk, alpha=alpha)


def kernel(x, c, ctx, c_ctx, mod_w, mod_b, ln1_g, ln1_b, ln2_g, ln2_b, router_w, router_b, moe_w_in, moe_b_in,
           moe_w_out, moe_b_out, ab_w_in, ab_w_out, na_rpb, diff_lq1, diff_lk1, diff_lq2, diff_lk2, diff_subln,
           cd_w_in, cd_w_out, gqa_q_norm, gqa_k_norm, swa_sink):
    batch, s_len, d = x.shape
    n_ctx = ctx.shape[1]
    t_len = n_ctx + s_len
    depth = mod_w.shape[0]
    n_experts = router_w.shape[2]
    alpha = (2.0 * depth) ** 0.25
    tm = ROW_TILE
    nbt = t_len // tm
    nbs = s_len // tm
    assert depth == 2 and n_ctx % tm == 0 and s_len % tm == 0 and batch + 1 <= 8

    cc = jnp.zeros((8, d), F32).at[:batch].set(c).at[batch].set(c_ctx)
    mod_all = _modulation(cc, mod_w, mod_b)
    mods = [mod_all[l, :batch + 1].reshape(batch + 1, 6, d) for l in range(depth)]
    cos_t, sin_t = _rope_tables(s_len, n_ctx)
    rw = [jnp.zeros((d, LANES), F32).at[:, :n_experts].set(router_w[l]).astype(_MXU) for l in range(depth)]
    rb = [jnp.full((1, LANES), NEG_INF, F32).at[0, :n_experts].set(router_b[l]) for l in range(depth)]

    stream = jnp.concatenate([x, ctx], axis=1).reshape(batch * t_len, d)
    by_batch = lambda a: a.reshape(batch, t_len, a.shape[-1])

    na_w = NA_HEADS * HEAD_DIM
    df_w = DIFF_HEADS * 2 * HEAD_DIM
    wab = ab_w_in[0]
    w0 = jnp.concatenate([wab[:, :3 * na_w], _rope_layout(wab[:, 3 * na_w:3 * na_w + df_w], 2 * DIFF_HEADS),
                          _rope_layout(wab[:, 3 * na_w + df_w:3 * na_w + 2 * df_w], 2 * DIFF_HEADS),
                          wab[:, 3 * na_w + 2 * df_w:]], axis=1).astype(_MXU)
    plan0 = ((0, na_w, None, False, QSCALE), (na_w, 2 * na_w, None, False, 1.0),
             (3 * na_w, df_w, None, True, QSCALE), (3 * na_w + df_w, df_w, None, True, 1.0),
             (3 * na_w + 2 * df_w, df_w, None, False, 1.0))
    nw0 = jnp.zeros((8, LANES), F32)
    nq, nkv, dq, dk, dv = map(by_batch, _in_proj(stream, mods[0], w0, cos_t, sin_t, nw0, plan0,
                                                 batch=batch, n_ctx=n_ctx))
    lam_init0 = 0.8 - 0.6 * math.exp(-0.3 * 0)
    par = jnp.zeros((8, LANES), F32)
    par = par.at[0, :HEAD_DIM].set(diff_lq1[0]).at[1, :HEAD_DIM].set(diff_lk1[0])
    par = par.at[2, :HEAD_DIM].set(diff_lq2[0]).at[3, :HEAD_DIM].set(diff_lk2[0]).at[4].set(diff_subln[0])
    bias = _na_bias_tables(na_rpb[0], s_len // GRID_W)
    y_na = _neighbourhood(nq, nkv, bias, s_len=s_len, n_ctx=n_ctx)
    y_na_c = _flash(nq, nkv, nkv, n_groups=NA_HEADS // 2, q_rows=n_ctx, q_off=s_len, kv_rows=n_ctx,
                    kv_off=s_len, k_col=0,
                    v_col=NA_HEADS // 2, mode="pair", name="neighbourhood_attn_ctx")
    y_df = _flash(dq, dk, dv, n_groups=DIFF_HEADS, q_rows=s_len, q_off=0, kv_rows=t_len, kv_off=0, k_col=0, v_col=0,
                  mode="diff", par=par, lam_init=lam_init0, name="diff_attn")
    y_df_c = _flash(dq, dk, dv, n_groups=DIFF_HEADS, q_rows=n_ctx, q_off=s_len, kv_rows=n_ctx, kv_off=s_len,
                    k_col=0, v_col=0,
                    mode="diff", par=par, lam_init=lam_init0, name="diff_attn_ctx")
    ya = jnp.concatenate([y_na, y_na_c], axis=1).reshape(batch * t_len, -1)
    yb = jnp.concatenate([y_df, y_df_c], axis=1).reshape(batch * t_len, -1)
    x1, tok, route = _post(stream, ya, yb, ab_w_out[0].astype(_MXU), mods[0], ln1_g[0][None], ln1_b[0][None],
                           rw[0], rb[0], batch=batch, nbt_in=nbt, nblk=nbt, n_lat_blk=nbs, alpha=alpha)
    stream = _moe_and_ln2(x1, tok, route, mods[0], n_experts, moe_w_in, moe_b_in, moe_w_out, moe_b_out, 0,
                          ln2_g[0][None], ln2_b[0][None], batch=batch, nblk=nbt, n_lat_blk=nbs, alpha=alpha)

    hw = GQA_HEADS * HEAD_DIM
    kw = GQA_KV_HEADS * HEAD_DIM
    wcd = cd_w_in[0]
    gq, gk, gv, wq, wk, wv = jnp.split(wcd, [hw, hw + kw, hw + 2 * kw, 2 * hw + 2 * kw, 2 * hw + 3 * kw], axis=1)
    w1 = jnp.concatenate([_pair_layout(_rope_layout(gq, GQA_HEADS), GQA_HEADS),
                          _pair_layout(_rope_layout(wq, SWA_HEADS), SWA_HEADS),
                          _rope_layout(gk, GQA_KV_HEADS), gv, _rope_layout(wk, SWA_KV_HEADS), wv],
                         axis=1).astype(_MXU)
    plan1 = ((0, hw, 0, True, QSCALE), (hw, hw, None, True, QSCALE), (2 * hw, kw, 1, True, 1.0),
             (2 * hw + kw, kw, None, False, 1.0), (2 * hw + 2 * kw, kw, None, True, 1.0),
             (2 * hw + 3 * kw, kw, None, False, 1.0))
    nw1 = jnp.zeros((8, LANES), F32).at[0].set(jnp.tile(_rope_layout(gqa_q_norm[0], 1), 2))
    nw1 = nw1.at[1].set(jnp.tile(_rope_layout(gqa_k_norm[0], 1), 2))
    gq, wq, gk, gv, wk, wv = map(by_batch, _in_proj(stream, mods[1], w1, cos_t, sin_t, nw1, plan1,
                                                    batch=batch, n_ctx=n_ctx))
    n_qp = GQA_HEADS // 2
    y_c = _flash(gq.reshape(batch, t_len * n_qp, LANES), gk, gv, n_groups=1, q_rows=s_len * n_qp,
                 q_off=0, kv_rows=t_len, kv_off=0, k_col=0, v_col=0, mode="pair", name="gqa_attn")
    sink = swa_sink[0].reshape(2, SWA_HEADS // 2).T.reshape(SWA_HEADS) * LOG2E
    init = jnp.broadcast_to(sink[:, None], (SWA_HEADS, LANES)).astype(F32)
    y_d = _windowed(wq, wk, wv, init, s_len=s_len, n_ctx=n_ctx)
    wo1 = jnp.concatenate([_pair_layout(cd_w_out[0][:hw], GQA_HEADS, axis=0),
                           _pair_layout(cd_w_out[0][hw:], SWA_HEADS, axis=0)], axis=0).astype(_MXU)
    x1, tok, route = _post(stream, y_c.reshape(batch * s_len, -1), y_d.reshape(batch * s_len, -1), wo1, mods[1],
                           ln1_g[1][None], ln1_b[1][None], rw[1], rb[1], batch=batch, nbt_in=nbt,
                           nblk=nbs, n_lat_blk=nbs, alpha=alpha)
    out = _moe_and_ln2(x1, tok, route, mods[1], n_experts, moe_w_in, moe_b_in, moe_w_out, moe_b_out, 1,
                       ln2_g[1][None], ln2_b[1][None], batch=batch, nblk=nbs, n_lat_blk=nbs, alpha=alpha)
    return out.reshape(batch, s_len, d)
```

```python
import functools
import math

import numpy as np
import jax
import jax.numpy as jnp
from jax import lax
from jax.experimental import pallas as pl
from jax.experimental.pallas import tpu as pltpu

F32 = jnp.float32
_MXU = jnp.bfloat16

HEAD_DIM = 64
GRID_W = 64
LOG2E = math.log2(math.e)
QSCALE = HEAD_DIM ** -0.5 * LOG2E
ROPE_THETA = 10000.0
NA_HEADS = 8
NA_KH = 8
NA_KW = 16
DIFF_HEADS = 4
GQA_HEADS = 8
GQA_KV_HEADS = 2
SWA_HEADS = 8
SWA_KV_HEADS = 2
SWA_WINDOW = 128
TOP_K = 4
SWIGLU_LIMIT = 7.0
SWIGLU_ALPHA = 1.702
LN_EPS = 1e-5
RMS_EPS = 1e-6
NEG_INF = -1e30

LANES = 128
ROW_TILE = 256
MOE_ROWS = 256
MOE_RING = 3
NA_Q_ROWS = 4
VMEM_LIMIT = 52 * 1024 * 1024


def _cparams(sem, vmem=VMEM_LIMIT):
    return pltpu.CompilerParams(dimension_semantics=sem, vmem_limit_bytes=vmem)


def _mod_kernel(c_ref, w_ref, b_ref, o_ref):
    c = c_ref[...]
    a = (c / (1.0 + jnp.exp(-c))).astype(_MXU)
    o_ref[...] = jnp.dot(a, w_ref[...].astype(_MXU), preferred_element_type=F32) + b_ref[...]


def _modulation(cc, mod_w, mod_b):
    depth, d, d6 = mod_w.shape
    tn = d6 // 4
    return pl.pallas_call(
        _mod_kernel,
        grid=(depth, d6 // tn),
        in_specs=[pl.BlockSpec((8, d), lambda l, j: (0, 0)),
                  pl.BlockSpec((None, d, tn), lambda l, j: (l, 0, j)),
                  pl.BlockSpec((None, 1, tn), lambda l, j: (l, 0, j))],
        out_specs=pl.BlockSpec((None, 8, tn), lambda l, j: (l, 0, j)),
        out_shape=jax.ShapeDtypeStruct((depth, 8, d6), F32),
        compiler_params=_cparams(("arbitrary", "arbitrary")),
        name="modulation",
    )(cc, mod_w, mod_b.reshape(depth, 1, d6))


def _in_kernel(x_ref, mod_ref, w_ref, cos_ref, sin_ref, nw_ref, *o_refs, plan):
    x = x_ref[...]
    h = (x * (1.0 + mod_ref[1:2, :]) + mod_ref[0:1, :]).astype(_MXU)
    tm = x.shape[0]
    lane = lax.broadcasted_iota(jnp.int32, (tm, LANES), 1)
    first_half = (lane & (HEAD_DIM // 2)) == 0
    gi = lax.broadcasted_iota(jnp.int32, (LANES, LANES), 0) // HEAD_DIM
    gj = lax.broadcasted_iota(jnp.int32, (LANES, LANES), 1) // HEAD_DIM
    seg = jnp.where(gi == gj, 1.0, 0.0).astype(_MXU)
    cos = cos_ref[...]
    sin = sin_ref[...]
    for o_ref, (c0, width, norm_row, rope, scale) in zip(o_refs, plan):
        acc = jnp.dot(h, w_ref[:, c0:c0 + width], preferred_element_type=F32)
        for j in range(width // LANES):
            a = acc[:, j * LANES:(j + 1) * LANES]
            if norm_row is not None:
                a2 = a * a
                hi = a2.astype(_MXU)
                lo = (a2 - hi.astype(F32)).astype(_MXU)
                ss = (jnp.dot(hi, seg, preferred_element_type=F32)
                      + jnp.dot(lo, seg, preferred_element_type=F32))
                a = a * lax.rsqrt(ss * (1.0 / HEAD_DIM) + RMS_EPS) * nw_ref[norm_row:norm_row + 1, :]
            if rope:
                partner = jnp.where(first_half, pltpu.roll(a, LANES - HEAD_DIM // 2, 1),
                                    pltpu.roll(a, HEAD_DIM // 2, 1))
                a = a * cos + partner * sin
            if scale != 1.0:
                a = a * scale
            o_ref[:, j * LANES:(j + 1) * LANES] = a.astype(o_ref.dtype)


def _in_proj(x2d, mods, w, cos_t, sin_t, nw, plan, *, batch, n_ctx):
    rows, d = x2d.shape
    ncols = w.shape[1]
    tm = ROW_TILE
    nbt = rows // batch // tm
    n_lat_blk = nbt - n_ctx // tm

    def mod_map(i):
        return (jnp.where(i % nbt >= n_lat_blk, batch, i // nbt), 0, 0)

    return pl.pallas_call(
        functools.partial(_in_kernel, plan=plan),
        grid=(rows // tm,),
        in_specs=[pl.BlockSpec((tm, d), lambda i: (i, 0)),
                  pl.BlockSpec((None, 6, d), mod_map),
                  pl.BlockSpec((d, ncols), lambda i: (0, 0)),
                  pl.BlockSpec((tm, LANES), lambda i: (i % nbt, 0)),
                  pl.BlockSpec((tm, LANES), lambda i: (i % nbt, 0)),
                  pl.BlockSpec((8, LANES), lambda i: (0, 0))],
        out_specs=[pl.BlockSpec((tm, p[1]), lambda i: (i, 0)) for p in plan],
        out_shape=[jax.ShapeDtypeStruct((rows, p[1]), _MXU) for p in plan],
        compiler_params=_cparams(("parallel",)),
        name="in_proj",
    )(x2d, mods, w, cos_t, sin_t, nw)


def _head_masks(dtype):
    lane = lax.broadcasted_iota(jnp.int32, (1, LANES), 1)
    lo = jnp.where(lane < HEAD_DIM, 1.0, 0.0).astype(dtype)
    return lo, (1.0 - lo).astype(dtype)


def _flash_kernel(*refs, mode, tk, n_chunks, lam_init):
    if mode == "diff":
        q_ref, k_ref, v_ref, par_ref, o_ref, qt_scr, vxt_scr, m_scr, acc_scr, st_scr = refs
    else:
        q_ref, k_ref, v_ref, o_ref, qt_scr, vxt_scr, m_scr, acc_scr, st_scr = refs
    sub = lax.broadcasted_iota(jnp.int32, (LANES, 1), 0)
    lo_col = jnp.where(sub < HEAD_DIM, 1.0, 0.0)

    @pl.when(pl.program_id(2) == 0)
    def _():
        for c in range(n_chunks):
            vt = v_ref[c * tk:(c + 1) * tk, :].astype(F32).T
            if mode == "diff":
                vxt_scr[c, :LANES, :] = vt.astype(vxt_scr.dtype)
                vxt_scr[c, LANES:, :] = jnp.ones((LANES, tk), vxt_scr.dtype)
            else:
                vxt_scr[0, c] = (vt * lo_col + (1.0 - lo_col)).astype(vxt_scr.dtype)
                vxt_scr[1, c] = (vt * (1.0 - lo_col) + lo_col).astype(vxt_scr.dtype)

    qt = q_ref[...].astype(F32).T
    qt_scr[0] = (qt * lo_col).astype(qt_scr.dtype)
    qt_scr[1] = (qt * (1.0 - lo_col)).astype(qt_scr.dtype)
    m_scr[...] = jnp.full(m_scr.shape, NEG_INF, F32)
    acc_scr[...] = jnp.zeros(acc_scr.shape, F32)

    def qk(c, slot):
        off = pl.multiple_of(c * tk, tk)
        k = k_ref[pl.ds(off, tk), :]
        for h in range(2):
            st_scr[slot, h] = jnp.dot(k, qt_scr[h], preferred_element_type=F32)

    def softmax_pv(c, slot):
        for h in range(2):
            st = st_scr[slot, h]
            m_prev = m_scr[h]
            m_new = jnp.maximum(m_prev, jnp.max(st, axis=0, keepdims=True))
            alpha = jnp.exp2(m_prev - m_new)
            pt = jnp.exp2(st - m_new).astype(vxt_scr.dtype)
            vxt = vxt_scr[c] if mode == "diff" else vxt_scr[h, c]
            acc_scr[h] = alpha * acc_scr[h] + jnp.dot(vxt, pt, preferred_element_type=F32)
            m_scr[h] = m_new

    qk(0, 0)

    def body(j, carry):
        c = 2 * j
        qk(c + 1, 1)
        softmax_pv(c, 0)
        qk(c + 2, 0)
        softmax_pv(c + 1, 1)
        return carry

    lax.fori_loop(0, (n_chunks - 1) // 2, body, 0)
    if n_chunks % 2 == 1:
        softmax_pv(n_chunks - 1, 0)
    else:
        qk(n_chunks - 1, 1)
        softmax_pv(n_chunks - 2, 0)
        softmax_pv(n_chunks - 1, 1)

    a_lo = acc_scr[0]
    a_hi = acc_scr[1]
    if mode == "diff":
        lam = (jnp.exp(jnp.sum(par_ref[0:1, :] * par_ref[1:2, :], axis=1, keepdims=True))
               - jnp.exp(jnp.sum(par_ref[2:3, :] * par_ref[3:4, :], axis=1, keepdims=True))
               + lam_init)
        out_t = a_lo[:LANES] / a_lo[LANES:] - lam * (a_hi[:LANES] / a_hi[LANES:])
        ms = jnp.mean(out_t * out_t, axis=0, keepdims=True)
        out = (out_t * lax.rsqrt(ms + RMS_EPS)).T * par_ref[4:5, :] * (1.0 - lam_init)
    else:
        out = jnp.concatenate([a_lo[:HEAD_DIM] / a_lo[HEAD_DIM:], a_hi[HEAD_DIM:] / a_hi[:HEAD_DIM]], axis=0).T
    o_ref[...] = out.astype(o_ref.dtype)


def _pick_tile(n, candidates):
    for c in candidates:
        if n % c == 0:
            return c
    raise ValueError(f"no tile for {n}")


def _flash(q_arr, k_arr, v_arr, *, n_groups, q_rows, q_off, kv_rows, kv_off, k_col, v_col, mode, par=None,
           lam_init=0.0, name):
    batch = q_arr.shape[0]
    tq = _pick_tile(q_rows, (1024, 512, 256))
    tk = _pick_tile(kv_rows, (768, 512, 384, 256, 128))
    nq, n_chunks = q_rows // tq, kv_rows // tk
    assert q_off % tq == 0 and kv_off % kv_rows == 0
    vw = 2 * LANES if mode == "diff" else LANES
    in_specs = [pl.BlockSpec((None, tq, LANES), lambda b, g, i: (b, q_off // tq + i, g)),
                pl.BlockSpec((None, kv_rows, LANES), lambda b, g, i: (b, kv_off // kv_rows, k_col + g)),
                pl.BlockSpec((None, kv_rows, LANES), lambda b, g, i: (b, kv_off // kv_rows, v_col + g))]
    args = [q_arr, k_arr, v_arr]
    if mode == "diff":
        in_specs.append(pl.BlockSpec(par.shape, lambda b, g, i: (0, 0)))
        args.append(par)
    return pl.pallas_call(
        functools.partial(_flash_kernel, mode=mode, tk=tk, n_chunks=n_chunks, lam_init=lam_init),
        grid=(batch, n_groups, nq),
        in_specs=in_specs,
        out_specs=pl.BlockSpec((None, tq, LANES), lambda b, g, i: (b, i, g)),
        out_shape=jax.ShapeDtypeStruct((batch, q_rows, n_groups * LANES), _MXU),
        scratch_shapes=[pltpu.VMEM((2, LANES, tq), _MXU),
                        pltpu.VMEM((n_chunks, vw, tk) if mode == "diff" else (2, n_chunks, vw, tk), _MXU),
                        pltpu.VMEM((2, 1, tq), F32),
                        pltpu.VMEM((2, vw, tq), F32),
                        pltpu.VMEM((2, 2, tk, tq), F32)],
        compiler_params=_cparams(("parallel", "parallel", "arbitrary")),
        name=name,
    )(*args)


def _seg_attn_kernel(*refs, n_pairs, nseg, bias_segs, band, q_axis, has_init, s_len):
    it = iter(refs)
    q_ref = next(it)
    k_refs = [next(it) for _ in range(nseg)]
    v_refs = [next(it) for _ in range(nseg)]
    bias_ref = next(it) if bias_segs else None
    init_ref = next(it) if has_init else None
    o_ref = next(it)

    tq = q_ref.shape[0]
    is_lo = lax.broadcasted_iota(jnp.int32, (tq, LANES), 1) < HEAD_DIM
    lo, hi = _head_masks(q_ref.dtype)

    masks = [None] * nseg
    if band:
        qi = pl.program_id(q_axis)
        qpos = qi * tq + lax.broadcasted_iota(jnp.int32, (tq, tq), 0)
        for j in range(1, nseg):
            kpos = (qi + j - 2) * tq + lax.broadcasted_iota(jnp.int32, (tq, tq), 1)
            dist = jnp.abs(qpos - kpos)
            ok = jnp.where(kpos >= 0, jnp.where(kpos < s_len, dist, SWA_WINDOW + 1), SWA_WINDOW + 1)
            masks[j] = ok <= SWA_WINDOW

    def one_head(hh):
        p_idx, half = hh // 2, hh % 2
        qm = q_ref[:, p_idx * LANES:(p_idx + 1) * LANES] * (lo if half == 0 else hi)
        ss = []
        boff = 0
        for j in range(nseg):
            s = lax.dot_general(qm, k_refs[j][...], (((1,), (1,)), ((), ())), preferred_element_type=F32)
            if j in bias_segs:
                n = k_refs[j].shape[0]
                s = s + bias_ref[half, :, boff:boff + n]
                boff += n
            if masks[j] is not None:
                s = jnp.where(masks[j], s, NEG_INF)
            ss.append(s)
        m = functools.reduce(jnp.maximum, [jnp.max(s, axis=1, keepdims=True) for s in ss])
        if has_init:
            sink = init_ref[hh:hh + 1, 0:1]
            m = jnp.maximum(m, sink)
            l = jnp.exp2(sink - m)
        else:
            l = jnp.zeros((tq, 1), F32)
        acc = jnp.zeros((tq, LANES), F32)
        for j in range(nseg):
            p = jnp.exp2(ss[j] - m)
            l = l + jnp.sum(p, axis=1, keepdims=True)
            acc = acc + jnp.dot(p.astype(_MXU), v_refs[j][...], preferred_element_type=F32)
        return acc / l

    for p_idx in range(n_pairs):
        out = jnp.where(is_lo, one_head(2 * p_idx), one_head(2 * p_idx + 1))
        o_ref[:, p_idx * LANES:(p_idx + 1) * LANES] = out.astype(o_ref.dtype)


def _seg_attention(q_arr, k_arr, v_arr, *, grid, q_spec, k_specs, v_specs, out_spec, out_shape, n_pairs,
                   bias=None, bias_spec=None, bias_segs=(), band=False, q_axis=0, init=None, s_len=0, sem, name):
    nseg = len(k_specs)
    args = [q_arr] + [k_arr] * nseg + [v_arr] * nseg
    specs = [q_spec] + list(k_specs) + list(v_specs)
    if bias is not None:
        args.append(bias)
        specs.append(bias_spec)
    if init is not None:
        args.append(init)
        specs.append(pl.BlockSpec(init.shape, lambda *_: (0, 0)))
    kern = functools.partial(_seg_attn_kernel, n_pairs=n_pairs, nseg=nseg, bias_segs=tuple(bias_segs), band=band,
                             q_axis=q_axis, has_init=init is not None, s_len=s_len)
    return pl.pallas_call(kern, grid=grid, in_specs=specs, out_specs=out_spec, out_shape=out_shape,
                          compiler_params=_cparams(sem), name=name)(*args)


def _na_bias_tables(rpb, rows):
    nh = rpb.shape[0]
    nkr = 3 * NA_Q_ROWS
    qc = np.arange(GRID_W)[:, None]
    kc = np.arange(GRID_W)[None, :]
    ws = np.clip(qc - NA_KW // 2, 0, GRID_W - NA_KW)
    cvalid = ((kc >= ws) & (kc < ws + NA_KW)).reshape(-1)
    dc = (kc - qc + NA_KW - 1).reshape(-1)
    onehot = ((np.arange(2 * NA_KW - 1)[:, None] == dc[None, :]) & cvalid[None, :]).astype(np.float32)
    tiles = jnp.einsum("hrd,dx->hrx", rpb.astype(F32) * LOG2E, jnp.asarray(onehot),
                       precision=lax.Precision.HIGHEST)
    tiles = jnp.where(cvalid[None, None, :], tiles, NEG_INF)
    qr = np.arange(NA_Q_ROWS)[:, None]
    kr = np.arange(nkr)[None, :]
    tables = []
    for variant in range(3):
        if variant == 0:
            r0, k0, nrows = 0, 0, rows
        elif variant == 1:
            r0, k0, nrows = 2 * NA_Q_ROWS, NA_Q_ROWS, 8 * NA_Q_ROWS
        else:
            r0, k0, nrows = rows - NA_Q_ROWS, rows - nkr, rows
        r = r0 + qr
        rp = k0 + kr
        rs = np.clip(r - NA_KH // 2, 0, nrows - NA_KH)
        rvalid = (rp >= rs) & (rp < rs + NA_KH)
        dr = np.clip(rp - r + NA_KH - 1, 0, 2 * NA_KH - 2)
        t = jnp.take(tiles, jnp.asarray(dr.reshape(-1).astype(np.int32)), axis=1)
        t = jnp.where(rvalid.reshape(-1)[None, :, None], t, NEG_INF)
        t = t.reshape(nh, NA_Q_ROWS, nkr, GRID_W, GRID_W).transpose(0, 1, 3, 2, 4)
        tables.append(t.reshape(nh // 2, 2, NA_Q_ROWS * GRID_W, nkr * GRID_W))
    return jnp.stack(tables)


def _neighbourhood(q_arr, kv_arr, bias, *, s_len, n_ctx):
    batch = q_arr.shape[0]
    tq = NA_Q_ROWS * GRID_W
    nq = s_len // tq
    n_pairs = NA_HEADS // 2
    cb = s_len // n_ctx
    q_spec = pl.BlockSpec((None, tq, LANES), lambda p, i, b: (b, i, p))

    def seg_specs(col):
        specs = [pl.BlockSpec((None, n_ctx, LANES), lambda p, i, b: (b, cb, col + p))]
        for j in range(3):
            specs.append(pl.BlockSpec(
                (None, tq, LANES),
                lambda p, i, b, j=j: (b, jnp.clip(i - 1, 0, nq - 3) + j, col + p)))
        return specs

    bias_spec = pl.BlockSpec((None, None, 2, tq, 3 * tq),
                             lambda p, i, b: (jnp.where(i == 0, 0, jnp.where(i == nq - 1, 2, 1)), p, 0, 0, 0))
    out_spec = pl.BlockSpec((None, tq, LANES), lambda p, i, b: (b, i, p))
    return _seg_attention(
        q_arr, kv_arr, kv_arr, grid=(n_pairs, nq, batch), q_spec=q_spec, k_specs=seg_specs(0),
        v_specs=seg_specs(n_pairs), out_spec=out_spec,
        out_shape=jax.ShapeDtypeStruct((batch, s_len, n_pairs * LANES), _MXU),
        n_pairs=1, bias=bias, bias_spec=bias_spec, bias_segs=(1, 2, 3),
        sem=("parallel", "parallel", "parallel"), name="neighbourhood_attn")


def _windowed_kernel(q_ref, kc_ref, k0_ref, k1_ref, k2_ref, vc_ref, v0_ref, v1_ref, v2_ref, sink_ref, o_ref, *,
                     n_pairs, s_len):
    tq = q_ref.shape[0]
    qi = pl.program_id(1)
    k_refs = (kc_ref, k0_ref, k1_ref, k2_ref)
    v_refs = (vc_ref, v0_ref, v1_ref, v2_ref)
    sub = lax.broadcasted_iota(jnp.int32, (LANES, 1), 0)
    lo_col = jnp.where(sub < HEAD_DIM, 1.0, 0.0)
    qt = jnp.concatenate([q_ref[:, p * LANES:(p + 1) * LANES].astype(F32).T for p in range(n_pairs)], axis=1)
    vts = [v_ref[...].astype(F32).T for v_ref in v_refs]

    qpos = qi * tq + (lax.broadcasted_iota(jnp.int32, (1, n_pairs * LANES), 1) & (LANES - 1))
    masks = [None]
    for j in range(3):
        kpos = (qi + j - 1) * tq + lax.broadcasted_iota(jnp.int32, (tq, 1), 0)
        dist = jnp.abs(qpos - kpos)
        ok = jnp.where(kpos >= 0, jnp.where(kpos < s_len, dist, SWA_WINDOW + 1), SWA_WINDOW + 1)
        masks.append(ok <= SWA_WINDOW)

    halves = []
    for half in range(2):
        col = lo_col if half == 0 else 1.0 - lo_col
        qth = (qt * col).astype(q_ref.dtype)
        sts = []
        for j in range(4):
            st = jnp.dot(k_refs[j][...], qth, preferred_element_type=F32)
            sts.append(st if masks[j] is None else jnp.where(masks[j], st, NEG_INF))
        sink = sink_ref[half:half + 1, :]
        m = functools.reduce(jnp.maximum, [jnp.max(st, axis=0, keepdims=True) for st in sts] + [sink])
        acc = jnp.exp2(sink - m) * (1.0 - col)
        for j in range(4):
            pt = jnp.exp2(sts[j] - m).astype(q_ref.dtype)
            vxt = (vts[j] * col + (1.0 - col)).astype(q_ref.dtype)
            acc = acc + jnp.dot(vxt, pt, preferred_element_type=F32)
        halves.append(acc)
    a_lo, a_hi = halves
    out_t = jnp.concatenate([a_lo[:HEAD_DIM] / a_lo[HEAD_DIM:], a_hi[HEAD_DIM:] / a_hi[:HEAD_DIM]], axis=0)
    for p in range(n_pairs):
        o_ref[:, p * LANES:(p + 1) * LANES] = out_t[:, p * LANES:(p + 1) * LANES].T.astype(o_ref.dtype)


def _windowed(q_arr, k_arr, v_arr, sink, *, s_len, n_ctx):
    batch = q_arr.shape[0]
    tq = SWA_WINDOW
    nb = s_len // tq
    n_pairs = SWA_HEADS // 2
    cb = s_len // n_ctx
    qw = n_pairs * LANES
    q_spec = pl.BlockSpec((None, tq, qw), lambda b, i: (b, i, 0))
    specs = [pl.BlockSpec((None, n_ctx, LANES), lambda b, i: (b, cb, 0))]
    for j in range(3):
        specs.append(pl.BlockSpec((None, tq, LANES),
                                  lambda b, i, j=j: (b, jnp.clip(i - 1 + j, 0, nb - 1), 0)))
    return pl.pallas_call(
        functools.partial(_windowed_kernel, n_pairs=n_pairs, s_len=s_len),
        grid=(batch, nb),
        in_specs=[q_spec] + specs + specs + [pl.BlockSpec(sink.shape, lambda b, i: (0, 0))],
        out_specs=pl.BlockSpec((None, tq, qw), lambda b, i: (b, i, 0)),
        out_shape=jax.ShapeDtypeStruct((batch, s_len, qw), _MXU),
        compiler_params=_cparams(("parallel", "parallel")),
        name="windowed_attn",
    )(q_arr, *([k_arr] * 4), *([v_arr] * 4), sink)


def _layernorm(z, g, b):
    mu = jnp.mean(z, axis=1, keepdims=True)
    zc = z - mu
    var = jnp.mean(zc * zc, axis=1, keepdims=True)
    return zc * lax.rsqrt(var + LN_EPS) * g + b


def _post_kernel(x_ref, ya_ref, yb_ref, wo_ref, mod_ref, g_ref, b_ref, rw_ref, rb_ref,
                 x1_ref, tok_ref, route_ref, *, alpha):
    half = ya_ref.shape[1]
    y = (jnp.dot(ya_ref[...], wo_ref[:half, :], preferred_element_type=F32)
         + jnp.dot(yb_ref[...], wo_ref[half:, :], preferred_element_type=F32))
    x1 = _layernorm(alpha * x_ref[...] + mod_ref[2:3, :] * y, g_ref[...], b_ref[...])
    x1_ref[...] = x1
    tok = x1 * (1.0 + mod_ref[4:5, :]) + mod_ref[3:4, :]
    tok_ref[...] = tok
    logits = jnp.dot(tok.astype(_MXU), rw_ref[...], preferred_element_type=F32) + rb_ref[...]
    tm = logits.shape[0]
    lane = lax.broadcasted_iota(jnp.int32, (tm, LANES), 1).astype(F32)
    vals, idxs = [], []
    for _ in range(TOP_K):
        mx = jnp.max(logits, axis=1, keepdims=True)
        ix = jnp.min(jnp.where(logits == mx, lane, float(LANES)), axis=1, keepdims=True)
        vals.append(mx)
        idxs.append(ix)
        logits = jnp.where(lane == ix, -3.0e38, logits)
    es = [jnp.exp(v - vals[0]) for v in vals]
    den = functools.reduce(lambda a, c: a + c, es)
    route = jnp.zeros((tm, LANES), F32)
    for k in range(TOP_K):
        route = jnp.where(lane == float(k), idxs[k], route)
        route = jnp.where(lane == float(TOP_K + k), es[k] / den, route)
    route_ref[...] = route


def _post(x2d, ya, yb, wo, mods, g, b, rw, rb, *, batch, nbt_in, nblk, n_lat_blk, alpha):
    d = x2d.shape[1]
    tm = ROW_TILE
    half = ya.shape[1]

    def mod_map(bi, t):
        return (jnp.where(t >= n_lat_blk, batch, bi), 0, 0)

    rows_out = batch * nblk * tm
    o_map = lambda bi, t: (bi * nblk + t, 0)
    return pl.pallas_call(
        functools.partial(_post_kernel, alpha=alpha),
        grid=(batch, nblk),
        in_specs=[pl.BlockSpec((tm, d), lambda bi, t: (bi * nbt_in + t, 0)),
                  pl.BlockSpec((tm, half), o_map),
                  pl.BlockSpec((tm, half), o_map),
                  pl.BlockSpec((d, d), lambda bi, t: (0, 0)),
                  pl.BlockSpec((None, 6, d), mod_map),
                  pl.BlockSpec((1, d), lambda bi, t: (0, 0)),
                  pl.BlockSpec((1, d), lambda bi, t: (0, 0)),
                  pl.BlockSpec((d, LANES), lambda bi, t: (0, 0)),
                  pl.BlockSpec((1, LANES), lambda bi, t: (0, 0))],
        out_specs=[pl.BlockSpec((tm, d), o_map), pl.BlockSpec((tm, d), o_map), pl.BlockSpec((tm, LANES), o_map)],
        out_shape=[jax.ShapeDtypeStruct((rows_out, d), F32), jax.ShapeDtypeStruct((rows_out, d), F32),
                   jax.ShapeDtypeStruct((rows_out, LANES), F32)],
        compiler_params=_cparams(("parallel", "parallel")),
        name="post_attn",
    )(x2d, ya, yb, wo, mods, g, b, rw, rb)


def _moe_kernel(be_ref, nu_ref, dst_ref, src0_ref, src1_ref, src2_ref, tok_hbm, wi_ref, bi_ref, wo_ref, bo_ref,
                y_hbm, xbuf0, xbuf1, xbuf2, obuf0, obuf1, obuf2, gsem, ssem, wi_s, wo_s, *, dump0):
    i = pl.program_id(0)
    n_used = nu_ref[0]
    f = wo_s.shape[0]
    xbufs = (xbuf0, xbuf1, xbuf2)
    obufs = (obuf0, obuf1, obuf2)

    def gather_copy(row, r, slot):
        return pltpu.make_async_copy(tok_hbm.at[pl.ds(row, 1), :], xbufs[slot].at[pl.ds(r, 1), :], gsem.at[slot])

    def scatter_copy(row, r, slot):
        return pltpu.make_async_copy(obufs[slot].at[pl.ds(r, 1), :], y_hbm.at[pl.ds(row, 1), :], ssem.at[slot])

    def start_gather(rows_ref, slot):
        for r in range(MOE_ROWS):
            gather_copy(rows_ref[0, r], r, slot).start(priority=r % 2)

    def wait_gather(slot):
        for _ in range(MOE_ROWS):
            gather_copy(0, 0, slot).wait()

    def wait_scatter(slot):
        for _ in range(MOE_ROWS):
            scatter_copy(0, 0, slot).wait()

    @pl.when(i == 0)
    def _():
        for slot in range(MOE_RING):
            obufs[slot][...] = jnp.zeros(obufs[slot].shape, F32)
            for r in range(MOE_ROWS):
                scatter_copy(dump0 + slot * MOE_ROWS + r, r, slot).start(priority=r % 2)
        start_gather(src0_ref, 0)
        start_gather(src1_ref, 1)

    last_used = n_used - 1
    changed = jnp.logical_or(i == 0, be_ref[jnp.clip(i - 1, 0, last_used)] != be_ref[jnp.minimum(i, last_used)])

    for slot in range(MOE_RING):
        @pl.when(jnp.logical_and(i < n_used, i % MOE_RING == slot))
        def _(slot=slot):
            wait_scatter(slot)
            wait_gather(slot)

            @pl.when(changed)
            def _():
                wi_s[...] = wi_ref[...].astype(wi_s.dtype)
                wo_s[...] = wo_ref[...].astype(wo_s.dtype)

            x = xbufs[slot][...].astype(wi_s.dtype)
            start_gather(src2_ref, (slot + 2) % MOE_RING)
            hh = jnp.dot(x, wi_s[...], preferred_element_type=F32) + bi_ref[...]
            gate = jnp.minimum(hh[:, :f], SWIGLU_LIMIT)
            up = jnp.clip(hh[:, f:], -SWIGLU_LIMIT, SWIGLU_LIMIT)
            act = gate * (1.0 / (1.0 + jnp.exp(-SWIGLU_ALPHA * gate))) * (up + 1.0)
            obufs[slot][...] = jnp.dot(act.astype(wo_s.dtype), wo_s[...], preferred_element_type=F32) + bo_ref[...]
            for r in range(MOE_ROWS):
                scatter_copy(dst_ref[0, r], r, slot).start(priority=r % 2)

        @pl.when(jnp.logical_and(i >= n_used, i % MOE_RING == slot))
        def _(slot=slot):
            @pl.when(i < n_used + 2)
            def _():
                wait_gather(slot)

            @pl.when(i < n_used + MOE_RING)
            def _():
                wait_scatter(slot)


def _experts(tok, src_rows, dst_rows, block_e, n_used, w_in, b_in, w_out, b_out, layer):
    n_tok, d = tok.shape
    _, n_e, _, f2 = w_in.shape
    f = f2 // 2
    n_blocks = src_rows.shape[0]

    def blk(i, be, nu):
        return jnp.minimum(i, nu[0] - 1)

    def e_map(i, be, nu):
        return (layer, be[blk(i, be, nu)], 0, 0)

    def rows_spec(ahead):
        return pl.BlockSpec((None, 1, MOE_ROWS), lambda i, be, nu: (blk(i + ahead, be, nu), 0, 0),
                            memory_space=pltpu.SMEM)

    row_buf = pltpu.VMEM((MOE_ROWS, d), F32)
    grid_spec = pltpu.PrefetchScalarGridSpec(
        num_scalar_prefetch=2,
        grid=(n_blocks + MOE_RING,),
        in_specs=[rows_spec(0), rows_spec(0), rows_spec(1), rows_spec(2),
                  pl.BlockSpec(memory_space=pl.ANY),
                  pl.BlockSpec((None, None, d, f2), e_map),
                  pl.BlockSpec((None, None, 1, f2), e_map),
                  pl.BlockSpec((None, None, f, d), e_map),
                  pl.BlockSpec((None, None, 1, d), e_map)],
        out_specs=pl.BlockSpec(memory_space=pl.ANY),
        scratch_shapes=[row_buf] * (2 * MOE_RING)
                       + [pltpu.SemaphoreType.DMA((MOE_RING,)), pltpu.SemaphoreType.DMA((MOE_RING,)),
                          pltpu.VMEM((d, f2), _MXU), pltpu.VMEM((f, d), _MXU)],
    )
    depth = w_in.shape[0]
    return pl.pallas_call(
        functools.partial(_moe_kernel, dump0=TOP_K * n_tok), grid_spec=grid_spec,
        out_shape=jax.ShapeDtypeStruct((TOP_K * n_tok + MOE_RING * MOE_ROWS, d), F32),
        compiler_params=_cparams(("arbitrary",)),
        name="experts",
    )(block_e, n_used, dst_rows, src_rows, src_rows, src_rows, tok, w_in, b_in.reshape(depth, n_e, 1, f2), w_out,
      b_out.reshape(depth, n_e, 1, d))


def _route_plan(route, n_experts):
    n = route.shape[0]
    top_idx = route[:, :TOP_K].astype(jnp.int32)
    nk = n * TOP_K
    flat_e = top_idx.reshape(-1)
    order = jnp.argsort(flat_e).astype(jnp.int32)
    experts = jnp.arange(n_experts, dtype=jnp.int32)
    counts = jnp.sum(flat_e[:, None] == experts[None, :], axis=0, dtype=jnp.int32)
    padded = (counts + MOE_ROWS - 1) // MOE_ROWS * MOE_ROWS
    start = jnp.cumsum(counts) - counts
    pend = jnp.cumsum(padded)
    pstart = pend - padded
    n_blocks = -(-nk // MOE_ROWS) + n_experts
    first = jnp.arange(n_blocks, dtype=jnp.int32) * MOE_ROWS
    block_e = jnp.minimum(jnp.sum(pend[None, :] <= first[:, None], axis=1, dtype=jnp.int32), n_experts - 1)
    n_used = (pend[-1:] // MOE_ROWS).astype(jnp.int32)
    blk = jnp.arange(n_blocks, dtype=jnp.int32)[:, None]
    row = jnp.arange(MOE_ROWS, dtype=jnp.int32)[None, :]
    rank = blk * MOE_ROWS + row - jnp.take(pstart, block_e, mode="clip")[:, None]
    src = jnp.clip(jnp.take(start, block_e, mode="clip")[:, None] + rank, 0, nk - 1)
    valid = rank < jnp.take(counts, block_e, mode="clip")[:, None]
    pair = jnp.take(order, src, mode="clip")
    token = pair // TOP_K
    src_rows = jnp.where(valid, token, 0)
    dst_rows = jnp.where(valid, (pair % TOP_K) * n + token, nk + (blk % MOE_RING) * MOE_ROWS + row)
    return (src_rows.reshape(n_blocks, 1, MOE_ROWS), dst_rows.reshape(n_blocks, 1, MOE_ROWS), block_e, n_used)


def _ln2_kernel(x_ref, y0_ref, y1_ref, y2_ref, y3_ref, route_ref, mod_ref, g_ref, b_ref, o_ref, *, alpha):
    ys = (y0_ref, y1_ref, y2_ref, y3_ref)
    f = route_ref[:, TOP_K:TOP_K + 1] * ys[0][...]
    for k in range(1, TOP_K):
        f = f + route_ref[:, TOP_K + k:TOP_K + k + 1] * ys[k][...]
    o_ref[...] = _layernorm(alpha * x_ref[...] + mod_ref[5:6, :] * f, g_ref[...], b_ref[...])


def _ln2(x1, y, route, mods, g, b, *, batch, nblk, n_lat_blk, alpha):
    rows, d = x1.shape
    tm = ROW_TILE
    r_map = lambda bi, t: (bi * nblk + t, 0)

    def y_spec(k):
        return pl.BlockSpec((tm, d), lambda bi, t: (k * (rows // tm) + bi * nblk + t, 0))

    def mod_map(bi, t):
        return (jnp.where(t >= n_lat_blk, batch, bi), 0, 0)

    return pl.pallas_call(
        functools.partial(_ln2_kernel, alpha=alpha),
        grid=(batch, nblk),
        in_specs=[pl.BlockSpec((tm, d), r_map)] + [y_spec(k) for k in range(TOP_K)]
                 + [pl.BlockSpec((tm, LANES), r_map), pl.BlockSpec((None, 6, d), mod_map),
                    pl.BlockSpec((1, d), lambda bi, t: (0, 0)), pl.BlockSpec((1, d), lambda bi, t: (0, 0))],
        out_specs=pl.BlockSpec((tm, d), r_map),
        out_shape=jax.ShapeDtypeStruct((rows, d), F32),
        compiler_params=_cparams(("parallel", "parallel")),
        name="combine_ln2",
    )(x1, y, y, y, y, route, mods, g, b)


def _rope_layout(w, n_heads):
    lead = w.shape[:-1]
    return w.reshape(lead + (n_heads, HEAD_DIM // 2, 2)).swapaxes(-1, -2).reshape(lead + (n_heads * HEAD_DIM,))


def _pair_layout(w, n_heads, axis=-1):
    axis = axis % w.ndim
    shape = w.shape
    w = w.reshape(shape[:axis] + (2, n_heads // 2, HEAD_DIM) + shape[axis + 1:])
    return w.swapaxes(axis, axis + 1).reshape(shape)


def _rope_tables(s_len, n_ctx):
    t = np.arange(s_len)
    row = (t // GRID_W).astype(np.float32)
    col = (t % GRID_W).astype(np.float32)
    axis_dim = HEAD_DIM // 2
    freqs = jnp.asarray(ROPE_THETA, F32) ** (-jnp.arange(0, axis_dim, 2, dtype=F32) / axis_dim)
    ang = jnp.concatenate([jnp.asarray(row)[:, None] * freqs, jnp.asarray(col)[:, None] * freqs], axis=-1)
    cos, sin = jnp.cos(ang), jnp.sin(ang)
    cos_h = jnp.concatenate([cos, cos], axis=-1)
    sin_h = jnp.concatenate([-sin, sin], axis=-1)
    cos_t = jnp.concatenate([cos_h, jnp.ones((n_ctx, HEAD_DIM), F32)], axis=0)
    sin_t = jnp.concatenate([sin_h, jnp.zeros((n_ctx, HEAD_DIM), F32)], axis=0)
    return jnp.tile(cos_t, (1, 2)), jnp.tile(sin_t, (1, 2))


def _moe_and_ln2(x1, tok, route, mods, n_experts, w_in, b_in, w_out, b_out, layer, g, b, *,
                 batch, nblk, n_lat_blk, alpha):
    src_rows, dst_rows, block_e, n_used = _route_plan(route, n_experts)
    y = _experts(tok, src_rows, dst_rows, block_e, n_used, w_in, b_in, w_out, b_out, layer)
    return _ln2(x1, y, route, mods, g, b, batch=batch, nblk=nblk, n_lat_blk=n_lat_blk, alpha=alpha)


def kernel(x, c, ctx, c_ctx, mod_w, mod_b, ln1_g, ln1_b, ln2_g, ln2_b, router_w, router_b, moe_w_in, moe_b_in,
           moe_w_out, moe_b_out, ab_w_in, ab_w_out, na_rpb, diff_lq1, diff_lk1, diff_lq2, diff_lk2, diff_subln,
           cd_w_in, cd_w_out, gqa_q_norm, gqa_k_norm, swa_sink):
    batch, s_len, d = x.shape
    n_ctx = ctx.shape[1]
    t_len = n_ctx + s_len
    depth = mod_w.shape[0]
    n_experts = router_w.shape[2]
    alpha = (2.0 * depth) ** 0.25
    tm = ROW_TILE
    nbt = t_len // tm
    nbs = s_len // tm
    assert depth == 2 and n_ctx % tm == 0 and s_len % tm == 0 and batch + 1 <= 8

    cc = jnp.zeros((8, d), F32).at[:batch].set(c).at[batch].set(c_ctx)
    mod_all = _modulation(cc, mod_w, mod_b)
    mods = [mod_all[l, :batch + 1].reshape(batch + 1, 6, d) for l in range(depth)]
    cos_t, sin_t = _rope_tables(s_len, n_ctx)
    rw = [jnp.zeros((d, LANES), F32).at[:, :n_experts].set(router_w[l]).astype(_MXU) for l in range(depth)]
    rb = [jnp.full((1, LANES), NEG_INF, F32).at[0, :n_experts].set(router_b[l]) for l in range(depth)]

    stream = jnp.concatenate([x, ctx], axis=1).reshape(batch * t_len, d)
    by_batch = lambda a: a.reshape(batch, t_len, a.shape[-1])

    na_w = NA_HEADS * HEAD_DIM
    df_w = DIFF_HEADS * 2 * HEAD_DIM
    wab = ab_w_in[0]
    w0 = jnp.concatenate([wab[:, :3 * na_w], _rope_layout(wab[:, 3 * na_w:3 * na_w + df_w], 2 * DIFF_HEADS),
                          _rope_layout(wab[:, 3 * na_w + df_w:3 * na_w + 2 * df_w], 2 * DIFF_HEADS),
                          wab[:, 3 * na_w + 2 * df_w:]], axis=1).astype(_MXU)
    plan0 = ((0, na_w, None, False, QSCALE), (na_w, 2 * na_w, None, False, 1.0),
             (3 * na_w, df_w, None, True, QSCALE), (3 * na_w + df_w, df_w, None, True, 1.0),
             (3 * na_w + 2 * df_w, df_w, None, False, 1.0))
    nw0 = jnp.zeros((8, LANES), F32)
    nq, nkv, dq, dk, dv = map(by_batch, _in_proj(stream, mods[0], w0, cos_t, sin_t, nw0, plan0,
                                                 batch=batch, n_ctx=n_ctx))
    lam_init0 = 0.8 - 0.6 * math.exp(-0.3 * 0)
    par = jnp.zeros((8, LANES), F32)
    par = par.at[0, :HEAD_DIM].set(diff_lq1[0]).at[1, :HEAD_DIM].set(diff_lk1[0])
    par = par.at[2, :HEAD_DIM].set(diff_lq2[0]).at[3, :HEAD_DIM].set(diff_lk2[0]).at[4].set(diff_subln[0])
    bias = _na_bias_tables(na_rpb[0], s_len // GRID_W)
    y_na = _neighbourhood(nq, nkv, bias, s_len=s_len, n_ctx=n_ctx)
    y_na_c = _flash(nq, nkv, nkv, n_groups=NA_HEADS // 2, q_rows=n_ctx, q_off=s_len, kv_rows=n_ctx,
                    kv_off=s_len, k_col=0,
                    v_col=NA_HEADS // 2, mode="pair", name="neighbourhood_attn_ctx")
    y_df = _flash(dq, dk, dv, n_groups=DIFF_HEADS, q_rows=s_len, q_off=0, kv_rows=t_len, kv_off=0, k_col=0, v_col=0,
                  mode="diff", par=par, lam_init=lam_init0, name="diff_attn")
    y_df_c = _flash(dq, dk, dv, n_groups=DIFF_HEADS, q_rows=n_ctx, q_off=s_len, kv_rows=n_ctx, kv_off=s_len,
                    k_col=0, v_col=0,
                    mode="diff", par=par, lam_init=lam_init0, name="diff_attn_ctx")
    ya = jnp.concatenate([y_na, y_na_c], axis=1).reshape(batch * t_len, -1)
    yb = jnp.concatenate([y_df, y_df_c], axis=1).reshape(batch * t_len, -1)
    x1, tok, route = _post(stream, ya, yb, ab_w_out[0].astype(_MXU), mods[0], ln1_g[0][None], ln1_b[0][None],
                           rw[0], rb[0], batch=batch, nbt_in=nbt, nblk=nbt, n_lat_blk=nbs, alpha=alpha)
    stream = _moe_and_ln2(x1, tok, route, mods[0], n_experts, moe_w_in, moe_b_in, moe_w_out, moe_b_out, 0,
                          ln2_g[0][None], ln2_b[0][None], batch=batch, nblk=nbt, n_lat_blk=nbs, alpha=alpha)

    hw = GQA_HEADS * HEAD_DIM
    kw = GQA_KV_HEADS * HEAD_DIM
    wcd = cd_w_in[0]
    gq, gk, gv, wq, wk, wv = jnp.split(wcd, [hw, hw + kw, hw + 2 * kw, 2 * hw + 2 * kw, 2 * hw + 3 * kw], axis=1)
    w1 = jnp.concatenate([_pair_layout(_rope_layout(gq, GQA_HEADS), GQA_HEADS),
                          _pair_layout(_rope_layout(wq, SWA_HEADS), SWA_HEADS),
                          _rope_layout(gk, GQA_KV_HEADS), gv, _rope_layout(wk, SWA_KV_HEADS), wv],
                         axis=1).astype(_MXU)
    plan1 = ((0, hw, 0, True, QSCALE), (hw, hw, None, True, QSCALE), (2 * hw, kw, 1, True, 1.0),
             (2 * hw + kw, kw, None, False, 1.0), (2 * hw + 2 * kw, kw, None, True, 1.0),
             (2 * hw + 3 * kw, kw, None, False, 1.0))
    nw1 = jnp.zeros((8, LANES), F32).at[0].set(jnp.tile(_rope_layout(gqa_q_norm[0], 1), 2))
    nw1 = nw1.at[1].set(jnp.tile(_rope_layout(gqa_k_norm[0], 1), 2))
    gq, wq, gk, gv, wk, wv = map(by_batch, _in_proj(stream, mods[1], w1, cos_t, sin_t, nw1, plan1,
                                                    batch=batch, n_ctx=n_ctx))
    n_qp = GQA_HEADS // 2
    y_c = _flash(gq.reshape(batch, t_len * n_qp, LANES), gk, gv, n_groups=1, q_rows=s_len * n_qp,
                 q_off=0, kv_rows=t_len, kv_off=0, k_col=0, v_col=0, mode="pair", name="gqa_attn")
    sink = jnp.repeat(swa_sink[0].reshape(2, SWA_HEADS // 2) * LOG2E, LANES, axis=1).astype(F32)
    y_d = _windowed(wq, wk, wv, sink, s_len=s_len, n_ctx=n_ctx)
    wo1 = jnp.concatenate([_pair_layout(cd_w_out[0][:hw], GQA_HEADS, axis=0),
                           _pair_layout(cd_w_out[0][hw:], SWA_HEADS, axis=0)], axis=0).astype(_MXU)
    x1, tok, route = _post(stream, y_c.reshape(batch * s_len, -1), y_d.reshape(batch * s_len, -1), wo1, mods[1],
                           ln1_g[1][None], ln1_b[1][None], rw[1], rb[1], batch=batch, nbt_in=nbt,
                           nblk=nbs, n_lat_blk=nbs, alpha=alpha)
    out = _moe_and_ln2(x1, tok, route, mods[1], n_experts, moe_w_in, moe_b_in, moe_w_out, moe_b_out, 1,
                       ln2_g[1][None], ln2_b[1][None], batch=batch, nblk=nbs, n_lat_blk=nbs, alpha=alpha)
    return out.reshape(batch, s_len, d)
```

```python
import functools
import math

import numpy as np
import jax
import jax.numpy as jnp
from jax import lax
from jax.experimental import pallas as pl
from jax.experimental.pallas import tpu as pltpu

F32 = jnp.float32
_MXU = jnp.bfloat16

HEAD_DIM = 64
GRID_W = 64
LOG2E = math.log2(math.e)
QSCALE = HEAD_DIM ** -0.5 * LOG2E
ROPE_THETA = 10000.0
NA_HEADS = 8
NA_KH = 8
NA_KW = 16
DIFF_HEADS = 4
GQA_HEADS = 8
GQA_KV_HEADS = 2
SWA_HEADS = 8
SWA_KV_HEADS = 2
SWA_WINDOW = 128
TOP_K = 4
SWIGLU_LIMIT = 7.0
SWIGLU_ALPHA = 1.702
LN_EPS = 1e-5
RMS_EPS = 1e-6
NEG_INF = -1e30

LANES = 128
ROW_TILE = 256
MOE_ROWS = 256
MOE_RING = 3
NA_Q_ROWS = 4
VMEM_LIMIT = 52 * 1024 * 1024


def _cparams(sem, vmem=VMEM_LIMIT):
    return pltpu.CompilerParams(dimension_semantics=sem, vmem_limit_bytes=vmem)


def _mod_kernel(c_ref, w_ref, b_ref, o_ref):
    c = c_ref[...]
    a = (c / (1.0 + jnp.exp(-c))).astype(_MXU)
    o_ref[...] = jnp.dot(a, w_ref[...].astype(_MXU), preferred_element_type=F32) + b_ref[...]


def _modulation(cc, mod_w, mod_b):
    depth, d, d6 = mod_w.shape
    tn = d6 // 4
    return pl.pallas_call(
        _mod_kernel,
        grid=(depth, d6 // tn),
        in_specs=[pl.BlockSpec((8, d), lambda l, j: (0, 0)),
                  pl.BlockSpec((None, d, tn), lambda l, j: (l, 0, j)),
                  pl.BlockSpec((None, 1, tn), lambda l, j: (l, 0, j))],
        out_specs=pl.BlockSpec((None, 8, tn), lambda l, j: (l, 0, j)),
        out_shape=jax.ShapeDtypeStruct((depth, 8, d6), F32),
        compiler_params=_cparams(("arbitrary", "arbitrary")),
        name="modulation",
    )(cc, mod_w, mod_b.reshape(depth, 1, d6))


def _in_kernel(x_ref, mod_ref, w_ref, cos_ref, sin_ref, nw_ref, *o_refs, plan):
    x = x_ref[...]
    h = (x * (1.0 + mod_ref[1:2, :]) + mod_ref[0:1, :]).astype(_MXU)
    tm = x.shape[0]
    lane = lax.broadcasted_iota(jnp.int32, (tm, LANES), 1)
    first_half = (lane & (HEAD_DIM // 2)) == 0
    gi = lax.broadcasted_iota(jnp.int32, (LANES, LANES), 0) // HEAD_DIM
    gj = lax.broadcasted_iota(jnp.int32, (LANES, LANES), 1) // HEAD_DIM
    seg = jnp.where(gi == gj, 1.0, 0.0).astype(_MXU)
    cos = cos_ref[...]
    sin = sin_ref[...]
    for o_ref, (c0, width, norm_row, rope, scale) in zip(o_refs, plan):
        acc = jnp.dot(h, w_ref[:, c0:c0 + width], preferred_element_type=F32)
        for j in range(width // LANES):
            a = acc[:, j * LANES:(j + 1) * LANES]
            if norm_row is not None:
                a2 = a * a
                hi = a2.astype(_MXU)
                lo = (a2 - hi.astype(F32)).astype(_MXU)
                ss = (jnp.dot(hi, seg, preferred_element_type=F32)
                      + jnp.dot(lo, seg, preferred_element_type=F32))
                a = a * lax.rsqrt(ss * (1.0 / HEAD_DIM) + RMS_EPS) * nw_ref[norm_row:norm_row + 1, :]
            if rope:
                partner = jnp.where(first_half, pltpu.roll(a, LANES - HEAD_DIM // 2, 1),
                                    pltpu.roll(a, HEAD_DIM // 2, 1))
                a = a * cos + partner * sin
            if scale != 1.0:
                a = a * scale
            o_ref[:, j * LANES:(j + 1) * LANES] = a.astype(o_ref.dtype)


def _in_proj(x2d, mods, w, cos_t, sin_t, nw, plan, *, batch, n_ctx):
    rows, d = x2d.shape
    ncols = w.shape[1]
    tm = ROW_TILE
    nbt = rows // batch // tm
    n_lat_blk = nbt - n_ctx // tm

    def mod_map(i):
        return (jnp.where(i % nbt >= n_lat_blk, batch, i // nbt), 0, 0)

    return pl.pallas_call(
        functools.partial(_in_kernel, plan=plan),
        grid=(rows // tm,),
        in_specs=[pl.BlockSpec((tm, d), lambda i: (i, 0)),
                  pl.BlockSpec((None, 6, d), mod_map),
                  pl.BlockSpec((d, ncols), lambda i: (0, 0)),
                  pl.BlockSpec((tm, LANES), lambda i: (i % nbt, 0)),
                  pl.BlockSpec((tm, LANES), lambda i: (i % nbt, 0)),
                  pl.BlockSpec((8, LANES), lambda i: (0, 0))],
        out_specs=[pl.BlockSpec((tm, p[1]), lambda i: (i, 0)) for p in plan],
        out_shape=[jax.ShapeDtypeStruct((rows, p[1]), _MXU) for p in plan],
        compiler_params=_cparams(("parallel",)),
        name="in_proj",
    )(x2d, mods, w, cos_t, sin_t, nw)


def _flash_kernel(*refs, mode, tk, n_chunks, lam_init):
    if mode == "diff":
        q_ref, k_ref, v_ref, par_ref, o_ref, qt_scr, vxt_scr, m_scr, acc_scr, st_scr = refs
    else:
        q_ref, k_ref, v_ref, o_ref, qt_scr, vxt_scr, m_scr, acc_scr, st_scr = refs
    lo_col = _head_col()

    @pl.when(pl.program_id(2) == 0)
    def _():
        for c in range(n_chunks):
            vt = v_ref[c * tk:(c + 1) * tk, :].astype(F32).T
            if mode == "diff":
                vxt_scr[c, :LANES, :] = vt.astype(vxt_scr.dtype)
                vxt_scr[c, LANES:, :] = jnp.ones((LANES, tk), vxt_scr.dtype)
            else:
                vxt_scr[0, c] = (vt * lo_col + (1.0 - lo_col)).astype(vxt_scr.dtype)
                vxt_scr[1, c] = (vt * (1.0 - lo_col) + lo_col).astype(vxt_scr.dtype)

    qt = q_ref[...].astype(F32).T
    qt_scr[0] = (qt * lo_col).astype(qt_scr.dtype)
    qt_scr[1] = (qt * (1.0 - lo_col)).astype(qt_scr.dtype)
    m_scr[...] = jnp.full(m_scr.shape, NEG_INF, F32)
    acc_scr[...] = jnp.zeros(acc_scr.shape, F32)

    def qk(c, slot):
        off = pl.multiple_of(c * tk, tk)
        k = k_ref[pl.ds(off, tk), :]
        for h in range(2):
            st_scr[slot, h] = jnp.dot(k, qt_scr[h], preferred_element_type=F32)

    def softmax_pv(c, slot):
        for h in range(2):
            st = st_scr[slot, h]
            m_prev = m_scr[h]
            m_new = jnp.maximum(m_prev, jnp.max(st, axis=0, keepdims=True))
            alpha = jnp.exp2(m_prev - m_new)
            pt = jnp.exp2(st - m_new).astype(vxt_scr.dtype)
            vxt = vxt_scr[c] if mode == "diff" else vxt_scr[h, c]
            acc_scr[h] = alpha * acc_scr[h] + jnp.dot(vxt, pt, preferred_element_type=F32)
            m_scr[h] = m_new

    qk(0, 0)

    def body(j, carry):
        c = 2 * j
        qk(c + 1, 1)
        softmax_pv(c, 0)
        qk(c + 2, 0)
        softmax_pv(c + 1, 1)
        return carry

    lax.fori_loop(0, (n_chunks - 1) // 2, body, 0)
    if n_chunks % 2 == 1:
        softmax_pv(n_chunks - 1, 0)
    else:
        qk(n_chunks - 1, 1)
        softmax_pv(n_chunks - 2, 0)
        softmax_pv(n_chunks - 1, 1)

    a_lo = acc_scr[0]
    a_hi = acc_scr[1]
    if mode == "diff":
        lam = (jnp.exp(jnp.sum(par_ref[0:1, :] * par_ref[1:2, :], axis=1, keepdims=True))
               - jnp.exp(jnp.sum(par_ref[2:3, :] * par_ref[3:4, :], axis=1, keepdims=True))
               + lam_init)
        out_t = a_lo[:LANES] / a_lo[LANES:] - lam * (a_hi[:LANES] / a_hi[LANES:])
        ms = jnp.mean(out_t * out_t, axis=0, keepdims=True)
        out = (out_t * lax.rsqrt(ms + RMS_EPS)).T * par_ref[4:5, :] * (1.0 - lam_init)
    else:
        out = _pair_out_t(a_lo, a_hi).T
    o_ref[...] = out.astype(o_ref.dtype)


def _pick_tile(n, candidates):
    for c in candidates:
        if n % c == 0:
            return c
    raise ValueError(f"no tile for {n}")


def _flash(q_arr, k_arr, v_arr, *, n_groups, q_rows, q_off, kv_rows, kv_off, k_col, v_col, mode, par=None,
           lam_init=0.0, name):
    batch = q_arr.shape[0]
    tq = _pick_tile(q_rows, (1024, 512, 256))
    tk = _pick_tile(kv_rows, (768, 512, 384, 256, 128))
    nq, n_chunks = q_rows // tq, kv_rows // tk
    assert q_off % tq == 0 and kv_off % kv_rows == 0
    vw = 2 * LANES if mode == "diff" else LANES
    in_specs = [pl.BlockSpec((None, tq, LANES), lambda b, g, i: (b, q_off // tq + i, g)),
                pl.BlockSpec((None, kv_rows, LANES), lambda b, g, i: (b, kv_off // kv_rows, k_col + g)),
                pl.BlockSpec((None, kv_rows, LANES), lambda b, g, i: (b, kv_off // kv_rows, v_col + g))]
    args = [q_arr, k_arr, v_arr]
    if mode == "diff":
        in_specs.append(pl.BlockSpec(par.shape, lambda b, g, i: (0, 0)))
        args.append(par)
    return pl.pallas_call(
        functools.partial(_flash_kernel, mode=mode, tk=tk, n_chunks=n_chunks, lam_init=lam_init),
        grid=(batch, n_groups, nq),
        in_specs=in_specs,
        out_specs=pl.BlockSpec((None, tq, LANES), lambda b, g, i: (b, i, g)),
        out_shape=jax.ShapeDtypeStruct((batch, q_rows, n_groups * LANES), _MXU),
        scratch_shapes=[pltpu.VMEM((2, LANES, tq), _MXU),
                        pltpu.VMEM((n_chunks, vw, tk) if mode == "diff" else (2, n_chunks, vw, tk), _MXU),
                        pltpu.VMEM((2, 1, tq), F32),
                        pltpu.VMEM((2, vw, tq), F32),
                        pltpu.VMEM((2, 2, tk, tq), F32)],
        compiler_params=_cparams(("parallel", "parallel", "arbitrary")),
        name=name,
    )(*args)


def _head_col():
    sub = lax.broadcasted_iota(jnp.int32, (LANES, 1), 0)
    return jnp.where(sub < HEAD_DIM, 1.0, 0.0)


def _pair_out_t(a_lo, a_hi):
    return jnp.concatenate([a_lo[:HEAD_DIM] / a_lo[HEAD_DIM:], a_hi[HEAD_DIM:] / a_hi[:HEAD_DIM]], axis=0)


def _na_kernel(q_ref, kc_ref, k0_ref, k1_ref, k2_ref, vc_ref, v0_ref, v1_ref, v2_ref, bias_ref, o_ref):
    tq = q_ref.shape[0]
    k_refs = (kc_ref, k0_ref, k1_ref, k2_ref)
    lo_col = _head_col()
    qt = q_ref[...].astype(F32).T
    vts = [v_ref[...].astype(F32).T for v_ref in (vc_ref, v0_ref, v1_ref, v2_ref)]
    halves = []
    for half in range(2):
        col = lo_col if half == 0 else 1.0 - lo_col
        qth = (qt * col).astype(q_ref.dtype)
        sts = [jnp.dot(kc_ref[...], qth, preferred_element_type=F32)]
        for j in range(1, 4):
            st = jnp.dot(k_refs[j][...], qth, preferred_element_type=F32)
            sts.append(st + bias_ref[half, (j - 1) * tq:j * tq, :])
        m = functools.reduce(jnp.maximum, [jnp.max(st, axis=0, keepdims=True) for st in sts])
        acc = jnp.zeros((LANES, tq), F32)
        for j in range(4):
            pt = jnp.exp2(sts[j] - m).astype(q_ref.dtype)
            vxt = (vts[j] * col + (1.0 - col)).astype(q_ref.dtype)
            acc = acc + jnp.dot(vxt, pt, preferred_element_type=F32)
        halves.append(acc)
    o_ref[...] = _pair_out_t(*halves).T.astype(o_ref.dtype)


def _na_bias_tables(rpb, rows):
    nh = rpb.shape[0]
    nkr = 3 * NA_Q_ROWS
    qc = np.arange(GRID_W)[:, None]
    kc = np.arange(GRID_W)[None, :]
    ws = np.clip(qc - NA_KW // 2, 0, GRID_W - NA_KW)
    cvalid = ((kc >= ws) & (kc < ws + NA_KW)).reshape(-1)
    dc = (kc - qc + NA_KW - 1).reshape(-1)
    onehot = ((np.arange(2 * NA_KW - 1)[:, None] == dc[None, :]) & cvalid[None, :]).astype(np.float32)
    tiles = jnp.einsum("hrd,dx->hrx", rpb.astype(F32) * LOG2E, jnp.asarray(onehot),
                       precision=lax.Precision.HIGHEST)
    tiles = jnp.where(cvalid[None, None, :], tiles, NEG_INF)
    qr = np.arange(NA_Q_ROWS)[:, None]
    kr = np.arange(nkr)[None, :]
    tables = []
    for variant in range(3):
        if variant == 0:
            r0, k0, nrows = 0, 0, rows
        elif variant == 1:
            r0, k0, nrows = 2 * NA_Q_ROWS, NA_Q_ROWS, 8 * NA_Q_ROWS
        else:
            r0, k0, nrows = rows - NA_Q_ROWS, rows - nkr, rows
        r = r0 + qr
        rp = k0 + kr
        rs = np.clip(r - NA_KH // 2, 0, nrows - NA_KH)
        rvalid = (rp >= rs) & (rp < rs + NA_KH)
        dr = np.clip(rp - r + NA_KH - 1, 0, 2 * NA_KH - 2)
        t = jnp.take(tiles, jnp.asarray(dr.reshape(-1).astype(np.int32)), axis=1)
        t = jnp.where(rvalid.reshape(-1)[None, :, None], t, NEG_INF)
        t = t.reshape(nh, NA_Q_ROWS, nkr, GRID_W, GRID_W).transpose(0, 2, 4, 1, 3)
        tables.append(t.reshape(nh // 2, 2, nkr * GRID_W, NA_Q_ROWS * GRID_W))
    return jnp.stack(tables)


def _neighbourhood(q_arr, kv_arr, bias, *, s_len, n_ctx):
    batch = q_arr.shape[0]
    tq = NA_Q_ROWS * GRID_W
    nq = s_len // tq
    n_pairs = NA_HEADS // 2
    cb = s_len // n_ctx
    q_spec = pl.BlockSpec((None, tq, LANES), lambda p, i, b: (b, i, p))

    def seg_specs(col):
        specs = [pl.BlockSpec((None, n_ctx, LANES), lambda p, i, b: (b, cb, col + p))]
        for j in range(3):
            specs.append(pl.BlockSpec(
                (None, tq, LANES),
                lambda p, i, b, j=j: (b, jnp.clip(i - 1, 0, nq - 3) + j, col + p)))
        return specs

    bias_spec = pl.BlockSpec((None, None, 2, 3 * tq, tq),
                             lambda p, i, b: (jnp.where(i == 0, 0, jnp.where(i == nq - 1, 2, 1)), p, 0, 0, 0))
    return pl.pallas_call(
        _na_kernel,
        grid=(n_pairs, nq, batch),
        in_specs=[q_spec] + seg_specs(0) + seg_specs(n_pairs) + [bias_spec],
        out_specs=pl.BlockSpec((None, tq, LANES), lambda p, i, b: (b, i, p)),
        out_shape=jax.ShapeDtypeStruct((batch, s_len, n_pairs * LANES), _MXU),
        compiler_params=_cparams(("parallel", "parallel", "parallel")),
        name="neighbourhood_attn",
    )(q_arr, *([kv_arr] * 8), bias)


def _windowed_kernel(q_ref, kc_ref, k0_ref, k1_ref, k2_ref, vc_ref, v0_ref, v1_ref, v2_ref, sink_ref, o_ref, *,
                     n_pairs, s_len):
    tq = q_ref.shape[0]
    qi = pl.program_id(1)
    k_refs = (kc_ref, k0_ref, k1_ref, k2_ref)
    v_refs = (vc_ref, v0_ref, v1_ref, v2_ref)
    lo_col = _head_col()
    qt = jnp.concatenate([q_ref[:, p * LANES:(p + 1) * LANES].astype(F32).T for p in range(n_pairs)], axis=1)
    vts = [v_ref[...].astype(F32).T for v_ref in v_refs]

    qpos = qi * tq + (lax.broadcasted_iota(jnp.int32, (1, n_pairs * LANES), 1) & (LANES - 1))
    masks = [None]
    for j in range(3):
        kpos = (qi + j - 1) * tq + lax.broadcasted_iota(jnp.int32, (tq, 1), 0)
        dist = jnp.abs(qpos - kpos)
        ok = jnp.where(kpos >= 0, jnp.where(kpos < s_len, dist, SWA_WINDOW + 1), SWA_WINDOW + 1)
        masks.append(ok <= SWA_WINDOW)

    halves = []
    for half in range(2):
        col = lo_col if half == 0 else 1.0 - lo_col
        qth = (qt * col).astype(q_ref.dtype)
        sts = []
        for j in range(4):
            st = jnp.dot(k_refs[j][...], qth, preferred_element_type=F32)
            sts.append(st if masks[j] is None else jnp.where(masks[j], st, NEG_INF))
        sink = sink_ref[half:half + 1, :]
        m = functools.reduce(jnp.maximum, [jnp.max(st, axis=0, keepdims=True) for st in sts] + [sink])
        acc = jnp.exp2(sink - m) * (1.0 - col)
        for j in range(4):
            pt = jnp.exp2(sts[j] - m).astype(q_ref.dtype)
            vxt = (vts[j] * col + (1.0 - col)).astype(q_ref.dtype)
            acc = acc + jnp.dot(vxt, pt, preferred_element_type=F32)
        halves.append(acc)
    out_t = _pair_out_t(*halves)
    for p in range(n_pairs):
        o_ref[:, p * LANES:(p + 1) * LANES] = out_t[:, p * LANES:(p + 1) * LANES].T.astype(o_ref.dtype)


def _windowed(q_arr, k_arr, v_arr, sink, *, s_len, n_ctx):
    batch = q_arr.shape[0]
    tq = SWA_WINDOW
    nb = s_len // tq
    n_pairs = SWA_HEADS // 2
    cb = s_len // n_ctx
    qw = n_pairs * LANES
    q_spec = pl.BlockSpec((None, tq, qw), lambda b, i: (b, i, 0))
    specs = [pl.BlockSpec((None, n_ctx, LANES), lambda b, i: (b, cb, 0))]
    for j in range(3):
        specs.append(pl.BlockSpec((None, tq, LANES),
                                  lambda b, i, j=j: (b, jnp.clip(i - 1 + j, 0, nb - 1), 0)))
    return pl.pallas_call(
        functools.partial(_windowed_kernel, n_pairs=n_pairs, s_len=s_len),
        grid=(batch, nb),
        in_specs=[q_spec] + specs + specs + [pl.BlockSpec(sink.shape, lambda b, i: (0, 0))],
        out_specs=pl.BlockSpec((None, tq, qw), lambda b, i: (b, i, 0)),
        out_shape=jax.ShapeDtypeStruct((batch, s_len, qw), _MXU),
        compiler_params=_cparams(("parallel", "parallel")),
        name="windowed_attn",
    )(q_arr, *([k_arr] * 4), *([v_arr] * 4), sink)


def _layernorm(z, g, b):
    mu = jnp.mean(z, axis=1, keepdims=True)
    zc = z - mu
    var = jnp.mean(zc * zc, axis=1, keepdims=True)
    return zc * lax.rsqrt(var + LN_EPS) * g + b


def _post_kernel(x_ref, ya_ref, yb_ref, wo_ref, mod_ref, g_ref, b_ref, rw_ref, rb_ref,
                 x1_ref, tok_ref, route_ref, *, alpha):
    half = ya_ref.shape[1]
    y = (jnp.dot(ya_ref[...], wo_ref[:half, :], preferred_element_type=F32)
         + jnp.dot(yb_ref[...], wo_ref[half:, :], preferred_element_type=F32))
    x1 = _layernorm(alpha * x_ref[...] + mod_ref[2:3, :] * y, g_ref[...], b_ref[...])
    x1_ref[...] = x1
    tok = x1 * (1.0 + mod_ref[4:5, :]) + mod_ref[3:4, :]
    tok_ref[...] = tok
    logits = jnp.dot(tok.astype(_MXU), rw_ref[...], preferred_element_type=F32) + rb_ref[...]
    tm = logits.shape[0]
    lane = lax.broadcasted_iota(jnp.int32, (tm, LANES), 1).astype(F32)
    vals, idxs = [], []
    for _ in range(TOP_K):
        mx = jnp.max(logits, axis=1, keepdims=True)
        ix = jnp.min(jnp.where(logits == mx, lane, float(LANES)), axis=1, keepdims=True)
        vals.append(mx)
        idxs.append(ix)
        logits = jnp.where(lane == ix, -3.0e38, logits)
    es = [jnp.exp(v - vals[0]) for v in vals]
    den = functools.reduce(lambda a, c: a + c, es)
    route = jnp.zeros((tm, LANES), F32)
    for k in range(TOP_K):
        route = jnp.where(lane == float(k), idxs[k], route)
        route = jnp.where(lane == float(TOP_K + k), es[k] / den, route)
    route_ref[...] = route


def _post(x2d, ya, yb, wo, mods, g, b, rw, rb, *, batch, nbt_in, nblk, n_lat_blk, alpha):
    d = x2d.shape[1]
    tm = ROW_TILE
    half = ya.shape[1]

    def mod_map(bi, t):
        return (jnp.where(t >= n_lat_blk, batch, bi), 0, 0)

    rows_out = batch * nblk * tm
    o_map = lambda bi, t: (bi * nblk + t, 0)
    return pl.pallas_call(
        functools.partial(_post_kernel, alpha=alpha),
        grid=(batch, nblk),
        in_specs=[pl.BlockSpec((tm, d), lambda bi, t: (bi * nbt_in + t, 0)),
                  pl.BlockSpec((tm, half), o_map),
                  pl.BlockSpec((tm, half), o_map),
                  pl.BlockSpec((d, d), lambda bi, t: (0, 0)),
                  pl.BlockSpec((None, 6, d), mod_map),
                  pl.BlockSpec((1, d), lambda bi, t: (0, 0)),
                  pl.BlockSpec((1, d), lambda bi, t: (0, 0)),
                  pl.BlockSpec((d, LANES), lambda bi, t: (0, 0)),
                  pl.BlockSpec((1, LANES), lambda bi, t: (0, 0))],
        out_specs=[pl.BlockSpec((tm, d), o_map), pl.BlockSpec((tm, d), o_map), pl.BlockSpec((tm, LANES), o_map)],
        out_shape=[jax.ShapeDtypeStruct((rows_out, d), F32), jax.ShapeDtypeStruct((rows_out, d), F32),
                   jax.ShapeDtypeStruct((rows_out, LANES), F32)],
        compiler_params=_cparams(("parallel", "parallel")),
        name="post_attn",
    )(x2d, ya, yb, wo, mods, g, b, rw, rb)


def _moe_kernel(be_ref, nu_ref, dst_ref, src0_ref, src1_ref, src2_ref, tok_hbm, wi_ref, bi_ref, wo_ref, bo_ref,
                y_hbm, xbuf0, xbuf1, xbuf2, obuf0, obuf1, obuf2, gsem, ssem, wi_s, wo_s, *, dump0):
    i = pl.program_id(0)
    n_used = nu_ref[0]
    f = wo_s.shape[0]
    xbufs = (xbuf0, xbuf1, xbuf2)
    obufs = (obuf0, obuf1, obuf2)

    def gather_copy(row, r, slot):
        return pltpu.make_async_copy(tok_hbm.at[pl.ds(row, 1), :], xbufs[slot].at[pl.ds(r, 1), :], gsem.at[slot])

    def scatter_copy(row, r, slot):
        return pltpu.make_async_copy(obufs[slot].at[pl.ds(r, 1), :], y_hbm.at[pl.ds(row, 1), :], ssem.at[slot])

    def start_gather(rows_ref, slot):
        for r in range(MOE_ROWS):
            gather_copy(rows_ref[0, r], r, slot).start(priority=r % 2)

    def wait_gather(slot):
        for _ in range(MOE_ROWS):
            gather_copy(0, 0, slot).wait()

    def wait_scatter(slot):
        for _ in range(MOE_ROWS):
            scatter_copy(0, 0, slot).wait()

    @pl.when(i == 0)
    def _():
        for slot in range(MOE_RING):
            obufs[slot][...] = jnp.zeros(obufs[slot].shape, F32)
            for r in range(MOE_ROWS):
                scatter_copy(dump0 + slot * MOE_ROWS + r, r, slot).start(priority=r % 2)
        start_gather(src0_ref, 0)
        start_gather(src1_ref, 1)

    last_used = n_used - 1
    changed = jnp.logical_or(i == 0, be_ref[jnp.clip(i - 1, 0, last_used)] != be_ref[jnp.minimum(i, last_used)])

    for slot in range(MOE_RING):
        @pl.when(jnp.logical_and(i < n_used, i % MOE_RING == slot))
        def _(slot=slot):
            wait_scatter(slot)
            wait_gather(slot)

            @pl.when(changed)
            def _():
                wi_s[...] = wi_ref[...].astype(wi_s.dtype)
                wo_s[...] = wo_ref[...].astype(wo_s.dtype)

            x = xbufs[slot][...].astype(wi_s.dtype)
            start_gather(src2_ref, (slot + 2) % MOE_RING)
            hh = jnp.dot(x, wi_s[...], preferred_element_type=F32) + bi_ref[...]
            gate = jnp.minimum(hh[:, :f], SWIGLU_LIMIT)
            up = jnp.clip(hh[:, f:], -SWIGLU_LIMIT, SWIGLU_LIMIT)
            act = gate * (1.0 / (1.0 + jnp.exp(-SWIGLU_ALPHA * gate))) * (up + 1.0)
            obufs[slot][...] = jnp.dot(act.astype(wo_s.dtype), wo_s[...], preferred_element_type=F32) + bo_ref[...]
            for r in range(MOE_ROWS):
                scatter_copy(dst_ref[0, r], r, slot).start(priority=r % 2)

        @pl.when(jnp.logical_and(i >= n_used, i % MOE_RING == slot))
        def _(slot=slot):
            @pl.when(i < n_used + 2)
            def _():
                wait_gather(slot)

            @pl.when(i < n_used + MOE_RING)
            def _():
                wait_scatter(slot)


def _experts(tok, src_rows, dst_rows, block_e, n_used, w_in, b_in, w_out, b_out, layer):
    n_tok, d = tok.shape
    _, n_e, _, f2 = w_in.shape
    f = f2 // 2
    n_blocks = src_rows.shape[0]

    def blk(i, be, nu):
        return jnp.minimum(i, nu[0] - 1)

    def e_map(i, be, nu):
        return (layer, be[blk(i, be, nu)], 0, 0)

    def rows_spec(ahead):
        return pl.BlockSpec((None, 1, MOE_ROWS), lambda i, be, nu: (blk(i + ahead, be, nu), 0, 0),
                            memory_space=pltpu.SMEM)

    row_buf = pltpu.VMEM((MOE_ROWS, d), F32)
    grid_spec = pltpu.PrefetchScalarGridSpec(
        num_scalar_prefetch=2,
        grid=(n_blocks + MOE_RING,),
        in_specs=[rows_spec(0), rows_spec(0), rows_spec(1), rows_spec(2),
                  pl.BlockSpec(memory_space=pl.ANY),
                  pl.BlockSpec((None, None, d, f2), e_map),
                  pl.BlockSpec((None, None, 1, f2), e_map),
                  pl.BlockSpec((None, None, f, d), e_map),
                  pl.BlockSpec((None, None, 1, d), e_map)],
        out_specs=pl.BlockSpec(memory_space=pl.ANY),
        scratch_shapes=[row_buf] * (2 * MOE_RING)
                       + [pltpu.SemaphoreType.DMA((MOE_RING,)), pltpu.SemaphoreType.DMA((MOE_RING,)),
                          pltpu.VMEM((d, f2), _MXU), pltpu.VMEM((f, d), _MXU)],
    )
    depth = w_in.shape[0]
    return pl.pallas_call(
        functools.partial(_moe_kernel, dump0=TOP_K * n_tok), grid_spec=grid_spec,
        out_shape=jax.ShapeDtypeStruct((TOP_K * n_tok + MOE_RING * MOE_ROWS, d), F32),
        compiler_params=_cparams(("arbitrary",)),
        name="experts",
    )(block_e, n_used, dst_rows, src_rows, src_rows, src_rows, tok, w_in, b_in.reshape(depth, n_e, 1, f2), w_out,
      b_out.reshape(depth, n_e, 1, d))


def _route_plan(route, n_experts):
    n = route.shape[0]
    top_idx = route[:, :TOP_K].astype(jnp.int32)
    nk = n * TOP_K
    flat_e = top_idx.reshape(-1)
    order = jnp.argsort(flat_e).astype(jnp.int32)
    experts = jnp.arange(n_experts, dtype=jnp.int32)
    counts = jnp.sum(flat_e[:, None] == experts[None, :], axis=0, dtype=jnp.int32)
    padded = (counts + MOE_ROWS - 1) // MOE_ROWS * MOE_ROWS
    start = jnp.cumsum(counts) - counts
    pend = jnp.cumsum(padded)
    pstart = pend - padded
    n_blocks = -(-nk // MOE_ROWS) + n_experts
    first = jnp.arange(n_blocks, dtype=jnp.int32) * MOE_ROWS
    block_e = jnp.minimum(jnp.sum(pend[None, :] <= first[:, None], axis=1, dtype=jnp.int32), n_experts - 1)
    n_used = (pend[-1:] // MOE_ROWS).astype(jnp.int32)
    blk = jnp.arange(n_blocks, dtype=jnp.int32)[:, None]
    row = jnp.arange(MOE_ROWS, dtype=jnp.int32)[None, :]
    rank = blk * MOE_ROWS + row - jnp.take(pstart, block_e, mode="clip")[:, None]
    src = jnp.clip(jnp.take(start, block_e, mode="clip")[:, None] + rank, 0, nk - 1)
    valid = rank < jnp.take(counts, block_e, mode="clip")[:, None]
    pair = jnp.take(order, src, mode="clip")
    token = pair // TOP_K
    src_rows = jnp.where(valid, token, 0)
    dst_rows = jnp.where(valid, (pair % TOP_K) * n + token, nk + (blk % MOE_RING) * MOE_ROWS + row)
    return (src_rows.reshape(n_blocks, 1, MOE_ROWS), dst_rows.reshape(n_blocks, 1, MOE_ROWS), block_e, n_used)


def _ln2_kernel(x_ref, y0_ref, y1_ref, y2_ref, y3_ref, route_ref, mod_ref, g_ref, b_ref, o_ref, *, alpha):
    ys = (y0_ref, y1_ref, y2_ref, y3_ref)
    f = route_ref[:, TOP_K:TOP_K + 1] * ys[0][...]
    for k in range(1, TOP_K):
        f = f + route_ref[:, TOP_K + k:TOP_K + k + 1] * ys[k][...]
    o_ref[...] = _layernorm(alpha * x_ref[...] + mod_ref[5:6, :] * f, g_ref[...], b_ref[...])


def _ln2(x1, y, route, mods, g, b, *, batch, nblk, n_lat_blk, alpha):
    rows, d = x1.shape
    tm = ROW_TILE
    r_map = lambda bi, t: (bi * nblk + t, 0)

    def y_spec(k):
        return pl.BlockSpec((tm, d), lambda bi, t: (k * (rows // tm) + bi * nblk + t, 0))

    def mod_map(bi, t):
        return (jnp.where(t >= n_lat_blk, batch, bi), 0, 0)

    return pl.pallas_call(
        functools.partial(_ln2_kernel, alpha=alpha),
        grid=(batch, nblk),
        in_specs=[pl.BlockSpec((tm, d), r_map)] + [y_spec(k) for k in range(TOP_K)]
                 + [pl.BlockSpec((tm, LANES), r_map), pl.BlockSpec((None, 6, d), mod_map),
                    pl.BlockSpec((1, d), lambda bi, t: (0, 0)), pl.BlockSpec((1, d), lambda bi, t: (0, 0))],
        out_specs=pl.BlockSpec((tm, d), r_map),
        out_shape=jax.ShapeDtypeStruct((rows, d), F32),
        compiler_params=_cparams(("parallel", "parallel")),
        name="combine_ln2",
    )(x1, y, y, y, y, route, mods, g, b)


def _rope_layout(w, n_heads):
    lead = w.shape[:-1]
    return w.reshape(lead + (n_heads, HEAD_DIM // 2, 2)).swapaxes(-1, -2).reshape(lead + (n_heads * HEAD_DIM,))


def _pair_layout(w, n_heads, axis=-1):
    axis = axis % w.ndim
    shape = w.shape
    w = w.reshape(shape[:axis] + (2, n_heads // 2, HEAD_DIM) + shape[axis + 1:])
    return w.swapaxes(axis, axis + 1).reshape(shape)


def _rope_tables(s_len, n_ctx):
    t = np.arange(s_len)
    row = (t // GRID_W).astype(np.float32)
    col = (t % GRID_W).astype(np.float32)
    axis_dim = HEAD_DIM // 2
    freqs = jnp.asarray(ROPE_THETA, F32) ** (-jnp.arange(0, axis_dim, 2, dtype=F32) / axis_dim)
    ang = jnp.concatenate([jnp.asarray(row)[:, None] * freqs, jnp.asarray(col)[:, None] * freqs], axis=-1)
    cos, sin = jnp.cos(ang), jnp.sin(ang)
    cos_h = jnp.concatenate([cos, cos], axis=-1)
    sin_h = jnp.concatenate([-sin, sin], axis=-1)
    cos_t = jnp.concatenate([cos_h, jnp.ones((n_ctx, HEAD_DIM), F32)], axis=0)
    sin_t = jnp.concatenate([sin_h, jnp.zeros((n_ctx, HEAD_DIM), F32)], axis=0)
    return jnp.tile(cos_t, (1, 2)), jnp.tile(sin_t, (1, 2))


def _moe_and_ln2(x1, tok, route, mods, n_experts, w_in, b_in, w_out, b_out, layer, g, b, *,
                 batch, nblk, n_lat_blk, alpha):
    src_rows, dst_rows, block_e, n_used = _route_plan(route, n_experts)
    y = _experts(tok, src_rows, dst_rows, block_e, n_used, w_in, b_in, w_out, b_out, layer)
    return _ln2(x1, y, route, mods, g, b, batch=batch, nblk=nblk, n_lat_blk=n_lat_blk, alpha=alpha)


def kernel(x, c, ctx, c_ctx, mod_w, mod_b, ln1_g, ln1_b, ln2_g, ln2_b, router_w, router_b, moe_w_in, moe_b_in,
           moe_w_out, moe_b_out, ab_w_in, ab_w_out, na_rpb, diff_lq1, diff_lk1, diff_lq2, diff_lk2, diff_subln,
           cd_w_in, cd_w_out, gqa_q_norm, gqa_k_norm, swa_sink):
    batch, s_len, d = x.shape
    n_ctx = ctx.shape[1]
    t_len = n_ctx + s_len
    depth = mod_w.shape[0]
    n_experts = router_w.shape[2]
    alpha = (2.0 * depth) ** 0.25
    tm = ROW_TILE
    nbt = t_len // tm
    nbs = s_len // tm
    assert depth == 2 and n_ctx % tm == 0 and s_len % tm == 0 and batch + 1 <= 8

    cc = jnp.zeros((8, d), F32).at[:batch].set(c).at[batch].set(c_ctx)
    mod_all = _modulation(cc, mod_w, mod_b)
    mods = [mod_all[l, :batch + 1].reshape(batch + 1, 6, d) for l in range(depth)]
    cos_t, sin_t = _rope_tables(s_len, n_ctx)
    rw = [jnp.zeros((d, LANES), F32).at[:, :n_experts].set(router_w[l]).astype(_MXU) for l in range(depth)]
    rb = [jnp.full((1, LANES), NEG_INF, F32).at[0, :n_experts].set(router_b[l]) for l in range(depth)]

    stream = jnp.concatenate([x, ctx], axis=1).reshape(batch * t_len, d)
    by_batch = lambda a: a.reshape(batch, t_len, a.shape[-1])

    na_w = NA_HEADS * HEAD_DIM
    df_w = DIFF_HEADS * 2 * HEAD_DIM
    wab = ab_w_in[0]
    w0 = jnp.concatenate([wab[:, :3 * na_w], _rope_layout(wab[:, 3 * na_w:3 * na_w + df_w], 2 * DIFF_HEADS),
                          _rope_layout(wab[:, 3 * na_w + df_w:3 * na_w + 2 * df_w], 2 * DIFF_HEADS),
                          wab[:, 3 * na_w + 2 * df_w:]], axis=1).astype(_MXU)
    plan0 = ((0, na_w, None, False, QSCALE), (na_w, 2 * na_w, None, False, 1.0),
             (3 * na_w, df_w, None, True, QSCALE), (3 * na_w + df_w, df_w, None, True, 1.0),
             (3 * na_w + 2 * df_w, df_w, None, False, 1.0))
    nw0 = jnp.zeros((8, LANES), F32)
    nq, nkv, dq, dk, dv = map(by_batch, _in_proj(stream, mods[0], w0, cos_t, sin_t, nw0, plan0,
                                                 batch=batch, n_ctx=n_ctx))
    lam_init0 = 0.8 - 0.6 * math.exp(-0.3 * 0)
    par = jnp.zeros((8, LANES), F32)
    par = par.at[0, :HEAD_DIM].set(diff_lq1[0]).at[1, :HEAD_DIM].set(diff_lk1[0])
    par = par.at[2, :HEAD_DIM].set(diff_lq2[0]).at[3, :HEAD_DIM].set(diff_lk2[0]).at[4].set(diff_subln[0])
    bias = _na_bias_tables(na_rpb[0], s_len // GRID_W)
    y_na = _neighbourhood(nq, nkv, bias, s_len=s_len, n_ctx=n_ctx)
    y_na_c = _flash(nq, nkv, nkv, n_groups=NA_HEADS // 2, q_rows=n_ctx, q_off=s_len, kv_rows=n_ctx,
                    kv_off=s_len, k_col=0,
                    v_col=NA_HEADS // 2, mode="pair", name="neighbourhood_attn_ctx")
    y_df = _flash(dq, dk, dv, n_groups=DIFF_HEADS, q_rows=s_len, q_off=0, kv_rows=t_len, kv_off=0, k_col=0, v_col=0,
                  mode="diff", par=par, lam_init=lam_init0, name="diff_attn")
    y_df_c = _flash(dq, dk, dv, n_groups=DIFF_HEADS, q_rows=n_ctx, q_off=s_len, kv_rows=n_ctx, kv_off=s_len,
                    k_col=0, v_col=0,
                    mode="diff", par=par, lam_init=lam_init0, name="diff_attn_ctx")
    ya = jnp.concatenate([y_na, y_na_c], axis=1).reshape(batch * t_len, -1)
    yb = jnp.concatenate([y_df, y_df_c], axis=1).reshape(batch * t_len, -1)
    x1, tok, route = _post(stream, ya, yb, ab_w_out[0].astype(_MXU), mods[0], ln1_g[0][None], ln1_b[0][None],
                           rw[0], rb[0], batch=batch, nbt_in=nbt, nblk=nbt, n_lat_blk=nbs, alpha=alpha)
    stream = _moe_and_ln2(x1, tok, route, mods[0], n_experts, moe_w_in, moe_b_in, moe_w_out, moe_b_out, 0,
                          ln2_g[0][None], ln2_b[0][None], batch=batch, nblk=nbt, n_lat_blk=nbs, alpha=alpha)

    hw = GQA_HEADS * HEAD_DIM
    kw = GQA_KV_HEADS * HEAD_DIM
    wcd = cd_w_in[0]
    gq, gk, gv, wq, wk, wv = jnp.split(wcd, [hw, hw + kw, hw + 2 * kw, 2 * hw + 2 * kw, 2 * hw + 3 * kw], axis=1)
    w1 = jnp.concatenate([_pair_layout(_rope_layout(gq, GQA_HEADS), GQA_HEADS),
                          _pair_layout(_rope_layout(wq, SWA_HEADS), SWA_HEADS),
                          _rope_layout(gk, GQA_KV_HEADS), gv, _rope_layout(wk, SWA_KV_HEADS), wv],
                         axis=1).astype(_MXU)
    plan1 = ((0, hw, 0, True, QSCALE), (hw, hw, None, True, QSCALE), (2 * hw, kw, 1, True, 1.0),
             (2 * hw + kw, kw, None, False, 1.0), (2 * hw + 2 * kw, kw, None, True, 1.0),
             (2 * hw + 3 * kw, kw, None, False, 1.0))
    nw1 = jnp.zeros((8, LANES), F32).at[0].set(jnp.tile(_rope_layout(gqa_q_norm[0], 1), 2))
    nw1 = nw1.at[1].set(jnp.tile(_rope_layout(gqa_k_norm[0], 1), 2))
    gq, wq, gk, gv, wk, wv = map(by_batch, _in_proj(stream, mods[1], w1, cos_t, sin_t, nw1, plan1,
                                                    batch=batch, n_ctx=n_ctx))
    n_qp = GQA_HEADS // 2
    y_c = _flash(gq.reshape(batch, t_len * n_qp, LANES), gk, gv, n_groups=1, q_rows=s_len * n_qp,
                 q_off=0, kv_rows=t_len, kv_off=0, k_col=0, v_col=0, mode="pair", name="gqa_attn")
    sink = jnp.repeat(swa_sink[0].reshape(2, SWA_HEADS // 2) * LOG2E, LANES, axis=1).astype(F32)
    y_d = _windowed(wq, wk, wv, sink, s_len=s_len, n_ctx=n_ctx)
    wo1 = jnp.concatenate([_pair_layout(cd_w_out[0][:hw], GQA_HEADS, axis=0),
                           _pair_layout(cd_w_out[0][hw:], SWA_HEADS, axis=0)], axis=0).astype(_MXU)
    x1, tok, route = _post(stream, y_c.reshape(batch * s_len, -1), y_d.reshape(batch * s_len, -1), wo1, mods[1],
                           ln1_g[1][None], ln1_b[1][None], rw[1], rb[1], batch=batch, nbt_in=nbt,
                           nblk=nbs, n_lat_blk=nbs, alpha=alpha)
    out = _moe_and_ln2(x1, tok, route, mods[1], n_experts, moe_w_in, moe_b_in, moe_w_out, moe_b_out, 1,
                       ln2_g[1][None], ln2_b[1][None], batch=batch, nblk=nbs, n_lat_blk=nbs, alpha=alpha)
    return out.reshape(batch, s_len, d)
```

```python
import functools
import math

import numpy as np
import jax
import jax.numpy as jnp
from jax import lax
from jax.experimental import pallas as pl
from jax.experimental.pallas import tpu as pltpu

F32 = jnp.float32
_MXU = jnp.bfloat16

HEAD_DIM = 64
GRID_W = 64
LOG2E = math.log2(math.e)
QSCALE = HEAD_DIM ** -0.5 * LOG2E
ROPE_THETA = 10000.0
NA_HEADS = 8
NA_KH = 8
NA_KW = 16
DIFF_HEADS = 4
GQA_HEADS = 8
GQA_KV_HEADS = 2
SWA_HEADS = 8
SWA_KV_HEADS = 2
SWA_WINDOW = 128
TOP_K = 4
SWIGLU_LIMIT = 7.0
SWIGLU_ALPHA = 1.702
LN_EPS = 1e-5
RMS_EPS = 1e-6
NEG_INF = -1e30

LANES = 128
ROW_TILE = 256
MOE_ROWS = 256
MOE_RING = 3
NA_Q_ROWS = 4
VMEM_LIMIT = 52 * 1024 * 1024


def _cparams(sem, vmem=VMEM_LIMIT):
    return pltpu.CompilerParams(dimension_semantics=sem, vmem_limit_bytes=vmem)


def _mod_kernel(c_ref, w_ref, b_ref, o_ref):
    c = c_ref[...]
    a = (c / (1.0 + jnp.exp(-c))).astype(_MXU)
    o_ref[...] = jnp.dot(a, w_ref[...].astype(_MXU), preferred_element_type=F32) + b_ref[...]


def _modulation(cc, mod_w, mod_b):
    depth, d, d6 = mod_w.shape
    tn = d6 // 4
    return pl.pallas_call(
        _mod_kernel,
        grid=(depth, d6 // tn),
        in_specs=[pl.BlockSpec((8, d), lambda l, j: (0, 0)),
                  pl.BlockSpec((None, d, tn), lambda l, j: (l, 0, j)),
                  pl.BlockSpec((None, 1, tn), lambda l, j: (l, 0, j))],
        out_specs=pl.BlockSpec((None, 8, tn), lambda l, j: (l, 0, j)),
        out_shape=jax.ShapeDtypeStruct((depth, 8, d6), F32),
        compiler_params=_cparams(("arbitrary", "arbitrary")),
        name="modulation",
    )(cc, mod_w, mod_b.reshape(depth, 1, d6))


def _in_kernel(x_ref, mod_ref, w_ref, cos_ref, sin_ref, nw_ref, *o_refs, plan):
    x = x_ref[...]
    h = (x * (1.0 + mod_ref[1:2, :]) + mod_ref[0:1, :]).astype(_MXU)
    tm = x.shape[0]
    lane = lax.broadcasted_iota(jnp.int32, (tm, LANES), 1)
    first_half = (lane & (HEAD_DIM // 2)) == 0
    gi = lax.broadcasted_iota(jnp.int32, (LANES, LANES), 0) // HEAD_DIM
    gj = lax.broadcasted_iota(jnp.int32, (LANES, LANES), 1) // HEAD_DIM
    seg = jnp.where(gi == gj, 1.0, 0.0).astype(_MXU)
    cos = cos_ref[...]
    sin = sin_ref[...]
    lo_col = _head_col()
    outs = iter(o_refs)
    for (c0, width, norm_row, rope, scale, kinds) in plan:
        acc = jnp.dot(h, w_ref[:, c0:c0 + width], preferred_element_type=F32)
        o_rows = next(outs) if "rows" in kinds else None
        o_lo, o_hi = (next(outs), next(outs)) if "value_t" in kinds else (None, None)
        for j in range(width // LANES):
            a = acc[:, j * LANES:(j + 1) * LANES]
            if norm_row is not None:
                a2 = a * a
                hi = a2.astype(_MXU)
                lo = (a2 - hi.astype(F32)).astype(_MXU)
                ss = (jnp.dot(hi, seg, preferred_element_type=F32)
                      + jnp.dot(lo, seg, preferred_element_type=F32))
                a = a * lax.rsqrt(ss * (1.0 / HEAD_DIM) + RMS_EPS) * nw_ref[norm_row:norm_row + 1, :]
            if rope:
                partner = jnp.where(first_half, pltpu.roll(a, LANES - HEAD_DIM // 2, 1),
                                    pltpu.roll(a, HEAD_DIM // 2, 1))
                a = a * cos + partner * sin
            if scale != 1.0:
                a = a * scale
            if o_rows is not None:
                o_rows[:, j * LANES:(j + 1) * LANES] = a.astype(o_rows.dtype)
            if o_lo is not None:
                at = a.T
                o_lo[j * LANES:(j + 1) * LANES, :] = (at * lo_col + (1.0 - lo_col)).astype(o_lo.dtype)
                o_hi[j * LANES:(j + 1) * LANES, :] = (at * (1.0 - lo_col) + lo_col).astype(o_hi.dtype)


def _in_proj(x2d, mods, w, cos_t, sin_t, nw, plan, *, batch, n_ctx):
    rows, d = x2d.shape
    ncols = w.shape[1]
    tm = ROW_TILE
    nbt = rows // batch // tm
    n_lat_blk = nbt - n_ctx // tm

    def mod_map(i):
        return (jnp.where(i % nbt >= n_lat_blk, batch, i // nbt), 0, 0)

    out_specs, out_shape = [], []
    for (_, width, _, _, _, kinds) in plan:
        if "rows" in kinds:
            out_specs.append(pl.BlockSpec((tm, width), lambda i: (i, 0)))
            out_shape.append(jax.ShapeDtypeStruct((rows, width), _MXU))
        if "value_t" in kinds:
            out_specs += [pl.BlockSpec((width, tm), lambda i: (0, i))] * 2
            out_shape += [jax.ShapeDtypeStruct((width, rows), _MXU)] * 2

    return pl.pallas_call(
        functools.partial(_in_kernel, plan=plan),
        grid=(rows // tm,),
        in_specs=[pl.BlockSpec((tm, d), lambda i: (i, 0)),
                  pl.BlockSpec((None, 6, d), mod_map),
                  pl.BlockSpec((d, ncols), lambda i: (0, 0)),
                  pl.BlockSpec((tm, LANES), lambda i: (i % nbt, 0)),
                  pl.BlockSpec((tm, LANES), lambda i: (i % nbt, 0)),
                  pl.BlockSpec((8, LANES), lambda i: (0, 0))],
        out_specs=out_specs,
        out_shape=out_shape,
        compiler_params=_cparams(("parallel",)),
        name="in_proj",
    )(x2d, mods, w, cos_t, sin_t, nw)


def _flash_kernel(*refs, mode, tk, n_chunks, lam_init):
    if mode == "diff":
        q_ref, k_ref, v_ref, par_ref, o_ref, qt_scr, vxt_scr, m_scr, acc_scr, st_scr = refs
    else:
        q_ref, k_ref, v_ref, o_ref, qt_scr, vxt_scr, m_scr, acc_scr, st_scr = refs
    lo_col = _head_col()

    @pl.when(pl.program_id(2) == 0)
    def _():
        for c in range(n_chunks):
            vt = v_ref[c * tk:(c + 1) * tk, :].astype(F32).T
            if mode == "diff":
                vxt_scr[c, :LANES, :] = vt.astype(vxt_scr.dtype)
                vxt_scr[c, LANES:, :] = jnp.ones((LANES, tk), vxt_scr.dtype)
            else:
                vxt_scr[0, c] = (vt * lo_col + (1.0 - lo_col)).astype(vxt_scr.dtype)
                vxt_scr[1, c] = (vt * (1.0 - lo_col) + lo_col).astype(vxt_scr.dtype)

    qt = q_ref[...].astype(F32).T
    qt_scr[0] = (qt * lo_col).astype(qt_scr.dtype)
    qt_scr[1] = (qt * (1.0 - lo_col)).astype(qt_scr.dtype)
    m_scr[...] = jnp.full(m_scr.shape, NEG_INF, F32)
    acc_scr[...] = jnp.zeros(acc_scr.shape, F32)

    def qk(c, slot):
        off = pl.multiple_of(c * tk, tk)
        k = k_ref[pl.ds(off, tk), :]
        for h in range(2):
            st_scr[slot, h] = jnp.dot(k, qt_scr[h], preferred_element_type=F32)

    def softmax_pv(c, slot):
        for h in range(2):
            st = st_scr[slot, h]
            m_prev = m_scr[h]
            m_new = jnp.maximum(m_prev, jnp.max(st, axis=0, keepdims=True))
            alpha = jnp.exp2(m_prev - m_new)
            pt = jnp.exp2(st - m_new).astype(vxt_scr.dtype)
            vxt = vxt_scr[c] if mode == "diff" else vxt_scr[h, c]
            acc_scr[h] = alpha * acc_scr[h] + jnp.dot(vxt, pt, preferred_element_type=F32)
            m_scr[h] = m_new

    qk(0, 0)

    def body(j, carry):
        c = 2 * j
        qk(c + 1, 1)
        softmax_pv(c, 0)
        qk(c + 2, 0)
        softmax_pv(c + 1, 1)
        return carry

    lax.fori_loop(0, (n_chunks - 1) // 2, body, 0)
    if n_chunks % 2 == 1:
        softmax_pv(n_chunks - 1, 0)
    else:
        qk(n_chunks - 1, 1)
        softmax_pv(n_chunks - 2, 0)
        softmax_pv(n_chunks - 1, 1)

    a_lo = acc_scr[0]
    a_hi = acc_scr[1]
    if mode == "diff":
        lam = (jnp.exp(jnp.sum(par_ref[0:1, :] * par_ref[1:2, :], axis=1, keepdims=True))
               - jnp.exp(jnp.sum(par_ref[2:3, :] * par_ref[3:4, :], axis=1, keepdims=True))
               + lam_init)
        out_t = a_lo[:LANES] / a_lo[LANES:] - lam * (a_hi[:LANES] / a_hi[LANES:])
        ms = jnp.mean(out_t * out_t, axis=0, keepdims=True)
        out = (out_t * lax.rsqrt(ms + RMS_EPS)).T * par_ref[4:5, :] * (1.0 - lam_init)
    else:
        out = _pair_out_t(a_lo, a_hi).T
    o_ref[...] = out.astype(o_ref.dtype)


def _pick_tile(n, candidates):
    for c in candidates:
        if n % c == 0:
            return c
    raise ValueError(f"no tile for {n}")


def _flash(q_arr, k_arr, v_arr, *, n_groups, q_rows, q_off, kv_rows, kv_off, k_col, v_col, mode, par=None,
           lam_init=0.0, name):
    batch = q_arr.shape[0]
    tq = _pick_tile(q_rows, (1024, 512, 256))
    tk = _pick_tile(kv_rows, (768, 512, 384, 256, 128))
    nq, n_chunks = q_rows // tq, kv_rows // tk
    assert q_off % tq == 0 and kv_off % kv_rows == 0
    vw = 2 * LANES if mode == "diff" else LANES
    in_specs = [pl.BlockSpec((None, tq, LANES), lambda b, g, i: (b, q_off // tq + i, g)),
                pl.BlockSpec((None, kv_rows, LANES), lambda b, g, i: (b, kv_off // kv_rows, k_col + g)),
                pl.BlockSpec((None, kv_rows, LANES), lambda b, g, i: (b, kv_off // kv_rows, v_col + g))]
    args = [q_arr, k_arr, v_arr]
    if mode == "diff":
        in_specs.append(pl.BlockSpec(par.shape, lambda b, g, i: (0, 0)))
        args.append(par)
    return pl.pallas_call(
        functools.partial(_flash_kernel, mode=mode, tk=tk, n_chunks=n_chunks, lam_init=lam_init),
        grid=(batch, n_groups, nq),
        in_specs=in_specs,
        out_specs=pl.BlockSpec((None, tq, LANES), lambda b, g, i: (b, i, g)),
        out_shape=jax.ShapeDtypeStruct((batch, q_rows, n_groups * LANES), _MXU),
        scratch_shapes=[pltpu.VMEM((2, LANES, tq), _MXU),
                        pltpu.VMEM((n_chunks, vw, tk) if mode == "diff" else (2, n_chunks, vw, tk), _MXU),
                        pltpu.VMEM((2, 1, tq), F32),
                        pltpu.VMEM((2, vw, tq), F32),
                        pltpu.VMEM((2, 2, tk, tq), F32)],
        compiler_params=_cparams(("parallel", "parallel", "arbitrary")),
        name=name,
    )(*args)


def _head_col():
    sub = lax.broadcasted_iota(jnp.int32, (LANES, 1), 0)
    return jnp.where(sub < HEAD_DIM, 1.0, 0.0)


def _pair_out_t(a_lo, a_hi):
    return jnp.concatenate([a_lo[:HEAD_DIM] / a_lo[HEAD_DIM:], a_hi[HEAD_DIM:] / a_hi[:HEAD_DIM]], axis=0)


def _na_kernel(q_ref, kc_ref, k0_ref, k1_ref, k2_ref, *rest, n_pairs):
    vx_refs = (rest[0:4], rest[4:8])
    bias_ref, o_ref, st_scr = rest[8], rest[9], rest[10]
    tq = q_ref.shape[0]
    k_refs = (kc_ref, k0_ref, k1_ref, k2_ref)
    lo_col = _head_col()
    for p in range(n_pairs):
        cols = slice(p * LANES, (p + 1) * LANES)
        qt = q_ref[:, cols].astype(F32).T
        for half in range(2):
            col = lo_col if half == 0 else 1.0 - lo_col
            qth = (qt * col).astype(q_ref.dtype)
            st_scr[p % 2, half, 0] = jnp.dot(kc_ref[:, cols], qth, preferred_element_type=F32)
            for j in range(1, 4):
                st_scr[p % 2, half, j] = (jnp.dot(k_refs[j][:, cols], qth, preferred_element_type=F32)
                                          + bias_ref[p, half, (j - 1) * tq:j * tq, :])
        halves = []
        for half in range(2):
            sts = [st_scr[p % 2, half, j] for j in range(4)]
            m = functools.reduce(jnp.maximum, [jnp.max(st, axis=0, keepdims=True) for st in sts])
            acc = jnp.zeros((LANES, tq), F32)
            for j in range(4):
                pt = jnp.exp2(sts[j] - m).astype(q_ref.dtype)
                acc = acc + jnp.dot(vx_refs[half][j][cols, :], pt, preferred_element_type=F32)
            halves.append(acc)
        o_ref[:, cols] = _pair_out_t(*halves).T.astype(o_ref.dtype)


def _na_bias_tables(rpb, rows):
    nh = rpb.shape[0]
    nkr = 3 * NA_Q_ROWS
    qc = np.arange(GRID_W)[:, None]
    kc = np.arange(GRID_W)[None, :]
    ws = np.clip(qc - NA_KW // 2, 0, GRID_W - NA_KW)
    cvalid = ((kc >= ws) & (kc < ws + NA_KW)).reshape(-1)
    dc = (kc - qc + NA_KW - 1).reshape(-1)
    onehot = ((np.arange(2 * NA_KW - 1)[:, None] == dc[None, :]) & cvalid[None, :]).astype(np.float32)
    tiles = jnp.einsum("hrd,dx->hrx", rpb.astype(F32) * LOG2E, jnp.asarray(onehot),
                       precision=lax.Precision.HIGHEST)
    tiles = jnp.where(cvalid[None, None, :], tiles, NEG_INF)
    qr = np.arange(NA_Q_ROWS)[:, None]
    kr = np.arange(nkr)[None, :]
    tables = []
    for variant in range(3):
        if variant == 0:
            r0, k0, nrows = 0, 0, rows
        elif variant == 1:
            r0, k0, nrows = 2 * NA_Q_ROWS, NA_Q_ROWS, 8 * NA_Q_ROWS
        else:
            r0, k0, nrows = rows - NA_Q_ROWS, rows - nkr, rows
        r = r0 + qr
        rp = k0 + kr
        rs = np.clip(r - NA_KH // 2, 0, nrows - NA_KH)
        rvalid = (rp >= rs) & (rp < rs + NA_KH)
        dr = np.clip(rp - r + NA_KH - 1, 0, 2 * NA_KH - 2)
        t = jnp.take(tiles, jnp.asarray(dr.reshape(-1).astype(np.int32)), axis=1)
        t = jnp.where(rvalid.reshape(-1)[None, :, None], t, NEG_INF)
        t = t.reshape(nh, NA_Q_ROWS, nkr, GRID_W, GRID_W).transpose(0, 2, 4, 1, 3)
        tables.append(t.reshape(nh // 2, 2, nkr * GRID_W, NA_Q_ROWS * GRID_W))
    return jnp.stack(tables)


def _neighbourhood(q_arr, k_arr, vx_lo, vx_hi, bias, *, s_len, n_ctx):
    batch, t_len, width = q_arr.shape
    tq = NA_Q_ROWS * GRID_W
    assert n_ctx == tq
    nq = s_len // tq
    nbt = t_len // tq
    n_pairs = NA_HEADS // 2
    cb = s_len // n_ctx

    def seg_blocks(i):
        first = jnp.clip(i - 1, 0, nq - 3)
        return [cb, first, first + 1, first + 2]

    q_spec = pl.BlockSpec((None, tq, width), lambda i, b: (b, i, 0))
    k_specs = [pl.BlockSpec((None, tq, width), lambda i, b, j=j: (b, seg_blocks(i)[j], 0)) for j in range(4)]
    v_specs = [pl.BlockSpec((width, tq), lambda i, b, j=j: (0, b * nbt + seg_blocks(i)[j])) for j in range(4)]
    bias_spec = pl.BlockSpec((None, n_pairs, 2, 3 * tq, tq),
                             lambda i, b: (jnp.where(i == 0, 0, jnp.where(i == nq - 1, 2, 1)), 0, 0, 0, 0))
    return pl.pallas_call(
        functools.partial(_na_kernel, n_pairs=n_pairs),
        grid=(nq, batch),
        in_specs=[q_spec] + k_specs + v_specs + v_specs + [bias_spec],
        out_specs=pl.BlockSpec((None, tq, width), lambda i, b: (b, i, 0)),
        out_shape=jax.ShapeDtypeStruct((batch, s_len, width), _MXU),
        scratch_shapes=[pltpu.VMEM((2, 2, 4, tq, tq), F32)],
        compiler_params=_cparams(("parallel", "parallel")),
        name="neighbourhood_attn",
    )(q_arr, *([k_arr] * 4), *([vx_lo] * 4), *([vx_hi] * 4), bias)


def _windowed_kernel(q_ref, kc_ref, k0_ref, k1_ref, k2_ref, *rest, n_pairs, s_len):
    vx_refs = (rest[0:4], rest[4:8])
    sink_ref, o_ref = rest[8], rest[9]
    tq = q_ref.shape[0]
    qi = pl.program_id(1)
    k_refs = (kc_ref, k0_ref, k1_ref, k2_ref)
    lo_col = _head_col()
    qt = jnp.concatenate([q_ref[:, p * LANES:(p + 1) * LANES].astype(F32).T for p in range(n_pairs)], axis=1)

    qpos = qi * tq + (lax.broadcasted_iota(jnp.int32, (1, n_pairs * LANES), 1) & (LANES - 1))
    masks = [None]
    for j in range(3):
        kpos = (qi + j - 1) * tq + lax.broadcasted_iota(jnp.int32, (tq, 1), 0)
        dist = jnp.abs(qpos - kpos)
        ok = jnp.where(kpos >= 0, jnp.where(kpos < s_len, dist, SWA_WINDOW + 1), SWA_WINDOW + 1)
        masks.append(ok <= SWA_WINDOW)

    halves = []
    for half in range(2):
        col = lo_col if half == 0 else 1.0 - lo_col
        qth = (qt * col).astype(q_ref.dtype)
        sts = []
        for j in range(4):
            st = jnp.dot(k_refs[j][...], qth, preferred_element_type=F32)
            sts.append(st if masks[j] is None else jnp.where(masks[j], st, NEG_INF))
        sink = sink_ref[half:half + 1, :]
        m = functools.reduce(jnp.maximum, [jnp.max(st, axis=0, keepdims=True) for st in sts] + [sink])
        acc = jnp.exp2(sink - m) * (1.0 - col)
        for j in range(4):
            pt = jnp.exp2(sts[j] - m).astype(q_ref.dtype)
            acc = acc + jnp.dot(vx_refs[half][j][...], pt, preferred_element_type=F32)
        halves.append(acc)
    out_t = _pair_out_t(*halves)
    for p in range(n_pairs):
        o_ref[:, p * LANES:(p + 1) * LANES] = out_t[:, p * LANES:(p + 1) * LANES].T.astype(o_ref.dtype)


def _windowed(q_arr, k_arr, vx_lo, vx_hi, sink, *, s_len, n_ctx):
    batch, t_len = q_arr.shape[0], q_arr.shape[1]
    tq = SWA_WINDOW
    nb = s_len // tq
    n_pairs = SWA_HEADS // 2
    cb = s_len // n_ctx
    qw = n_pairs * LANES
    q_spec = pl.BlockSpec((None, tq, qw), lambda b, i: (b, i, 0))
    k_specs = [pl.BlockSpec((None, n_ctx, LANES), lambda b, i: (b, cb, 0))]
    v_specs = [pl.BlockSpec((LANES, n_ctx), lambda b, i: (0, b * (t_len // n_ctx) + cb))]
    for j in range(3):
        k_specs.append(pl.BlockSpec((None, tq, LANES),
                                    lambda b, i, j=j: (b, jnp.clip(i - 1 + j, 0, nb - 1), 0)))
        v_specs.append(pl.BlockSpec((LANES, tq),
                                    lambda b, i, j=j: (0, b * (t_len // tq) + jnp.clip(i - 1 + j, 0, nb - 1))))
    return pl.pallas_call(
        functools.partial(_windowed_kernel, n_pairs=n_pairs, s_len=s_len),
        grid=(batch, nb),
        in_specs=[q_spec] + k_specs + v_specs + v_specs + [pl.BlockSpec(sink.shape, lambda b, i: (0, 0))],
        out_specs=pl.BlockSpec((None, tq, qw), lambda b, i: (b, i, 0)),
        out_shape=jax.ShapeDtypeStruct((batch, s_len, qw), _MXU),
        compiler_params=_cparams(("parallel", "parallel")),
        name="windowed_attn",
    )(q_arr, *([k_arr] * 4), *([vx_lo] * 4), *([vx_hi] * 4), sink)


def _layernorm(z, g, b):
    mu = jnp.mean(z, axis=1, keepdims=True)
    zc = z - mu
    var = jnp.mean(zc * zc, axis=1, keepdims=True)
    return zc * lax.rsqrt(var + LN_EPS) * g + b


def _post_kernel(x_ref, ya_ref, yb_ref, wo_ref, mod_ref, g_ref, b_ref, rw_ref, rb_ref,
                 x1_ref, tok_ref, route_ref, *, alpha):
    half = ya_ref.shape[1]
    y = (jnp.dot(ya_ref[...], wo_ref[:half, :], preferred_element_type=F32)
         + jnp.dot(yb_ref[...], wo_ref[half:, :], preferred_element_type=F32))
    x1 = _layernorm(alpha * x_ref[...] + mod_ref[2:3, :] * y, g_ref[...], b_ref[...])
    x1_ref[...] = x1
    tok = x1 * (1.0 + mod_ref[4:5, :]) + mod_ref[3:4, :]
    tok_ref[...] = tok
    logits = jnp.dot(tok.astype(_MXU), rw_ref[...], preferred_element_type=F32) + rb_ref[...]
    tm = logits.shape[0]
    lane = lax.broadcasted_iota(jnp.int32, (tm, LANES), 1).astype(F32)
    vals, idxs = [], []
    for _ in range(TOP_K):
        mx = jnp.max(logits, axis=1, keepdims=True)
        ix = jnp.min(jnp.where(logits == mx, lane, float(LANES)), axis=1, keepdims=True)
        vals.append(mx)
        idxs.append(ix)
        logits = jnp.where(lane == ix, -3.0e38, logits)
    es = [jnp.exp(v - vals[0]) for v in vals]
    den = functools.reduce(lambda a, c: a + c, es)
    route = jnp.zeros((tm, LANES), F32)
    for k in range(TOP_K):
        route = jnp.where(lane == float(k), idxs[k], route)
        route = jnp.where(lane == float(TOP_K + k), es[k] / den, route)
    route_ref[...] = route


def _post(x2d, ya, yb, wo, mods, g, b, rw, rb, *, batch, nbt_in, nblk, n_lat_blk, alpha):
    d = x2d.shape[1]
    tm = ROW_TILE
    half = ya.shape[1]

    def mod_map(bi, t):
        return (jnp.where(t >= n_lat_blk, batch, bi), 0, 0)

    rows_out = batch * nblk * tm
    o_map = lambda bi, t: (bi * nblk + t, 0)
    return pl.pallas_call(
        functools.partial(_post_kernel, alpha=alpha),
        grid=(batch, nblk),
        in_specs=[pl.BlockSpec((tm, d), lambda bi, t: (bi * nbt_in + t, 0)),
                  pl.BlockSpec((tm, half), o_map),
                  pl.BlockSpec((tm, half), o_map),
                  pl.BlockSpec((d, d), lambda bi, t: (0, 0)),
                  pl.BlockSpec((None, 6, d), mod_map),
                  pl.BlockSpec((1, d), lambda bi, t: (0, 0)),
                  pl.BlockSpec((1, d), lambda bi, t: (0, 0)),
                  pl.BlockSpec((d, LANES), lambda bi, t: (0, 0)),
                  pl.BlockSpec((1, LANES), lambda bi, t: (0, 0))],
        out_specs=[pl.BlockSpec((tm, d), o_map), pl.BlockSpec((tm, d), o_map), pl.BlockSpec((tm, LANES), o_map)],
        out_shape=[jax.ShapeDtypeStruct((rows_out, d), F32), jax.ShapeDtypeStruct((rows_out, d), F32),
                   jax.ShapeDtypeStruct((rows_out, LANES), F32)],
        compiler_params=_cparams(("parallel", "parallel")),
        name="post_attn",
    )(x2d, ya, yb, wo, mods, g, b, rw, rb)


def _moe_kernel(be_ref, nu_ref, dst_ref, src0_ref, src1_ref, src2_ref, tok_hbm, wi_ref, bi_ref, wo_ref, bo_ref,
                y_hbm, xbuf0, xbuf1, xbuf2, obuf0, obuf1, obuf2, gsem, ssem, wi_s, wo_s, *, dump0):
    i = pl.program_id(0)
    n_used = nu_ref[0]
    f = wo_s.shape[0]
    xbufs = (xbuf0, xbuf1, xbuf2)
    obufs = (obuf0, obuf1, obuf2)

    def gather_copy(row, r, slot):
        return pltpu.make_async_copy(tok_hbm.at[pl.ds(row, 1), :], xbufs[slot].at[pl.ds(r, 1), :], gsem.at[slot])

    def scatter_copy(row, r, slot):
        return pltpu.make_async_copy(obufs[slot].at[pl.ds(r, 1), :], y_hbm.at[pl.ds(row, 1), :], ssem.at[slot])

    def start_gather(rows_ref, slot):
        for r in range(MOE_ROWS):
            gather_copy(rows_ref[0, r], r, slot).start(priority=r % 2)

    def wait_gather(slot):
        for _ in range(MOE_ROWS):
            gather_copy(0, 0, slot).wait()

    def wait_scatter(slot):
        for _ in range(MOE_ROWS):
            scatter_copy(0, 0, slot).wait()

    @pl.when(i == 0)
    def _():
        for slot in range(MOE_RING):
            obufs[slot][...] = jnp.zeros(obufs[slot].shape, F32)
            for r in range(MOE_ROWS):
                scatter_copy(dump0 + slot * MOE_ROWS + r, r, slot).start(priority=r % 2)
        start_gather(src0_ref, 0)
        start_gather(src1_ref, 1)

    last_used = n_used - 1
    changed = jnp.logical_or(i == 0, be_ref[jnp.clip(i - 1, 0, last_used)] != be_ref[jnp.minimum(i, last_used)])

    for slot in range(MOE_RING):
        @pl.when(jnp.logical_and(i < n_used, i % MOE_RING == slot))
        def _(slot=slot):
            wait_scatter(slot)
            wait_gather(slot)

            @pl.when(changed)
            def _():
                wi_s[...] = wi_ref[...].astype(wi_s.dtype)
                wo_s[...] = wo_ref[...].astype(wo_s.dtype)

            x = xbufs[slot][...].astype(wi_s.dtype)
            start_gather(src2_ref, (slot + 2) % MOE_RING)
            hh = jnp.dot(x, wi_s[...], preferred_element_type=F32) + bi_ref[...]
            gate = jnp.minimum(hh[:, :f], SWIGLU_LIMIT)
            up = jnp.clip(hh[:, f:], -SWIGLU_LIMIT, SWIGLU_LIMIT)
            act = gate * (1.0 / (1.0 + jnp.exp(-SWIGLU_ALPHA * gate))) * (up + 1.0)
            obufs[slot][...] = jnp.dot(act.astype(wo_s.dtype), wo_s[...], preferred_element_type=F32) + bo_ref[...]
            for r in range(MOE_ROWS):
                scatter_copy(dst_ref[0, r], r, slot).start(priority=r % 2)

        @pl.when(jnp.logical_and(i >= n_used, i % MOE_RING == slot))
        def _(slot=slot):
            @pl.when(i < n_used + 2)
            def _():
                wait_gather(slot)

            @pl.when(i < n_used + MOE_RING)
            def _():
                wait_scatter(slot)


def _experts(tok, src_rows, dst_rows, block_e, n_used, w_in, b_in, w_out, b_out, layer):
    n_tok, d = tok.shape
    _, n_e, _, f2 = w_in.shape
    f = f2 // 2
    n_blocks = src_rows.shape[0]

    def blk(i, be, nu):
        return jnp.minimum(i, nu[0] - 1)

    def e_map(i, be, nu):
        return (layer, be[blk(i, be, nu)], 0, 0)

    def rows_spec(ahead):
        return pl.BlockSpec((None, 1, MOE_ROWS), lambda i, be, nu: (blk(i + ahead, be, nu), 0, 0),
                            memory_space=pltpu.SMEM)

    row_buf = pltpu.VMEM((MOE_ROWS, d), F32)
    grid_spec = pltpu.PrefetchScalarGridSpec(
        num_scalar_prefetch=2,
        grid=(n_blocks + MOE_RING,),
        in_specs=[rows_spec(0), rows_spec(0), rows_spec(1), rows_spec(2),
                  pl.BlockSpec(memory_space=pl.ANY),
                  pl.BlockSpec((None, None, d, f2), e_map),
                  pl.BlockSpec((None, None, 1, f2), e_map),
                  pl.BlockSpec((None, None, f, d), e_map),
                  pl.BlockSpec((None, None, 1, d), e_map)],
        out_specs=pl.BlockSpec(memory_space=pl.ANY),
        scratch_shapes=[row_buf] * (2 * MOE_RING)
                       + [pltpu.SemaphoreType.DMA((MOE_RING,)), pltpu.SemaphoreType.DMA((MOE_RING,)),
                          pltpu.VMEM((d, f2), _MXU), pltpu.VMEM((f, d), _MXU)],
    )
    depth = w_in.shape[0]
    return pl.pallas_call(
        functools.partial(_moe_kernel, dump0=TOP_K * n_tok), grid_spec=grid_spec,
        out_shape=jax.ShapeDtypeStruct((TOP_K * n_tok + MOE_RING * MOE_ROWS, d), F32),
        compiler_params=_cparams(("arbitrary",)),
        name="experts",
    )(block_e, n_used, dst_rows, src_rows, src_rows, src_rows, tok, w_in, b_in.reshape(depth, n_e, 1, f2), w_out,
      b_out.reshape(depth, n_e, 1, d))


def _route_plan(route, n_experts):
    n = route.shape[0]
    top_idx = route[:, :TOP_K].astype(jnp.int32)
    nk = n * TOP_K
    flat_e = top_idx.reshape(-1)
    order = jnp.argsort(flat_e).astype(jnp.int32)
    experts = jnp.arange(n_experts, dtype=jnp.int32)
    counts = jnp.sum(flat_e[:, None] == experts[None, :], axis=0, dtype=jnp.int32)
    padded = (counts + MOE_ROWS - 1) // MOE_ROWS * MOE_ROWS
    start = jnp.cumsum(counts) - counts
    pend = jnp.cumsum(padded)
    pstart = pend - padded
    n_blocks = -(-nk // MOE_ROWS) + n_experts
    first = jnp.arange(n_blocks, dtype=jnp.int32) * MOE_ROWS
    block_e = jnp.minimum(jnp.sum(pend[None, :] <= first[:, None], axis=1, dtype=jnp.int32), n_experts - 1)
    n_used = (pend[-1:] // MOE_ROWS).astype(jnp.int32)
    blk = jnp.arange(n_blocks, dtype=jnp.int32)[:, None]
    row = jnp.arange(MOE_ROWS, dtype=jnp.int32)[None, :]
    rank = blk * MOE_ROWS + row - jnp.take(pstart, block_e, mode="clip")[:, None]
    src = jnp.clip(jnp.take(start, block_e, mode="clip")[:, None] + rank, 0, nk - 1)
    valid = rank < jnp.take(counts, block_e, mode="clip")[:, None]
    pair = jnp.take(order, src, mode="clip")
    token = pair // TOP_K
    src_rows = jnp.where(valid, token, 0)
    dst_rows = jnp.where(valid, (pair % TOP_K) * n + token, nk + (blk % MOE_RING) * MOE_ROWS + row)
    return (src_rows.reshape(n_blocks, 1, MOE_ROWS), dst_rows.reshape(n_blocks, 1, MOE_ROWS), block_e, n_used)


def _ln2_kernel(x_ref, y0_ref, y1_ref, y2_ref, y3_ref, route_ref, mod_ref, g_ref, b_ref, o_ref, *, alpha):
    ys = (y0_ref, y1_ref, y2_ref, y3_ref)
    f = route_ref[:, TOP_K:TOP_K + 1] * ys[0][...]
    for k in range(1, TOP_K):
        f = f + route_ref[:, TOP_K + k:TOP_K + k + 1] * ys[k][...]
    o_ref[...] = _layernorm(alpha * x_ref[...] + mod_ref[5:6, :] * f, g_ref[...], b_ref[...])


def _ln2(x1, y, route, mods, g, b, *, batch, nblk, n_lat_blk, alpha):
    rows, d = x1.shape
    tm = ROW_TILE
    r_map = lambda bi, t: (bi * nblk + t, 0)

    def y_spec(k):
        return pl.BlockSpec((tm, d), lambda bi, t: (k * (rows // tm) + bi * nblk + t, 0))

    def mod_map(bi, t):
        return (jnp.where(t >= n_lat_blk, batch, bi), 0, 0)

    return pl.pallas_call(
        functools.partial(_ln2_kernel, alpha=alpha),
        grid=(batch, nblk),
        in_specs=[pl.BlockSpec((tm, d), r_map)] + [y_spec(k) for k in range(TOP_K)]
                 + [pl.BlockSpec((tm, LANES), r_map), pl.BlockSpec((None, 6, d), mod_map),
                    pl.BlockSpec((1, d), lambda bi, t: (0, 0)), pl.BlockSpec((1, d), lambda bi, t: (0, 0))],
        out_specs=pl.BlockSpec((tm, d), r_map),
        out_shape=jax.ShapeDtypeStruct((rows, d), F32),
        compiler_params=_cparams(("parallel", "parallel")),
        name="combine_ln2",
    )(x1, y, y, y, y, route, mods, g, b)


def _rope_layout(w, n_heads):
    lead = w.shape[:-1]
    return w.reshape(lead + (n_heads, HEAD_DIM // 2, 2)).swapaxes(-1, -2).reshape(lead + (n_heads * HEAD_DIM,))


def _pair_layout(w, n_heads, axis=-1):
    axis = axis % w.ndim
    shape = w.shape
    w = w.reshape(shape[:axis] + (2, n_heads // 2, HEAD_DIM) + shape[axis + 1:])
    return w.swapaxes(axis, axis + 1).reshape(shape)


def _rope_tables(s_len, n_ctx):
    t = np.arange(s_len)
    row = (t // GRID_W).astype(np.float32)
    col = (t % GRID_W).astype(np.float32)
    axis_dim = HEAD_DIM // 2
    freqs = jnp.asarray(ROPE_THETA, F32) ** (-jnp.arange(0, axis_dim, 2, dtype=F32) / axis_dim)
    ang = jnp.concatenate([jnp.asarray(row)[:, None] * freqs, jnp.asarray(col)[:, None] * freqs], axis=-1)
    cos, sin = jnp.cos(ang), jnp.sin(ang)
    cos_h = jnp.concatenate([cos, cos], axis=-1)
    sin_h = jnp.concatenate([-sin, sin], axis=-1)
    cos_t = jnp.concatenate([cos_h, jnp.ones((n_ctx, HEAD_DIM), F32)], axis=0)
    sin_t = jnp.concatenate([sin_h, jnp.zeros((n_ctx, HEAD_DIM), F32)], axis=0)
    return jnp.tile(cos_t, (1, 2)), jnp.tile(sin_t, (1, 2))


def _moe_and_ln2(x1, tok, route, mods, n_experts, w_in, b_in, w_out, b_out, layer, g, b, *,
                 batch, nblk, n_lat_blk, alpha):
    src_rows, dst_rows, block_e, n_used = _route_plan(route, n_experts)
    y = _experts(tok, src_rows, dst_rows, block_e, n_used, w_in, b_in, w_out, b_out, layer)
    return _ln2(x1, y, route, mods, g, b, batch=batch, nblk=nblk, n_lat_blk=n_lat_blk, alpha=alpha)


def kernel(x, c, ctx, c_ctx, mod_w, mod_b, ln1_g, ln1_b, ln2_g, ln2_b, router_w, router_b, moe_w_in, moe_b_in,
           moe_w_out, moe_b_out, ab_w_in, ab_w_out, na_rpb, diff_lq1, diff_lk1, diff_lq2, diff_lk2, diff_subln,
           cd_w_in, cd_w_out, gqa_q_norm, gqa_k_norm, swa_sink):
    batch, s_len, d = x.shape
    n_ctx = ctx.shape[1]
    t_len = n_ctx + s_len
    depth = mod_w.shape[0]
    n_experts = router_w.shape[2]
    alpha = (2.0 * depth) ** 0.25
    tm = ROW_TILE
    nbt = t_len // tm
    nbs = s_len // tm
    assert depth == 2 and n_ctx % tm == 0 and s_len % tm == 0 and batch + 1 <= 8

    cc = jnp.zeros((8, d), F32).at[:batch].set(c).at[batch].set(c_ctx)
    mod_all = _modulation(cc, mod_w, mod_b)
    mods = [mod_all[l, :batch + 1].reshape(batch + 1, 6, d) for l in range(depth)]
    cos_t, sin_t = _rope_tables(s_len, n_ctx)
    rw = [jnp.zeros((d, LANES), F32).at[:, :n_experts].set(router_w[l]).astype(_MXU) for l in range(depth)]
    rb = [jnp.full((1, LANES), NEG_INF, F32).at[0, :n_experts].set(router_b[l]) for l in range(depth)]

    stream = jnp.concatenate([x, ctx], axis=1).reshape(batch * t_len, d)
    by_batch = lambda a: a.reshape(batch, t_len, a.shape[-1])

    na_w = NA_HEADS * HEAD_DIM
    df_w = DIFF_HEADS * 2 * HEAD_DIM
    wab = ab_w_in[0]
    w0 = jnp.concatenate([wab[:, :3 * na_w], _rope_layout(wab[:, 3 * na_w:3 * na_w + df_w], 2 * DIFF_HEADS),
                          _rope_layout(wab[:, 3 * na_w + df_w:3 * na_w + 2 * df_w], 2 * DIFF_HEADS),
                          wab[:, 3 * na_w + 2 * df_w:]], axis=1).astype(_MXU)
    rows_only = ("rows",)
    plan0 = ((0, na_w, None, False, QSCALE, rows_only), (na_w, na_w, None, False, 1.0, rows_only),
             (2 * na_w, na_w, None, False, 1.0, ("rows", "value_t")),
             (3 * na_w, df_w, None, True, QSCALE, rows_only), (3 * na_w + df_w, df_w, None, True, 1.0, rows_only),
             (3 * na_w + 2 * df_w, df_w, None, False, 1.0, rows_only))
    nw0 = jnp.zeros((8, LANES), F32)
    nq, nk, nv, nvx_lo, nvx_hi, dq, dk, dv = _in_proj(stream, mods[0], w0, cos_t, sin_t, nw0, plan0,
                                                      batch=batch, n_ctx=n_ctx)
    nq, nk, nv, dq, dk, dv = map(by_batch, (nq, nk, nv, dq, dk, dv))
    lam_init0 = 0.8 - 0.6 * math.exp(-0.3 * 0)
    par = jnp.zeros((8, LANES), F32)
    par = par.at[0, :HEAD_DIM].set(diff_lq1[0]).at[1, :HEAD_DIM].set(diff_lk1[0])
    par = par.at[2, :HEAD_DIM].set(diff_lq2[0]).at[3, :HEAD_DIM].set(diff_lk2[0]).at[4].set(diff_subln[0])
    bias = _na_bias_tables(na_rpb[0], s_len // GRID_W)
    y_na = _neighbourhood(nq, nk, nvx_lo, nvx_hi, bias, s_len=s_len, n_ctx=n_ctx)
    y_na_c = _flash(nq, nk, nv, n_groups=NA_HEADS // 2, q_rows=n_ctx, q_off=s_len, kv_rows=n_ctx,
                    kv_off=s_len, k_col=0, v_col=0, mode="pair", name="neighbourhood_attn_ctx")
    y_df = _flash(dq, dk, dv, n_groups=DIFF_HEADS, q_rows=s_len, q_off=0, kv_rows=t_len, kv_off=0, k_col=0, v_col=0,
                  mode="diff", par=par, lam_init=lam_init0, name="diff_attn")
    y_df_c = _flash(dq, dk, dv, n_groups=DIFF_HEADS, q_rows=n_ctx, q_off=s_len, kv_rows=n_ctx, kv_off=s_len,
                    k_col=0, v_col=0,
                    mode="diff", par=par, lam_init=lam_init0, name="diff_attn_ctx")
    ya = jnp.concatenate([y_na, y_na_c], axis=1).reshape(batch * t_len, -1)
    yb = jnp.concatenate([y_df, y_df_c], axis=1).reshape(batch * t_len, -1)
    x1, tok, route = _post(stream, ya, yb, ab_w_out[0].astype(_MXU), mods[0], ln1_g[0][None], ln1_b[0][None],
                           rw[0], rb[0], batch=batch, nbt_in=nbt, nblk=nbt, n_lat_blk=nbs, alpha=alpha)
    stream = _moe_and_ln2(x1, tok, route, mods[0], n_experts, moe_w_in, moe_b_in, moe_w_out, moe_b_out, 0,
                          ln2_g[0][None], ln2_b[0][None], batch=batch, nblk=nbt, n_lat_blk=nbs, alpha=alpha)

    hw = GQA_HEADS * HEAD_DIM
    kw = GQA_KV_HEADS * HEAD_DIM
    wcd = cd_w_in[0]
    gq, gk, gv, wq, wk, wv = jnp.split(wcd, [hw, hw + kw, hw + 2 * kw, 2 * hw + 2 * kw, 2 * hw + 3 * kw], axis=1)
    w1 = jnp.concatenate([_pair_layout(_rope_layout(gq, GQA_HEADS), GQA_HEADS),
                          _pair_layout(_rope_layout(wq, SWA_HEADS), SWA_HEADS),
                          _rope_layout(gk, GQA_KV_HEADS), gv, _rope_layout(wk, SWA_KV_HEADS), wv],
                         axis=1).astype(_MXU)
    plan1 = ((0, hw, 0, True, QSCALE, rows_only), (hw, hw, None, True, QSCALE, rows_only),
             (2 * hw, kw, 1, True, 1.0, rows_only), (2 * hw + kw, kw, None, False, 1.0, rows_only),
             (2 * hw + 2 * kw, kw, None, True, 1.0, rows_only),
             (2 * hw + 3 * kw, kw, None, False, 1.0, ("value_t",)))
    nw1 = jnp.zeros((8, LANES), F32).at[0].set(jnp.tile(_rope_layout(gqa_q_norm[0], 1), 2))
    nw1 = nw1.at[1].set(jnp.tile(_rope_layout(gqa_k_norm[0], 1), 2))
    gq, wq, gk, gv, wk, wvx_lo, wvx_hi = _in_proj(stream, mods[1], w1, cos_t, sin_t, nw1, plan1,
                                                  batch=batch, n_ctx=n_ctx)
    gq, wq, gk, gv, wk = map(by_batch, (gq, wq, gk, gv, wk))
    n_qp = GQA_HEADS // 2
    y_c = _flash(gq.reshape(batch, t_len * n_qp, LANES), gk, gv, n_groups=1, q_rows=s_len * n_qp,
                 q_off=0, kv_rows=t_len, kv_off=0, k_col=0, v_col=0, mode="pair", name="gqa_attn")
    sink = jnp.repeat(swa_sink[0].reshape(2, SWA_HEADS // 2) * LOG2E, LANES, axis=1).astype(F32)
    y_d = _windowed(wq, wk, wvx_lo, wvx_hi, sink, s_len=s_len, n_ctx=n_ctx)
    wo1 = jnp.concatenate([_pair_layout(cd_w_out[0][:hw], GQA_HEADS, axis=0),
                           _pair_layout(cd_w_out[0][hw:], SWA_HEADS, axis=0)], axis=0).astype(_MXU)
    x1, tok, route = _post(stream, y_c.reshape(batch * s_len, -1), y_d.reshape(batch * s_len, -1), wo1, mods[1],
                           ln1_g[1][None], ln1_b[1][None], rw[1], rb[1], batch=batch, nbt_in=nbt,
                           nblk=nbs, n_lat_blk=nbs, alpha=alpha)
    out = _moe_and_ln2(x1, tok, route, mods[1], n_experts, moe_w_in, moe_b_in, moe_w_out, moe_b_out, 1,
                       ln2_g[1][None], ln2_b[1][None], batch=batch, nblk=nbs, n_lat_blk=nbs, alpha=alpha)
    return out.reshape(batch, s_len, d)
```

```python
import functools
import math

import numpy as np
import jax
import jax.numpy as jnp
from jax import lax
from jax.experimental import pallas as pl
from jax.experimental.pallas import tpu as pltpu

F32 = jnp.float32
_MXU = jnp.bfloat16

HEAD_DIM = 64
GRID_W = 64
LOG2E = math.log2(math.e)
QSCALE = HEAD_DIM ** -0.5 * LOG2E
ROPE_THETA = 10000.0
NA_HEADS = 8
NA_KH = 8
NA_KW = 16
DIFF_HEADS = 4
GQA_HEADS = 8
GQA_KV_HEADS = 2
SWA_HEADS = 8
SWA_KV_HEADS = 2
SWA_WINDOW = 128
TOP_K = 4
SWIGLU_LIMIT = 7.0
SWIGLU_ALPHA = 1.702
LN_EPS = 1e-5
RMS_EPS = 1e-6
NEG_INF = -1e30

LANES = 128
ROW_TILE = 256
MOE_ROWS = 256
MOE_RING = 3
NA_Q_ROWS = 4
VMEM_LIMIT = 52 * 1024 * 1024


def _cparams(sem, vmem=VMEM_LIMIT):
    return pltpu.CompilerParams(dimension_semantics=sem, vmem_limit_bytes=vmem)


def _mod_kernel(c_ref, w_ref, b_ref, o_ref):
    c = c_ref[...]
    a = (c / (1.0 + jnp.exp(-c))).astype(_MXU)
    o_ref[...] = jnp.dot(a, w_ref[...].astype(_MXU), preferred_element_type=F32) + b_ref[...]


def _modulation(cc, mod_w, mod_b):
    depth, d, d6 = mod_w.shape
    tn = d6 // 4
    return pl.pallas_call(
        _mod_kernel,
        grid=(depth, d6 // tn),
        in_specs=[pl.BlockSpec((8, d), lambda l, j: (0, 0)),
                  pl.BlockSpec((None, d, tn), lambda l, j: (l, 0, j)),
                  pl.BlockSpec((None, 1, tn), lambda l, j: (l, 0, j))],
        out_specs=pl.BlockSpec((None, 8, tn), lambda l, j: (l, 0, j)),
        out_shape=jax.ShapeDtypeStruct((depth, 8, d6), F32),
        compiler_params=_cparams(("arbitrary", "arbitrary")),
        name="modulation",
    )(cc, mod_w, mod_b.reshape(depth, 1, d6))


def _in_kernel(x_ref, mod_ref, w_ref, cos_ref, sin_ref, nw_ref, *o_refs, plan):
    x = x_ref[...]
    h = (x * (1.0 + mod_ref[1:2, :]) + mod_ref[0:1, :]).astype(_MXU)
    tm = x.shape[0]
    lane = lax.broadcasted_iota(jnp.int32, (tm, LANES), 1)
    first_half = (lane & (HEAD_DIM // 2)) == 0
    gi = lax.broadcasted_iota(jnp.int32, (LANES, LANES), 0) // HEAD_DIM
    gj = lax.broadcasted_iota(jnp.int32, (LANES, LANES), 1) // HEAD_DIM
    seg = jnp.where(gi == gj, 1.0, 0.0).astype(_MXU)
    cos = cos_ref[...]
    sin = sin_ref[...]
    lo_col = _head_col()
    outs = iter(o_refs)
    for (c0, width, norm_row, rope, scale, kinds) in plan:
        acc = jnp.dot(h, w_ref[:, c0:c0 + width], preferred_element_type=F32)
        o_rows = next(outs) if "rows" in kinds else None
        o_lo, o_hi = (next(outs), next(outs)) if "value_t" in kinds else (None, None)
        for j in range(width // LANES):
            a = acc[:, j * LANES:(j + 1) * LANES]
            if norm_row is not None:
                a2 = a * a
                hi = a2.astype(_MXU)
                lo = (a2 - hi.astype(F32)).astype(_MXU)
                ss = (jnp.dot(hi, seg, preferred_element_type=F32)
                      + jnp.dot(lo, seg, preferred_element_type=F32))
                a = a * lax.rsqrt(ss * (1.0 / HEAD_DIM) + RMS_EPS) * nw_ref[norm_row:norm_row + 1, :]
            if rope:
                partner = jnp.where(first_half, pltpu.roll(a, LANES - HEAD_DIM // 2, 1),
                                    pltpu.roll(a, HEAD_DIM // 2, 1))
                a = a * cos + partner * sin
            if scale != 1.0:
                a = a * scale
            if o_rows is not None:
                o_rows[:, j * LANES:(j + 1) * LANES] = a.astype(o_rows.dtype)
            if o_lo is not None:
                at = a.T
                o_lo[j * LANES:(j + 1) * LANES, :] = (at * lo_col + (1.0 - lo_col)).astype(o_lo.dtype)
                o_hi[j * LANES:(j + 1) * LANES, :] = (at * (1.0 - lo_col) + lo_col).astype(o_hi.dtype)


def _in_proj(x2d, mods, w, cos_t, sin_t, nw, plan, *, batch, n_ctx):
    rows, d = x2d.shape
    ncols = w.shape[1]
    tm = ROW_TILE
    nbt = rows // batch // tm
    n_lat_blk = nbt - n_ctx // tm

    def mod_map(i):
        return (jnp.where(i % nbt >= n_lat_blk, batch, i // nbt), 0, 0)

    out_specs, out_shape = [], []
    for (_, width, _, _, _, kinds) in plan:
        if "rows" in kinds:
            out_specs.append(pl.BlockSpec((tm, width), lambda i: (i, 0)))
            out_shape.append(jax.ShapeDtypeStruct((rows, width), _MXU))
        if "value_t" in kinds:
            out_specs += [pl.BlockSpec((width, tm), lambda i: (0, i))] * 2
            out_shape += [jax.ShapeDtypeStruct((width, rows), _MXU)] * 2

    return pl.pallas_call(
        functools.partial(_in_kernel, plan=plan),
        grid=(rows // tm,),
        in_specs=[pl.BlockSpec((tm, d), lambda i: (i, 0)),
                  pl.BlockSpec((None, 6, d), mod_map),
                  pl.BlockSpec((d, ncols), lambda i: (0, 0)),
                  pl.BlockSpec((tm, LANES), lambda i: (i % nbt, 0)),
                  pl.BlockSpec((tm, LANES), lambda i: (i % nbt, 0)),
                  pl.BlockSpec((8, LANES), lambda i: (0, 0))],
        out_specs=out_specs,
        out_shape=out_shape,
        compiler_params=_cparams(("parallel",)),
        name="in_proj",
    )(x2d, mods, w, cos_t, sin_t, nw)


def _flash_kernel(*refs, mode, tk, n_chunks, lam_init):
    if mode == "diff":
        q_ref, k_ref, v_ref, par_ref, o_ref, qt_scr, vxt_scr, m_scr, acc_scr, st_scr = refs
    else:
        q_ref, k_ref, v_ref, o_ref, qt_scr, vxt_scr, m_scr, acc_scr, st_scr = refs
    lo_col = _head_col()

    @pl.when(pl.program_id(2) == 0)
    def _():
        for c in range(n_chunks):
            vt = v_ref[c * tk:(c + 1) * tk, :].astype(F32).T
            if mode == "diff":
                vxt_scr[c, :LANES, :] = vt.astype(vxt_scr.dtype)
                vxt_scr[c, LANES:, :] = jnp.ones((LANES, tk), vxt_scr.dtype)
            else:
                vxt_scr[0, c] = (vt * lo_col + (1.0 - lo_col)).astype(vxt_scr.dtype)
                vxt_scr[1, c] = (vt * (1.0 - lo_col) + lo_col).astype(vxt_scr.dtype)

    qt = q_ref[...].astype(F32).T
    qt_scr[0] = (qt * lo_col).astype(qt_scr.dtype)
    qt_scr[1] = (qt * (1.0 - lo_col)).astype(qt_scr.dtype)
    m_scr[...] = jnp.full(m_scr.shape, NEG_INF, F32)
    acc_scr[...] = jnp.zeros(acc_scr.shape, F32)

    def qk(c, slot):
        off = pl.multiple_of(c * tk, tk)
        k = k_ref[pl.ds(off, tk), :]
        for h in range(2):
            st_scr[slot, h] = jnp.dot(k, qt_scr[h], preferred_element_type=F32)

    def softmax_pv(c, slot):
        for h in range(2):
            st = st_scr[slot, h]
            m_prev = m_scr[h]
            m_new = jnp.maximum(m_prev, jnp.max(st, axis=0, keepdims=True))
            alpha = jnp.exp2(m_prev - m_new)
            pt = jnp.exp2(st - m_new).astype(vxt_scr.dtype)
            vxt = vxt_scr[c] if mode == "diff" else vxt_scr[h, c]
            acc_scr[h] = alpha * acc_scr[h] + jnp.dot(vxt, pt, preferred_element_type=F32)
            m_scr[h] = m_new

    qk(0, 0)

    def body(j, carry):
        c = 2 * j
        qk(c + 1, 1)
        softmax_pv(c, 0)
        qk(c + 2, 0)
        softmax_pv(c + 1, 1)
        return carry

    lax.fori_loop(0, (n_chunks - 1) // 2, body, 0)
    if n_chunks % 2 == 1:
        softmax_pv(n_chunks - 1, 0)
    else:
        qk(n_chunks - 1, 1)
        softmax_pv(n_chunks - 2, 0)
        softmax_pv(n_chunks - 1, 1)

    a_lo = acc_scr[0]
    a_hi = acc_scr[1]
    if mode == "diff":
        lam = (jnp.exp(jnp.sum(par_ref[0:1, :] * par_ref[1:2, :], axis=1, keepdims=True))
               - jnp.exp(jnp.sum(par_ref[2:3, :] * par_ref[3:4, :], axis=1, keepdims=True))
               + lam_init)
        out_t = a_lo[:LANES] / a_lo[LANES:] - lam * (a_hi[:LANES] / a_hi[LANES:])
        ms = jnp.mean(out_t * out_t, axis=0, keepdims=True)
        out = (out_t * lax.rsqrt(ms + RMS_EPS)).T * par_ref[4:5, :] * (1.0 - lam_init)
    else:
        out = _pair_out_t(a_lo, a_hi).T
    o_ref[...] = out.astype(o_ref.dtype)


def _pick_tile(n, candidates):
    for c in candidates:
        if n % c == 0:
            return c
    raise ValueError(f"no tile for {n}")


def _flash(q_arr, k_arr, v_arr, *, n_groups, q_rows, q_off, kv_rows, kv_off, k_col, v_col, mode, par=None,
           lam_init=0.0, name):
    batch = q_arr.shape[0]
    tq = _pick_tile(q_rows, (1024, 512, 256))
    tk = _pick_tile(kv_rows, (768, 512, 384, 256, 128))
    nq, n_chunks = q_rows // tq, kv_rows // tk
    assert q_off % tq == 0 and kv_off % kv_rows == 0
    vw = 2 * LANES if mode == "diff" else LANES
    in_specs = [pl.BlockSpec((None, tq, LANES), lambda b, g, i: (b, q_off // tq + i, g)),
                pl.BlockSpec((None, kv_rows, LANES), lambda b, g, i: (b, kv_off // kv_rows, k_col + g)),
                pl.BlockSpec((None, kv_rows, LANES), lambda b, g, i: (b, kv_off // kv_rows, v_col + g))]
    args = [q_arr, k_arr, v_arr]
    if mode == "diff":
        in_specs.append(pl.BlockSpec(par.shape, lambda b, g, i: (0, 0)))
        args.append(par)
    return pl.pallas_call(
        functools.partial(_flash_kernel, mode=mode, tk=tk, n_chunks=n_chunks, lam_init=lam_init),
        grid=(batch, n_groups, nq),
        in_specs=in_specs,
        out_specs=pl.BlockSpec((None, tq, LANES), lambda b, g, i: (b, i, g)),
        out_shape=jax.ShapeDtypeStruct((batch, q_rows, n_groups * LANES), _MXU),
        scratch_shapes=[pltpu.VMEM((2, LANES, tq), _MXU),
                        pltpu.VMEM((n_chunks, vw, tk) if mode == "diff" else (2, n_chunks, vw, tk), _MXU),
                        pltpu.VMEM((2, 1, tq), F32),
                        pltpu.VMEM((2, vw, tq), F32),
                        pltpu.VMEM((2, 2, tk, tq), F32)],
        compiler_params=_cparams(("parallel", "parallel", "arbitrary")),
        name=name,
    )(*args)


def _head_col():
    sub = lax.broadcasted_iota(jnp.int32, (LANES, 1), 0)
    return jnp.where(sub < HEAD_DIM, 1.0, 0.0)


def _pair_out_t(a_lo, a_hi):
    return jnp.concatenate([a_lo[:HEAD_DIM] / a_lo[HEAD_DIM:], a_hi[HEAD_DIM:] / a_hi[:HEAD_DIM]], axis=0)


def _na_kernel(q_ref, kc_ref, k0_ref, k1_ref, k2_ref, *rest, n_pairs):
    vx_refs = (rest[0:4], rest[4:8])
    bias_ref, o_ref, st_scr = rest[8], rest[9], rest[10]
    tq = q_ref.shape[0]
    k_refs = (kc_ref, k0_ref, k1_ref, k2_ref)
    lo_col = _head_col()
    for p in range(n_pairs):
        cols = slice(p * LANES, (p + 1) * LANES)
        qt = q_ref[:, cols].astype(F32).T
        for half in range(2):
            col = lo_col if half == 0 else 1.0 - lo_col
            qth = (qt * col).astype(q_ref.dtype)
            st_scr[p % 2, half, 0] = jnp.dot(kc_ref[:, cols], qth, preferred_element_type=F32)
            for j in range(1, 4):
                st_scr[p % 2, half, j] = (jnp.dot(k_refs[j][:, cols], qth, preferred_element_type=F32)
                                          + bias_ref[p, half, (j - 1) * tq:j * tq, :])
        halves = []
        for half in range(2):
            sts = [st_scr[p % 2, half, j] for j in range(4)]
            m = functools.reduce(jnp.maximum, [jnp.max(st, axis=0, keepdims=True) for st in sts])
            acc = jnp.zeros((LANES, tq), F32)
            for j in range(4):
                pt = jnp.exp2(sts[j] - m).astype(q_ref.dtype)
                acc = acc + jnp.dot(vx_refs[half][j][cols, :], pt, preferred_element_type=F32)
            halves.append(acc)
        o_ref[:, cols] = _pair_out_t(*halves).T.astype(o_ref.dtype)


def _na_bias_tables(rpb, rows):
    nh = rpb.shape[0]
    nkr = 3 * NA_Q_ROWS
    qc = np.arange(GRID_W)[:, None]
    kc = np.arange(GRID_W)[None, :]
    ws = np.clip(qc - NA_KW // 2, 0, GRID_W - NA_KW)
    cvalid = ((kc >= ws) & (kc < ws + NA_KW)).reshape(-1)
    dc = (kc - qc + NA_KW - 1).reshape(-1)
    onehot = ((np.arange(2 * NA_KW - 1)[:, None] == dc[None, :]) & cvalid[None, :]).astype(np.float32)
    tiles = jnp.einsum("hrd,dx->hrx", rpb.astype(F32) * LOG2E, jnp.asarray(onehot),
                       precision=lax.Precision.HIGHEST)
    tiles = jnp.where(cvalid[None, None, :], tiles, NEG_INF)
    qr = np.arange(NA_Q_ROWS)[:, None]
    kr = np.arange(nkr)[None, :]
    tables = []
    for variant in range(3):
        if variant == 0:
            r0, k0, nrows = 0, 0, rows
        elif variant == 1:
            r0, k0, nrows = 2 * NA_Q_ROWS, NA_Q_ROWS, 8 * NA_Q_ROWS
        else:
            r0, k0, nrows = rows - NA_Q_ROWS, rows - nkr, rows
        r = r0 + qr
        rp = k0 + kr
        rs = np.clip(r - NA_KH // 2, 0, nrows - NA_KH)
        rvalid = (rp >= rs) & (rp < rs + NA_KH)
        dr = np.clip(rp - r + NA_KH - 1, 0, 2 * NA_KH - 2)
        t = jnp.take(tiles, jnp.asarray(dr.reshape(-1).astype(np.int32)), axis=1)
        t = jnp.where(rvalid.reshape(-1)[None, :, None], t, NEG_INF)
        t = t.reshape(nh, NA_Q_ROWS, nkr, GRID_W, GRID_W).transpose(0, 2, 4, 1, 3)
        tables.append(t.reshape(nh // 2, 2, nkr * GRID_W, NA_Q_ROWS * GRID_W))
    return jnp.stack(tables)


def _neighbourhood(q_arr, k_arr, vx_lo, vx_hi, bias, *, s_len, n_ctx):
    batch, t_len, width = q_arr.shape
    tq = NA_Q_ROWS * GRID_W
    assert n_ctx == tq
    nq = s_len // tq
    nbt = t_len // tq
    n_pairs = NA_HEADS // 2
    cb = s_len // n_ctx

    def seg_blocks(i):
        first = jnp.clip(i - 1, 0, nq - 3)
        return [cb, first, first + 1, first + 2]

    q_spec = pl.BlockSpec((None, tq, width), lambda i, b: (b, i, 0))
    k_specs = [pl.BlockSpec((None, tq, width), lambda i, b, j=j: (b, seg_blocks(i)[j], 0)) for j in range(4)]
    v_specs = [pl.BlockSpec((width, tq), lambda i, b, j=j: (0, b * nbt + seg_blocks(i)[j])) for j in range(4)]
    bias_spec = pl.BlockSpec((None, n_pairs, 2, 3 * tq, tq),
                             lambda i, b: (jnp.where(i == 0, 0, jnp.where(i == nq - 1, 2, 1)), 0, 0, 0, 0))
    return pl.pallas_call(
        functools.partial(_na_kernel, n_pairs=n_pairs),
        grid=(nq, batch),
        in_specs=[q_spec] + k_specs + v_specs + v_specs + [bias_spec],
        out_specs=pl.BlockSpec((None, tq, width), lambda i, b: (b, i, 0)),
        out_shape=jax.ShapeDtypeStruct((batch, s_len, width), _MXU),
        scratch_shapes=[pltpu.VMEM((2, 2, 4, tq, tq), F32)],
        compiler_params=_cparams(("parallel", "parallel")),
        name="neighbourhood_attn",
    )(q_arr, *([k_arr] * 4), *([vx_lo] * 4), *([vx_hi] * 4), bias)


def _windowed_kernel(q_ref, kc_ref, k0_ref, k1_ref, k2_ref, *rest, n_pairs):
    vx_refs = (rest[0:4], rest[4:8])
    sink_ref, o_ref, st_scr = rest[8], rest[9], rest[10]
    tq = q_ref.shape[0]
    qi = pl.program_id(1)
    k_refs = (kc_ref, k0_ref, k1_ref, k2_ref)
    lo_col = _head_col()
    qt = jnp.concatenate([q_ref[:, p * LANES:(p + 1) * LANES].astype(F32).T for p in range(n_pairs)], axis=1)

    qq = lax.broadcasted_iota(jnp.int32, (1, n_pairs * LANES), 1) & (LANES - 1)
    kk = lax.broadcasted_iota(jnp.int32, (tq, 1), 0)
    masks = (None, jnp.logical_and(kk >= qq, qi >= 1), None,
             jnp.logical_and(kk <= qq, qi < pl.num_programs(1) - 1))
    n_ctx = kc_ref.shape[0]
    offs = (0, n_ctx, n_ctx + tq, n_ctx + 2 * tq, n_ctx + 3 * tq)
    for half in range(2):
        col = lo_col if half == 0 else 1.0 - lo_col
        qth = (qt * col).astype(q_ref.dtype)
        for j in range(4):
            st = jnp.dot(k_refs[j][...], qth, preferred_element_type=F32)
            st_scr[half, offs[j]:offs[j + 1], :] = st if masks[j] is None else jnp.where(masks[j], st, NEG_INF)
    halves = []
    for half in range(2):
        col = lo_col if half == 0 else 1.0 - lo_col
        sink = sink_ref[half:half + 1, :]
        m = jnp.maximum(jnp.max(st_scr[half], axis=0, keepdims=True), sink)
        acc = jnp.exp2(sink - m) * (1.0 - col)
        for j in range(4):
            pt = jnp.exp2(st_scr[half, offs[j]:offs[j + 1], :] - m).astype(q_ref.dtype)
            acc = acc + jnp.dot(vx_refs[half][j][...], pt, preferred_element_type=F32)
        halves.append(acc)
    out_t = _pair_out_t(*halves)
    for p in range(n_pairs):
        o_ref[:, p * LANES:(p + 1) * LANES] = out_t[:, p * LANES:(p + 1) * LANES].T.astype(o_ref.dtype)


def _windowed(q_arr, k_arr, vx_lo, vx_hi, sink, *, s_len, n_ctx):
    batch, t_len = q_arr.shape[0], q_arr.shape[1]
    tq = SWA_WINDOW
    nb = s_len // tq
    n_pairs = SWA_HEADS // 2
    cb = s_len // n_ctx
    qw = n_pairs * LANES
    q_spec = pl.BlockSpec((None, tq, qw), lambda b, i: (b, i, 0))
    k_specs = [pl.BlockSpec((None, n_ctx, LANES), lambda b, i: (b, cb, 0))]
    v_specs = [pl.BlockSpec((LANES, n_ctx), lambda b, i: (0, b * (t_len // n_ctx) + cb))]
    for j in range(3):
        k_specs.append(pl.BlockSpec((None, tq, LANES),
                                    lambda b, i, j=j: (b, jnp.clip(i - 1 + j, 0, nb - 1), 0)))
        v_specs.append(pl.BlockSpec((LANES, tq),
                                    lambda b, i, j=j: (0, b * (t_len // tq) + jnp.clip(i - 1 + j, 0, nb - 1))))
    return pl.pallas_call(
        functools.partial(_windowed_kernel, n_pairs=n_pairs),
        grid=(batch, nb),
        in_specs=[q_spec] + k_specs + v_specs + v_specs + [pl.BlockSpec(sink.shape, lambda b, i: (0, 0))],
        out_specs=pl.BlockSpec((None, tq, qw), lambda b, i: (b, i, 0)),
        out_shape=jax.ShapeDtypeStruct((batch, s_len, qw), _MXU),
        scratch_shapes=[pltpu.VMEM((2, n_ctx + 3 * tq, qw), F32)],
        compiler_params=_cparams(("parallel", "parallel")),
        name="windowed_attn",
    )(q_arr, *([k_arr] * 4), *([vx_lo] * 4), *([vx_hi] * 4), sink)


def _layernorm(z, g, b):
    mu = jnp.mean(z, axis=1, keepdims=True)
    zc = z - mu
    var = jnp.mean(zc * zc, axis=1, keepdims=True)
    return zc * lax.rsqrt(var + LN_EPS) * g + b


def _post_kernel(x_ref, ya_ref, yb_ref, wo_ref, mod_ref, g_ref, b_ref, rw_ref, rb_ref,
                 x1_ref, tok_ref, route_ref, *, alpha):
    half = ya_ref.shape[1]
    y = (jnp.dot(ya_ref[...], wo_ref[:half, :], preferred_element_type=F32)
         + jnp.dot(yb_ref[...], wo_ref[half:, :], preferred_element_type=F32))
    x1 = _layernorm(alpha * x_ref[...] + mod_ref[2:3, :] * y, g_ref[...], b_ref[...])
    x1_ref[...] = x1
    tok = x1 * (1.0 + mod_ref[4:5, :]) + mod_ref[3:4, :]
    tok_ref[...] = tok
    logits = jnp.dot(tok.astype(_MXU), rw_ref[...], preferred_element_type=F32) + rb_ref[...]
    tm = logits.shape[0]
    lane = lax.broadcasted_iota(jnp.int32, (tm, LANES), 1).astype(F32)
    vals, idxs = [], []
    for _ in range(TOP_K):
        mx = jnp.max(logits, axis=1, keepdims=True)
        ix = jnp.min(jnp.where(logits == mx, lane, float(LANES)), axis=1, keepdims=True)
        vals.append(mx)
        idxs.append(ix)
        logits = jnp.where(lane == ix, -3.0e38, logits)
    es = [jnp.exp(v - vals[0]) for v in vals]
    den = functools.reduce(lambda a, c: a + c, es)
    route = jnp.zeros((tm, LANES), F32)
    for k in range(TOP_K):
        route = jnp.where(lane == float(k), idxs[k], route)
        route = jnp.where(lane == float(TOP_K + k), es[k] / den, route)
    route_ref[...] = route


def _post(x2d, ya, yb, wo, mods, g, b, rw, rb, *, batch, nbt_in, nblk, n_lat_blk, alpha):
    d = x2d.shape[1]
    tm = ROW_TILE
    half = ya.shape[1]

    def mod_map(bi, t):
        return (jnp.where(t >= n_lat_blk, batch, bi), 0, 0)

    rows_out = batch * nblk * tm
    o_map = lambda bi, t: (bi * nblk + t, 0)
    return pl.pallas_call(
        functools.partial(_post_kernel, alpha=alpha),
        grid=(batch, nblk),
        in_specs=[pl.BlockSpec((tm, d), lambda bi, t: (bi * nbt_in + t, 0)),
                  pl.BlockSpec((tm, half), o_map),
                  pl.BlockSpec((tm, half), o_map),
                  pl.BlockSpec((d, d), lambda bi, t: (0, 0)),
                  pl.BlockSpec((None, 6, d), mod_map),
                  pl.BlockSpec((1, d), lambda bi, t: (0, 0)),
                  pl.BlockSpec((1, d), lambda bi, t: (0, 0)),
                  pl.BlockSpec((d, LANES), lambda bi, t: (0, 0)),
                  pl.BlockSpec((1, LANES), lambda bi, t: (0, 0))],
        out_specs=[pl.BlockSpec((tm, d), o_map), pl.BlockSpec((tm, d), o_map), pl.BlockSpec((tm, LANES), o_map)],
        out_shape=[jax.ShapeDtypeStruct((rows_out, d), F32), jax.ShapeDtypeStruct((rows_out, d), F32),
                   jax.ShapeDtypeStruct((rows_out, LANES), F32)],
        compiler_params=_cparams(("parallel", "parallel")),
        name="post_attn",
    )(x2d, ya, yb, wo, mods, g, b, rw, rb)


def _moe_kernel(be_ref, nu_ref, dst_ref, src0_ref, src1_ref, src2_ref, tok_hbm, wi_ref, bi_ref, wo_ref, bo_ref,
                y_hbm, xbuf0, xbuf1, xbuf2, obuf0, obuf1, obuf2, gsem, ssem, wi_s, wo_s, *, dump0):
    i = pl.program_id(0)
    n_used = nu_ref[0]
    f = wo_s.shape[0]
    xbufs = (xbuf0, xbuf1, xbuf2)
    obufs = (obuf0, obuf1, obuf2)

    def gather_copy(row, r, slot):
        return pltpu.make_async_copy(tok_hbm.at[pl.ds(row, 1), :], xbufs[slot].at[pl.ds(r, 1), :], gsem.at[slot])

    def scatter_copy(row, r, slot):
        return pltpu.make_async_copy(obufs[slot].at[pl.ds(r, 1), :], y_hbm.at[pl.ds(row, 1), :], ssem.at[slot])

    def start_gather(rows_ref, slot):
        for r in range(MOE_ROWS):
            gather_copy(rows_ref[0, r], r, slot).start(priority=r % 2)

    def wait_gather(slot):
        for _ in range(MOE_ROWS):
            gather_copy(0, 0, slot).wait()

    def wait_scatter(slot):
        for _ in range(MOE_ROWS):
            scatter_copy(0, 0, slot).wait()

    @pl.when(i == 0)
    def _():
        for slot in range(MOE_RING):
            obufs[slot][...] = jnp.zeros(obufs[slot].shape, F32)
            for r in range(MOE_ROWS):
                scatter_copy(dump0 + slot * MOE_ROWS + r, r, slot).start(priority=r % 2)
        start_gather(src0_ref, 0)
        start_gather(src1_ref, 1)

    last_used = n_used - 1
    changed = jnp.logical_or(i == 0, be_ref[jnp.clip(i - 1, 0, last_used)] != be_ref[jnp.minimum(i, last_used)])

    for slot in range(MOE_RING):
        @pl.when(jnp.logical_and(i < n_used, i % MOE_RING == slot))
        def _(slot=slot):
            wait_scatter(slot)
            wait_gather(slot)

            @pl.when(changed)
            def _():
                wi_s[...] = wi_ref[...].astype(wi_s.dtype)
                wo_s[...] = wo_ref[...].astype(wo_s.dtype)

            x = xbufs[slot][...].astype(wi_s.dtype)
            start_gather(src2_ref, (slot + 2) % MOE_RING)
            hh = jnp.dot(x, wi_s[...], preferred_element_type=F32) + bi_ref[...]
            gate = jnp.minimum(hh[:, :f], SWIGLU_LIMIT)
            up = jnp.clip(hh[:, f:], -SWIGLU_LIMIT, SWIGLU_LIMIT)
            act = gate * (1.0 / (1.0 + jnp.exp(-SWIGLU_ALPHA * gate))) * (up + 1.0)
            obufs[slot][...] = jnp.dot(act.astype(wo_s.dtype), wo_s[...], preferred_element_type=F32) + bo_ref[...]
            for r in range(MOE_ROWS):
                scatter_copy(dst_ref[0, r], r, slot).start(priority=r % 2)

        @pl.when(jnp.logical_and(i >= n_used, i % MOE_RING == slot))
        def _(slot=slot):
            @pl.when(i < n_used + 2)
            def _():
                wait_gather(slot)

            @pl.when(i < n_used + MOE_RING)
            def _():
                wait_scatter(slot)


def _experts(tok, src_rows, dst_rows, block_e, n_used, w_in, b_in, w_out, b_out, layer):
    n_tok, d = tok.shape
    _, n_e, _, f2 = w_in.shape
    f = f2 // 2
    n_blocks = src_rows.shape[0]

    def blk(i, be, nu):
        return jnp.minimum(i, nu[0] - 1)

    def e_map(i, be, nu):
        return (layer, be[blk(i, be, nu)], 0, 0)

    def rows_spec(ahead):
        return pl.BlockSpec((None, 1, MOE_ROWS), lambda i, be, nu: (blk(i + ahead, be, nu), 0, 0),
                            memory_space=pltpu.SMEM)

    row_buf = pltpu.VMEM((MOE_ROWS, d), F32)
    grid_spec = pltpu.PrefetchScalarGridSpec(
        num_scalar_prefetch=2,
        grid=(n_blocks + MOE_RING,),
        in_specs=[rows_spec(0), rows_spec(0), rows_spec(1), rows_spec(2),
                  pl.BlockSpec(memory_space=pl.ANY),
                  pl.BlockSpec((None, None, d, f2), e_map),
                  pl.BlockSpec((None, None, 1, f2), e_map),
                  pl.BlockSpec((None, None, f, d), e_map),
                  pl.BlockSpec((None, None, 1, d), e_map)],
        out_specs=pl.BlockSpec(memory_space=pl.ANY),
        scratch_shapes=[row_buf] * (2 * MOE_RING)
                       + [pltpu.SemaphoreType.DMA((MOE_RING,)), pltpu.SemaphoreType.DMA((MOE_RING,)),
                          pltpu.VMEM((d, f2), _MXU), pltpu.VMEM((f, d), _MXU)],
    )
    depth = w_in.shape[0]
    return pl.pallas_call(
        functools.partial(_moe_kernel, dump0=TOP_K * n_tok), grid_spec=grid_spec,
        out_shape=jax.ShapeDtypeStruct((TOP_K * n_tok + MOE_RING * MOE_ROWS, d), F32),
        compiler_params=_cparams(("arbitrary",)),
        name="experts",
    )(block_e, n_used, dst_rows, src_rows, src_rows, src_rows, tok, w_in, b_in.reshape(depth, n_e, 1, f2), w_out,
      b_out.reshape(depth, n_e, 1, d))


def _route_plan(route, n_experts):
    n = route.shape[0]
    top_idx = route[:, :TOP_K].astype(jnp.int32)
    nk = n * TOP_K
    flat_e = top_idx.reshape(-1)
    order = jnp.argsort(flat_e).astype(jnp.int32)
    experts = jnp.arange(n_experts, dtype=jnp.int32)
    counts = jnp.sum(flat_e[:, None] == experts[None, :], axis=0, dtype=jnp.int32)
    padded = (counts + MOE_ROWS - 1) // MOE_ROWS * MOE_ROWS
    start = jnp.cumsum(counts) - counts
    pend = jnp.cumsum(padded)
    pstart = pend - padded
    n_blocks = -(-nk // MOE_ROWS) + n_experts
    first = jnp.arange(n_blocks, dtype=jnp.int32) * MOE_ROWS
    block_e = jnp.minimum(jnp.sum(pend[None, :] <= first[:, None], axis=1, dtype=jnp.int32), n_experts - 1)
    n_used = (pend[-1:] // MOE_ROWS).astype(jnp.int32)
    blk = jnp.arange(n_blocks, dtype=jnp.int32)[:, None]
    row = jnp.arange(MOE_ROWS, dtype=jnp.int32)[None, :]
    rank = blk * MOE_ROWS + row - jnp.take(pstart, block_e, mode="clip")[:, None]
    src = jnp.clip(jnp.take(start, block_e, mode="clip")[:, None] + rank, 0, nk - 1)
    valid = rank < jnp.take(counts, block_e, mode="clip")[:, None]
    pair = jnp.take(order, src, mode="clip")
    token = pair // TOP_K
    src_rows = jnp.where(valid, token, 0)
    dst_rows = jnp.where(valid, (pair % TOP_K) * n + token, nk + (blk % MOE_RING) * MOE_ROWS + row)
    return (src_rows.reshape(n_blocks, 1, MOE_ROWS), dst_rows.reshape(n_blocks, 1, MOE_ROWS), block_e, n_used)


def _ln2_kernel(x_ref, y0_ref, y1_ref, y2_ref, y3_ref, route_ref, mod_ref, g_ref, b_ref, o_ref, *, alpha):
    ys = (y0_ref, y1_ref, y2_ref, y3_ref)
    f = route_ref[:, TOP_K:TOP_K + 1] * ys[0][...]
    for k in range(1, TOP_K):
        f = f + route_ref[:, TOP_K + k:TOP_K + k + 1] * ys[k][...]
    o_ref[...] = _layernorm(alpha * x_ref[...] + mod_ref[5:6, :] * f, g_ref[...], b_ref[...])


def _ln2(x1, y, route, mods, g, b, *, batch, nblk, n_lat_blk, alpha):
    rows, d = x1.shape
    tm = ROW_TILE
    r_map = lambda bi, t: (bi * nblk + t, 0)

    def y_spec(k):
        return pl.BlockSpec((tm, d), lambda bi, t: (k * (rows // tm) + bi * nblk + t, 0))

    def mod_map(bi, t):
        return (jnp.where(t >= n_lat_blk, batch, bi), 0, 0)

    return pl.pallas_call(
        functools.partial(_ln2_kernel, alpha=alpha),
        grid=(batch, nblk),
        in_specs=[pl.BlockSpec((tm, d), r_map)] + [y_spec(k) for k in range(TOP_K)]
                 + [pl.BlockSpec((tm, LANES), r_map), pl.BlockSpec((None, 6, d), mod_map),
                    pl.BlockSpec((1, d), lambda bi, t: (0, 0)), pl.BlockSpec((1, d), lambda bi, t: (0, 0))],
        out_specs=pl.BlockSpec((tm, d), r_map),
        out_shape=jax.ShapeDtypeStruct((rows, d), F32),
        compiler_params=_cparams(("parallel", "parallel")),
        name="combine_ln2",
    )(x1, y, y, y, y, route, mods, g, b)


def _rope_layout(w, n_heads):
    lead = w.shape[:-1]
    return w.reshape(lead + (n_heads, HEAD_DIM // 2, 2)).swapaxes(-1, -2).reshape(lead + (n_heads * HEAD_DIM,))


def _pair_layout(w, n_heads, axis=-1):
    axis = axis % w.ndim
    shape = w.shape
    w = w.reshape(shape[:axis] + (2, n_heads // 2, HEAD_DIM) + shape[axis + 1:])
    return w.swapaxes(axis, axis + 1).reshape(shape)


def _rope_tables(s_len, n_ctx):
    t = np.arange(s_len)
    row = (t // GRID_W).astype(np.float32)
    col = (t % GRID_W).astype(np.float32)
    axis_dim = HEAD_DIM // 2
    freqs = jnp.asarray(ROPE_THETA, F32) ** (-jnp.arange(0, axis_dim, 2, dtype=F32) / axis_dim)
    ang = jnp.concatenate([jnp.asarray(row)[:, None] * freqs, jnp.asarray(col)[:, None] * freqs], axis=-1)
    cos, sin = jnp.cos(ang), jnp.sin(ang)
    cos_h = jnp.concatenate([cos, cos], axis=-1)
    sin_h = jnp.concatenate([-sin, sin], axis=-1)
    cos_t = jnp.concatenate([cos_h, jnp.ones((n_ctx, HEAD_DIM), F32)], axis=0)
    sin_t = jnp.concatenate([sin_h, jnp.zeros((n_ctx, HEAD_DIM), F32)], axis=0)
    return jnp.tile(cos_t, (1, 2)), jnp.tile(sin_t, (1, 2))


def _moe_and_ln2(x1, tok, route, mods, n_experts, w_in, b_in, w_out, b_out, layer, g, b, *,
                 batch, nblk, n_lat_blk, alpha):
    src_rows, dst_rows, block_e, n_used = _route_plan(route, n_experts)
    y = _experts(tok, src_rows, dst_rows, block_e, n_used, w_in, b_in, w_out, b_out, layer)
    return _ln2(x1, y, route, mods, g, b, batch=batch, nblk=nblk, n_lat_blk=n_lat_blk, alpha=alpha)


def kernel(x, c, ctx, c_ctx, mod_w, mod_b, ln1_g, ln1_b, ln2_g, ln2_b, router_w, router_b, moe_w_in, moe_b_in,
           moe_w_out, moe_b_out, ab_w_in, ab_w_out, na_rpb, diff_lq1, diff_lk1, diff_lq2, diff_lk2, diff_subln,
           cd_w_in, cd_w_out, gqa_q_norm, gqa_k_norm, swa_sink):
    batch, s_len, d = x.shape
    n_ctx = ctx.shape[1]
    t_len = n_ctx + s_len
    depth = mod_w.shape[0]
    n_experts = router_w.shape[2]
    alpha = (2.0 * depth) ** 0.25
    tm = ROW_TILE
    nbt = t_len // tm
    nbs = s_len // tm
    assert depth == 2 and n_ctx % tm == 0 and s_len % tm == 0 and batch + 1 <= 8

    cc = jnp.zeros((8, d), F32).at[:batch].set(c).at[batch].set(c_ctx)
    mod_all = _modulation(cc, mod_w, mod_b)
    mods = [mod_all[l, :batch + 1].reshape(batch + 1, 6, d) for l in range(depth)]
    cos_t, sin_t = _rope_tables(s_len, n_ctx)
    rw = [jnp.zeros((d, LANES), F32).at[:, :n_experts].set(router_w[l]).astype(_MXU) for l in range(depth)]
    rb = [jnp.full((1, LANES), NEG_INF, F32).at[0, :n_experts].set(router_b[l]) for l in range(depth)]

    stream = jnp.concatenate([x, ctx], axis=1).reshape(batch * t_len, d)
    by_batch = lambda a: a.reshape(batch, t_len, a.shape[-1])

    na_w = NA_HEADS * HEAD_DIM
    df_w = DIFF_HEADS * 2 * HEAD_DIM
    wab = ab_w_in[0]
    w0 = jnp.concatenate([wab[:, :3 * na_w], _rope_layout(wab[:, 3 * na_w:3 * na_w + df_w], 2 * DIFF_HEADS),
                          _rope_layout(wab[:, 3 * na_w + df_w:3 * na_w + 2 * df_w], 2 * DIFF_HEADS),
                          wab[:, 3 * na_w + 2 * df_w:]], axis=1).astype(_MXU)
    rows_only = ("rows",)
    plan0 = ((0, na_w, None, False, QSCALE, rows_only), (na_w, na_w, None, False, 1.0, rows_only),
             (2 * na_w, na_w, None, False, 1.0, ("rows", "value_t")),
             (3 * na_w, df_w, None, True, QSCALE, rows_only), (3 * na_w + df_w, df_w, None, True, 1.0, rows_only),
             (3 * na_w + 2 * df_w, df_w, None, False, 1.0, rows_only))
    nw0 = jnp.zeros((8, LANES), F32)
    nq, nk, nv, nvx_lo, nvx_hi, dq, dk, dv = _in_proj(stream, mods[0], w0, cos_t, sin_t, nw0, plan0,
                                                      batch=batch, n_ctx=n_ctx)
    nq, nk, nv, dq, dk, dv = map(by_batch, (nq, nk, nv, dq, dk, dv))
    lam_init0 = 0.8 - 0.6 * math.exp(-0.3 * 0)
    par = jnp.zeros((8, LANES), F32)
    par = par.at[0, :HEAD_DIM].set(diff_lq1[0]).at[1, :HEAD_DIM].set(diff_lk1[0])
    par = par.at[2, :HEAD_DIM].set(diff_lq2[0]).at[3, :HEAD_DIM].set(diff_lk2[0]).at[4].set(diff_subln[0])
    bias = _na_bias_tables(na_rpb[0], s_len // GRID_W)
    y_na = _neighbourhood(nq, nk, nvx_lo, nvx_hi, bias, s_len=s_len, n_ctx=n_ctx)
    y_na_c = _flash(nq, nk, nv, n_groups=NA_HEADS // 2, q_rows=n_ctx, q_off=s_len, kv_rows=n_ctx,
                    kv_off=s_len, k_col=0, v_col=0, mode="pair", name="neighbourhood_attn_ctx")
    y_df = _flash(dq, dk, dv, n_groups=DIFF_HEADS, q_rows=s_len, q_off=0, kv_rows=t_len, kv_off=0, k_col=0, v_col=0,
                  mode="diff", par=par, lam_init=lam_init0, name="diff_attn")
    y_df_c = _flash(dq, dk, dv, n_groups=DIFF_HEADS, q_rows=n_ctx, q_off=s_len, kv_rows=n_ctx, kv_off=s_len,
                    k_col=0, v_col=0,
                    mode="diff", par=par, lam_init=lam_init0, name="diff_attn_ctx")
    ya = jnp.concatenate([y_na, y_na_c], axis=1).reshape(batch * t_len, -1)
    yb = jnp.concatenate([y_df, y_df_c], axis=1).reshape(batch * t_len, -1)
    x1, tok, route = _post(stream, ya, yb, ab_w_out[0].astype(_MXU), mods[0], ln1_g[0][None], ln1_b[0][None],
                           rw[0], rb[0], batch=batch, nbt_in=nbt, nblk=nbt, n_lat_blk=nbs, alpha=alpha)
    stream = _moe_and_ln2(x1, tok, route, mods[0], n_experts, moe_w_in, moe_b_in, moe_w_out, moe_b_out, 0,
                          ln2_g[0][None], ln2_b[0][None], batch=batch, nblk=nbt, n_lat_blk=nbs, alpha=alpha)

    hw = GQA_HEADS * HEAD_DIM
    kw = GQA_KV_HEADS * HEAD_DIM
    wcd = cd_w_in[0]
    gq, gk, gv, wq, wk, wv = jnp.split(wcd, [hw, hw + kw, hw + 2 * kw, 2 * hw + 2 * kw, 2 * hw + 3 * kw], axis=1)
    w1 = jnp.concatenate([_pair_layout(_rope_layout(gq, GQA_HEADS), GQA_HEADS),
                          _pair_layout(_rope_layout(wq, SWA_HEADS), SWA_HEADS),
                          _rope_layout(gk, GQA_KV_HEADS), gv, _rope_layout(wk, SWA_KV_HEADS), wv],
                         axis=1).astype(_MXU)
    plan1 = ((0, hw, 0, True, QSCALE, rows_only), (hw, hw, None, True, QSCALE, rows_only),
             (2 * hw, kw, 1, True, 1.0, rows_only), (2 * hw + kw, kw, None, False, 1.0, rows_only),
             (2 * hw + 2 * kw, kw, None, True, 1.0, rows_only),
             (2 * hw + 3 * kw, kw, None, False, 1.0, ("value_t",)))
    nw1 = jnp.zeros((8, LANES), F32).at[0].set(jnp.tile(_rope_layout(gqa_q_norm[0], 1), 2))
    nw1 = nw1.at[1].set(jnp.tile(_rope_layout(gqa_k_norm[0], 1), 2))
    gq, wq, gk, gv, wk, wvx_lo, wvx_hi = _in_proj(stream, mods[1], w1, cos_t, sin_t, nw1, plan1,
                                                  batch=batch, n_ctx=n_ctx)
    gq, wq, gk, gv, wk = map(by_batch, (gq, wq, gk, gv, wk))
    n_qp = GQA_HEADS // 2
    y_c = _flash(gq.reshape(batch, t_len * n_qp, LANES), gk, gv, n_groups=1, q_rows=s_len * n_qp,
                 q_off=0, kv_rows=t_len, kv_off=0, k_col=0, v_col=0, mode="pair", name="gqa_attn")
    sink = jnp.repeat(swa_sink[0].reshape(2, SWA_HEADS // 2) * LOG2E, LANES, axis=1).astype(F32)
    y_d = _windowed(wq, wk, wvx_lo, wvx_hi, sink, s_len=s_len, n_ctx=n_ctx)
    wo1 = jnp.concatenate([_pair_layout(cd_w_out[0][:hw], GQA_HEADS, axis=0),
                           _pair_layout(cd_w_out[0][hw:], SWA_HEADS, axis=0)], axis=0).astype(_MXU)
    x1, tok, route = _post(stream, y_c.reshape(batch * s_len, -1), y_d.reshape(batch * s_len, -1), wo1, mods[1],
                           ln1_g[1][None], ln1_b[1][None], rw[1], rb[1], batch=batch, nbt_in=nbt,
                           nblk=nbs, n_lat_blk=nbs, alpha=alpha)
    out = _moe_and_ln2(x1, tok, route, mods[1], n_experts, moe_w_in, moe_b_in, moe_w_out, moe_b_out, 1,
                       ln2_g[1][None], ln2_b[1][None], batch=batch, nblk=nbs, n_lat_blk=nbs, alpha=alpha)
    return out.reshape(batch, s_len, d)
```

```python
import functools
import math

import numpy as np
import jax
import jax.numpy as jnp
from jax import lax
from jax.experimental import pallas as pl
from jax.experimental.pallas import tpu as pltpu

F32 = jnp.float32
_MXU = jnp.bfloat16

HEAD_DIM = 64
GRID_W = 64
LOG2E = math.log2(math.e)
QSCALE = HEAD_DIM ** -0.5 * LOG2E
ROPE_THETA = 10000.0
NA_HEADS = 8
NA_KH = 8
NA_KW = 16
DIFF_HEADS = 4
GQA_HEADS = 8
GQA_KV_HEADS = 2
SWA_HEADS = 8
SWA_KV_HEADS = 2
SWA_WINDOW = 128
TOP_K = 4
SWIGLU_LIMIT = 7.0
SWIGLU_ALPHA = 1.702
LN_EPS = 1e-5
RMS_EPS = 1e-6
NEG_INF = -1e30

LANES = 128
ROW_TILE = 256
MOE_ROWS = 256
MOE_RING = 3
NA_Q_ROWS = 4
VMEM_LIMIT = 52 * 1024 * 1024


def _cparams(sem, vmem=VMEM_LIMIT):
    return pltpu.CompilerParams(dimension_semantics=sem, vmem_limit_bytes=vmem)


def _mod_kernel(c_ref, w_ref, b_ref, o_ref):
    c = c_ref[...]
    a = (c / (1.0 + jnp.exp(-c))).astype(_MXU)
    o_ref[...] = jnp.dot(a, w_ref[...].astype(_MXU), preferred_element_type=F32) + b_ref[...]


def _modulation(cc, mod_w, mod_b):
    depth, d, d6 = mod_w.shape
    tn = d6 // 4
    return pl.pallas_call(
        _mod_kernel,
        grid=(depth, d6 // tn),
        in_specs=[pl.BlockSpec((8, d), lambda l, j: (0, 0)),
                  pl.BlockSpec((None, d, tn), lambda l, j: (l, 0, j)),
                  pl.BlockSpec((None, 1, tn), lambda l, j: (l, 0, j))],
        out_specs=pl.BlockSpec((None, 8, tn), lambda l, j: (l, 0, j)),
        out_shape=jax.ShapeDtypeStruct((depth, 8, d6), F32),
        compiler_params=_cparams(("arbitrary", "arbitrary")),
        name="modulation",
    )(cc, mod_w, mod_b.reshape(depth, 1, d6))


def _in_kernel(x_ref, mod_ref, w_ref, cos_ref, sin_ref, nw_ref, *o_refs, plan):
    x = x_ref[...]
    h = (x * (1.0 + mod_ref[1:2, :]) + mod_ref[0:1, :]).astype(_MXU)
    tm = x.shape[0]
    lane = lax.broadcasted_iota(jnp.int32, (tm, LANES), 1)
    even_lane = (lane & 1) == 0
    gi = lax.broadcasted_iota(jnp.int32, (LANES, LANES), 0) // HEAD_DIM
    gj = lax.broadcasted_iota(jnp.int32, (LANES, LANES), 1) // HEAD_DIM
    seg = jnp.where(gi == gj, 1.0, 0.0).astype(_MXU)
    cos = cos_ref[...]
    sin = sin_ref[...]
    lo_col = _head_col()
    outs = iter(o_refs)
    for (c0, width, norm_row, rope, scale, kinds) in plan:
        acc = jnp.dot(h, w_ref[:, c0:c0 + width], preferred_element_type=F32)
        o_rows = next(outs) if "rows" in kinds else None
        o_lo, o_hi = (next(outs), next(outs)) if "value_t" in kinds else (None, None)
        for j in range(width // LANES):
            a = acc[:, j * LANES:(j + 1) * LANES]
            if norm_row is not None:
                a2 = a * a
                hi = a2.astype(_MXU)
                lo = (a2 - hi.astype(F32)).astype(_MXU)
                ss = (jnp.dot(hi, seg, preferred_element_type=F32)
                      + jnp.dot(lo, seg, preferred_element_type=F32))
                a = a * lax.rsqrt(ss * (1.0 / HEAD_DIM) + RMS_EPS) * nw_ref[norm_row:norm_row + 1, :]
            if rope:
                partner = jnp.where(even_lane, pltpu.roll(a, LANES - 1, 1), pltpu.roll(a, 1, 1))
                a = a * cos + partner * sin
            if scale != 1.0:
                a = a * scale
            if o_rows is not None:
                o_rows[:, j * LANES:(j + 1) * LANES] = a.astype(o_rows.dtype)
            if o_lo is not None:
                at = a.T
                o_lo[j * LANES:(j + 1) * LANES, :] = (at * lo_col + (1.0 - lo_col)).astype(o_lo.dtype)
                o_hi[j * LANES:(j + 1) * LANES, :] = (at * (1.0 - lo_col) + lo_col).astype(o_hi.dtype)


def _in_proj(x2d, mods, w, cos_t, sin_t, nw, plan, *, batch, n_ctx):
    rows, d = x2d.shape
    ncols = w.shape[1]
    tm = ROW_TILE
    nbt = rows // batch // tm
    n_lat_blk = nbt - n_ctx // tm

    def mod_map(i):
        return (jnp.where(i % nbt >= n_lat_blk, batch, i // nbt), 0, 0)

    out_specs, out_shape = [], []
    for (_, width, _, _, _, kinds) in plan:
        if "rows" in kinds:
            out_specs.append(pl.BlockSpec((tm, width), lambda i: (i, 0)))
            out_shape.append(jax.ShapeDtypeStruct((rows, width), _MXU))
        if "value_t" in kinds:
            out_specs += [pl.BlockSpec((width, tm), lambda i: (0, i))] * 2
            out_shape += [jax.ShapeDtypeStruct((width, rows), _MXU)] * 2

    return pl.pallas_call(
        functools.partial(_in_kernel, plan=plan),
        grid=(rows // tm,),
        in_specs=[pl.BlockSpec((tm, d), lambda i: (i, 0)),
                  pl.BlockSpec((None, 6, d), mod_map),
                  pl.BlockSpec((d, ncols), lambda i: (0, 0)),
                  pl.BlockSpec((tm, LANES), lambda i: (i % nbt, 0)),
                  pl.BlockSpec((tm, LANES), lambda i: (i % nbt, 0)),
                  pl.BlockSpec((8, LANES), lambda i: (0, 0))],
        out_specs=out_specs,
        out_shape=out_shape,
        compiler_params=_cparams(("parallel",)),
        name="in_proj",
    )(x2d, mods, w, cos_t, sin_t, nw)


def _flash_kernel(*refs, mode, tk, n_chunks, lam_init):
    if mode == "diff":
        q_ref, k_ref, v_ref, par_ref, o_ref, qt_scr, vxt_scr, m_scr, acc_scr, st_scr = refs
    else:
        q_ref, k_ref, v_ref, o_ref, qt_scr, vxt_scr, m_scr, acc_scr, st_scr = refs
    lo_col = _head_col()

    @pl.when(pl.program_id(2) == 0)
    def _():
        for c in range(n_chunks):
            vt = v_ref[c * tk:(c + 1) * tk, :].astype(F32).T
            if mode == "diff":
                vxt_scr[c, :LANES, :] = vt.astype(vxt_scr.dtype)
                vxt_scr[c, LANES:, :] = jnp.ones((LANES, tk), vxt_scr.dtype)
            else:
                vxt_scr[0, c] = (vt * lo_col + (1.0 - lo_col)).astype(vxt_scr.dtype)
                vxt_scr[1, c] = (vt * (1.0 - lo_col) + lo_col).astype(vxt_scr.dtype)

    n_qp = q_ref.shape[1] // LANES
    qt = jnp.concatenate([q_ref[:, p * LANES:(p + 1) * LANES].astype(F32).T for p in range(n_qp)], axis=1)
    qt_scr[0] = (qt * lo_col).astype(qt_scr.dtype)
    qt_scr[1] = (qt * (1.0 - lo_col)).astype(qt_scr.dtype)
    m_scr[...] = jnp.full(m_scr.shape, NEG_INF, F32)
    acc_scr[...] = jnp.zeros(acc_scr.shape, F32)

    def qk(c, slot):
        off = pl.multiple_of(c * tk, tk)
        k = k_ref[pl.ds(off, tk), :]
        for h in range(2):
            st_scr[slot, h] = jnp.dot(k, qt_scr[h], preferred_element_type=F32)

    def softmax_pv(c, slot):
        for h in range(2):
            st = st_scr[slot, h]
            m_prev = m_scr[h]
            m_new = jnp.maximum(m_prev, jnp.max(st, axis=0, keepdims=True))
            alpha = jnp.exp2(m_prev - m_new)
            pt = jnp.exp2(st - m_new).astype(vxt_scr.dtype)
            vxt = vxt_scr[c] if mode == "diff" else vxt_scr[h, c]
            acc_scr[h] = alpha * acc_scr[h] + jnp.dot(vxt, pt, preferred_element_type=F32)
            m_scr[h] = m_new

    qk(0, 0)

    def body(j, carry):
        c = 2 * j
        qk(c + 1, 1)
        softmax_pv(c, 0)
        qk(c + 2, 0)
        softmax_pv(c + 1, 1)
        return carry

    lax.fori_loop(0, (n_chunks - 1) // 2, body, 0)
    if n_chunks % 2 == 1:
        softmax_pv(n_chunks - 1, 0)
    else:
        qk(n_chunks - 1, 1)
        softmax_pv(n_chunks - 2, 0)
        softmax_pv(n_chunks - 1, 1)

    a_lo = acc_scr[0]
    a_hi = acc_scr[1]
    if mode == "diff":
        lam = (jnp.exp(jnp.sum(par_ref[0:1, :] * par_ref[1:2, :], axis=1, keepdims=True))
               - jnp.exp(jnp.sum(par_ref[2:3, :] * par_ref[3:4, :], axis=1, keepdims=True))
               + lam_init)
        out_t = a_lo[:LANES] / a_lo[LANES:] - lam * (a_hi[:LANES] / a_hi[LANES:])
        ms = jnp.mean(out_t * out_t, axis=0, keepdims=True)
        o_ref[...] = ((out_t * lax.rsqrt(ms + RMS_EPS)).T * par_ref[4:5, :] * (1.0 - lam_init)).astype(o_ref.dtype)
    else:
        out_t = _pair_out_t(a_lo, a_hi)
        rows = o_ref.shape[0]
        for p in range(n_qp):
            o_ref[:, p * LANES:(p + 1) * LANES] = out_t[:, p * rows:(p + 1) * rows].T.astype(o_ref.dtype)


def _pick_tile(n, candidates):
    for c in candidates:
        if n % c == 0:
            return c
    raise ValueError(f"no tile for {n}")


def _flash(q_arr, k_arr, v_arr, *, n_groups, q_rows, q_off, kv_rows, kv_off, k_col, v_col, mode, n_qp=1, par=None,
           lam_init=0.0, name):
    batch = q_arr.shape[0]
    tr = _pick_tile(q_rows, (1024 // n_qp, 512 // n_qp, 256 // n_qp))
    tq = tr * n_qp
    qw = n_qp * LANES
    tk = _pick_tile(kv_rows, (768, 512, 384, 256, 128))
    nq, n_chunks = q_rows // tr, kv_rows // tk
    assert q_off % tr == 0 and kv_off % kv_rows == 0
    vw = 2 * LANES if mode == "diff" else LANES
    in_specs = [pl.BlockSpec((None, tr, qw), lambda b, g, i: (b, q_off // tr + i, g)),
                pl.BlockSpec((None, kv_rows, LANES), lambda b, g, i: (b, kv_off // kv_rows, k_col + g)),
                pl.BlockSpec((None, kv_rows, LANES), lambda b, g, i: (b, kv_off // kv_rows, v_col + g))]
    args = [q_arr, k_arr, v_arr]
    if mode == "diff":
        in_specs.append(pl.BlockSpec(par.shape, lambda b, g, i: (0, 0)))
        args.append(par)
    return pl.pallas_call(
        functools.partial(_flash_kernel, mode=mode, tk=tk, n_chunks=n_chunks, lam_init=lam_init),
        grid=(batch, n_groups, nq),
        in_specs=in_specs,
        out_specs=pl.BlockSpec((None, tr, qw), lambda b, g, i: (b, i, g)),
        out_shape=jax.ShapeDtypeStruct((batch, q_rows, n_groups * qw), _MXU),
        scratch_shapes=[pltpu.VMEM((2, LANES, tq), _MXU),
                        pltpu.VMEM((n_chunks, vw, tk) if mode == "diff" else (2, n_chunks, vw, tk), _MXU),
                        pltpu.VMEM((2, 1, tq), F32),
                        pltpu.VMEM((2, vw, tq), F32),
                        pltpu.VMEM((2, 2, tk, tq), F32)],
        compiler_params=_cparams(("parallel", "parallel", "arbitrary")),
        name=name,
    )(*args)


def _head_col():
    sub = lax.broadcasted_iota(jnp.int32, (LANES, 1), 0)
    return jnp.where(sub < HEAD_DIM, 1.0, 0.0)


def _pair_out_t(a_lo, a_hi):
    return jnp.concatenate([a_lo[:HEAD_DIM] / a_lo[HEAD_DIM:], a_hi[HEAD_DIM:] / a_hi[:HEAD_DIM]], axis=0)


def _na_kernel(q_ref, kc_ref, k0_ref, k1_ref, k2_ref, *rest, n_pairs):
    vx_refs = (rest[0:4], rest[4:8])
    bias_ref, o_ref, st_scr = rest[8], rest[9], rest[10]
    tq = q_ref.shape[0]
    k_refs = (kc_ref, k0_ref, k1_ref, k2_ref)
    lo_col = _head_col()
    for p in range(n_pairs):
        cols = slice(p * LANES, (p + 1) * LANES)
        qt = q_ref[:, cols].astype(F32).T
        for half in range(2):
            col = lo_col if half == 0 else 1.0 - lo_col
            qth = (qt * col).astype(q_ref.dtype)
            st_scr[p % 2, half, 0] = jnp.dot(kc_ref[:, cols], qth, preferred_element_type=F32)
            for j in range(1, 4):
                st_scr[p % 2, half, j] = (jnp.dot(k_refs[j][:, cols], qth, preferred_element_type=F32)
                                          + bias_ref[p, half, (j - 1) * tq:j * tq, :])
        halves = []
        for half in range(2):
            sts = [st_scr[p % 2, half, j] for j in range(4)]
            m = functools.reduce(jnp.maximum, [jnp.max(st, axis=0, keepdims=True) for st in sts])
            acc = jnp.zeros((LANES, tq), F32)
            for j in range(4):
                pt = jnp.exp2(sts[j] - m).astype(q_ref.dtype)
                acc = acc + jnp.dot(vx_refs[half][j][cols, :], pt, preferred_element_type=F32)
            halves.append(acc)
        o_ref[:, cols] = _pair_out_t(*halves).T.astype(o_ref.dtype)


def _na_bias_tables(rpb, rows):
    nh = rpb.shape[0]
    nkr = 3 * NA_Q_ROWS
    qc = np.arange(GRID_W)[:, None]
    kc = np.arange(GRID_W)[None, :]
    ws = np.clip(qc - NA_KW // 2, 0, GRID_W - NA_KW)
    cvalid = ((kc >= ws) & (kc < ws + NA_KW)).reshape(-1)
    dc = (kc - qc + NA_KW - 1).reshape(-1)
    onehot = ((np.arange(2 * NA_KW - 1)[:, None] == dc[None, :]) & cvalid[None, :]).astype(np.float32)
    tiles = jnp.einsum("hrd,dx->hrx", rpb.astype(F32) * LOG2E, jnp.asarray(onehot),
                       precision=lax.Precision.HIGHEST)
    tiles = jnp.where(cvalid[None, None, :], tiles, NEG_INF)
    qr = np.arange(NA_Q_ROWS)[:, None]
    kr = np.arange(nkr)[None, :]
    tables = []
    for variant in range(3):
        if variant == 0:
            r0, k0, nrows = 0, 0, rows
        elif variant == 1:
            r0, k0, nrows = 2 * NA_Q_ROWS, NA_Q_ROWS, 8 * NA_Q_ROWS
        else:
            r0, k0, nrows = rows - NA_Q_ROWS, rows - nkr, rows
        r = r0 + qr
        rp = k0 + kr
        rs = np.clip(r - NA_KH // 2, 0, nrows - NA_KH)
        rvalid = (rp >= rs) & (rp < rs + NA_KH)
        dr = np.clip(rp - r + NA_KH - 1, 0, 2 * NA_KH - 2)
        t = jnp.take(tiles, jnp.asarray(dr.reshape(-1).astype(np.int32)), axis=1)
        t = jnp.where(rvalid.reshape(-1)[None, :, None], t, NEG_INF)
        t = t.reshape(nh, NA_Q_ROWS, nkr, GRID_W, GRID_W).transpose(0, 2, 4, 1, 3)
        tables.append(t.reshape(nh // 2, 2, nkr * GRID_W, NA_Q_ROWS * GRID_W))
    return jnp.stack(tables)


def _neighbourhood(q_arr, k_arr, vx_lo, vx_hi, bias, *, s_len, n_ctx):
    batch, t_len, width = q_arr.shape
    tq = NA_Q_ROWS * GRID_W
    assert n_ctx == tq
    nq = s_len // tq
    nbt = t_len // tq
    n_pairs = NA_HEADS // 2
    cb = s_len // n_ctx

    def seg_blocks(i):
        first = jnp.clip(i - 1, 0, nq - 3)
        return [cb, first, first + 1, first + 2]

    q_spec = pl.BlockSpec((None, tq, width), lambda i, b: (b, i, 0))
    k_specs = [pl.BlockSpec((None, tq, width), lambda i, b, j=j: (b, seg_blocks(i)[j], 0)) for j in range(4)]
    v_specs = [pl.BlockSpec((width, tq), lambda i, b, j=j: (0, b * nbt + seg_blocks(i)[j])) for j in range(4)]
    bias_spec = pl.BlockSpec((None, n_pairs, 2, 3 * tq, tq),
                             lambda i, b: (jnp.where(i == 0, 0, jnp.where(i == nq - 1, 2, 1)), 0, 0, 0, 0))
    return pl.pallas_call(
        functools.partial(_na_kernel, n_pairs=n_pairs),
        grid=(nq, batch),
        in_specs=[q_spec] + k_specs + v_specs + v_specs + [bias_spec],
        out_specs=pl.BlockSpec((None, tq, width), lambda i, b: (b, i, 0)),
        out_shape=jax.ShapeDtypeStruct((batch, s_len, width), _MXU),
        scratch_shapes=[pltpu.VMEM((2, 2, 4, tq, tq), F32)],
        compiler_params=_cparams(("parallel", "parallel")),
        name="neighbourhood_attn",
    )(q_arr, *([k_arr] * 4), *([vx_lo] * 4), *([vx_hi] * 4), bias)


def _windowed_kernel(q_ref, kc_ref, k0_ref, k1_ref, k2_ref, *rest, n_pairs):
    vx_refs = (rest[0:4], rest[4:8])
    sink_ref, o_ref, st_scr = rest[8], rest[9], rest[10]
    tq = q_ref.shape[0]
    qi = pl.program_id(1)
    k_refs = (kc_ref, k0_ref, k1_ref, k2_ref)
    lo_col = _head_col()
    qt = jnp.concatenate([q_ref[:, p * LANES:(p + 1) * LANES].astype(F32).T for p in range(n_pairs)], axis=1)

    qq = lax.broadcasted_iota(jnp.int32, (1, n_pairs * LANES), 1) & (LANES - 1)
    kk = lax.broadcasted_iota(jnp.int32, (tq, 1), 0)
    masks = (None, jnp.logical_and(kk >= qq, qi >= 1), None,
             jnp.logical_and(kk <= qq, qi < pl.num_programs(1) - 1))
    n_ctx = kc_ref.shape[0]
    offs = (0, n_ctx, n_ctx + tq, n_ctx + 2 * tq, n_ctx + 3 * tq)
    for half in range(2):
        col = lo_col if half == 0 else 1.0 - lo_col
        qth = (qt * col).astype(q_ref.dtype)
        for j in range(4):
            st = jnp.dot(k_refs[j][...], qth, preferred_element_type=F32)
            st_scr[half, offs[j]:offs[j + 1], :] = st if masks[j] is None else jnp.where(masks[j], st, NEG_INF)
    halves = []
    for half in range(2):
        col = lo_col if half == 0 else 1.0 - lo_col
        sink = sink_ref[half:half + 1, :]
        m = jnp.maximum(jnp.max(st_scr[half], axis=0, keepdims=True), sink)
        acc = jnp.exp2(sink - m) * (1.0 - col)
        for j in range(4):
            pt = jnp.exp2(st_scr[half, offs[j]:offs[j + 1], :] - m).astype(q_ref.dtype)
            acc = acc + jnp.dot(vx_refs[half][j][...], pt, preferred_element_type=F32)
        halves.append(acc)
    out_t = _pair_out_t(*halves)
    for p in range(n_pairs):
        o_ref[:, p * LANES:(p + 1) * LANES] = out_t[:, p * LANES:(p + 1) * LANES].T.astype(o_ref.dtype)


def _windowed(q_arr, k_arr, vx_lo, vx_hi, sink, *, s_len, n_ctx):
    batch, t_len = q_arr.shape[0], q_arr.shape[1]
    tq = SWA_WINDOW
    nb = s_len // tq
    n_pairs = SWA_HEADS // 2
    cb = s_len // n_ctx
    qw = n_pairs * LANES
    q_spec = pl.BlockSpec((None, tq, qw), lambda b, i: (b, i, 0))
    k_specs = [pl.BlockSpec((None, n_ctx, LANES), lambda b, i: (b, cb, 0))]
    v_specs = [pl.BlockSpec((LANES, n_ctx), lambda b, i: (0, b * (t_len // n_ctx) + cb))]
    for j in range(3):
        k_specs.append(pl.BlockSpec((None, tq, LANES),
                                    lambda b, i, j=j: (b, jnp.clip(i - 1 + j, 0, nb - 1), 0)))
        v_specs.append(pl.BlockSpec((LANES, tq),
                                    lambda b, i, j=j: (0, b * (t_len // tq) + jnp.clip(i - 1 + j, 0, nb - 1))))
    return pl.pallas_call(
        functools.partial(_windowed_kernel, n_pairs=n_pairs),
        grid=(batch, nb),
        in_specs=[q_spec] + k_specs + v_specs + v_specs + [pl.BlockSpec(sink.shape, lambda b, i: (0, 0))],
        out_specs=pl.BlockSpec((None, tq, qw), lambda b, i: (b, i, 0)),
        out_shape=jax.ShapeDtypeStruct((batch, s_len, qw), _MXU),
        scratch_shapes=[pltpu.VMEM((2, n_ctx + 3 * tq, qw), F32)],
        compiler_params=_cparams(("parallel", "parallel")),
        name="windowed_attn",
    )(q_arr, *([k_arr] * 4), *([vx_lo] * 4), *([vx_hi] * 4), sink)


def _layernorm(z, g, b):
    mu = jnp.mean(z, axis=1, keepdims=True)
    zc = z - mu
    var = jnp.mean(zc * zc, axis=1, keepdims=True)
    return zc * lax.rsqrt(var + LN_EPS) * g + b


def _post_kernel(x_ref, ya_ref, yb_ref, wo_ref, mod_ref, g_ref, b_ref, rw_ref, rb_ref,
                 x1_ref, tok_ref, route_ref, *, alpha):
    half = ya_ref.shape[1]
    y = (jnp.dot(ya_ref[...], wo_ref[:half, :], preferred_element_type=F32)
         + jnp.dot(yb_ref[...], wo_ref[half:, :], preferred_element_type=F32))
    x1 = _layernorm(alpha * x_ref[...] + mod_ref[2:3, :] * y, g_ref[...], b_ref[...])
    x1_ref[...] = x1
    tok = x1 * (1.0 + mod_ref[4:5, :]) + mod_ref[3:4, :]
    tok_ref[...] = tok
    logits = jnp.dot(tok.astype(_MXU), rw_ref[...], preferred_element_type=F32) + rb_ref[...]
    tm = logits.shape[0]
    lane = lax.broadcasted_iota(jnp.int32, (tm, LANES), 1).astype(F32)
    vals, idxs = [], []
    for _ in range(TOP_K):
        mx = jnp.max(logits, axis=1, keepdims=True)
        ix = jnp.min(jnp.where(logits == mx, lane, float(LANES)), axis=1, keepdims=True)
        vals.append(mx)
        idxs.append(ix)
        logits = jnp.where(lane == ix, -3.0e38, logits)
    es = [jnp.exp(v - vals[0]) for v in vals]
    den = functools.reduce(lambda a, c: a + c, es)
    route = jnp.zeros((tm, LANES), F32)
    for k in range(TOP_K):
        route = jnp.where(lane == float(k), idxs[k], route)
        route = jnp.where(lane == float(TOP_K + k), es[k] / den, route)
    route_ref[...] = route


def _post(x2d, ya, yb, wo, mods, g, b, rw, rb, *, batch, nbt_in, nblk, n_lat_blk, alpha):
    d = x2d.shape[1]
    tm = ROW_TILE
    half = ya.shape[1]

    def mod_map(bi, t):
        return (jnp.where(t >= n_lat_blk, batch, bi), 0, 0)

    rows_out = batch * nblk * tm
    o_map = lambda bi, t: (bi * nblk + t, 0)
    return pl.pallas_call(
        functools.partial(_post_kernel, alpha=alpha),
        grid=(batch, nblk),
        in_specs=[pl.BlockSpec((tm, d), lambda bi, t: (bi * nbt_in + t, 0)),
                  pl.BlockSpec((tm, half), o_map),
                  pl.BlockSpec((tm, half), o_map),
                  pl.BlockSpec((d, d), lambda bi, t: (0, 0)),
                  pl.BlockSpec((None, 6, d), mod_map),
                  pl.BlockSpec((1, d), lambda bi, t: (0, 0)),
                  pl.BlockSpec((1, d), lambda bi, t: (0, 0)),
                  pl.BlockSpec((d, LANES), lambda bi, t: (0, 0)),
                  pl.BlockSpec((1, LANES), lambda bi, t: (0, 0))],
        out_specs=[pl.BlockSpec((tm, d), o_map), pl.BlockSpec((tm, d), o_map), pl.BlockSpec((tm, LANES), o_map)],
        out_shape=[jax.ShapeDtypeStruct((rows_out, d), F32), jax.ShapeDtypeStruct((rows_out, d), F32),
                   jax.ShapeDtypeStruct((rows_out, LANES), F32)],
        compiler_params=_cparams(("parallel", "parallel")),
        name="post_attn",
    )(x2d, ya, yb, wo, mods, g, b, rw, rb)


def _moe_kernel(be_ref, nu_ref, dst_ref, src0_ref, src1_ref, src2_ref, tok_hbm, wi_ref, bi_ref, wo_ref, bo_ref,
                y_hbm, xbuf0, xbuf1, xbuf2, obuf0, obuf1, obuf2, gsem, ssem, wi_s, wo_s, *, dump0):
    i = pl.program_id(0)
    n_used = nu_ref[0]
    f = wo_s.shape[0]
    xbufs = (xbuf0, xbuf1, xbuf2)
    obufs = (obuf0, obuf1, obuf2)

    def gather_copy(row, r, slot):
        return pltpu.make_async_copy(tok_hbm.at[pl.ds(row, 1), :], xbufs[slot].at[pl.ds(r, 1), :], gsem.at[slot])

    def scatter_copy(row, r, slot):
        return pltpu.make_async_copy(obufs[slot].at[pl.ds(r, 1), :], y_hbm.at[pl.ds(row, 1), :], ssem.at[slot])

    def start_gather(rows_ref, slot):
        for r in range(MOE_ROWS):
            gather_copy(rows_ref[0, r], r, slot).start(priority=r % 2)

    def wait_gather(slot):
        for _ in range(MOE_ROWS):
            gather_copy(0, 0, slot).wait()

    def wait_scatter(slot):
        for _ in range(MOE_ROWS):
            scatter_copy(0, 0, slot).wait()

    @pl.when(i == 0)
    def _():
        for slot in range(MOE_RING):
            obufs[slot][...] = jnp.zeros(obufs[slot].shape, F32)
            for r in range(MOE_ROWS):
                scatter_copy(dump0 + slot * MOE_ROWS + r, r, slot).start(priority=r % 2)
        start_gather(src0_ref, 0)
        start_gather(src1_ref, 1)

    last_used = n_used - 1
    changed = jnp.logical_or(i == 0, be_ref[jnp.clip(i - 1, 0, last_used)] != be_ref[jnp.minimum(i, last_used)])

    for slot in range(MOE_RING):
        @pl.when(jnp.logical_and(i < n_used, i % MOE_RING == slot))
        def _(slot=slot):
            wait_scatter(slot)
            wait_gather(slot)

            @pl.when(changed)
            def _():
                wi_s[...] = wi_ref[...].astype(wi_s.dtype)
                wo_s[...] = wo_ref[...].astype(wo_s.dtype)

            x = xbufs[slot][...].astype(wi_s.dtype)
            start_gather(src2_ref, (slot + 2) % MOE_RING)
            hh = jnp.dot(x, wi_s[...], preferred_element_type=F32) + bi_ref[...]
            gate = jnp.minimum(hh[:, :f], SWIGLU_LIMIT)
            up = jnp.clip(hh[:, f:], -SWIGLU_LIMIT, SWIGLU_LIMIT)
            act = gate * (1.0 / (1.0 + jnp.exp(-SWIGLU_ALPHA * gate))) * (up + 1.0)
            obufs[slot][...] = jnp.dot(act.astype(wo_s.dtype), wo_s[...], preferred_element_type=F32) + bo_ref[...]
            for r in range(MOE_ROWS):
                scatter_copy(dst_ref[0, r], r, slot).start(priority=r % 2)

        @pl.when(jnp.logical_and(i >= n_used, i % MOE_RING == slot))
        def _(slot=slot):
            @pl.when(i < n_used + 2)
            def _():
                wait_gather(slot)

            @pl.when(i < n_used + MOE_RING)
            def _():
                wait_scatter(slot)


def _experts(tok, src_rows, dst_rows, block_e, n_used, w_in, b_in, w_out, b_out, layer):
    n_tok, d = tok.shape
    _, n_e, _, f2 = w_in.shape
    f = f2 // 2
    n_blocks = src_rows.shape[0]

    def blk(i, be, nu):
        return jnp.minimum(i, nu[0] - 1)

    def e_map(i, be, nu):
        return (layer, be[blk(i, be, nu)], 0, 0)

    def rows_spec(ahead):
        return pl.BlockSpec((None, 1, MOE_ROWS), lambda i, be, nu: (blk(i + ahead, be, nu), 0, 0),
                            memory_space=pltpu.SMEM)

    row_buf = pltpu.VMEM((MOE_ROWS, d), F32)
    grid_spec = pltpu.PrefetchScalarGridSpec(
        num_scalar_prefetch=2,
        grid=(n_blocks + MOE_RING,),
        in_specs=[rows_spec(0), rows_spec(0), rows_spec(1), rows_spec(2),
                  pl.BlockSpec(memory_space=pl.ANY),
                  pl.BlockSpec((None, None, d, f2), e_map),
                  pl.BlockSpec((None, None, 1, f2), e_map),
                  pl.BlockSpec((None, None, f, d), e_map),
                  pl.BlockSpec((None, None, 1, d), e_map)],
        out_specs=pl.BlockSpec(memory_space=pl.ANY),
        scratch_shapes=[row_buf] * (2 * MOE_RING)
                       + [pltpu.SemaphoreType.DMA((MOE_RING,)), pltpu.SemaphoreType.DMA((MOE_RING,)),
                          pltpu.VMEM((d, f2), _MXU), pltpu.VMEM((f, d), _MXU)],
    )
    depth = w_in.shape[0]
    return pl.pallas_call(
        functools.partial(_moe_kernel, dump0=TOP_K * n_tok), grid_spec=grid_spec,
        out_shape=jax.ShapeDtypeStruct((TOP_K * n_tok + MOE_RING * MOE_ROWS, d), F32),
        compiler_params=_cparams(("arbitrary",)),
        name="experts",
    )(block_e, n_used, dst_rows, src_rows, src_rows, src_rows, tok, w_in, b_in.reshape(depth, n_e, 1, f2), w_out,
      b_out.reshape(depth, n_e, 1, d))


def _route_plan(route, n_experts):
    n = route.shape[0]
    top_idx = route[:, :TOP_K].astype(jnp.int32)
    nk = n * TOP_K
    flat_e = top_idx.reshape(-1)
    order = jnp.argsort(flat_e).astype(jnp.int32)
    experts = jnp.arange(n_experts, dtype=jnp.int32)
    counts = jnp.sum(flat_e[:, None] == experts[None, :], axis=0, dtype=jnp.int32)
    padded = (counts + MOE_ROWS - 1) // MOE_ROWS * MOE_ROWS
    start = jnp.cumsum(counts) - counts
    pend = jnp.cumsum(padded)
    pstart = pend - padded
    n_blocks = -(-nk // MOE_ROWS) + n_experts
    first = jnp.arange(n_blocks, dtype=jnp.int32) * MOE_ROWS
    block_e = jnp.minimum(jnp.sum(pend[None, :] <= first[:, None], axis=1, dtype=jnp.int32), n_experts - 1)
    n_used = (pend[-1:] // MOE_ROWS).astype(jnp.int32)
    blk = jnp.arange(n_blocks, dtype=jnp.int32)[:, None]
    row = jnp.arange(MOE_ROWS, dtype=jnp.int32)[None, :]
    rank = blk * MOE_ROWS + row - jnp.take(pstart, block_e, mode="clip")[:, None]
    src = jnp.clip(jnp.take(start, block_e, mode="clip")[:, None] + rank, 0, nk - 1)
    valid = rank < jnp.take(counts, block_e, mode="clip")[:, None]
    pair = jnp.take(order, src, mode="clip")
    token = pair // TOP_K
    src_rows = jnp.where(valid, token, 0)
    dst_rows = jnp.where(valid, (pair % TOP_K) * n + token, nk + (blk % MOE_RING) * MOE_ROWS + row)
    return (src_rows.reshape(n_blocks, 1, MOE_ROWS), dst_rows.reshape(n_blocks, 1, MOE_ROWS), block_e, n_used)


def _ln2_kernel(x_ref, y0_ref, y1_ref, y2_ref, y3_ref, route_ref, mod_ref, g_ref, b_ref, o_ref, *, alpha):
    ys = (y0_ref, y1_ref, y2_ref, y3_ref)
    f = route_ref[:, TOP_K:TOP_K + 1] * ys[0][...]
    for k in range(1, TOP_K):
        f = f + route_ref[:, TOP_K + k:TOP_K + k + 1] * ys[k][...]
    o_ref[...] = _layernorm(alpha * x_ref[...] + mod_ref[5:6, :] * f, g_ref[...], b_ref[...])


def _ln2(x1, y, route, mods, g, b, *, batch, nblk, n_lat_blk, alpha):
    rows, d = x1.shape
    tm = ROW_TILE
    r_map = lambda bi, t: (bi * nblk + t, 0)

    def y_spec(k):
        return pl.BlockSpec((tm, d), lambda bi, t: (k * (rows // tm) + bi * nblk + t, 0))

    def mod_map(bi, t):
        return (jnp.where(t >= n_lat_blk, batch, bi), 0, 0)

    return pl.pallas_call(
        functools.partial(_ln2_kernel, alpha=alpha),
        grid=(batch, nblk),
        in_specs=[pl.BlockSpec((tm, d), r_map)] + [y_spec(k) for k in range(TOP_K)]
                 + [pl.BlockSpec((tm, LANES), r_map), pl.BlockSpec((None, 6, d), mod_map),
                    pl.BlockSpec((1, d), lambda bi, t: (0, 0)), pl.BlockSpec((1, d), lambda bi, t: (0, 0))],
        out_specs=pl.BlockSpec((tm, d), r_map),
        out_shape=jax.ShapeDtypeStruct((rows, d), F32),
        compiler_params=_cparams(("parallel", "parallel")),
        name="combine_ln2",
    )(x1, y, y, y, y, route, mods, g, b)


def _pair_layout(w, n_heads, axis=-1):
    axis = axis % w.ndim
    shape = w.shape
    w = w.reshape(shape[:axis] + (2, n_heads // 2, HEAD_DIM) + shape[axis + 1:])
    return w.swapaxes(axis, axis + 1).reshape(shape)


def _rope_tables(s_len, n_ctx):
    t = np.arange(s_len)
    row = (t // GRID_W).astype(np.float32)
    col = (t % GRID_W).astype(np.float32)
    axis_dim = HEAD_DIM // 2
    freqs = jnp.asarray(ROPE_THETA, F32) ** (-jnp.arange(0, axis_dim, 2, dtype=F32) / axis_dim)
    ang = jnp.concatenate([jnp.asarray(row)[:, None] * freqs, jnp.asarray(col)[:, None] * freqs], axis=-1)
    cos, sin = jnp.cos(ang), jnp.sin(ang)
    cos_h = jnp.repeat(cos, 2, axis=-1)
    sin_h = jnp.stack([-sin, sin], axis=-1).reshape(s_len, HEAD_DIM)
    cos_t = jnp.concatenate([cos_h, jnp.ones((n_ctx, HEAD_DIM), F32)], axis=0)
    sin_t = jnp.concatenate([sin_h, jnp.zeros((n_ctx, HEAD_DIM), F32)], axis=0)
    return jnp.tile(cos_t, (1, 2)), jnp.tile(sin_t, (1, 2))


def _moe_and_ln2(x1, tok, route, mods, n_experts, w_in, b_in, w_out, b_out, layer, g, b, *,
                 batch, nblk, n_lat_blk, alpha):
    src_rows, dst_rows, block_e, n_used = _route_plan(route, n_experts)
    y = _experts(tok, src_rows, dst_rows, block_e, n_used, w_in, b_in, w_out, b_out, layer)
    return _ln2(x1, y, route, mods, g, b, batch=batch, nblk=nblk, n_lat_blk=n_lat_blk, alpha=alpha)


def kernel(x, c, ctx, c_ctx, mod_w, mod_b, ln1_g, ln1_b, ln2_g, ln2_b, router_w, router_b, moe_w_in, moe_b_in,
           moe_w_out, moe_b_out, ab_w_in, ab_w_out, na_rpb, diff_lq1, diff_lk1, diff_lq2, diff_lk2, diff_subln,
           cd_w_in, cd_w_out, gqa_q_norm, gqa_k_norm, swa_sink):
    batch, s_len, d = x.shape
    n_ctx = ctx.shape[1]
    t_len = n_ctx + s_len
    depth = mod_w.shape[0]
    n_experts = router_w.shape[2]
    alpha = (2.0 * depth) ** 0.25
    tm = ROW_TILE
    nbt = t_len // tm
    nbs = s_len // tm
    assert depth == 2 and n_ctx % tm == 0 and s_len % tm == 0 and batch + 1 <= 8

    cc = jnp.zeros((8, d), F32).at[:batch].set(c).at[batch].set(c_ctx)
    mod_all = _modulation(cc, mod_w, mod_b)
    mods = [mod_all[l, :batch + 1].reshape(batch + 1, 6, d) for l in range(depth)]
    cos_t, sin_t = _rope_tables(s_len, n_ctx)
    rw = [jnp.zeros((d, LANES), F32).at[:, :n_experts].set(router_w[l]).astype(_MXU) for l in range(depth)]
    rb = [jnp.full((1, LANES), NEG_INF, F32).at[0, :n_experts].set(router_b[l]) for l in range(depth)]

    stream = jnp.concatenate([x, ctx], axis=1).reshape(batch * t_len, d)
    by_batch = lambda a: a.reshape(batch, t_len, a.shape[-1])

    na_w = NA_HEADS * HEAD_DIM
    df_w = DIFF_HEADS * 2 * HEAD_DIM
    w0 = ab_w_in[0].astype(_MXU)
    rows_only = ("rows",)
    plan0 = ((0, na_w, None, False, QSCALE, rows_only), (na_w, na_w, None, False, 1.0, rows_only),
             (2 * na_w, na_w, None, False, 1.0, ("rows", "value_t")),
             (3 * na_w, df_w, None, True, QSCALE, rows_only), (3 * na_w + df_w, df_w, None, True, 1.0, rows_only),
             (3 * na_w + 2 * df_w, df_w, None, False, 1.0, rows_only))
    nw0 = jnp.zeros((8, LANES), F32)
    nq, nk, nv, nvx_lo, nvx_hi, dq, dk, dv = _in_proj(stream, mods[0], w0, cos_t, sin_t, nw0, plan0,
                                                      batch=batch, n_ctx=n_ctx)
    nq, nk, nv, dq, dk, dv = map(by_batch, (nq, nk, nv, dq, dk, dv))
    lam_init0 = 0.8 - 0.6 * math.exp(-0.3 * 0)
    par = jnp.zeros((8, LANES), F32)
    par = par.at[0, :HEAD_DIM].set(diff_lq1[0]).at[1, :HEAD_DIM].set(diff_lk1[0])
    par = par.at[2, :HEAD_DIM].set(diff_lq2[0]).at[3, :HEAD_DIM].set(diff_lk2[0]).at[4].set(diff_subln[0])
    bias = _na_bias_tables(na_rpb[0], s_len // GRID_W)
    y_na = _neighbourhood(nq, nk, nvx_lo, nvx_hi, bias, s_len=s_len, n_ctx=n_ctx)
    y_na_c = _flash(nq, nk, nv, n_groups=NA_HEADS // 2, q_rows=n_ctx, q_off=s_len, kv_rows=n_ctx,
                    kv_off=s_len, k_col=0, v_col=0, mode="pair", name="neighbourhood_attn_ctx")
    y_df = _flash(dq, dk, dv, n_groups=DIFF_HEADS, q_rows=s_len, q_off=0, kv_rows=t_len, kv_off=0, k_col=0, v_col=0,
                  mode="diff", par=par, lam_init=lam_init0, name="diff_attn")
    y_df_c = _flash(dq, dk, dv, n_groups=DIFF_HEADS, q_rows=n_ctx, q_off=s_len, kv_rows=n_ctx, kv_off=s_len,
                    k_col=0, v_col=0,
                    mode="diff", par=par, lam_init=lam_init0, name="diff_attn_ctx")
    ya = jnp.concatenate([y_na, y_na_c], axis=1).reshape(batch * t_len, -1)
    yb = jnp.concatenate([y_df, y_df_c], axis=1).reshape(batch * t_len, -1)
    x1, tok, route = _post(stream, ya, yb, ab_w_out[0].astype(_MXU), mods[0], ln1_g[0][None], ln1_b[0][None],
                           rw[0], rb[0], batch=batch, nbt_in=nbt, nblk=nbt, n_lat_blk=nbs, alpha=alpha)
    stream = _moe_and_ln2(x1, tok, route, mods[0], n_experts, moe_w_in, moe_b_in, moe_w_out, moe_b_out, 0,
                          ln2_g[0][None], ln2_b[0][None], batch=batch, nblk=nbt, n_lat_blk=nbs, alpha=alpha)

    hw = GQA_HEADS * HEAD_DIM
    kw = GQA_KV_HEADS * HEAD_DIM
    wcd = cd_w_in[0]
    gq, gk, gv, wq, wk, wv = jnp.split(wcd, [hw, hw + kw, hw + 2 * kw, 2 * hw + 2 * kw, 2 * hw + 3 * kw], axis=1)
    w1 = jnp.concatenate([_pair_layout(gq, GQA_HEADS), _pair_layout(wq, SWA_HEADS), gk, gv, wk, wv],
                         axis=1).astype(_MXU)
    plan1 = ((0, hw, 0, True, QSCALE, rows_only), (hw, hw, None, True, QSCALE, rows_only),
             (2 * hw, kw, 1, True, 1.0, rows_only), (2 * hw + kw, kw, None, False, 1.0, rows_only),
             (2 * hw + 2 * kw, kw, None, True, 1.0, rows_only),
             (2 * hw + 3 * kw, kw, None, False, 1.0, ("value_t",)))
    nw1 = jnp.zeros((8, LANES), F32).at[0].set(jnp.tile(gqa_q_norm[0], 2)).at[1].set(jnp.tile(gqa_k_norm[0], 2))
    gq, wq, gk, gv, wk, wvx_lo, wvx_hi = _in_proj(stream, mods[1], w1, cos_t, sin_t, nw1, plan1,
                                                  batch=batch, n_ctx=n_ctx)
    gq, wq, gk, gv, wk = map(by_batch, (gq, wq, gk, gv, wk))
    y_c = _flash(gq, gk, gv, n_groups=1, n_qp=GQA_HEADS // 2, q_rows=s_len, q_off=0, kv_rows=t_len, kv_off=0,
                 k_col=0, v_col=0, mode="pair", name="gqa_attn")
    sink = jnp.repeat(swa_sink[0].reshape(2, SWA_HEADS // 2) * LOG2E, LANES, axis=1).astype(F32)
    y_d = _windowed(wq, wk, wvx_lo, wvx_hi, sink, s_len=s_len, n_ctx=n_ctx)
    wo1 = jnp.concatenate([_pair_layout(cd_w_out[0][:hw], GQA_HEADS, axis=0),
                           _pair_layout(cd_w_out[0][hw:], SWA_HEADS, axis=0)], axis=0).astype(_MXU)
    x1, tok, route = _post(stream, y_c.reshape(batch * s_len, -1), y_d.reshape(batch * s_len, -1), wo1, mods[1],
                           ln1_g[1][None], ln1_b[1][None], rw[1], rb[1], batch=batch, nbt_in=nbt,
                           nblk=nbs, n_lat_blk=nbs, alpha=alpha)
    out = _moe_and_ln2(x1, tok, route, mods[1], n_experts, moe_w_in, moe_b_in, moe_w_out, moe_b_out, 1,
                       ln2_g[1][None], ln2_b[1][None], batch=batch, nblk=nbs, n_lat_blk=nbs, alpha=alpha)
    return out.reshape(batch, s_len, d)
```

```python
import functools
import math

import numpy as np
import jax
import jax.numpy as jnp
from jax import lax
from jax.experimental import pallas as pl
from jax.experimental.pallas import tpu as pltpu

F32 = jnp.float32
_MXU = jnp.bfloat16

HEAD_DIM = 64
GRID_W = 64
LOG2E = math.log2(math.e)
QSCALE = HEAD_DIM ** -0.5 * LOG2E
ROPE_THETA = 10000.0
NA_HEADS = 8
NA_KH = 8
NA_KW = 16
DIFF_HEADS = 4
GQA_HEADS = 8
GQA_KV_HEADS = 2
SWA_HEADS = 8
SWA_KV_HEADS = 2
SWA_WINDOW = 128
TOP_K = 4
SWIGLU_LIMIT = 7.0
SWIGLU_ALPHA = 1.702
LN_EPS = 1e-5
RMS_EPS = 1e-6
NEG_INF = -1e30

LANES = 128
ROW_TILE = 256
MOE_ROWS = 256
MOE_RING = 3
NA_Q_ROWS = 4
VMEM_LIMIT = 52 * 1024 * 1024


def _cparams(sem, vmem=VMEM_LIMIT):
    return pltpu.CompilerParams(dimension_semantics=sem, vmem_limit_bytes=vmem)


def _mod_kernel(c_ref, w_ref, b_ref, o_ref):
    c = c_ref[...]
    a = (c / (1.0 + jnp.exp(-c))).astype(_MXU)
    o_ref[...] = jnp.dot(a, w_ref[...].astype(_MXU), preferred_element_type=F32) + b_ref[...]


def _modulation(cc, mod_w, mod_b):
    depth, d, d6 = mod_w.shape
    tn = d6 // 4
    return pl.pallas_call(
        _mod_kernel,
        grid=(depth, d6 // tn),
        in_specs=[pl.BlockSpec((8, d), lambda l, j: (0, 0)),
                  pl.BlockSpec((None, d, tn), lambda l, j: (l, 0, j)),
                  pl.BlockSpec((None, 1, tn), lambda l, j: (l, 0, j))],
        out_specs=pl.BlockSpec((None, 8, tn), lambda l, j: (l, 0, j)),
        out_shape=jax.ShapeDtypeStruct((depth, 8, d6), F32),
        compiler_params=_cparams(("arbitrary", "arbitrary")),
        name="modulation",
    )(cc, mod_w, mod_b.reshape(depth, 1, d6))


def _in_kernel(x_ref, mod_ref, w_ref, cos_ref, sin_ref, nw_ref, *o_refs, plan):
    x = x_ref[...]
    h = (x * (1.0 + mod_ref[1:2, :]) + mod_ref[0:1, :]).astype(_MXU)
    tm = x.shape[0]
    lane = lax.broadcasted_iota(jnp.int32, (tm, LANES), 1)
    even_lane = (lane & 1) == 0
    gi = lax.broadcasted_iota(jnp.int32, (LANES, LANES), 0) // HEAD_DIM
    gj = lax.broadcasted_iota(jnp.int32, (LANES, LANES), 1) // HEAD_DIM
    seg = jnp.where(gi == gj, 1.0, 0.0).astype(_MXU)
    cos = cos_ref[...]
    sin = sin_ref[...]
    lo_col = _head_col()
    outs = iter(o_refs)
    for (c0, width, norm_row, rope, scale, kinds) in plan:
        acc = jnp.dot(h, w_ref[:, c0:c0 + width], preferred_element_type=F32)
        o_rows = next(outs) if "rows" in kinds else None
        o_lo, o_hi = (next(outs), next(outs)) if "value_t" in kinds else (None, None)
        for j in range(width // LANES):
            a = acc[:, j * LANES:(j + 1) * LANES]
            if norm_row is not None:
                a2 = a * a
                hi = a2.astype(_MXU)
                lo = (a2 - hi.astype(F32)).astype(_MXU)
                ss = (jnp.dot(hi, seg, preferred_element_type=F32)
                      + jnp.dot(lo, seg, preferred_element_type=F32))
                a = a * lax.rsqrt(ss * (1.0 / HEAD_DIM) + RMS_EPS) * nw_ref[norm_row:norm_row + 1, :]
            if rope:
                partner = jnp.where(even_lane, pltpu.roll(a, LANES - 1, 1), pltpu.roll(a, 1, 1))
                a = a * cos + partner * sin
            if scale != 1.0:
                a = a * scale
            if o_rows is not None:
                o_rows[:, j * LANES:(j + 1) * LANES] = a.astype(o_rows.dtype)
            if o_lo is not None:
                at = a.T
                o_lo[j * LANES:(j + 1) * LANES, :] = (at * lo_col + (1.0 - lo_col)).astype(o_lo.dtype)
                o_hi[j * LANES:(j + 1) * LANES, :] = (at * (1.0 - lo_col) + lo_col).astype(o_hi.dtype)


def _in_proj(x2d, mods, w, cos_t, sin_t, nw, plan, *, batch, n_ctx):
    rows, d = x2d.shape
    ncols = w.shape[1]
    tm = ROW_TILE
    nbt = rows // batch // tm
    n_lat_blk = nbt - n_ctx // tm

    def mod_map(i):
        return (jnp.where(i % nbt >= n_lat_blk, batch, i // nbt), 0, 0)

    out_specs, out_shape = [], []
    for (_, width, _, _, _, kinds) in plan:
        if "rows" in kinds:
            out_specs.append(pl.BlockSpec((tm, width), lambda i: (i, 0)))
            out_shape.append(jax.ShapeDtypeStruct((rows, width), _MXU))
        if "value_t" in kinds:
            out_specs += [pl.BlockSpec((width, tm), lambda i: (0, i))] * 2
            out_shape += [jax.ShapeDtypeStruct((width, rows), _MXU)] * 2

    return pl.pallas_call(
        functools.partial(_in_kernel, plan=plan),
        grid=(rows // tm,),
        in_specs=[pl.BlockSpec((tm, d), lambda i: (i, 0)),
                  pl.BlockSpec((None, 6, d), mod_map),
                  pl.BlockSpec((d, ncols), lambda i: (0, 0)),
                  pl.BlockSpec((tm, LANES), lambda i: (i % nbt, 0)),
                  pl.BlockSpec((tm, LANES), lambda i: (i % nbt, 0)),
                  pl.BlockSpec((8, LANES), lambda i: (0, 0))],
        out_specs=out_specs,
        out_shape=out_shape,
        compiler_params=_cparams(("parallel",)),
        name="in_proj",
    )(x2d, mods, w, cos_t, sin_t, nw)


def _flash_kernel(*refs, mode, tk, n_chunks, lam_init):
    if mode == "diff":
        q_ref, k_ref, v_ref, par_ref, o_ref, qt_scr, vxt_scr, m_scr, acc_scr, st_scr = refs
    else:
        q_ref, k_ref, v_ref, o_ref, qt_scr, vxt_scr, m_scr, acc_scr, st_scr = refs
    lo_col = _head_col()

    @pl.when(pl.program_id(2) == 0)
    def _():
        for c in range(n_chunks):
            vt = v_ref[c * tk:(c + 1) * tk, :].astype(F32).T
            if mode == "diff":
                vxt_scr[c, :LANES, :] = vt.astype(vxt_scr.dtype)
                vxt_scr[c, LANES:, :] = jnp.ones((LANES, tk), vxt_scr.dtype)
            else:
                vxt_scr[0, c] = (vt * lo_col + (1.0 - lo_col)).astype(vxt_scr.dtype)
                vxt_scr[1, c] = (vt * (1.0 - lo_col) + lo_col).astype(vxt_scr.dtype)

    n_qp = q_ref.shape[1] // LANES
    qt = jnp.concatenate([q_ref[:, p * LANES:(p + 1) * LANES].astype(F32).T for p in range(n_qp)], axis=1)
    qt_scr[0] = (qt * lo_col).astype(qt_scr.dtype)
    qt_scr[1] = (qt * (1.0 - lo_col)).astype(qt_scr.dtype)
    m_scr[...] = jnp.full(m_scr.shape, NEG_INF, F32)
    acc_scr[...] = jnp.zeros(acc_scr.shape, F32)

    def qk(c, slot):
        off = pl.multiple_of(c * tk, tk)
        k = k_ref[pl.ds(off, tk), :]
        for h in range(2):
            st_scr[slot, h] = jnp.dot(k, qt_scr[h], preferred_element_type=F32)

    def softmax_pv(c, slot):
        for h in range(2):
            st = st_scr[slot, h]
            m_prev = m_scr[h]
            m_new = jnp.maximum(m_prev, jnp.max(st, axis=0, keepdims=True))
            alpha = jnp.exp2(m_prev - m_new)
            pt = jnp.exp2(st - m_new).astype(vxt_scr.dtype)
            vxt = vxt_scr[c] if mode == "diff" else vxt_scr[h, c]
            acc_scr[h] = alpha * acc_scr[h] + jnp.dot(vxt, pt, preferred_element_type=F32)
            m_scr[h] = m_new

    qk(0, 0)

    def body(j, carry):
        c = 2 * j
        qk(c + 1, 1)
        softmax_pv(c, 0)
        qk(c + 2, 0)
        softmax_pv(c + 1, 1)
        return carry

    lax.fori_loop(0, (n_chunks - 1) // 2, body, 0)
    if n_chunks % 2 == 1:
        softmax_pv(n_chunks - 1, 0)
    else:
        qk(n_chunks - 1, 1)
        softmax_pv(n_chunks - 2, 0)
        softmax_pv(n_chunks - 1, 1)

    a_lo = acc_scr[0]
    a_hi = acc_scr[1]
    if mode == "diff":
        lam = (jnp.exp(jnp.sum(par_ref[0:1, :] * par_ref[1:2, :], axis=1, keepdims=True))
               - jnp.exp(jnp.sum(par_ref[2:3, :] * par_ref[3:4, :], axis=1, keepdims=True))
               + lam_init)
        out_t = a_lo[:LANES] / a_lo[LANES:] - lam * (a_hi[:LANES] / a_hi[LANES:])
        ms = jnp.mean(out_t * out_t, axis=0, keepdims=True)
        o_ref[...] = ((out_t * lax.rsqrt(ms + RMS_EPS)).T * par_ref[4:5, :] * (1.0 - lam_init)).astype(o_ref.dtype)
    else:
        out_t = _pair_out_t(a_lo, a_hi)
        rows = o_ref.shape[0]
        for p in range(n_qp):
            o_ref[:, p * LANES:(p + 1) * LANES] = out_t[:, p * rows:(p + 1) * rows].T.astype(o_ref.dtype)


def _pick_tile(n, candidates):
    for c in candidates:
        if n % c == 0:
            return c
    raise ValueError(f"no tile for {n}")


def _flash(q_arr, k_arr, v_arr, *, n_groups, q_rows, q_off, kv_rows, kv_off, k_col, v_col, mode, n_qp=1, par=None,
           lam_init=0.0, name):
    batch = q_arr.shape[0]
    tr = _pick_tile(q_rows, (1024 // n_qp, 512 // n_qp, 256 // n_qp))
    tq = tr * n_qp
    qw = n_qp * LANES
    tk = _pick_tile(kv_rows, (768, 512, 384, 256, 128))
    nq, n_chunks = q_rows // tr, kv_rows // tk
    assert q_off % tr == 0 and kv_off % kv_rows == 0
    vw = 2 * LANES if mode == "diff" else LANES
    in_specs = [pl.BlockSpec((None, tr, qw), lambda b, g, i: (b, q_off // tr + i, g)),
                pl.BlockSpec((None, kv_rows, LANES), lambda b, g, i: (b, kv_off // kv_rows, k_col + g)),
                pl.BlockSpec((None, kv_rows, LANES), lambda b, g, i: (b, kv_off // kv_rows, v_col + g))]
    args = [q_arr, k_arr, v_arr]
    if mode == "diff":
        in_specs.append(pl.BlockSpec(par.shape, lambda b, g, i: (0, 0)))
        args.append(par)
    return pl.pallas_call(
        functools.partial(_flash_kernel, mode=mode, tk=tk, n_chunks=n_chunks, lam_init=lam_init),
        grid=(batch, n_groups, nq),
        in_specs=in_specs,
        out_specs=pl.BlockSpec((None, tr, qw), lambda b, g, i: (b, i, g)),
        out_shape=jax.ShapeDtypeStruct((batch, q_rows, n_groups * qw), _MXU),
        scratch_shapes=[pltpu.VMEM((2, LANES, tq), _MXU),
                        pltpu.VMEM((n_chunks, vw, tk) if mode == "diff" else (2, n_chunks, vw, tk), _MXU),
                        pltpu.VMEM((2, 1, tq), F32),
                        pltpu.VMEM((2, vw, tq), F32),
                        pltpu.VMEM((2, 2, tk, tq), F32)],
        compiler_params=_cparams(("parallel", "parallel", "arbitrary")),
        name=name,
    )(*args)


def _head_col():
    sub = lax.broadcasted_iota(jnp.int32, (LANES, 1), 0)
    return jnp.where(sub < HEAD_DIM, 1.0, 0.0)


def _pair_out_t(a_lo, a_hi):
    return jnp.concatenate([a_lo[:HEAD_DIM] / a_lo[HEAD_DIM:], a_hi[HEAD_DIM:] / a_hi[:HEAD_DIM]], axis=0)


def _na_kernel(q_ref, kc_ref, k0_ref, k1_ref, k2_ref, *rest, n_pairs):
    vx_refs = (rest[0:4], rest[4:8])
    bias_ref, o_ref, st_scr = rest[8], rest[9], rest[10]
    tq = q_ref.shape[0]
    k_refs = (kc_ref, k0_ref, k1_ref, k2_ref)
    lo_col = _head_col()
    for p in range(n_pairs):
        cols = slice(p * LANES, (p + 1) * LANES)
        qt = q_ref[:, cols].astype(F32).T
        for half in range(2):
            col = lo_col if half == 0 else 1.0 - lo_col
            qth = (qt * col).astype(q_ref.dtype)
            st_scr[p % 2, half, 0] = jnp.dot(kc_ref[:, cols], qth, preferred_element_type=F32)
            for j in range(1, 4):
                st_scr[p % 2, half, j] = (jnp.dot(k_refs[j][:, cols], qth, preferred_element_type=F32)
                                          + bias_ref[p, half, (j - 1) * tq:j * tq, :])
        halves = []
        for half in range(2):
            sts = [st_scr[p % 2, half, j] for j in range(4)]
            m = functools.reduce(jnp.maximum, [jnp.max(st, axis=0, keepdims=True) for st in sts])
            acc = jnp.zeros((LANES, tq), F32)
            for j in range(4):
                pt = jnp.exp2(sts[j] - m).astype(q_ref.dtype)
                acc = acc + jnp.dot(vx_refs[half][j][cols, :], pt, preferred_element_type=F32)
            halves.append(acc)
        o_ref[:, cols] = _pair_out_t(*halves).T.astype(o_ref.dtype)


def _na_bias_tables(rpb, rows):
    nh = rpb.shape[0]
    nkr = 3 * NA_Q_ROWS
    qc = np.arange(GRID_W)[:, None]
    kc = np.arange(GRID_W)[None, :]
    ws = np.clip(qc - NA_KW // 2, 0, GRID_W - NA_KW)
    cvalid = ((kc >= ws) & (kc < ws + NA_KW)).reshape(-1)
    dc = (kc - qc + NA_KW - 1).reshape(-1)
    onehot = ((np.arange(2 * NA_KW - 1)[:, None] == dc[None, :]) & cvalid[None, :]).astype(np.float32)
    tiles = jnp.einsum("hrd,dx->hrx", rpb.astype(F32) * LOG2E, jnp.asarray(onehot),
                       precision=lax.Precision.HIGHEST)
    tiles = jnp.where(cvalid[None, None, :], tiles, NEG_INF)
    tiles = tiles.reshape(nh, 2 * NA_KH - 1, GRID_W, GRID_W).swapaxes(-1, -2)
    masked_tile = 2 * NA_KH - 1
    tiles = jnp.concatenate([tiles, jnp.full((nh, 1, GRID_W, GRID_W), NEG_INF, F32)], axis=1)
    kr = np.arange(nkr)[:, None]
    qr = np.arange(NA_Q_ROWS)[None, :]
    tile_idx = []
    for variant in range(3):
        if variant == 0:
            r0, k0, nrows = 0, 0, rows
        elif variant == 1:
            r0, k0, nrows = 2 * NA_Q_ROWS, NA_Q_ROWS, 8 * NA_Q_ROWS
        else:
            r0, k0, nrows = rows - NA_Q_ROWS, rows - nkr, rows
        r = r0 + qr
        rp = k0 + kr
        rs = np.clip(r - NA_KH // 2, 0, nrows - NA_KH)
        rvalid = (rp >= rs) & (rp < rs + NA_KH)
        tile_idx.append(np.where(rvalid, rp - r + NA_KH - 1, masked_tile))
    idx = jnp.asarray(np.stack(tile_idx).reshape(-1).astype(np.int32))
    t = jnp.take(tiles, idx, axis=1)
    t = t.reshape(nh, 3, nkr, NA_Q_ROWS, GRID_W, GRID_W).transpose(1, 0, 2, 4, 3, 5)
    return t.reshape(3, nh // 2, 2, nkr * GRID_W, NA_Q_ROWS * GRID_W)


def _neighbourhood(q_arr, k_arr, vx_lo, vx_hi, bias, *, s_len, n_ctx):
    batch, t_len, width = q_arr.shape
    tq = NA_Q_ROWS * GRID_W
    assert n_ctx == tq
    nq = s_len // tq
    nbt = t_len // tq
    n_pairs = NA_HEADS // 2
    cb = s_len // n_ctx

    def seg_blocks(i):
        first = jnp.clip(i - 1, 0, nq - 3)
        return [cb, first, first + 1, first + 2]

    q_spec = pl.BlockSpec((None, tq, width), lambda i, b: (b, i, 0))
    k_specs = [pl.BlockSpec((None, tq, width), lambda i, b, j=j: (b, seg_blocks(i)[j], 0)) for j in range(4)]
    v_specs = [pl.BlockSpec((width, tq), lambda i, b, j=j: (0, b * nbt + seg_blocks(i)[j])) for j in range(4)]
    bias_spec = pl.BlockSpec((None, n_pairs, 2, 3 * tq, tq),
                             lambda i, b: (jnp.where(i == 0, 0, jnp.where(i == nq - 1, 2, 1)), 0, 0, 0, 0))
    return pl.pallas_call(
        functools.partial(_na_kernel, n_pairs=n_pairs),
        grid=(nq, batch),
        in_specs=[q_spec] + k_specs + v_specs + v_specs + [bias_spec],
        out_specs=pl.BlockSpec((None, tq, width), lambda i, b: (b, i, 0)),
        out_shape=jax.ShapeDtypeStruct((batch, s_len, width), _MXU),
        scratch_shapes=[pltpu.VMEM((2, 2, 4, tq, tq), F32)],
        compiler_params=_cparams(("parallel", "parallel")),
        name="neighbourhood_attn",
    )(q_arr, *([k_arr] * 4), *([vx_lo] * 4), *([vx_hi] * 4), bias)


def _windowed_kernel(q_ref, kc_ref, k0_ref, k1_ref, k2_ref, *rest, n_pairs):
    vx_refs = (rest[0:4], rest[4:8])
    sink_ref, o_ref, st_scr = rest[8], rest[9], rest[10]
    tq = q_ref.shape[0]
    qi = pl.program_id(1)
    k_refs = (kc_ref, k0_ref, k1_ref, k2_ref)
    lo_col = _head_col()
    qt = jnp.concatenate([q_ref[:, p * LANES:(p + 1) * LANES].astype(F32).T for p in range(n_pairs)], axis=1)

    qq = lax.broadcasted_iota(jnp.int32, (1, n_pairs * LANES), 1) & (LANES - 1)
    kk = lax.broadcasted_iota(jnp.int32, (tq, 1), 0)
    masks = (None, jnp.logical_and(kk >= qq, qi >= 1), None,
             jnp.logical_and(kk <= qq, qi < pl.num_programs(1) - 1))
    n_ctx = kc_ref.shape[0]
    offs = (0, n_ctx, n_ctx + tq, n_ctx + 2 * tq, n_ctx + 3 * tq)
    for half in range(2):
        col = lo_col if half == 0 else 1.0 - lo_col
        qth = (qt * col).astype(q_ref.dtype)
        for j in range(4):
            st = jnp.dot(k_refs[j][...], qth, preferred_element_type=F32)
            st_scr[half, offs[j]:offs[j + 1], :] = st if masks[j] is None else jnp.where(masks[j], st, NEG_INF)
    halves = []
    for half in range(2):
        col = lo_col if half == 0 else 1.0 - lo_col
        sink = sink_ref[half:half + 1, :]
        m = jnp.maximum(jnp.max(st_scr[half], axis=0, keepdims=True), sink)
        acc = jnp.exp2(sink - m) * (1.0 - col)
        for j in range(4):
            pt = jnp.exp2(st_scr[half, offs[j]:offs[j + 1], :] - m).astype(q_ref.dtype)
            acc = acc + jnp.dot(vx_refs[half][j][...], pt, preferred_element_type=F32)
        halves.append(acc)
    out_t = _pair_out_t(*halves)
    for p in range(n_pairs):
        o_ref[:, p * LANES:(p + 1) * LANES] = out_t[:, p * LANES:(p + 1) * LANES].T.astype(o_ref.dtype)


def _windowed(q_arr, k_arr, vx_lo, vx_hi, sink, *, s_len, n_ctx):
    batch, t_len = q_arr.shape[0], q_arr.shape[1]
    tq = SWA_WINDOW
    nb = s_len // tq
    n_pairs = SWA_HEADS // 2
    cb = s_len // n_ctx
    qw = n_pairs * LANES
    q_spec = pl.BlockSpec((None, tq, qw), lambda b, i: (b, i, 0))
    k_specs = [pl.BlockSpec((None, n_ctx, LANES), lambda b, i: (b, cb, 0))]
    v_specs = [pl.BlockSpec((LANES, n_ctx), lambda b, i: (0, b * (t_len // n_ctx) + cb))]
    for j in range(3):
        k_specs.append(pl.BlockSpec((None, tq, LANES),
                                    lambda b, i, j=j: (b, jnp.clip(i - 1 + j, 0, nb - 1), 0)))
        v_specs.append(pl.BlockSpec((LANES, tq),
                                    lambda b, i, j=j: (0, b * (t_len // tq) + jnp.clip(i - 1 + j, 0, nb - 1))))
    return pl.pallas_call(
        functools.partial(_windowed_kernel, n_pairs=n_pairs),
        grid=(batch, nb),
        in_specs=[q_spec] + k_specs + v_specs + v_specs + [pl.BlockSpec(sink.shape, lambda b, i: (0, 0))],
        out_specs=pl.BlockSpec((None, tq, qw), lambda b, i: (b, i, 0)),
        out_shape=jax.ShapeDtypeStruct((batch, s_len, qw), _MXU),
        scratch_shapes=[pltpu.VMEM((2, n_ctx + 3 * tq, qw), F32)],
        compiler_params=_cparams(("parallel", "parallel")),
        name="windowed_attn",
    )(q_arr, *([k_arr] * 4), *([vx_lo] * 4), *([vx_hi] * 4), sink)


def _layernorm(z, g, b):
    mu = jnp.mean(z, axis=1, keepdims=True)
    zc = z - mu
    var = jnp.mean(zc * zc, axis=1, keepdims=True)
    return zc * lax.rsqrt(var + LN_EPS) * g + b


def _post_kernel(x_ref, ya_ref, yb_ref, wo_ref, mod_ref, g_ref, b_ref, rw_ref, rb_ref,
                 x1_ref, tok_ref, route_ref, *, alpha):
    half = ya_ref.shape[1]
    y = (jnp.dot(ya_ref[...], wo_ref[:half, :], preferred_element_type=F32)
         + jnp.dot(yb_ref[...], wo_ref[half:, :], preferred_element_type=F32))
    x1 = _layernorm(alpha * x_ref[...] + mod_ref[2:3, :] * y, g_ref[...], b_ref[...])
    x1_ref[...] = x1
    tok = x1 * (1.0 + mod_ref[4:5, :]) + mod_ref[3:4, :]
    tok_ref[...] = tok
    logits = jnp.dot(tok.astype(_MXU), rw_ref[...], preferred_element_type=F32) + rb_ref[...]
    tm = logits.shape[0]
    lane = lax.broadcasted_iota(jnp.int32, (tm, LANES), 1).astype(F32)
    vals, idxs = [], []
    for _ in range(TOP_K):
        mx = jnp.max(logits, axis=1, keepdims=True)
        ix = jnp.min(jnp.where(logits == mx, lane, float(LANES)), axis=1, keepdims=True)
        vals.append(mx)
        idxs.append(ix)
        logits = jnp.where(lane == ix, -3.0e38, logits)
    es = [jnp.exp(v - vals[0]) for v in vals]
    den = functools.reduce(lambda a, c: a + c, es)
    route = jnp.zeros((tm, LANES), F32)
    for k in range(TOP_K):
        route = jnp.where(lane == float(k), idxs[k], route)
        route = jnp.where(lane == float(TOP_K + k), es[k] / den, route)
    route_ref[...] = route


def _post(x2d, ya, yb, wo, mods, g, b, rw, rb, *, batch, nbt_in, nblk, n_lat_blk, alpha):
    d = x2d.shape[1]
    tm = ROW_TILE
    half = ya.shape[1]

    def mod_map(bi, t):
        return (jnp.where(t >= n_lat_blk, batch, bi), 0, 0)

    rows_out = batch * nblk * tm
    o_map = lambda bi, t: (bi * nblk + t, 0)
    return pl.pallas_call(
        functools.partial(_post_kernel, alpha=alpha),
        grid=(batch, nblk),
        in_specs=[pl.BlockSpec((tm, d), lambda bi, t: (bi * nbt_in + t, 0)),
                  pl.BlockSpec((tm, half), o_map),
                  pl.BlockSpec((tm, half), o_map),
                  pl.BlockSpec((d, d), lambda bi, t: (0, 0)),
                  pl.BlockSpec((None, 6, d), mod_map),
                  pl.BlockSpec((1, d), lambda bi, t: (0, 0)),
                  pl.BlockSpec((1, d), lambda bi, t: (0, 0)),
                  pl.BlockSpec((d, LANES), lambda bi, t: (0, 0)),
                  pl.BlockSpec((1, LANES), lambda bi, t: (0, 0))],
        out_specs=[pl.BlockSpec((tm, d), o_map), pl.BlockSpec((tm, d), o_map), pl.BlockSpec((tm, LANES), o_map)],
        out_shape=[jax.ShapeDtypeStruct((rows_out, d), F32), jax.ShapeDtypeStruct((rows_out, d), F32),
                   jax.ShapeDtypeStruct((rows_out, LANES), F32)],
        compiler_params=_cparams(("parallel", "parallel")),
        name="post_attn",
    )(x2d, ya, yb, wo, mods, g, b, rw, rb)


def _moe_kernel(be_ref, nu_ref, dst_ref, src0_ref, src1_ref, src2_ref, tok_hbm, wi_ref, bi_ref, wo_ref, bo_ref,
                y_hbm, xbuf0, xbuf1, xbuf2, obuf0, obuf1, obuf2, gsem, ssem, wi_s, wo_s, *, dump0):
    i = pl.program_id(0)
    n_used = nu_ref[0]
    f = wo_s.shape[0]
    xbufs = (xbuf0, xbuf1, xbuf2)
    obufs = (obuf0, obuf1, obuf2)

    def gather_copy(row, r, slot):
        return pltpu.make_async_copy(tok_hbm.at[pl.ds(row, 1), :], xbufs[slot].at[pl.ds(r, 1), :], gsem.at[slot])

    def scatter_copy(row, r, slot):
        return pltpu.make_async_copy(obufs[slot].at[pl.ds(r, 1), :], y_hbm.at[pl.ds(row, 1), :], ssem.at[slot])

    def start_gather(rows_ref, slot):
        for r in range(MOE_ROWS):
            gather_copy(rows_ref[0, r], r, slot).start(priority=r % 2)

    def wait_gather(slot):
        for _ in range(MOE_ROWS):
            gather_copy(0, 0, slot).wait()

    def wait_scatter(slot):
        for _ in range(MOE_ROWS):
            scatter_copy(0, 0, slot).wait()

    @pl.when(i == 0)
    def _():
        for slot in range(MOE_RING):
            obufs[slot][...] = jnp.zeros(obufs[slot].shape, F32)
            for r in range(MOE_ROWS):
                scatter_copy(dump0 + slot * MOE_ROWS + r, r, slot).start(priority=r % 2)
        start_gather(src0_ref, 0)
        start_gather(src1_ref, 1)

    last_used = n_used - 1
    changed = jnp.logical_or(i == 0, be_ref[jnp.clip(i - 1, 0, last_used)] != be_ref[jnp.minimum(i, last_used)])

    for slot in range(MOE_RING):
        @pl.when(jnp.logical_and(i < n_used, i % MOE_RING == slot))
        def _(slot=slot):
            wait_scatter(slot)
            wait_gather(slot)

            @pl.when(changed)
            def _():
                wi_s[...] = wi_ref[...].astype(wi_s.dtype)
                wo_s[...] = wo_ref[...].astype(wo_s.dtype)

            x = xbufs[slot][...].astype(wi_s.dtype)
            start_gather(src2_ref, (slot + 2) % MOE_RING)
            hh = jnp.dot(x, wi_s[...], preferred_element_type=F32) + bi_ref[...]
            gate = jnp.minimum(hh[:, :f], SWIGLU_LIMIT)
            up = jnp.clip(hh[:, f:], -SWIGLU_LIMIT, SWIGLU_LIMIT)
            act = gate * (1.0 / (1.0 + jnp.exp(-SWIGLU_ALPHA * gate))) * (up + 1.0)
            obufs[slot][...] = jnp.dot(act.astype(wo_s.dtype), wo_s[...], preferred_element_type=F32) + bo_ref[...]
            for r in range(MOE_ROWS):
                scatter_copy(dst_ref[0, r], r, slot).start(priority=r % 2)

        @pl.when(jnp.logical_and(i >= n_used, i % MOE_RING == slot))
        def _(slot=slot):
            @pl.when(i < n_used + 2)
            def _():
                wait_gather(slot)

            @pl.when(i < n_used + MOE_RING)
            def _():
                wait_scatter(slot)


def _experts(tok, src_rows, dst_rows, block_e, n_used, w_in, b_in, w_out, b_out, layer):
    n_tok, d = tok.shape
    _, n_e, _, f2 = w_in.shape
    f = f2 // 2
    n_blocks = src_rows.shape[0]

    def blk(i, be, nu):
        return jnp.minimum(i, nu[0] - 1)

    def e_map(i, be, nu):
        return (layer, be[blk(i, be, nu)], 0, 0)

    def rows_spec(ahead):
        return pl.BlockSpec((None, 1, MOE_ROWS), lambda i, be, nu: (blk(i + ahead, be, nu), 0, 0),
                            memory_space=pltpu.SMEM)

    row_buf = pltpu.VMEM((MOE_ROWS, d), F32)
    grid_spec = pltpu.PrefetchScalarGridSpec(
        num_scalar_prefetch=2,
        grid=(n_blocks + MOE_RING,),
        in_specs=[rows_spec(0), rows_spec(0), rows_spec(1), rows_spec(2),
                  pl.BlockSpec(memory_space=pl.ANY),
                  pl.BlockSpec((None, None, d, f2), e_map),
                  pl.BlockSpec((None, None, 1, f2), e_map),
                  pl.BlockSpec((None, None, f, d), e_map),
                  pl.BlockSpec((None, None, 1, d), e_map)],
        out_specs=pl.BlockSpec(memory_space=pl.ANY),
        scratch_shapes=[row_buf] * (2 * MOE_RING)
                       + [pltpu.SemaphoreType.DMA((MOE_RING,)), pltpu.SemaphoreType.DMA((MOE_RING,)),
                          pltpu.VMEM((d, f2), _MXU), pltpu.VMEM((f, d), _MXU)],
    )
    depth = w_in.shape[0]
    return pl.pallas_call(
        functools.partial(_moe_kernel, dump0=TOP_K * n_tok), grid_spec=grid_spec,
        out_shape=jax.ShapeDtypeStruct((TOP_K * n_tok + MOE_RING * MOE_ROWS, d), F32),
        compiler_params=_cparams(("arbitrary",)),
        name="experts",
    )(block_e, n_used, dst_rows, src_rows, src_rows, src_rows, tok, w_in, b_in.reshape(depth, n_e, 1, f2), w_out,
      b_out.reshape(depth, n_e, 1, d))


def _route_plan(route, n_experts):
    n = route.shape[0]
    top_idx = route[:, :TOP_K].astype(jnp.int32)
    nk = n * TOP_K
    flat_e = top_idx.reshape(-1)
    order = jnp.argsort(flat_e).astype(jnp.int32)
    experts = jnp.arange(n_experts, dtype=jnp.int32)
    counts = jnp.sum(flat_e[:, None] == experts[None, :], axis=0, dtype=jnp.int32)
    padded = (counts + MOE_ROWS - 1) // MOE_ROWS * MOE_ROWS
    start = jnp.cumsum(counts) - counts
    pend = jnp.cumsum(padded)
    pstart = pend - padded
    n_blocks = -(-nk // MOE_ROWS) + n_experts
    first = jnp.arange(n_blocks, dtype=jnp.int32) * MOE_ROWS
    block_e = jnp.minimum(jnp.sum(pend[None, :] <= first[:, None], axis=1, dtype=jnp.int32), n_experts - 1)
    n_used = (pend[-1:] // MOE_ROWS).astype(jnp.int32)
    blk = jnp.arange(n_blocks, dtype=jnp.int32)[:, None]
    row = jnp.arange(MOE_ROWS, dtype=jnp.int32)[None, :]
    rank = blk * MOE_ROWS + row - jnp.take(pstart, block_e, mode="clip")[:, None]
    src = jnp.clip(jnp.take(start, block_e, mode="clip")[:, None] + rank, 0, nk - 1)
    valid = rank < jnp.take(counts, block_e, mode="clip")[:, None]
    pair = jnp.take(order, src, mode="clip")
    token = pair // TOP_K
    src_rows = jnp.where(valid, token, 0)
    dst_rows = jnp.where(valid, (pair % TOP_K) * n + token, nk + (blk % MOE_RING) * MOE_ROWS + row)
    return (src_rows.reshape(n_blocks, 1, MOE_ROWS), dst_rows.reshape(n_blocks, 1, MOE_ROWS), block_e, n_used)


def _ln2_kernel(x_ref, y0_ref, y1_ref, y2_ref, y3_ref, route_ref, mod_ref, g_ref, b_ref, o_ref, *, alpha):
    ys = (y0_ref, y1_ref, y2_ref, y3_ref)
    f = route_ref[:, TOP_K:TOP_K + 1] * ys[0][...]
    for k in range(1, TOP_K):
        f = f + route_ref[:, TOP_K + k:TOP_K + k + 1] * ys[k][...]
    o_ref[...] = _layernorm(alpha * x_ref[...] + mod_ref[5:6, :] * f, g_ref[...], b_ref[...])


def _ln2(x1, y, route, mods, g, b, *, batch, nblk, n_lat_blk, alpha):
    rows, d = x1.shape
    tm = ROW_TILE
    r_map = lambda bi, t: (bi * nblk + t, 0)

    def y_spec(k):
        return pl.BlockSpec((tm, d), lambda bi, t: (k * (rows // tm) + bi * nblk + t, 0))

    def mod_map(bi, t):
        return (jnp.where(t >= n_lat_blk, batch, bi), 0, 0)

    return pl.pallas_call(
        functools.partial(_ln2_kernel, alpha=alpha),
        grid=(batch, nblk),
        in_specs=[pl.BlockSpec((tm, d), r_map)] + [y_spec(k) for k in range(TOP_K)]
                 + [pl.BlockSpec((tm, LANES), r_map), pl.BlockSpec((None, 6, d), mod_map),
                    pl.BlockSpec((1, d), lambda bi, t: (0, 0)), pl.BlockSpec((1, d), lambda bi, t: (0, 0))],
        out_specs=pl.BlockSpec((tm, d), r_map),
        out_shape=jax.ShapeDtypeStruct((rows, d), F32),
        compiler_params=_cparams(("parallel", "parallel")),
        name="combine_ln2",
    )(x1, y, y, y, y, route, mods, g, b)


def _pair_layout(w, n_heads, axis=-1):
    axis = axis % w.ndim
    shape = w.shape
    w = w.reshape(shape[:axis] + (2, n_heads // 2, HEAD_DIM) + shape[axis + 1:])
    return w.swapaxes(axis, axis + 1).reshape(shape)


def _rope_tables(s_len, n_ctx):
    t = np.arange(s_len)
    row = (t // GRID_W).astype(np.float32)
    col = (t % GRID_W).astype(np.float32)
    axis_dim = HEAD_DIM // 2
    freqs = jnp.asarray(ROPE_THETA, F32) ** (-jnp.arange(0, axis_dim, 2, dtype=F32) / axis_dim)
    ang = jnp.concatenate([jnp.asarray(row)[:, None] * freqs, jnp.asarray(col)[:, None] * freqs], axis=-1)
    cos, sin = jnp.cos(ang), jnp.sin(ang)
    cos_h = jnp.repeat(cos, 2, axis=-1)
    sin_h = jnp.stack([-sin, sin], axis=-1).reshape(s_len, HEAD_DIM)
    cos_t = jnp.concatenate([cos_h, jnp.ones((n_ctx, HEAD_DIM), F32)], axis=0)
    sin_t = jnp.concatenate([sin_h, jnp.zeros((n_ctx, HEAD_DIM), F32)], axis=0)
    return jnp.tile(cos_t, (1, 2)), jnp.tile(sin_t, (1, 2))


def _moe_and_ln2(x1, tok, route, mods, n_experts, w_in, b_in, w_out, b_out, layer, g, b, *,
                 batch, nblk, n_lat_blk, alpha):
    src_rows, dst_rows, block_e, n_used = _route_plan(route, n_experts)
    y = _experts(tok, src_rows, dst_rows, block_e, n_used, w_in, b_in, w_out, b_out, layer)
    return _ln2(x1, y, route, mods, g, b, batch=batch, nblk=nblk, n_lat_blk=n_lat_blk, alpha=alpha)


def kernel(x, c, ctx, c_ctx, mod_w, mod_b, ln1_g, ln1_b, ln2_g, ln2_b, router_w, router_b, moe_w_in, moe_b_in,
           moe_w_out, moe_b_out, ab_w_in, ab_w_out, na_rpb, diff_lq1, diff_lk1, diff_lq2, diff_lk2, diff_subln,
           cd_w_in, cd_w_out, gqa_q_norm, gqa_k_norm, swa_sink):
    batch, s_len, d = x.shape
    n_ctx = ctx.shape[1]
    t_len = n_ctx + s_len
    depth = mod_w.shape[0]
    n_experts = router_w.shape[2]
    alpha = (2.0 * depth) ** 0.25
    tm = ROW_TILE
    nbt = t_len // tm
    nbs = s_len // tm
    assert depth == 2 and n_ctx % tm == 0 and s_len % tm == 0 and batch + 1 <= 8

    cc = jnp.zeros((8, d), F32).at[:batch].set(c).at[batch].set(c_ctx)
    mod_all = _modulation(cc, mod_w, mod_b)
    mods = [mod_all[l, :batch + 1].reshape(batch + 1, 6, d) for l in range(depth)]
    cos_t, sin_t = _rope_tables(s_len, n_ctx)
    rw = [jnp.zeros((d, LANES), F32).at[:, :n_experts].set(router_w[l]).astype(_MXU) for l in range(depth)]
    rb = [jnp.full((1, LANES), NEG_INF, F32).at[0, :n_experts].set(router_b[l]) for l in range(depth)]

    stream = jnp.concatenate([x, ctx], axis=1).reshape(batch * t_len, d)
    by_batch = lambda a: a.reshape(batch, t_len, a.shape[-1])

    na_w = NA_HEADS * HEAD_DIM
    df_w = DIFF_HEADS * 2 * HEAD_DIM
    w0 = ab_w_in[0].astype(_MXU)
    rows_only = ("rows",)
    plan0 = ((0, na_w, None, False, QSCALE, rows_only), (na_w, na_w, None, False, 1.0, rows_only),
             (2 * na_w, na_w, None, False, 1.0, ("rows", "value_t")),
             (3 * na_w, df_w, None, True, QSCALE, rows_only), (3 * na_w + df_w, df_w, None, True, 1.0, rows_only),
             (3 * na_w + 2 * df_w, df_w, None, False, 1.0, rows_only))
    nw0 = jnp.zeros((8, LANES), F32)
    nq, nk, nv, nvx_lo, nvx_hi, dq, dk, dv = _in_proj(stream, mods[0], w0, cos_t, sin_t, nw0, plan0,
                                                      batch=batch, n_ctx=n_ctx)
    nq, nk, nv, dq, dk, dv = map(by_batch, (nq, nk, nv, dq, dk, dv))
    lam_init0 = 0.8 - 0.6 * math.exp(-0.3 * 0)
    par = jnp.zeros((8, LANES), F32)
    par = par.at[0, :HEAD_DIM].set(diff_lq1[0]).at[1, :HEAD_DIM].set(diff_lk1[0])
    par = par.at[2, :HEAD_DIM].set(diff_lq2[0]).at[3, :HEAD_DIM].set(diff_lk2[0]).at[4].set(diff_subln[0])
    bias = _na_bias_tables(na_rpb[0], s_len // GRID_W)
    y_na = _neighbourhood(nq, nk, nvx_lo, nvx_hi, bias, s_len=s_len, n_ctx=n_ctx)
    y_na_c = _flash(nq, nk, nv, n_groups=NA_HEADS // 2, q_rows=n_ctx, q_off=s_len, kv_rows=n_ctx,
                    kv_off=s_len, k_col=0, v_col=0, mode="pair", name="neighbourhood_attn_ctx")
    y_df = _flash(dq, dk, dv, n_groups=DIFF_HEADS, q_rows=s_len, q_off=0, kv_rows=t_len, kv_off=0, k_col=0, v_col=0,
                  mode="diff", par=par, lam_init=lam_init0, name="diff_attn")
    y_df_c = _flash(dq, dk, dv, n_groups=DIFF_HEADS, q_rows=n_ctx, q_off=s_len, kv_rows=n_ctx, kv_off=s_len,
                    k_col=0, v_col=0,
                    mode="diff", par=par, lam_init=lam_init0, name="diff_attn_ctx")
    ya = jnp.concatenate([y_na, y_na_c], axis=1).reshape(batch * t_len, -1)
    yb = jnp.concatenate([y_df, y_df_c], axis=1).reshape(batch * t_len, -1)
    x1, tok, route = _post(stream, ya, yb, ab_w_out[0].astype(_MXU), mods[0], ln1_g[0][None], ln1_b[0][None],
                           rw[0], rb[0], batch=batch, nbt_in=nbt, nblk=nbt, n_lat_blk=nbs, alpha=alpha)
    stream = _moe_and_ln2(x1, tok, route, mods[0], n_experts, moe_w_in, moe_b_in, moe_w_out, moe_b_out, 0,
                          ln2_g[0][None], ln2_b[0][None], batch=batch, nblk=nbt, n_lat_blk=nbs, alpha=alpha)

    hw = GQA_HEADS * HEAD_DIM
    kw = GQA_KV_HEADS * HEAD_DIM
    wcd = cd_w_in[0]
    gq, gk, gv, wq, wk, wv = jnp.split(wcd, [hw, hw + kw, hw + 2 * kw, 2 * hw + 2 * kw, 2 * hw + 3 * kw], axis=1)
    w1 = jnp.concatenate([_pair_layout(gq, GQA_HEADS), _pair_layout(wq, SWA_HEADS), gk, gv, wk, wv],
                         axis=1).astype(_MXU)
    plan1 = ((0, hw, 0, True, QSCALE, rows_only), (hw, hw, None, True, QSCALE, rows_only),
             (2 * hw, kw, 1, True, 1.0, rows_only), (2 * hw + kw, kw, None, False, 1.0, rows_only),
             (2 * hw + 2 * kw, kw, None, True, 1.0, rows_only),
             (2 * hw + 3 * kw, kw, None, False, 1.0, ("value_t",)))
    nw1 = jnp.zeros((8, LANES), F32).at[0].set(jnp.tile(gqa_q_norm[0], 2)).at[1].set(jnp.tile(gqa_k_norm[0], 2))
    gq, wq, gk, gv, wk, wvx_lo, wvx_hi = _in_proj(stream, mods[1], w1, cos_t, sin_t, nw1, plan1,
                                                  batch=batch, n_ctx=n_ctx)
    gq, wq, gk, gv, wk = map(by_batch, (gq, wq, gk, gv, wk))
    y_c = _flash(gq, gk, gv, n_groups=1, n_qp=GQA_HEADS // 2, q_rows=s_len, q_off=0, kv_rows=t_len, kv_off=0,
                 k_col=0, v_col=0, mode="pair", name="gqa_attn")
    sink = jnp.repeat(swa_sink[0].reshape(2, SWA_HEADS // 2) * LOG2E, LANES, axis=1).astype(F32)
    y_d = _windowed(wq, wk, wvx_lo, wvx_hi, sink, s_len=s_len, n_ctx=n_ctx)
    wo1 = jnp.concatenate([_pair_layout(cd_w_out[0][:hw], GQA_HEADS, axis=0),
                           _pair_layout(cd_w_out[0][hw:], SWA_HEADS, axis=0)], axis=0).astype(_MXU)
    x1, tok, route = _post(stream, y_c.reshape(batch * s_len, -1), y_d.reshape(batch * s_len, -1), wo1, mods[1],
                           ln1_g[1][None], ln1_b[1][None], rw[1], rb[1], batch=batch, nbt_in=nbt,
                           nblk=nbs, n_lat_blk=nbs, alpha=alpha)
    out = _moe_and_ln2(x1, tok, route, mods[1], n_experts, moe_w_in, moe_b_in, moe_w_out, moe_b_out, 1,
                       ln2_g[1][None], ln2_b[1][None], batch=batch, nblk=nbs, n_lat_blk=nbs, alpha=alpha)
    return out.reshape(batch, s_len, d)
```

```python
import functools
import math

import numpy as np
import jax
import jax.numpy as jnp
from jax import lax
from jax.experimental import pallas as pl
from jax.experimental.pallas import tpu as pltpu

F32 = jnp.float32
_MXU = jnp.bfloat16

HEAD_DIM = 64
GRID_W = 64
LOG2E = math.log2(math.e)
QSCALE = HEAD_DIM ** -0.5 * LOG2E
ROPE_THETA = 10000.0
NA_HEADS = 8
NA_KH = 8
NA_KW = 16
DIFF_HEADS = 4
GQA_HEADS = 8
GQA_KV_HEADS = 2
SWA_HEADS = 8
SWA_KV_HEADS = 2
SWA_WINDOW = 128
TOP_K = 4
SWIGLU_LIMIT = 7.0
SWIGLU_ALPHA = 1.702
LN_EPS = 1e-5
RMS_EPS = 1e-6
NEG_INF = -1e30

LANES = 128
ROW_TILE = 256
MOE_ROWS = 256
MOE_RING = 3
NA_Q_ROWS = 4
VMEM_LIMIT = 52 * 1024 * 1024


def _cparams(sem, vmem=VMEM_LIMIT):
    return pltpu.CompilerParams(dimension_semantics=sem, vmem_limit_bytes=vmem)


def _mod_kernel(c_ref, w_ref, b_ref, o_ref):
    c = c_ref[...]
    a = (c / (1.0 + jnp.exp(-c))).astype(_MXU)
    o_ref[...] = jnp.dot(a, w_ref[...].astype(_MXU), preferred_element_type=F32) + b_ref[...]


def _modulation(cc, mod_w, mod_b):
    depth, d, d6 = mod_w.shape
    tn = d6 // 4
    return pl.pallas_call(
        _mod_kernel,
        grid=(depth, d6 // tn),
        in_specs=[pl.BlockSpec((8, d), lambda l, j: (0, 0)),
                  pl.BlockSpec((None, d, tn), lambda l, j: (l, 0, j)),
                  pl.BlockSpec((None, 1, tn), lambda l, j: (l, 0, j))],
        out_specs=pl.BlockSpec((None, 8, tn), lambda l, j: (l, 0, j)),
        out_shape=jax.ShapeDtypeStruct((depth, 8, d6), F32),
        compiler_params=_cparams(("arbitrary", "arbitrary")),
        name="modulation",
    )(cc, mod_w, mod_b.reshape(depth, 1, d6))


def _in_kernel(x_ref, mod_ref, w_ref, cos_ref, sin_ref, nw_ref, *o_refs, plan):
    x = x_ref[...]
    h = (x * (1.0 + mod_ref[1:2, :]) + mod_ref[0:1, :]).astype(_MXU)
    tm = x.shape[0]
    lane = lax.broadcasted_iota(jnp.int32, (tm, LANES), 1)
    even_lane = (lane & 1) == 0
    gi = lax.broadcasted_iota(jnp.int32, (LANES, LANES), 0) // HEAD_DIM
    gj = lax.broadcasted_iota(jnp.int32, (LANES, LANES), 1) // HEAD_DIM
    seg = jnp.where(gi == gj, 1.0, 0.0).astype(_MXU)
    cos = cos_ref[...]
    sin = sin_ref[...]
    lo_col = _head_col()
    outs = iter(o_refs)
    for (c0, width, norm_row, rope, scale, kinds) in plan:
        acc = jnp.dot(h, w_ref[:, c0:c0 + width], preferred_element_type=F32)
        o_rows = next(outs) if "rows" in kinds else None
        o_lo, o_hi = (next(outs), next(outs)) if "value_t" in kinds else (None, None)
        for j in range(width // LANES):
            a = acc[:, j * LANES:(j + 1) * LANES]
            if norm_row is not None:
                a2 = a * a
                hi = a2.astype(_MXU)
                lo = (a2 - hi.astype(F32)).astype(_MXU)
                ss = (jnp.dot(hi, seg, preferred_element_type=F32)
                      + jnp.dot(lo, seg, preferred_element_type=F32))
                a = a * lax.rsqrt(ss * (1.0 / HEAD_DIM) + RMS_EPS) * nw_ref[norm_row:norm_row + 1, :]
            if rope:
                partner = jnp.where(even_lane, pltpu.roll(a, LANES - 1, 1), pltpu.roll(a, 1, 1))
                a = a * cos + partner * sin
            if scale != 1.0:
                a = a * scale
            if o_rows is not None:
                o_rows[:, j * LANES:(j + 1) * LANES] = a.astype(o_rows.dtype)
            if o_lo is not None:
                at = a.T
                o_lo[j * LANES:(j + 1) * LANES, :] = (at * lo_col + (1.0 - lo_col)).astype(o_lo.dtype)
                o_hi[j * LANES:(j + 1) * LANES, :] = (at * (1.0 - lo_col) + lo_col).astype(o_hi.dtype)


def _in_proj(x2d, mods, w, cos_t, sin_t, nw, plan, *, batch, n_ctx):
    rows, d = x2d.shape
    ncols = w.shape[1]
    tm = ROW_TILE
    nbt = rows // batch // tm
    n_lat_blk = nbt - n_ctx // tm

    def mod_map(i):
        return (jnp.where(i % nbt >= n_lat_blk, batch, i // nbt), 0, 0)

    out_specs, out_shape = [], []
    for (_, width, _, _, _, kinds) in plan:
        if "rows" in kinds:
            out_specs.append(pl.BlockSpec((tm, width), lambda i: (i, 0)))
            out_shape.append(jax.ShapeDtypeStruct((rows, width), _MXU))
        if "value_t" in kinds:
            out_specs += [pl.BlockSpec((width, tm), lambda i: (0, i))] * 2
            out_shape += [jax.ShapeDtypeStruct((width, rows), _MXU)] * 2

    return pl.pallas_call(
        functools.partial(_in_kernel, plan=plan),
        grid=(rows // tm,),
        in_specs=[pl.BlockSpec((tm, d), lambda i: (i, 0)),
                  pl.BlockSpec((None, 6, d), mod_map),
                  pl.BlockSpec((d, ncols), lambda i: (0, 0)),
                  pl.BlockSpec((tm, LANES), lambda i: (i % nbt, 0)),
                  pl.BlockSpec((tm, LANES), lambda i: (i % nbt, 0)),
                  pl.BlockSpec((8, LANES), lambda i: (0, 0))],
        out_specs=out_specs,
        out_shape=out_shape,
        compiler_params=_cparams(("parallel",)),
        name="in_proj",
    )(x2d, mods, w, cos_t, sin_t, nw)


def _flash_kernel(*refs, mode, tk, n_chunks, lam_init):
    if mode == "diff":
        q_ref, k_ref, v_ref, par_ref, o_ref, qt_scr, vxt_scr, m_scr, acc_scr, st_scr = refs
    else:
        q_ref, k_ref, v_ref, o_ref, qt_scr, vxt_scr, m_scr, acc_scr, st_scr = refs
    lo_col = _head_col()

    @pl.when(pl.program_id(2) == 0)
    def _():
        for c in range(n_chunks):
            vt = v_ref[c * tk:(c + 1) * tk, :].astype(F32).T
            if mode == "diff":
                vxt_scr[c, :LANES, :] = vt.astype(vxt_scr.dtype)
                vxt_scr[c, LANES:, :] = jnp.ones((LANES, tk), vxt_scr.dtype)
            else:
                vxt_scr[0, c] = (vt * lo_col + (1.0 - lo_col)).astype(vxt_scr.dtype)
                vxt_scr[1, c] = (vt * (1.0 - lo_col) + lo_col).astype(vxt_scr.dtype)

    n_qp = q_ref.shape[1] // LANES
    qt = jnp.concatenate([q_ref[:, p * LANES:(p + 1) * LANES].astype(F32).T for p in range(n_qp)], axis=1)
    qt_scr[0] = (qt * lo_col).astype(qt_scr.dtype)
    qt_scr[1] = (qt * (1.0 - lo_col)).astype(qt_scr.dtype)
    m_scr[...] = jnp.full(m_scr.shape, NEG_INF, F32)
    acc_scr[...] = jnp.zeros(acc_scr.shape, F32)

    def qk(c, slot):
        off = pl.multiple_of(c * tk, tk)
        k = k_ref[pl.ds(off, tk), :]
        for h in range(2):
            st_scr[slot, h] = jnp.dot(k, qt_scr[h], preferred_element_type=F32)

    def softmax_pv(c, slot):
        for h in range(2):
            st = st_scr[slot, h]
            m_prev = m_scr[h]
            m_new = jnp.maximum(m_prev, jnp.max(st, axis=0, keepdims=True))
            alpha = jnp.exp2(m_prev - m_new)
            pt = jnp.exp2(st - m_new).astype(vxt_scr.dtype)
            vxt = vxt_scr[c] if mode == "diff" else vxt_scr[h, c]
            acc_scr[h] = alpha * acc_scr[h] + jnp.dot(vxt, pt, preferred_element_type=F32)
            m_scr[h] = m_new

    qk(0, 0)

    def body(j, carry):
        c = 2 * j
        qk(c + 1, 1)
        softmax_pv(c, 0)
        qk(c + 2, 0)
        softmax_pv(c + 1, 1)
        return carry

    lax.fori_loop(0, (n_chunks - 1) // 2, body, 0)
    if n_chunks % 2 == 1:
        softmax_pv(n_chunks - 1, 0)
    else:
        qk(n_chunks - 1, 1)
        softmax_pv(n_chunks - 2, 0)
        softmax_pv(n_chunks - 1, 1)

    a_lo = acc_scr[0]
    a_hi = acc_scr[1]
    if mode == "diff":
        lam = (jnp.exp(jnp.sum(par_ref[0:1, :] * par_ref[1:2, :], axis=1, keepdims=True))
               - jnp.exp(jnp.sum(par_ref[2:3, :] * par_ref[3:4, :], axis=1, keepdims=True))
               + lam_init)
        out_t = a_lo[:LANES] / a_lo[LANES:] - lam * (a_hi[:LANES] / a_hi[LANES:])
        ms = jnp.mean(out_t * out_t, axis=0, keepdims=True)
        o_ref[...] = ((out_t * lax.rsqrt(ms + RMS_EPS)).T * par_ref[4:5, :] * (1.0 - lam_init)).astype(o_ref.dtype)
    else:
        out_t = _pair_out_t(a_lo, a_hi)
        rows = o_ref.shape[0]
        for p in range(n_qp):
            o_ref[:, p * LANES:(p + 1) * LANES] = out_t[:, p * rows:(p + 1) * rows].T.astype(o_ref.dtype)


def _pick_tile(n, candidates):
    for c in candidates:
        if n % c == 0:
            return c
    raise ValueError(f"no tile for {n}")


def _flash(q_arr, k_arr, v_arr, *, n_groups, q_rows, q_off, kv_rows, kv_off, k_col, v_col, mode, n_qp=1, par=None,
           lam_init=0.0, name):
    batch = q_arr.shape[0]
    tr = _pick_tile(q_rows, (1024 // n_qp, 512 // n_qp, 256 // n_qp))
    tq = tr * n_qp
    qw = n_qp * LANES
    tk = _pick_tile(kv_rows, (768, 512, 384, 256, 128))
    nq, n_chunks = q_rows // tr, kv_rows // tk
    assert q_off % tr == 0 and kv_off % kv_rows == 0
    vw = 2 * LANES if mode == "diff" else LANES
    in_specs = [pl.BlockSpec((None, tr, qw), lambda b, g, i: (b, q_off // tr + i, g)),
                pl.BlockSpec((None, kv_rows, LANES), lambda b, g, i: (b, kv_off // kv_rows, k_col + g)),
                pl.BlockSpec((None, kv_rows, LANES), lambda b, g, i: (b, kv_off // kv_rows, v_col + g))]
    args = [q_arr, k_arr, v_arr]
    if mode == "diff":
        in_specs.append(pl.BlockSpec(par.shape, lambda b, g, i: (0, 0)))
        args.append(par)
    return pl.pallas_call(
        functools.partial(_flash_kernel, mode=mode, tk=tk, n_chunks=n_chunks, lam_init=lam_init),
        grid=(batch, n_groups, nq),
        in_specs=in_specs,
        out_specs=pl.BlockSpec((None, tr, qw), lambda b, g, i: (b, i, g)),
        out_shape=jax.ShapeDtypeStruct((batch, q_rows, n_groups * qw), _MXU),
        scratch_shapes=[pltpu.VMEM((2, LANES, tq), _MXU),
                        pltpu.VMEM((n_chunks, vw, tk) if mode == "diff" else (2, n_chunks, vw, tk), _MXU),
                        pltpu.VMEM((2, 1, tq), F32),
                        pltpu.VMEM((2, vw, tq), F32),
                        pltpu.VMEM((2, 2, tk, tq), F32)],
        compiler_params=_cparams(("parallel", "parallel", "arbitrary")),
        name=name,
    )(*args)


def _head_col():
    sub = lax.broadcasted_iota(jnp.int32, (LANES, 1), 0)
    return jnp.where(sub < HEAD_DIM, 1.0, 0.0)


def _pair_out_t(a_lo, a_hi):
    return jnp.concatenate([a_lo[:HEAD_DIM] / a_lo[HEAD_DIM:], a_hi[HEAD_DIM:] / a_hi[:HEAD_DIM]], axis=0)


def _na_kernel(q_ref, kc_ref, k0_ref, k1_ref, k2_ref, *rest, n_pairs):
    vx_refs = (rest[0:4], rest[4:8])
    bias_ref, o_ref, st_scr = rest[8], rest[9], rest[10]
    tq = q_ref.shape[0]
    k_refs = (kc_ref, k0_ref, k1_ref, k2_ref)
    lo_col = _head_col()
    for p in range(n_pairs):
        cols = slice(p * LANES, (p + 1) * LANES)
        qt = q_ref[:, cols].astype(F32).T
        for half in range(2):
            col = lo_col if half == 0 else 1.0 - lo_col
            qth = (qt * col).astype(q_ref.dtype)
            st_scr[p % 2, half, 0] = jnp.dot(kc_ref[:, cols], qth, preferred_element_type=F32)
            for j in range(1, 4):
                st_scr[p % 2, half, j] = (jnp.dot(k_refs[j][:, cols], qth, preferred_element_type=F32)
                                          + bias_ref[p, half, (j - 1) * tq:j * tq, :])
        halves = []
        for half in range(2):
            sts = [st_scr[p % 2, half, j] for j in range(4)]
            m = functools.reduce(jnp.maximum, [jnp.max(st, axis=0, keepdims=True) for st in sts])
            acc = jnp.zeros((LANES, tq), F32)
            for j in range(4):
                pt = jnp.exp2(sts[j] - m).astype(q_ref.dtype)
                acc = acc + jnp.dot(vx_refs[half][j][cols, :], pt, preferred_element_type=F32)
            halves.append(acc)
        o_ref[:, cols] = _pair_out_t(*halves).T.astype(o_ref.dtype)


def _na_bias_tables(rpb, rows):
    nh = rpb.shape[0]
    nkr = 3 * NA_Q_ROWS
    qc = np.arange(GRID_W)[:, None]
    kc = np.arange(GRID_W)[None, :]
    ws = np.clip(qc - NA_KW // 2, 0, GRID_W - NA_KW)
    cvalid = ((kc >= ws) & (kc < ws + NA_KW)).reshape(-1)
    dc = (kc - qc + NA_KW - 1).reshape(-1)
    onehot = ((np.arange(2 * NA_KW - 1)[:, None] == dc[None, :]) & cvalid[None, :]).astype(np.float32)
    tiles = jnp.einsum("hrd,dx->hrx", rpb.astype(F32) * LOG2E, jnp.asarray(onehot),
                       precision=lax.Precision.HIGHEST)
    tiles = jnp.where(cvalid[None, None, :], tiles, NEG_INF)
    tiles = tiles.reshape(nh, 2 * NA_KH - 1, GRID_W, GRID_W).swapaxes(-1, -2)
    masked_tile = 2 * NA_KH - 1
    tiles = jnp.concatenate([tiles, jnp.full((nh, 1, GRID_W, GRID_W), NEG_INF, F32)], axis=1)
    kr = np.arange(nkr)[:, None]
    qr = np.arange(NA_Q_ROWS)[None, :]
    tile_idx = []
    for variant in range(3):
        if variant == 0:
            r0, k0, nrows = 0, 0, rows
        elif variant == 1:
            r0, k0, nrows = 2 * NA_Q_ROWS, NA_Q_ROWS, 8 * NA_Q_ROWS
        else:
            r0, k0, nrows = rows - NA_Q_ROWS, rows - nkr, rows
        r = r0 + qr
        rp = k0 + kr
        rs = np.clip(r - NA_KH // 2, 0, nrows - NA_KH)
        rvalid = (rp >= rs) & (rp < rs + NA_KH)
        tile_idx.append(np.where(rvalid, rp - r + NA_KH - 1, masked_tile))
    idx = jnp.asarray(np.stack(tile_idx).reshape(-1).astype(np.int32))
    t = jnp.take(tiles, idx, axis=1)
    t = t.reshape(nh, 3, nkr, NA_Q_ROWS, GRID_W, GRID_W).transpose(1, 0, 2, 4, 3, 5)
    return t.reshape(3, nh // 2, 2, nkr * GRID_W, NA_Q_ROWS * GRID_W)


def _neighbourhood(q_arr, k_arr, vx_lo, vx_hi, bias, *, s_len, n_ctx):
    batch, t_len, width = q_arr.shape
    tq = NA_Q_ROWS * GRID_W
    assert n_ctx == tq
    nq = s_len // tq
    nbt = t_len // tq
    n_pairs = NA_HEADS // 2
    cb = s_len // n_ctx

    def seg_blocks(i):
        first = jnp.clip(i - 1, 0, nq - 3)
        return [cb, first, first + 1, first + 2]

    q_spec = pl.BlockSpec((None, tq, width), lambda i, b: (b, i, 0))
    k_specs = [pl.BlockSpec((None, tq, width), lambda i, b, j=j: (b, seg_blocks(i)[j], 0)) for j in range(4)]
    v_specs = [pl.BlockSpec((width, tq), lambda i, b, j=j: (0, b * nbt + seg_blocks(i)[j])) for j in range(4)]
    bias_spec = pl.BlockSpec((None, n_pairs, 2, 3 * tq, tq),
                             lambda i, b: (jnp.where(i == 0, 0, jnp.where(i == nq - 1, 2, 1)), 0, 0, 0, 0))
    return pl.pallas_call(
        functools.partial(_na_kernel, n_pairs=n_pairs),
        grid=(nq, batch),
        in_specs=[q_spec] + k_specs + v_specs + v_specs + [bias_spec],
        out_specs=pl.BlockSpec((None, tq, width), lambda i, b: (b, i, 0)),
        out_shape=jax.ShapeDtypeStruct((batch, s_len, width), _MXU),
        scratch_shapes=[pltpu.VMEM((2, 2, 4, tq, tq), F32)],
        compiler_params=_cparams(("parallel", "parallel")),
        name="neighbourhood_attn",
    )(q_arr, *([k_arr] * 4), *([vx_lo] * 4), *([vx_hi] * 4), bias)


def _windowed_kernel(q_ref, kc_ref, k0_ref, k1_ref, k2_ref, *rest, n_pairs):
    vx_refs = (rest[0:4], rest[4:8])
    sink_ref, o_ref, st_scr = rest[8], rest[9], rest[10]
    tq = q_ref.shape[0]
    qi = pl.program_id(1)
    k_refs = (kc_ref, k0_ref, k1_ref, k2_ref)
    lo_col = _head_col()
    qt = jnp.concatenate([q_ref[:, p * LANES:(p + 1) * LANES].astype(F32).T for p in range(n_pairs)], axis=1)

    qq = lax.broadcasted_iota(jnp.int32, (1, n_pairs * LANES), 1) & (LANES - 1)
    kk = lax.broadcasted_iota(jnp.int32, (tq, 1), 0)
    masks = (None, jnp.logical_and(kk >= qq, qi >= 1), None,
             jnp.logical_and(kk <= qq, qi < pl.num_programs(1) - 1))
    n_ctx = kc_ref.shape[0]
    offs = (0, n_ctx, n_ctx + tq, n_ctx + 2 * tq, n_ctx + 3 * tq)
    for half in range(2):
        col = lo_col if half == 0 else 1.0 - lo_col
        qth = (qt * col).astype(q_ref.dtype)
        for j in range(4):
            st = jnp.dot(k_refs[j][...], qth, preferred_element_type=F32)
            st_scr[half, offs[j]:offs[j + 1], :] = st if masks[j] is None else jnp.where(masks[j], st, NEG_INF)
    halves = []
    for half in range(2):
        col = lo_col if half == 0 else 1.0 - lo_col
        sink = sink_ref[half:half + 1, :]
        m = jnp.maximum(jnp.max(st_scr[half], axis=0, keepdims=True), sink)
        acc = jnp.exp2(sink - m) * (1.0 - col)
        for j in range(4):
            pt = jnp.exp2(st_scr[half, offs[j]:offs[j + 1], :] - m).astype(q_ref.dtype)
            acc = acc + jnp.dot(vx_refs[half][j][...], pt, preferred_element_type=F32)
        halves.append(acc)
    out_t = _pair_out_t(*halves)
    for p in range(n_pairs):
        o_ref[:, p * LANES:(p + 1) * LANES] = out_t[:, p * LANES:(p + 1) * LANES].T.astype(o_ref.dtype)


def _windowed(q_arr, k_arr, vx_lo, vx_hi, sink, *, s_len, n_ctx):
    batch, t_len = q_arr.shape[0], q_arr.shape[1]
    tq = SWA_WINDOW
    nb = s_len // tq
    n_pairs = SWA_HEADS // 2
    cb = s_len // n_ctx
    qw = n_pairs * LANES
    q_spec = pl.BlockSpec((None, tq, qw), lambda b, i: (b, i, 0))
    k_specs = [pl.BlockSpec((None, n_ctx, LANES), lambda b, i: (b, cb, 0))]
    v_specs = [pl.BlockSpec((LANES, n_ctx), lambda b, i: (0, b * (t_len // n_ctx) + cb))]
    for j in range(3):
        k_specs.append(pl.BlockSpec((None, tq, LANES),
                                    lambda b, i, j=j: (b, jnp.clip(i - 1 + j, 0, nb - 1), 0)))
        v_specs.append(pl.BlockSpec((LANES, tq),
                                    lambda b, i, j=j: (0, b * (t_len // tq) + jnp.clip(i - 1 + j, 0, nb - 1))))
    return pl.pallas_call(
        functools.partial(_windowed_kernel, n_pairs=n_pairs),
        grid=(batch, nb),
        in_specs=[q_spec] + k_specs + v_specs + v_specs + [pl.BlockSpec(sink.shape, lambda b, i: (0, 0))],
        out_specs=pl.BlockSpec((None, tq, qw), lambda b, i: (b, i, 0)),
        out_shape=jax.ShapeDtypeStruct((batch, s_len, qw), _MXU),
        scratch_shapes=[pltpu.VMEM((2, n_ctx + 3 * tq, qw), F32)],
        compiler_params=_cparams(("parallel", "parallel")),
        name="windowed_attn",
    )(q_arr, *([k_arr] * 4), *([vx_lo] * 4), *([vx_hi] * 4), sink)


def _layernorm(z, g, b):
    mu = jnp.mean(z, axis=1, keepdims=True)
    zc = z - mu
    var = jnp.mean(zc * zc, axis=1, keepdims=True)
    return zc * lax.rsqrt(var + LN_EPS) * g + b


def _post_kernel(x_ref, ya_ref, yb_ref, wo_ref, mod_ref, g_ref, b_ref, rw_ref, rb_ref,
                 x1_ref, tok_ref, route_ref, *, alpha):
    half = ya_ref.shape[1]
    y = (jnp.dot(ya_ref[...], wo_ref[:half, :], preferred_element_type=F32)
         + jnp.dot(yb_ref[...], wo_ref[half:, :], preferred_element_type=F32))
    x1 = _layernorm(alpha * x_ref[...] + mod_ref[2:3, :] * y, g_ref[...], b_ref[...])
    x1_ref[...] = x1
    tok = x1 * (1.0 + mod_ref[4:5, :]) + mod_ref[3:4, :]
    tok_ref[...] = tok
    logits = jnp.dot(tok.astype(_MXU), rw_ref[...], preferred_element_type=F32) + rb_ref[...]
    tm = logits.shape[0]
    lane = lax.broadcasted_iota(jnp.int32, (tm, LANES), 1).astype(F32)
    vals, idxs = [], []
    for _ in range(TOP_K):
        mx = jnp.max(logits, axis=1, keepdims=True)
        ix = jnp.min(jnp.where(logits == mx, lane, float(LANES)), axis=1, keepdims=True)
        vals.append(mx)
        idxs.append(ix)
        logits = jnp.where(lane == ix, -3.0e38, logits)
    es = [jnp.exp(v - vals[0]) for v in vals]
    den = functools.reduce(lambda a, c: a + c, es)
    route = jnp.zeros((tm, LANES), F32)
    for k in range(TOP_K):
        route = jnp.where(lane == float(k), idxs[k], route)
        route = jnp.where(lane == float(TOP_K + k), es[k] / den, route)
    route_ref[...] = route


def _post(x2d, ya, yb, wo, mods, g, b, rw, rb, *, batch, nbt_in, nblk, n_lat_blk, alpha):
    d = x2d.shape[1]
    tm = ROW_TILE
    half = ya.shape[1]

    def mod_map(bi, t):
        return (jnp.where(t >= n_lat_blk, batch, bi), 0, 0)

    rows_out = batch * nblk * tm
    o_map = lambda bi, t: (bi * nblk + t, 0)
    return pl.pallas_call(
        functools.partial(_post_kernel, alpha=alpha),
        grid=(batch, nblk),
        in_specs=[pl.BlockSpec((tm, d), lambda bi, t: (bi * nbt_in + t, 0)),
                  pl.BlockSpec((tm, half), o_map),
                  pl.BlockSpec((tm, half), o_map),
                  pl.BlockSpec((d, d), lambda bi, t: (0, 0)),
                  pl.BlockSpec((None, 6, d), mod_map),
                  pl.BlockSpec((1, d), lambda bi, t: (0, 0)),
                  pl.BlockSpec((1, d), lambda bi, t: (0, 0)),
                  pl.BlockSpec((d, LANES), lambda bi, t: (0, 0)),
                  pl.BlockSpec((1, LANES), lambda bi, t: (0, 0))],
        out_specs=[pl.BlockSpec((tm, d), o_map), pl.BlockSpec((tm, d), o_map), pl.BlockSpec((tm, LANES), o_map)],
        out_shape=[jax.ShapeDtypeStruct((rows_out, d), F32), jax.ShapeDtypeStruct((rows_out, d), F32),
                   jax.ShapeDtypeStruct((rows_out, LANES), F32)],
        compiler_params=_cparams(("parallel", "parallel")),
        name="post_attn",
    )(x2d, ya, yb, wo, mods, g, b, rw, rb)


def _moe_kernel(be_ref, nu_ref, dst_ref, src0_ref, src1_ref, src2_ref, tok_hbm, wi_ref, bi_ref, wo_ref, bo_ref,
                y_hbm, xbuf0, xbuf1, xbuf2, obuf0, obuf1, obuf2, gsem, ssem, wi_s, wo_s, *, dump0):
    i = pl.program_id(0)
    n_used = nu_ref[0]
    f = wo_s.shape[0]
    xbufs = (xbuf0, xbuf1, xbuf2)
    obufs = (obuf0, obuf1, obuf2)

    def gather_copy(row, r, slot):
        return pltpu.make_async_copy(tok_hbm.at[pl.ds(row, 1), :], xbufs[slot].at[pl.ds(r, 1), :], gsem.at[slot])

    def scatter_copy(row, r, slot):
        return pltpu.make_async_copy(obufs[slot].at[pl.ds(r, 1), :], y_hbm.at[pl.ds(row, 1), :], ssem.at[slot])

    def start_gather(rows_ref, slot):
        for r in range(MOE_ROWS):
            gather_copy(rows_ref[0, r], r, slot).start(priority=r % 2)

    def wait_gather(slot):
        for _ in range(MOE_ROWS):
            gather_copy(0, 0, slot).wait()

    def wait_scatter(slot):
        for _ in range(MOE_ROWS):
            scatter_copy(0, 0, slot).wait()

    @pl.when(i == 0)
    def _():
        for slot in range(MOE_RING):
            obufs[slot][...] = jnp.zeros(obufs[slot].shape, F32)
            for r in range(MOE_ROWS):
                scatter_copy(dump0 + slot * MOE_ROWS + r, r, slot).start(priority=r % 2)
        start_gather(src0_ref, 0)
        start_gather(src1_ref, 1)

    last_used = n_used - 1
    changed = jnp.logical_or(i == 0, be_ref[jnp.clip(i - 1, 0, last_used)] != be_ref[jnp.minimum(i, last_used)])

    for slot in range(MOE_RING):
        @pl.when(jnp.logical_and(i < n_used, i % MOE_RING == slot))
        def _(slot=slot):
            wait_scatter(slot)
            wait_gather(slot)

            @pl.when(changed)
            def _():
                wi_s[...] = wi_ref[...].astype(wi_s.dtype)
                wo_s[...] = wo_ref[...].astype(wo_s.dtype)

            x = xbufs[slot][...].astype(wi_s.dtype)
            start_gather(src2_ref, (slot + 2) % MOE_RING)
            hh = jnp.dot(x, wi_s[...], preferred_element_type=F32) + bi_ref[...]
            gate = jnp.minimum(hh[:, :f], SWIGLU_LIMIT)
            up = jnp.clip(hh[:, f:], -SWIGLU_LIMIT, SWIGLU_LIMIT)
            act = gate * (1.0 / (1.0 + jnp.exp(-SWIGLU_ALPHA * gate))) * (up + 1.0)
            obufs[slot][...] = jnp.dot(act.astype(wo_s.dtype), wo_s[...], preferred_element_type=F32) + bo_ref[...]
            for r in range(MOE_ROWS):
                scatter_copy(dst_ref[0, r], r, slot).start(priority=r % 2)

        @pl.when(jnp.logical_and(i >= n_used, i % MOE_RING == slot))
        def _(slot=slot):
            @pl.when(i < n_used + 2)
            def _():
                wait_gather(slot)

            @pl.when(i < n_used + MOE_RING)
            def _():
                wait_scatter(slot)


def _experts(tok, src_rows, dst_rows, block_e, n_used, w_in, b_in, w_out, b_out, layer):
    n_tok, d = tok.shape
    _, n_e, _, f2 = w_in.shape
    f = f2 // 2
    n_blocks = src_rows.shape[0]

    def blk(i, be, nu):
        return jnp.minimum(i, nu[0] - 1)

    def e_map(i, be, nu):
        return (layer, be[blk(i, be, nu)], 0, 0)

    def rows_spec(ahead):
        return pl.BlockSpec((None, 1, MOE_ROWS), lambda i, be, nu: (blk(i + ahead, be, nu), 0, 0),
                            memory_space=pltpu.SMEM)

    row_buf = pltpu.VMEM((MOE_ROWS, d), F32)
    grid_spec = pltpu.PrefetchScalarGridSpec(
        num_scalar_prefetch=2,
        grid=(n_blocks + MOE_RING,),
        in_specs=[rows_spec(0), rows_spec(0), rows_spec(1), rows_spec(2),
                  pl.BlockSpec(memory_space=pl.ANY),
                  pl.BlockSpec((None, None, d, f2), e_map),
                  pl.BlockSpec((None, None, 1, f2), e_map),
                  pl.BlockSpec((None, None, f, d), e_map),
                  pl.BlockSpec((None, None, 1, d), e_map)],
        out_specs=pl.BlockSpec(memory_space=pl.ANY),
        scratch_shapes=[row_buf] * (2 * MOE_RING)
                       + [pltpu.SemaphoreType.DMA((MOE_RING,)), pltpu.SemaphoreType.DMA((MOE_RING,)),
                          pltpu.VMEM((d, f2), _MXU), pltpu.VMEM((f, d), _MXU)],
    )
    depth = w_in.shape[0]
    return pl.pallas_call(
        functools.partial(_moe_kernel, dump0=TOP_K * n_tok), grid_spec=grid_spec,
        out_shape=jax.ShapeDtypeStruct((TOP_K * n_tok + MOE_RING * MOE_ROWS, d), F32),
        compiler_params=_cparams(("arbitrary",)),
        name="experts",
    )(block_e, n_used, dst_rows, src_rows, src_rows, src_rows, tok, w_in, b_in.reshape(depth, n_e, 1, f2), w_out,
      b_out.reshape(depth, n_e, 1, d))


def _route_plan(route, n_experts):
    n = route.shape[0]
    top_idx = route[:, :TOP_K].astype(jnp.int32)
    nk = n * TOP_K
    flat_e = top_idx.reshape(-1)
    order = jnp.argsort(flat_e).astype(jnp.int32)
    experts = jnp.arange(n_experts, dtype=jnp.int32)
    counts = jnp.sum(flat_e[:, None] == experts[None, :], axis=0, dtype=jnp.int32)
    padded = (counts + MOE_ROWS - 1) // MOE_ROWS * MOE_ROWS
    start = jnp.cumsum(counts) - counts
    pend = jnp.cumsum(padded)
    pstart = pend - padded
    n_blocks = -(-nk // MOE_ROWS) + n_experts
    first = jnp.arange(n_blocks, dtype=jnp.int32) * MOE_ROWS
    block_e = jnp.minimum(jnp.sum(pend[None, :] <= first[:, None], axis=1, dtype=jnp.int32), n_experts - 1)
    n_used = (pend[-1:] // MOE_ROWS).astype(jnp.int32)
    blk = jnp.arange(n_blocks, dtype=jnp.int32)[:, None]
    row = jnp.arange(MOE_ROWS, dtype=jnp.int32)[None, :]
    onehot = (block_e[:, None] == experts[None, :]).astype(jnp.int32)
    b_pstart, b_start, b_count = jnp.sum(onehot[None] * jnp.stack([pstart, start, counts])[:, None, :], axis=-1)
    rank = blk * MOE_ROWS + row - b_pstart[:, None]
    src = jnp.clip(b_start[:, None] + rank, 0, nk - 1)
    valid = rank < b_count[:, None]
    pair = jnp.take(order, src, mode="clip")
    token = pair // TOP_K
    src_rows = jnp.where(valid, token, 0)
    dst_rows = jnp.where(valid, (pair % TOP_K) * n + token, nk + (blk % MOE_RING) * MOE_ROWS + row)
    return (src_rows.reshape(n_blocks, 1, MOE_ROWS), dst_rows.reshape(n_blocks, 1, MOE_ROWS), block_e, n_used)


def _ln2_kernel(x_ref, y0_ref, y1_ref, y2_ref, y3_ref, route_ref, mod_ref, g_ref, b_ref, o_ref, *, alpha):
    ys = (y0_ref, y1_ref, y2_ref, y3_ref)
    f = route_ref[:, TOP_K:TOP_K + 1] * ys[0][...]
    for k in range(1, TOP_K):
        f = f + route_ref[:, TOP_K + k:TOP_K + k + 1] * ys[k][...]
    o_ref[...] = _layernorm(alpha * x_ref[...] + mod_ref[5:6, :] * f, g_ref[...], b_ref[...])


def _ln2(x1, y, route, mods, g, b, *, batch, nblk, n_lat_blk, alpha):
    rows, d = x1.shape
    tm = ROW_TILE
    r_map = lambda bi, t: (bi * nblk + t, 0)

    def y_spec(k):
        return pl.BlockSpec((tm, d), lambda bi, t: (k * (rows // tm) + bi * nblk + t, 0))

    def mod_map(bi, t):
        return (jnp.where(t >= n_lat_blk, batch, bi), 0, 0)

    return pl.pallas_call(
        functools.partial(_ln2_kernel, alpha=alpha),
        grid=(batch, nblk),
        in_specs=[pl.BlockSpec((tm, d), r_map)] + [y_spec(k) for k in range(TOP_K)]
                 + [pl.BlockSpec((tm, LANES), r_map), pl.BlockSpec((None, 6, d), mod_map),
                    pl.BlockSpec((1, d), lambda bi, t: (0, 0)), pl.BlockSpec((1, d), lambda bi, t: (0, 0))],
        out_specs=pl.BlockSpec((tm, d), r_map),
        out_shape=jax.ShapeDtypeStruct((rows, d), F32),
        compiler_params=_cparams(("parallel", "parallel")),
        name="combine_ln2",
    )(x1, y, y, y, y, route, mods, g, b)


def _pair_layout(w, n_heads, axis=-1):
    axis = axis % w.ndim
    shape = w.shape
    w = w.reshape(shape[:axis] + (2, n_heads // 2, HEAD_DIM) + shape[axis + 1:])
    return w.swapaxes(axis, axis + 1).reshape(shape)


def _rope_tables(s_len, n_ctx):
    t = np.arange(s_len)
    row = (t // GRID_W).astype(np.float32)
    col = (t % GRID_W).astype(np.float32)
    axis_dim = HEAD_DIM // 2
    freqs = jnp.asarray(ROPE_THETA, F32) ** (-jnp.arange(0, axis_dim, 2, dtype=F32) / axis_dim)
    ang = jnp.concatenate([jnp.asarray(row)[:, None] * freqs, jnp.asarray(col)[:, None] * freqs], axis=-1)
    cos, sin = jnp.cos(ang), jnp.sin(ang)
    cos_h = jnp.repeat(cos, 2, axis=-1)
    sin_h = jnp.stack([-sin, sin], axis=-1).reshape(s_len, HEAD_DIM)
    cos_t = jnp.concatenate([cos_h, jnp.ones((n_ctx, HEAD_DIM), F32)], axis=0)
    sin_t = jnp.concatenate([sin_h, jnp.zeros((n_ctx, HEAD_DIM), F32)], axis=0)
    return jnp.tile(cos_t, (1, 2)), jnp.tile(sin_t, (1, 2))


def _moe_and_ln2(x1, tok, route, mods, n_experts, w_in, b_in, w_out, b_out, layer, g, b, *,
                 batch, nblk, n_lat_blk, alpha):
    src_rows, dst_rows, block_e, n_used = _route_plan(route, n_experts)
    y = _experts(tok, src_rows, dst_rows, block_e, n_used, w_in, b_in, w_out, b_out, layer)
    return _ln2(x1, y, route, mods, g, b, batch=batch, nblk=nblk, n_lat_blk=n_lat_blk, alpha=alpha)


def kernel(x, c, ctx, c_ctx, mod_w, mod_b, ln1_g, ln1_b, ln2_g, ln2_b, router_w, router_b, moe_w_in, moe_b_in,
           moe_w_out, moe_b_out, ab_w_in, ab_w_out, na_rpb, diff_lq1, diff_lk1, diff_lq2, diff_lk2, diff_subln,
           cd_w_in, cd_w_out, gqa_q_norm, gqa_k_norm, swa_sink):
    batch, s_len, d = x.shape
    n_ctx = ctx.shape[1]
    t_len = n_ctx + s_len
    depth = mod_w.shape[0]
    n_experts = router_w.shape[2]
    alpha = (2.0 * depth) ** 0.25
    tm = ROW_TILE
    nbt = t_len // tm
    nbs = s_len // tm
    assert depth == 2 and n_ctx % tm == 0 and s_len % tm == 0 and batch + 1 <= 8

    cc = jnp.zeros((8, d), F32).at[:batch].set(c).at[batch].set(c_ctx)
    mod_all = _modulation(cc, mod_w, mod_b)
    mods = [mod_all[l, :batch + 1].reshape(batch + 1, 6, d) for l in range(depth)]
    cos_t, sin_t = _rope_tables(s_len, n_ctx)
    rw = [jnp.zeros((d, LANES), F32).at[:, :n_experts].set(router_w[l]).astype(_MXU) for l in range(depth)]
    rb = [jnp.full((1, LANES), NEG_INF, F32).at[0, :n_experts].set(router_b[l]) for l in range(depth)]

    stream = jnp.concatenate([x, ctx], axis=1).reshape(batch * t_len, d)
    by_batch = lambda a: a.reshape(batch, t_len, a.shape[-1])

    na_w = NA_HEADS * HEAD_DIM
    df_w = DIFF_HEADS * 2 * HEAD_DIM
    w0 = ab_w_in[0].astype(_MXU)
    rows_only = ("rows",)
    plan0 = ((0, na_w, None, False, QSCALE, rows_only), (na_w, na_w, None, False, 1.0, rows_only),
             (2 * na_w, na_w, None, False, 1.0, ("rows", "value_t")),
             (3 * na_w, df_w, None, True, QSCALE, rows_only), (3 * na_w + df_w, df_w, None, True, 1.0, rows_only),
             (3 * na_w + 2 * df_w, df_w, None, False, 1.0, rows_only))
    nw0 = jnp.zeros((8, LANES), F32)
    nq, nk, nv, nvx_lo, nvx_hi, dq, dk, dv = _in_proj(stream, mods[0], w0, cos_t, sin_t, nw0, plan0,
                                                      batch=batch, n_ctx=n_ctx)
    nq, nk, nv, dq, dk, dv = map(by_batch, (nq, nk, nv, dq, dk, dv))
    lam_init0 = 0.8 - 0.6 * math.exp(-0.3 * 0)
    par = jnp.zeros((8, LANES), F32)
    par = par.at[0, :HEAD_DIM].set(diff_lq1[0]).at[1, :HEAD_DIM].set(diff_lk1[0])
    par = par.at[2, :HEAD_DIM].set(diff_lq2[0]).at[3, :HEAD_DIM].set(diff_lk2[0]).at[4].set(diff_subln[0])
    bias = _na_bias_tables(na_rpb[0], s_len // GRID_W)
    y_na = _neighbourhood(nq, nk, nvx_lo, nvx_hi, bias, s_len=s_len, n_ctx=n_ctx)
    y_na_c = _flash(nq, nk, nv, n_groups=NA_HEADS // 2, q_rows=n_ctx, q_off=s_len, kv_rows=n_ctx,
                    kv_off=s_len, k_col=0, v_col=0, mode="pair", name="neighbourhood_attn_ctx")
    y_df = _flash(dq, dk, dv, n_groups=DIFF_HEADS, q_rows=s_len, q_off=0, kv_rows=t_len, kv_off=0, k_col=0, v_col=0,
                  mode="diff", par=par, lam_init=lam_init0, name="diff_attn")
    y_df_c = _flash(dq, dk, dv, n_groups=DIFF_HEADS, q_rows=n_ctx, q_off=s_len, kv_rows=n_ctx, kv_off=s_len,
                    k_col=0, v_col=0,
                    mode="diff", par=par, lam_init=lam_init0, name="diff_attn_ctx")
    ya = jnp.concatenate([y_na, y_na_c], axis=1).reshape(batch * t_len, -1)
    yb = jnp.concatenate([y_df, y_df_c], axis=1).reshape(batch * t_len, -1)
    x1, tok, route = _post(stream, ya, yb, ab_w_out[0].astype(_MXU), mods[0], ln1_g[0][None], ln1_b[0][None],
                           rw[0], rb[0], batch=batch, nbt_in=nbt, nblk=nbt, n_lat_blk=nbs, alpha=alpha)
    stream = _moe_and_ln2(x1, tok, route, mods[0], n_experts, moe_w_in, moe_b_in, moe_w_out, moe_b_out, 0,
                          ln2_g[0][None], ln2_b[0][None], batch=batch, nblk=nbt, n_lat_blk=nbs, alpha=alpha)

    hw = GQA_HEADS * HEAD_DIM
    kw = GQA_KV_HEADS * HEAD_DIM
    wcd = cd_w_in[0]
    gq, gk, gv, wq, wk, wv = jnp.split(wcd, [hw, hw + kw, hw + 2 * kw, 2 * hw + 2 * kw, 2 * hw + 3 * kw], axis=1)
    w1 = jnp.concatenate([_pair_layout(gq, GQA_HEADS), _pair_layout(wq, SWA_HEADS), gk, gv, wk, wv],
                         axis=1).astype(_MXU)
    plan1 = ((0, hw, 0, True, QSCALE, rows_only), (hw, hw, None, True, QSCALE, rows_only),
             (2 * hw, kw, 1, True, 1.0, rows_only), (2 * hw + kw, kw, None, False, 1.0, rows_only),
             (2 * hw + 2 * kw, kw, None, True, 1.0, rows_only),
             (2 * hw + 3 * kw, kw, None, False, 1.0, ("value_t",)))
    nw1 = jnp.zeros((8, LANES), F32).at[0].set(jnp.tile(gqa_q_norm[0], 2)).at[1].set(jnp.tile(gqa_k_norm[0], 2))
    gq, wq, gk, gv, wk, wvx_lo, wvx_hi = _in_proj(stream, mods[1], w1, cos_t, sin_t, nw1, plan1,
                                                  batch=batch, n_ctx=n_ctx)
    gq, wq, gk, gv, wk = map(by_batch, (gq, wq, gk, gv, wk))
    y_c = _flash(gq, gk, gv, n_groups=1, n_qp=GQA_HEADS // 2, q_rows=s_len, q_off=0, kv_rows=t_len, kv_off=0,
                 k_col=0, v_col=0, mode="pair", name="gqa_attn")
    sink = jnp.repeat(swa_sink[0].reshape(2, SWA_HEADS // 2) * LOG2E, LANES, axis=1).astype(F32)
    y_d = _windowed(wq, wk, wvx_lo, wvx_hi, sink, s_len=s_len, n_ctx=n_ctx)
    wo1 = jnp.concatenate([_pair_layout(cd_w_out[0][:hw], GQA_HEADS, axis=0),
                           _pair_layout(cd_w_out[0][hw:], SWA_HEADS, axis=0)], axis=0).astype(_MXU)
    x1, tok, route = _post(stream, y_c.reshape(batch * s_len, -1), y_d.reshape(batch * s_len, -1), wo1, mods[1],
                           ln1_g[1][None], ln1_b[1][None], rw[1], rb[1], batch=batch, nbt_in=nbt,
                           nblk=nbs, n_lat_blk=nbs, alpha=alpha)
    out = _moe_and_ln2(x1, tok, route, mods[1], n_experts, moe_w_in, moe_b_in, moe_w_out, moe_b_out, 1,
                       ln2_g[1][None], ln2_b[1][None], batch=batch, nblk=nbs, n_lat_blk=nbs, alpha=alpha)
    return out.reshape(batch, s_len, d)
```

```python
import functools
import math

import numpy as np
import jax
import jax.numpy as jnp
from jax import lax
from jax.experimental import pallas as pl
from jax.experimental.pallas import tpu as pltpu

F32 = jnp.float32
_MXU = jnp.bfloat16

HEAD_DIM = 64
GRID_W = 64
LOG2E = math.log2(math.e)
QSCALE = HEAD_DIM ** -0.5 * LOG2E
ROPE_THETA = 10000.0
NA_HEADS = 8
NA_KH = 8
NA_KW = 16
DIFF_HEADS = 4
GQA_HEADS = 8
GQA_KV_HEADS = 2
SWA_HEADS = 8
SWA_KV_HEADS = 2
SWA_WINDOW = 128
TOP_K = 4
SWIGLU_LIMIT = 7.0
SWIGLU_ALPHA = 1.702
LN_EPS = 1e-5
RMS_EPS = 1e-6
NEG_INF = -1e30

LANES = 128
ROW_TILE = 256
MOE_ROWS = 256
MOE_RING = 3
NA_Q_ROWS = 4
VMEM_LIMIT = 52 * 1024 * 1024


def _cparams(sem, vmem=VMEM_LIMIT):
    return pltpu.CompilerParams(dimension_semantics=sem, vmem_limit_bytes=vmem)


def _mod_kernel(c_ref, w_ref, b_ref, o_ref):
    c = c_ref[...]
    a = (c / (1.0 + jnp.exp(-c))).astype(_MXU)
    o_ref[...] = jnp.dot(a, w_ref[...].astype(_MXU), preferred_element_type=F32) + b_ref[...]


def _modulation(cc, mod_w, mod_b):
    depth, d, d6 = mod_w.shape
    tn = d6 // 4
    return pl.pallas_call(
        _mod_kernel,
        grid=(depth, d6 // tn),
        in_specs=[pl.BlockSpec((8, d), lambda l, j: (0, 0)),
                  pl.BlockSpec((None, d, tn), lambda l, j: (l, 0, j)),
                  pl.BlockSpec((None, 1, tn), lambda l, j: (l, 0, j))],
        out_specs=pl.BlockSpec((None, 8, tn), lambda l, j: (l, 0, j)),
        out_shape=jax.ShapeDtypeStruct((depth, 8, d6), F32),
        compiler_params=_cparams(("arbitrary", "arbitrary")),
        name="modulation",
    )(cc, mod_w, mod_b.reshape(depth, 1, d6))


def _in_kernel(x_ref, mod_ref, w_ref, cos_ref, sin_ref, nw_ref, *o_refs, plan):
    x = x_ref[...]
    h = (x * (1.0 + mod_ref[1:2, :]) + mod_ref[0:1, :]).astype(_MXU)
    tm = x.shape[0]
    lane = lax.broadcasted_iota(jnp.int32, (tm, LANES), 1)
    even_lane = (lane & 1) == 0
    gi = lax.broadcasted_iota(jnp.int32, (LANES, LANES), 0) // HEAD_DIM
    gj = lax.broadcasted_iota(jnp.int32, (LANES, LANES), 1) // HEAD_DIM
    seg = jnp.where(gi == gj, 1.0, 0.0).astype(_MXU)
    cos = cos_ref[...]
    sin = sin_ref[...]
    lo_col = _head_col()
    outs = iter(o_refs)
    for (c0, width, norm_row, rope, scale, kinds) in plan:
        acc = jnp.dot(h, w_ref[:, c0:c0 + width], preferred_element_type=F32)
        o_rows = next(outs) if "rows" in kinds else None
        o_lo, o_hi = (next(outs), next(outs)) if "value_t" in kinds else (None, None)
        for j in range(width // LANES):
            a = acc[:, j * LANES:(j + 1) * LANES]
            if norm_row is not None:
                a2 = a * a
                hi = a2.astype(_MXU)
                lo = (a2 - hi.astype(F32)).astype(_MXU)
                ss = (jnp.dot(hi, seg, preferred_element_type=F32)
                      + jnp.dot(lo, seg, preferred_element_type=F32))
                a = a * lax.rsqrt(ss * (1.0 / HEAD_DIM) + RMS_EPS) * nw_ref[norm_row:norm_row + 1, :]
            if rope:
                partner = jnp.where(even_lane, pltpu.roll(a, LANES - 1, 1), pltpu.roll(a, 1, 1))
                a = a * cos + partner * sin
            if scale != 1.0:
                a = a * scale
            if o_rows is not None:
                o_rows[:, j * LANES:(j + 1) * LANES] = a.astype(o_rows.dtype)
            if o_lo is not None:
                at = a.T
                o_lo[j * LANES:(j + 1) * LANES, :] = (at * lo_col + (1.0 - lo_col)).astype(o_lo.dtype)
                o_hi[j * LANES:(j + 1) * LANES, :] = (at * (1.0 - lo_col) + lo_col).astype(o_hi.dtype)


def _in_proj(x2d, mods, w, cos_t, sin_t, nw, plan, *, batch, n_ctx):
    rows, d = x2d.shape
    ncols = w.shape[1]
    tm = ROW_TILE
    nbt = rows // batch // tm
    n_lat_blk = nbt - n_ctx // tm

    def mod_map(i):
        return (jnp.where(i % nbt >= n_lat_blk, batch, i // nbt), 0, 0)

    out_specs, out_shape = [], []
    for (_, width, _, _, _, kinds) in plan:
        if "rows" in kinds:
            out_specs.append(pl.BlockSpec((tm, width), lambda i: (i, 0)))
            out_shape.append(jax.ShapeDtypeStruct((rows, width), _MXU))
        if "value_t" in kinds:
            out_specs += [pl.BlockSpec((width, tm), lambda i: (0, i))] * 2
            out_shape += [jax.ShapeDtypeStruct((width, rows), _MXU)] * 2

    return pl.pallas_call(
        functools.partial(_in_kernel, plan=plan),
        grid=(rows // tm,),
        in_specs=[pl.BlockSpec((tm, d), lambda i: (i, 0)),
                  pl.BlockSpec((None, 6, d), mod_map),
                  pl.BlockSpec((d, ncols), lambda i: (0, 0)),
                  pl.BlockSpec((tm, LANES), lambda i: (i % nbt, 0)),
                  pl.BlockSpec((tm, LANES), lambda i: (i % nbt, 0)),
                  pl.BlockSpec((8, LANES), lambda i: (0, 0))],
        out_specs=out_specs,
        out_shape=out_shape,
        compiler_params=_cparams(("parallel",)),
        name="in_proj",
    )(x2d, mods, w, cos_t, sin_t, nw)


def _flash_kernel(*refs, mode, tk, n_chunks, lam_init):
    if mode == "diff":
        q_ref, k_ref, v_ref, par_ref, o_ref, qt_scr, vxt_scr, m_scr, acc_scr, st_scr = refs
    else:
        q_ref, k_ref, v_ref, o_ref, qt_scr, vxt_scr, m_scr, acc_scr, st_scr = refs
    lo_col = _head_col()

    @pl.when(pl.program_id(2) == 0)
    def _():
        for c in range(n_chunks):
            vt = v_ref[c * tk:(c + 1) * tk, :].astype(F32).T
            if mode == "diff":
                vxt_scr[c, :LANES, :] = vt.astype(vxt_scr.dtype)
                vxt_scr[c, LANES:, :] = jnp.ones((LANES, tk), vxt_scr.dtype)
            else:
                vxt_scr[0, c] = (vt * lo_col + (1.0 - lo_col)).astype(vxt_scr.dtype)
                vxt_scr[1, c] = (vt * (1.0 - lo_col) + lo_col).astype(vxt_scr.dtype)

    n_qp = q_ref.shape[1] // LANES
    qt = jnp.concatenate([q_ref[:, p * LANES:(p + 1) * LANES].astype(F32).T for p in range(n_qp)], axis=1)
    qt_scr[0] = (qt * lo_col).astype(qt_scr.dtype)
    qt_scr[1] = (qt * (1.0 - lo_col)).astype(qt_scr.dtype)
    m_scr[...] = jnp.full(m_scr.shape, NEG_INF, F32)
    acc_scr[...] = jnp.zeros(acc_scr.shape, F32)

    def qk(c, slot):
        off = pl.multiple_of(c * tk, tk)
        k = k_ref[pl.ds(off, tk), :]
        for h in range(2):
            st_scr[slot, h] = jnp.dot(k, qt_scr[h], preferred_element_type=F32)

    def softmax_pv(c, slot):
        for h in range(2):
            st = st_scr[slot, h]
            m_prev = m_scr[h]
            m_new = jnp.maximum(m_prev, jnp.max(st, axis=0, keepdims=True))
            alpha = jnp.exp2(m_prev - m_new)
            pt = jnp.exp2(st - m_new).astype(vxt_scr.dtype)
            vxt = vxt_scr[c] if mode == "diff" else vxt_scr[h, c]
            acc_scr[h] = alpha * acc_scr[h] + jnp.dot(vxt, pt, preferred_element_type=F32)
            m_scr[h] = m_new

    qk(0, 0)

    def body(j, carry):
        c = 2 * j
        qk(c + 1, 1)
        softmax_pv(c, 0)
        qk(c + 2, 0)
        softmax_pv(c + 1, 1)
        return carry

    lax.fori_loop(0, (n_chunks - 1) // 2, body, 0)
    if n_chunks % 2 == 1:
        softmax_pv(n_chunks - 1, 0)
    else:
        qk(n_chunks - 1, 1)
        softmax_pv(n_chunks - 2, 0)
        softmax_pv(n_chunks - 1, 1)

    a_lo = acc_scr[0]
    a_hi = acc_scr[1]
    if mode == "diff":
        lam = (jnp.exp(jnp.sum(par_ref[0:1, :] * par_ref[1:2, :], axis=1, keepdims=True))
               - jnp.exp(jnp.sum(par_ref[2:3, :] * par_ref[3:4, :], axis=1, keepdims=True))
               + lam_init)
        out_t = a_lo[:LANES] / a_lo[LANES:] - lam * (a_hi[:LANES] / a_hi[LANES:])
        ms = jnp.mean(out_t * out_t, axis=0, keepdims=True)
        o_ref[...] = ((out_t * lax.rsqrt(ms + RMS_EPS)).T * par_ref[4:5, :] * (1.0 - lam_init)).astype(o_ref.dtype)
    else:
        out_t = _pair_out_t(a_lo, a_hi)
        rows = o_ref.shape[0]
        for p in range(n_qp):
            o_ref[:, p * LANES:(p + 1) * LANES] = out_t[:, p * rows:(p + 1) * rows].T.astype(o_ref.dtype)


def _pick_tile(n, candidates):
    for c in candidates:
        if n % c == 0:
            return c
    raise ValueError(f"no tile for {n}")


def _flash(q_arr, k_arr, v_arr, *, n_groups, q_rows, q_off, kv_rows, kv_off, k_col, v_col, mode, n_qp=1, par=None,
           lam_init=0.0, name):
    batch = q_arr.shape[0]
    tr = _pick_tile(q_rows, (1024 // n_qp, 512 // n_qp, 256 // n_qp))
    tq = tr * n_qp
    qw = n_qp * LANES
    tk = _pick_tile(kv_rows, (768, 512, 384, 256, 128))
    nq, n_chunks = q_rows // tr, kv_rows // tk
    assert q_off % tr == 0 and kv_off % kv_rows == 0
    vw = 2 * LANES if mode == "diff" else LANES
    in_specs = [pl.BlockSpec((None, tr, qw), lambda b, g, i: (b, q_off // tr + i, g)),
                pl.BlockSpec((None, kv_rows, LANES), lambda b, g, i: (b, kv_off // kv_rows, k_col + g)),
                pl.BlockSpec((None, kv_rows, LANES), lambda b, g, i: (b, kv_off // kv_rows, v_col + g))]
    args = [q_arr, k_arr, v_arr]
    if mode == "diff":
        in_specs.append(pl.BlockSpec(par.shape, lambda b, g, i: (0, 0)))
        args.append(par)
    return pl.pallas_call(
        functools.partial(_flash_kernel, mode=mode, tk=tk, n_chunks=n_chunks, lam_init=lam_init),
        grid=(batch, n_groups, nq),
        in_specs=in_specs,
        out_specs=pl.BlockSpec((None, tr, qw), lambda b, g, i: (b, i, g)),
        out_shape=jax.ShapeDtypeStruct((batch, q_rows, n_groups * qw), _MXU),
        scratch_shapes=[pltpu.VMEM((2, LANES, tq), _MXU),
                        pltpu.VMEM((n_chunks, vw, tk) if mode == "diff" else (2, n_chunks, vw, tk), _MXU),
                        pltpu.VMEM((2, 1, tq), F32),
                        pltpu.VMEM((2, vw, tq), F32),
                        pltpu.VMEM((2, 2, tk, tq), F32)],
        compiler_params=_cparams(("parallel", "parallel", "arbitrary")),
        name=name,
    )(*args)


def _head_col():
    sub = lax.broadcasted_iota(jnp.int32, (LANES, 1), 0)
    return jnp.where(sub < HEAD_DIM, 1.0, 0.0)


def _pair_out_t(a_lo, a_hi):
    return jnp.concatenate([a_lo[:HEAD_DIM] / a_lo[HEAD_DIM:], a_hi[HEAD_DIM:] / a_hi[:HEAD_DIM]], axis=0)


def _na_kernel(q_ref, kc_ref, k0_ref, k1_ref, k2_ref, *rest, n_pairs):
    vx_refs = (rest[0:4], rest[4:8])
    bias_ref, o_ref, st_scr = rest[8], rest[9], rest[10]
    tq = q_ref.shape[0]
    k_refs = (kc_ref, k0_ref, k1_ref, k2_ref)
    lo_col = _head_col()
    for p in range(n_pairs):
        cols = slice(p * LANES, (p + 1) * LANES)
        qt = q_ref[:, cols].astype(F32).T
        for half in range(2):
            col = lo_col if half == 0 else 1.0 - lo_col
            qth = (qt * col).astype(q_ref.dtype)
            st_scr[p % 2, half, 0] = jnp.dot(kc_ref[:, cols], qth, preferred_element_type=F32)
            for j in range(1, 4):
                st_scr[p % 2, half, j] = (jnp.dot(k_refs[j][:, cols], qth, preferred_element_type=F32)
                                          + bias_ref[p, half, (j - 1) * tq:j * tq, :])
        halves = []
        for half in range(2):
            sts = [st_scr[p % 2, half, j] for j in range(4)]
            m = functools.reduce(jnp.maximum, [jnp.max(st, axis=0, keepdims=True) for st in sts])
            acc = jnp.zeros((LANES, tq), F32)
            for j in range(4):
                pt = jnp.exp2(sts[j] - m).astype(q_ref.dtype)
                acc = acc + jnp.dot(vx_refs[half][j][cols, :], pt, preferred_element_type=F32)
            halves.append(acc)
        o_ref[:, cols] = _pair_out_t(*halves).T.astype(o_ref.dtype)


def _na_bias_tables(rpb, rows):
    nh = rpb.shape[0]
    nkr = 3 * NA_Q_ROWS
    qc = np.arange(GRID_W)[:, None]
    kc = np.arange(GRID_W)[None, :]
    ws = np.clip(qc - NA_KW // 2, 0, GRID_W - NA_KW)
    cvalid = ((kc >= ws) & (kc < ws + NA_KW)).reshape(-1)
    dc = (kc - qc + NA_KW - 1).reshape(-1)
    onehot = ((np.arange(2 * NA_KW - 1)[:, None] == dc[None, :]) & cvalid[None, :]).astype(np.float32)
    tiles = jnp.einsum("hrd,dx->hrx", rpb.astype(F32) * LOG2E, jnp.asarray(onehot),
                       precision=lax.Precision.HIGHEST)
    tiles = jnp.where(cvalid[None, None, :], tiles, NEG_INF)
    tiles = tiles.reshape(nh, 2 * NA_KH - 1, GRID_W, GRID_W).swapaxes(-1, -2)
    masked_tile = 2 * NA_KH - 1
    tiles = jnp.concatenate([tiles, jnp.full((nh, 1, GRID_W, GRID_W), NEG_INF, F32)], axis=1)
    kr = np.arange(nkr)[:, None]
    qr = np.arange(NA_Q_ROWS)[None, :]
    tile_idx = []
    for variant in range(3):
        if variant == 0:
            r0, k0, nrows = 0, 0, rows
        elif variant == 1:
            r0, k0, nrows = 2 * NA_Q_ROWS, NA_Q_ROWS, 8 * NA_Q_ROWS
        else:
            r0, k0, nrows = rows - NA_Q_ROWS, rows - nkr, rows
        r = r0 + qr
        rp = k0 + kr
        rs = np.clip(r - NA_KH // 2, 0, nrows - NA_KH)
        rvalid = (rp >= rs) & (rp < rs + NA_KH)
        tile_idx.append(np.where(rvalid, rp - r + NA_KH - 1, masked_tile))
    idx = jnp.asarray(np.stack(tile_idx).reshape(-1).astype(np.int32))
    t = jnp.take(tiles, idx, axis=1)
    t = t.reshape(nh, 3, nkr, NA_Q_ROWS, GRID_W, GRID_W).transpose(1, 0, 2, 4, 3, 5)
    return t.reshape(3, nh // 2, 2, nkr * GRID_W, NA_Q_ROWS * GRID_W)


def _neighbourhood(q_arr, k_arr, vx_lo, vx_hi, bias, *, s_len, n_ctx):
    batch, t_len, width = q_arr.shape
    tq = NA_Q_ROWS * GRID_W
    assert n_ctx == tq
    nq = s_len // tq
    nbt = t_len // tq
    n_pairs = NA_HEADS // 2
    cb = s_len // n_ctx

    def seg_blocks(i):
        first = jnp.clip(i - 1, 0, nq - 3)
        return [cb, first, first + 1, first + 2]

    q_spec = pl.BlockSpec((None, tq, width), lambda i, b: (b, i, 0))
    k_specs = [pl.BlockSpec((None, tq, width), lambda i, b, j=j: (b, seg_blocks(i)[j], 0)) for j in range(4)]
    v_specs = [pl.BlockSpec((width, tq), lambda i, b, j=j: (0, b * nbt + seg_blocks(i)[j])) for j in range(4)]
    bias_spec = pl.BlockSpec((None, n_pairs, 2, 3 * tq, tq),
                             lambda i, b: (jnp.where(i == 0, 0, jnp.where(i == nq - 1, 2, 1)), 0, 0, 0, 0))
    return pl.pallas_call(
        functools.partial(_na_kernel, n_pairs=n_pairs),
        grid=(nq, batch),
        in_specs=[q_spec] + k_specs + v_specs + v_specs + [bias_spec],
        out_specs=pl.BlockSpec((None, tq, width), lambda i, b: (b, i, 0)),
        out_shape=jax.ShapeDtypeStruct((batch, s_len, width), _MXU),
        scratch_shapes=[pltpu.VMEM((2, 2, 4, tq, tq), F32)],
        compiler_params=_cparams(("parallel", "parallel")),
        name="neighbourhood_attn",
    )(q_arr, *([k_arr] * 4), *([vx_lo] * 4), *([vx_hi] * 4), bias)


def _windowed_kernel(q_ref, kc_ref, k0_ref, k1_ref, k2_ref, *rest, n_pairs):
    vx_refs = (rest[0:4], rest[4:8])
    sink_ref, o_ref, st_scr = rest[8], rest[9], rest[10]
    tq = q_ref.shape[0]
    qi = pl.program_id(1)
    k_refs = (kc_ref, k0_ref, k1_ref, k2_ref)
    lo_col = _head_col()
    qt = jnp.concatenate([q_ref[:, p * LANES:(p + 1) * LANES].astype(F32).T for p in range(n_pairs)], axis=1)

    qq = lax.broadcasted_iota(jnp.int32, (1, n_pairs * LANES), 1) & (LANES - 1)
    kk = lax.broadcasted_iota(jnp.int32, (tq, 1), 0)
    masks = (None, jnp.logical_and(kk >= qq, qi >= 1), None,
             jnp.logical_and(kk <= qq, qi < pl.num_programs(1) - 1))
    n_ctx = kc_ref.shape[0]
    offs = (0, n_ctx, n_ctx + tq, n_ctx + 2 * tq, n_ctx + 3 * tq)
    for half in range(2):
        col = lo_col if half == 0 else 1.0 - lo_col
        qth = (qt * col).astype(q_ref.dtype)
        for j in range(4):
            st = jnp.dot(k_refs[j][...], qth, preferred_element_type=F32)
            st_scr[half, offs[j]:offs[j + 1], :] = st if masks[j] is None else jnp.where(masks[j], st, NEG_INF)
    halves = []
    for half in range(2):
        col = lo_col if half == 0 else 1.0 - lo_col
        sink = sink_ref[half:half + 1, :]
        m = jnp.maximum(jnp.max(st_scr[half], axis=0, keepdims=True), sink)
        acc = jnp.exp2(sink - m) * (1.0 - col)
        for j in range(4):
            pt = jnp.exp2(st_scr[half, offs[j]:offs[j + 1], :] - m).astype(q_ref.dtype)
            acc = acc + jnp.dot(vx_refs[half][j][...], pt, preferred_element_type=F32)
        halves.append(acc)
    out_t = _pair_out_t(*halves)
    for p in range(n_pairs):
        o_ref[:, p * LANES:(p + 1) * LANES] = out_t[:, p * LANES:(p + 1) * LANES].T.astype(o_ref.dtype)


def _windowed(q_arr, k_arr, vx_lo, vx_hi, sink, *, s_len, n_ctx):
    batch, t_len = q_arr.shape[0], q_arr.shape[1]
    tq = SWA_WINDOW
    nb = s_len // tq
    n_pairs = SWA_HEADS // 2
    cb = s_len // n_ctx
    qw = n_pairs * LANES
    q_spec = pl.BlockSpec((None, tq, qw), lambda b, i: (b, i, 0))
    k_specs = [pl.BlockSpec((None, n_ctx, LANES), lambda b, i: (b, cb, 0))]
    v_specs = [pl.BlockSpec((LANES, n_ctx), lambda b, i: (0, b * (t_len // n_ctx) + cb))]
    for j in range(3):
        k_specs.append(pl.BlockSpec((None, tq, LANES),
                                    lambda b, i, j=j: (b, jnp.clip(i - 1 + j, 0, nb - 1), 0)))
        v_specs.append(pl.BlockSpec((LANES, tq),
                                    lambda b, i, j=j: (0, b * (t_len // tq) + jnp.clip(i - 1 + j, 0, nb - 1))))
    return pl.pallas_call(
        functools.partial(_windowed_kernel, n_pairs=n_pairs),
        grid=(batch, nb),
        in_specs=[q_spec] + k_specs + v_specs + v_specs + [pl.BlockSpec(sink.shape, lambda b, i: (0, 0))],
        out_specs=pl.BlockSpec((None, tq, qw), lambda b, i: (b, i, 0)),
        out_shape=jax.ShapeDtypeStruct((batch, s_len, qw), _MXU),
        scratch_shapes=[pltpu.VMEM((2, n_ctx + 3 * tq, qw), F32)],
        compiler_params=_cparams(("parallel", "parallel")),
        name="windowed_attn",
    )(q_arr, *([k_arr] * 4), *([vx_lo] * 4), *([vx_hi] * 4), sink)


def _layernorm(z, g, b):
    mu = jnp.mean(z, axis=1, keepdims=True)
    zc = z - mu
    var = jnp.mean(zc * zc, axis=1, keepdims=True)
    return zc * lax.rsqrt(var + LN_EPS) * g + b


def _post_kernel(x_ref, ya_ref, yb_ref, wo_ref, mod_ref, g_ref, b_ref, rw_ref, rb_ref,
                 x1_ref, tok_ref, route_ref, *, alpha):
    half = ya_ref.shape[1]
    y = (jnp.dot(ya_ref[...], wo_ref[:half, :], preferred_element_type=F32)
         + jnp.dot(yb_ref[...], wo_ref[half:, :], preferred_element_type=F32))
    x1 = _layernorm(alpha * x_ref[...] + mod_ref[2:3, :] * y, g_ref[...], b_ref[...])
    x1_ref[...] = x1
    tok = x1 * (1.0 + mod_ref[4:5, :]) + mod_ref[3:4, :]
    tok_ref[...] = tok
    logits = jnp.dot(tok.astype(_MXU), rw_ref[...], preferred_element_type=F32) + rb_ref[...]
    tm = logits.shape[0]
    lane = lax.broadcasted_iota(jnp.int32, (tm, LANES), 1).astype(F32)
    vals, idxs = [], []
    for _ in range(TOP_K):
        mx = jnp.max(logits, axis=1, keepdims=True)
        ix = jnp.min(jnp.where(logits == mx, lane, float(LANES)), axis=1, keepdims=True)
        vals.append(mx)
        idxs.append(ix)
        logits = jnp.where(lane == ix, -3.0e38, logits)
    es = [jnp.exp(v - vals[0]) for v in vals]
    den = functools.reduce(lambda a, c: a + c, es)
    route = jnp.zeros((tm, LANES), F32)
    for k in range(TOP_K):
        route = jnp.where(lane == float(k), idxs[k], route)
        route = jnp.where(lane == float(TOP_K + k), es[k] / den, route)
    route_ref[...] = route


def _post(x2d, ya, yb, wo, mods, g, b, rw, rb, *, batch, nbt_in, nblk, n_lat_blk, alpha):
    d = x2d.shape[1]
    tm = ROW_TILE
    half = ya.shape[1]

    def mod_map(bi, t):
        return (jnp.where(t >= n_lat_blk, batch, bi), 0, 0)

    rows_out = batch * nblk * tm
    o_map = lambda bi, t: (bi * nblk + t, 0)
    return pl.pallas_call(
        functools.partial(_post_kernel, alpha=alpha),
        grid=(batch, nblk),
        in_specs=[pl.BlockSpec((tm, d), lambda bi, t: (bi * nbt_in + t, 0)),
                  pl.BlockSpec((tm, half), o_map),
                  pl.BlockSpec((tm, half), o_map),
                  pl.BlockSpec((d, d), lambda bi, t: (0, 0)),
                  pl.BlockSpec((None, 6, d), mod_map),
                  pl.BlockSpec((1, d), lambda bi, t: (0, 0)),
                  pl.BlockSpec((1, d), lambda bi, t: (0, 0)),
                  pl.BlockSpec((d, LANES), lambda bi, t: (0, 0)),
                  pl.BlockSpec((1, LANES), lambda bi, t: (0, 0))],
        out_specs=[pl.BlockSpec((tm, d), o_map), pl.BlockSpec((tm, d), o_map), pl.BlockSpec((tm, LANES), o_map)],
        out_shape=[jax.ShapeDtypeStruct((rows_out, d), F32), jax.ShapeDtypeStruct((rows_out, d), F32),
                   jax.ShapeDtypeStruct((rows_out, LANES), F32)],
        compiler_params=_cparams(("parallel", "parallel")),
        name="post_attn",
    )(x2d, ya, yb, wo, mods, g, b, rw, rb)


def _moe_kernel(be_ref, nu_ref, dst_ref, src0_ref, src1_ref, src2_ref, tok_hbm, wi_ref, bi_ref, wo_ref, bo_ref,
                y_hbm, xbuf0, xbuf1, xbuf2, obuf0, obuf1, obuf2, gsem, ssem, wi_s, wo_s, *, dump0):
    i = pl.program_id(0)
    n_used = nu_ref[0]
    f = wo_s.shape[0]
    xbufs = (xbuf0, xbuf1, xbuf2)
    obufs = (obuf0, obuf1, obuf2)

    def gather_copy(row, r, slot):
        return pltpu.make_async_copy(tok_hbm.at[pl.ds(row, 1), :], xbufs[slot].at[pl.ds(r, 1), :], gsem.at[slot])

    def scatter_copy(row, r, slot):
        return pltpu.make_async_copy(obufs[slot].at[pl.ds(r, 1), :], y_hbm.at[pl.ds(row, 1), :], ssem.at[slot])

    def start_gather(rows_ref, slot):
        for r in range(MOE_ROWS):
            gather_copy(rows_ref[0, r], r, slot).start(priority=r % 2)

    def wait_gather(slot):
        for _ in range(MOE_ROWS):
            gather_copy(0, 0, slot).wait()

    def wait_scatter(slot):
        for _ in range(MOE_ROWS):
            scatter_copy(0, 0, slot).wait()

    @pl.when(i == 0)
    def _():
        for slot in range(MOE_RING):
            obufs[slot][...] = jnp.zeros(obufs[slot].shape, F32)
            for r in range(MOE_ROWS):
                scatter_copy(dump0 + slot * MOE_ROWS + r, r, slot).start(priority=r % 2)
        start_gather(src0_ref, 0)
        start_gather(src1_ref, 1)

    last_used = n_used - 1
    changed = jnp.logical_or(i == 0, be_ref[jnp.clip(i - 1, 0, last_used)] != be_ref[jnp.minimum(i, last_used)])

    for slot in range(MOE_RING):
        @pl.when(jnp.logical_and(i < n_used, i % MOE_RING == slot))
        def _(slot=slot):
            start_gather(src2_ref, (slot + 2) % MOE_RING)
            wait_scatter(slot)
            wait_gather(slot)

            @pl.when(changed)
            def _():
                wi_s[...] = wi_ref[...].astype(wi_s.dtype)
                wo_s[...] = wo_ref[...].astype(wo_s.dtype)

            x = xbufs[slot][...].astype(wi_s.dtype)
            hh = jnp.dot(x, wi_s[...], preferred_element_type=F32) + bi_ref[...]
            gate = jnp.minimum(hh[:, :f], SWIGLU_LIMIT)
            up = jnp.clip(hh[:, f:], -SWIGLU_LIMIT, SWIGLU_LIMIT)
            act = gate * (1.0 / (1.0 + jnp.exp(-SWIGLU_ALPHA * gate))) * (up + 1.0)
            obufs[slot][...] = jnp.dot(act.astype(wo_s.dtype), wo_s[...], preferred_element_type=F32) + bo_ref[...]
            for r in range(MOE_ROWS):
                scatter_copy(dst_ref[0, r], r, slot).start(priority=r % 2)

        @pl.when(jnp.logical_and(i >= n_used, i % MOE_RING == slot))
        def _(slot=slot):
            @pl.when(i < n_used + 2)
            def _():
                wait_gather(slot)

            @pl.when(i < n_used + MOE_RING)
            def _():
                wait_scatter(slot)


def _experts(tok, src_rows, dst_rows, block_e, n_used, w_in, b_in, w_out, b_out, layer):
    n_tok, d = tok.shape
    _, n_e, _, f2 = w_in.shape
    f = f2 // 2
    n_blocks = src_rows.shape[0]

    def blk(i, be, nu):
        return jnp.minimum(i, nu[0] - 1)

    def e_map(i, be, nu):
        return (layer, be[blk(i, be, nu)], 0, 0)

    def rows_spec(ahead):
        return pl.BlockSpec((None, 1, MOE_ROWS), lambda i, be, nu: (blk(i + ahead, be, nu), 0, 0),
                            memory_space=pltpu.SMEM)

    row_buf = pltpu.VMEM((MOE_ROWS, d), F32)
    grid_spec = pltpu.PrefetchScalarGridSpec(
        num_scalar_prefetch=2,
        grid=(n_blocks + MOE_RING,),
        in_specs=[rows_spec(0), rows_spec(0), rows_spec(1), rows_spec(2),
                  pl.BlockSpec(memory_space=pl.ANY),
                  pl.BlockSpec((None, None, d, f2), e_map),
                  pl.BlockSpec((None, None, 1, f2), e_map),
                  pl.BlockSpec((None, None, f, d), e_map),
                  pl.BlockSpec((None, None, 1, d), e_map)],
        out_specs=pl.BlockSpec(memory_space=pl.ANY),
        scratch_shapes=[row_buf] * (2 * MOE_RING)
                       + [pltpu.SemaphoreType.DMA((MOE_RING,)), pltpu.SemaphoreType.DMA((MOE_RING,)),
                          pltpu.VMEM((d, f2), _MXU), pltpu.VMEM((f, d), _MXU)],
    )
    depth = w_in.shape[0]
    return pl.pallas_call(
        functools.partial(_moe_kernel, dump0=TOP_K * n_tok), grid_spec=grid_spec,
        out_shape=jax.ShapeDtypeStruct((TOP_K * n_tok + MOE_RING * MOE_ROWS, d), F32),
        compiler_params=_cparams(("arbitrary",)),
        name="experts",
    )(block_e, n_used, dst_rows, src_rows, src_rows, src_rows, tok, w_in, b_in.reshape(depth, n_e, 1, f2), w_out,
      b_out.reshape(depth, n_e, 1, d))


def _route_plan(route, n_experts):
    n = route.shape[0]
    top_idx = route[:, :TOP_K].astype(jnp.int32)
    nk = n * TOP_K
    flat_e = top_idx.reshape(-1)
    order = jnp.argsort(flat_e).astype(jnp.int32)
    experts = jnp.arange(n_experts, dtype=jnp.int32)
    counts = jnp.sum(flat_e[:, None] == experts[None, :], axis=0, dtype=jnp.int32)
    padded = (counts + MOE_ROWS - 1) // MOE_ROWS * MOE_ROWS
    start = jnp.cumsum(counts) - counts
    pend = jnp.cumsum(padded)
    pstart = pend - padded
    n_blocks = -(-nk // MOE_ROWS) + n_experts
    first = jnp.arange(n_blocks, dtype=jnp.int32) * MOE_ROWS
    block_e = jnp.minimum(jnp.sum(pend[None, :] <= first[:, None], axis=1, dtype=jnp.int32), n_experts - 1)
    n_used = (pend[-1:] // MOE_ROWS).astype(jnp.int32)
    blk = jnp.arange(n_blocks, dtype=jnp.int32)[:, None]
    row = jnp.arange(MOE_ROWS, dtype=jnp.int32)[None, :]
    onehot = (block_e[:, None] == experts[None, :]).astype(jnp.int32)
    b_pstart, b_start, b_count = jnp.sum(onehot[None] * jnp.stack([pstart, start, counts])[:, None, :], axis=-1)
    rank = blk * MOE_ROWS + row - b_pstart[:, None]
    src = jnp.clip(b_start[:, None] + rank, 0, nk - 1)
    valid = rank < b_count[:, None]
    pair = jnp.take(order, src, mode="clip")
    token = pair // TOP_K
    src_rows = jnp.where(valid, token, 0)
    dst_rows = jnp.where(valid, (pair % TOP_K) * n + token, nk + (blk % MOE_RING) * MOE_ROWS + row)
    return (src_rows.reshape(n_blocks, 1, MOE_ROWS), dst_rows.reshape(n_blocks, 1, MOE_ROWS), block_e, n_used)


def _ln2_kernel(x_ref, y0_ref, y1_ref, y2_ref, y3_ref, route_ref, mod_ref, g_ref, b_ref, o_ref, *, alpha):
    ys = (y0_ref, y1_ref, y2_ref, y3_ref)
    f = route_ref[:, TOP_K:TOP_K + 1] * ys[0][...]
    for k in range(1, TOP_K):
        f = f + route_ref[:, TOP_K + k:TOP_K + k + 1] * ys[k][...]
    o_ref[...] = _layernorm(alpha * x_ref[...] + mod_ref[5:6, :] * f, g_ref[...], b_ref[...])


def _ln2(x1, y, route, mods, g, b, *, batch, nblk, n_lat_blk, alpha):
    rows, d = x1.shape
    tm = ROW_TILE
    r_map = lambda bi, t: (bi * nblk + t, 0)

    def y_spec(k):
        return pl.BlockSpec((tm, d), lambda bi, t: (k * (rows // tm) + bi * nblk + t, 0))

    def mod_map(bi, t):
        return (jnp.where(t >= n_lat_blk, batch, bi), 0, 0)

    return pl.pallas_call(
        functools.partial(_ln2_kernel, alpha=alpha),
        grid=(batch, nblk),
        in_specs=[pl.BlockSpec((tm, d), r_map)] + [y_spec(k) for k in range(TOP_K)]
                 + [pl.BlockSpec((tm, LANES), r_map), pl.BlockSpec((None, 6, d), mod_map),
                    pl.BlockSpec((1, d), lambda bi, t: (0, 0)), pl.BlockSpec((1, d), lambda bi, t: (0, 0))],
        out_specs=pl.BlockSpec((tm, d), r_map),
        out_shape=jax.ShapeDtypeStruct((rows, d), F32),
        compiler_params=_cparams(("parallel", "parallel")),
        name="combine_ln2",
    )(x1, y, y, y, y, route, mods, g, b)


def _pair_layout(w, n_heads, axis=-1):
    axis = axis % w.ndim
    shape = w.shape
    w = w.reshape(shape[:axis] + (2, n_heads // 2, HEAD_DIM) + shape[axis + 1:])
    return w.swapaxes(axis, axis + 1).reshape(shape)


def _rope_tables(s_len, n_ctx):
    t = np.arange(s_len)
    row = (t // GRID_W).astype(np.float32)
    col = (t % GRID_W).astype(np.float32)
    axis_dim = HEAD_DIM // 2
    freqs = jnp.asarray(ROPE_THETA, F32) ** (-jnp.arange(0, axis_dim, 2, dtype=F32) / axis_dim)
    ang = jnp.concatenate([jnp.asarray(row)[:, None] * freqs, jnp.asarray(col)[:, None] * freqs], axis=-1)
    cos, sin = jnp.cos(ang), jnp.sin(ang)
    cos_h = jnp.repeat(cos, 2, axis=-1)
    sin_h = jnp.stack([-sin, sin], axis=-1).reshape(s_len, HEAD_DIM)
    cos_t = jnp.concatenate([cos_h, jnp.ones((n_ctx, HEAD_DIM), F32)], axis=0)
    sin_t = jnp.concatenate([sin_h, jnp.zeros((n_ctx, HEAD_DIM), F32)], axis=0)
    return jnp.tile(cos_t, (1, 2)), jnp.tile(sin_t, (1, 2))


def _moe_and_ln2(x1, tok, route, mods, n_experts, w_in, b_in, w_out, b_out, layer, g, b, *,
                 batch, nblk, n_lat_blk, alpha):
    src_rows, dst_rows, block_e, n_used = _route_plan(route, n_experts)
    y = _experts(tok, src_rows, dst_rows, block_e, n_used, w_in, b_in, w_out, b_out, layer)
    return _ln2(x1, y, route, mods, g, b, batch=batch, nblk=nblk, n_lat_blk=n_lat_blk, alpha=alpha)


def kernel(x, c, ctx, c_ctx, mod_w, mod_b, ln1_g, ln1_b, ln2_g, ln2_b, router_w, router_b, moe_w_in, moe_b_in,
           moe_w_out, moe_b_out, ab_w_in, ab_w_out, na_rpb, diff_lq1, diff_lk1, diff_lq2, diff_lk2, diff_subln,
           cd_w_in, cd_w_out, gqa_q_norm, gqa_k_norm, swa_sink):
    batch, s_len, d = x.shape
    n_ctx = ctx.shape[1]
    t_len = n_ctx + s_len
    depth = mod_w.shape[0]
    n_experts = router_w.shape[2]
    alpha = (2.0 * depth) ** 0.25
    tm = ROW_TILE
    nbt = t_len // tm
    nbs = s_len // tm
    assert depth == 2 and n_ctx % tm == 0 and s_len % tm == 0 and batch + 1 <= 8

    cc = jnp.zeros((8, d), F32).at[:batch].set(c).at[batch].set(c_ctx)
    mod_all = _modulation(cc, mod_w, mod_b)
    mods = [mod_all[l, :batch + 1].reshape(batch + 1, 6, d) for l in range(depth)]
    cos_t, sin_t = _rope_tables(s_len, n_ctx)
    rw = [jnp.zeros((d, LANES), F32).at[:, :n_experts].set(router_w[l]).astype(_MXU) for l in range(depth)]
    rb = [jnp.full((1, LANES), NEG_INF, F32).at[0, :n_experts].set(router_b[l]) for l in range(depth)]

    stream = jnp.concatenate([x, ctx], axis=1).reshape(batch * t_len, d)
    by_batch = lambda a: a.reshape(batch, t_len, a.shape[-1])

    na_w = NA_HEADS * HEAD_DIM
    df_w = DIFF_HEADS * 2 * HEAD_DIM
    w0 = ab_w_in[0].astype(_MXU)
    rows_only = ("rows",)
    plan0 = ((0, na_w, None, False, QSCALE, rows_only), (na_w, na_w, None, False, 1.0, rows_only),
             (2 * na_w, na_w, None, False, 1.0, ("rows", "value_t")),
             (3 * na_w, df_w, None, True, QSCALE, rows_only), (3 * na_w + df_w, df_w, None, True, 1.0, rows_only),
             (3 * na_w + 2 * df_w, df_w, None, False, 1.0, rows_only))
    nw0 = jnp.zeros((8, LANES), F32)
    nq, nk, nv, nvx_lo, nvx_hi, dq, dk, dv = _in_proj(stream, mods[0], w0, cos_t, sin_t, nw0, plan0,
                                                      batch=batch, n_ctx=n_ctx)
    nq, nk, nv, dq, dk, dv = map(by_batch, (nq, nk, nv, dq, dk, dv))
    lam_init0 = 0.8 - 0.6 * math.exp(-0.3 * 0)
    par = jnp.zeros((8, LANES), F32)
    par = par.at[0, :HEAD_DIM].set(diff_lq1[0]).at[1, :HEAD_DIM].set(diff_lk1[0])
    par = par.at[2, :HEAD_DIM].set(diff_lq2[0]).at[3, :HEAD_DIM].set(diff_lk2[0]).at[4].set(diff_subln[0])
    bias = _na_bias_tables(na_rpb[0], s_len // GRID_W)
    y_na = _neighbourhood(nq, nk, nvx_lo, nvx_hi, bias, s_len=s_len, n_ctx=n_ctx)
    y_na_c = _flash(nq, nk, nv, n_groups=NA_HEADS // 2, q_rows=n_ctx, q_off=s_len, kv_rows=n_ctx,
                    kv_off=s_len, k_col=0, v_col=0, mode="pair", name="neighbourhood_attn_ctx")
    y_df = _flash(dq, dk, dv, n_groups=DIFF_HEADS, q_rows=s_len, q_off=0, kv_rows=t_len, kv_off=0, k_col=0, v_col=0,
                  mode="diff", par=par, lam_init=lam_init0, name="diff_attn")
    y_df_c = _flash(dq, dk, dv, n_groups=DIFF_HEADS, q_rows=n_ctx, q_off=s_len, kv_rows=n_ctx, kv_off=s_len,
                    k_col=0, v_col=0,
                    mode="diff", par=par, lam_init=lam_init0, name="diff_attn_ctx")
    ya = jnp.concatenate([y_na, y_na_c], axis=1).reshape(batch * t_len, -1)
    yb = jnp.concatenate([y_df, y_df_c], axis=1).reshape(batch * t_len, -1)
    x1, tok, route = _post(stream, ya, yb, ab_w_out[0].astype(_MXU), mods[0], ln1_g[0][None], ln1_b[0][None],
                           rw[0], rb[0], batch=batch, nbt_in=nbt, nblk=nbt, n_lat_blk=nbs, alpha=alpha)
    stream = _moe_and_ln2(x1, tok, route, mods[0], n_experts, moe_w_in, moe_b_in, moe_w_out, moe_b_out, 0,
                          ln2_g[0][None], ln2_b[0][None], batch=batch, nblk=nbt, n_lat_blk=nbs, alpha=alpha)

    hw = GQA_HEADS * HEAD_DIM
    kw = GQA_KV_HEADS * HEAD_DIM
    wcd = cd_w_in[0]
    gq, gk, gv, wq, wk, wv = jnp.split(wcd, [hw, hw + kw, hw + 2 * kw, 2 * hw + 2 * kw, 2 * hw + 3 * kw], axis=1)
    w1 = jnp.concatenate([_pair_layout(gq, GQA_HEADS), _pair_layout(wq, SWA_HEADS), gk, gv, wk, wv],
                         axis=1).astype(_MXU)
    plan1 = ((0, hw, 0, True, QSCALE, rows_only), (hw, hw, None, True, QSCALE, rows_only),
             (2 * hw, kw, 1, True, 1.0, rows_only), (2 * hw + kw, kw, None, False, 1.0, rows_only),
             (2 * hw + 2 * kw, kw, None, True, 1.0, rows_only),
             (2 * hw + 3 * kw, kw, None, False, 1.0, ("value_t",)))
    nw1 = jnp.zeros((8, LANES), F32).at[0].set(jnp.tile(gqa_q_norm[0], 2)).at[1].set(jnp.tile(gqa_k_norm[0], 2))
    gq, wq, gk, gv, wk, wvx_lo, wvx_hi = _in_proj(stream, mods[1], w1, cos_t, sin_t, nw1, plan1,
                                                  batch=batch, n_ctx=n_ctx)
    gq, wq, gk, gv, wk = map(by_batch, (gq, wq, gk, gv, wk))
    y_c = _flash(gq, gk, gv, n_groups=1, n_qp=GQA_HEADS // 2, q_rows=s_len, q_off=0, kv_rows=t_len, kv_off=0,
                 k_col=0, v_col=0, mode="pair", name="gqa_attn")
    sink = jnp.repeat(swa_sink[0].reshape(2, SWA_HEADS // 2) * LOG2E, LANES, axis=1).astype(F32)
    y_d = _windowed(wq, wk, wvx_lo, wvx_hi, sink, s_len=s_len, n_ctx=n_ctx)
    wo1 = jnp.concatenate([_pair_layout(cd_w_out[0][:hw], GQA_HEADS, axis=0),
                           _pair_layout(cd_w_out[0][hw:], SWA_HEADS, axis=0)], axis=0).astype(_MXU)
    x1, tok, route = _post(stream, y_c.reshape(batch * s_len, -1), y_d.reshape(batch * s_len, -1), wo1, mods[1],
                           ln1_g[1][None], ln1_b[1][None], rw[1], rb[1], batch=batch, nbt_in=nbt,
                           nblk=nbs, n_lat_blk=nbs, alpha=alpha)
    out = _moe_and_ln2(x1, tok, route, mods[1], n_experts, moe_w_in, moe_b_in, moe_w_out, moe_b_out, 1,
                       ln2_g[1][None], ln2_b[1][None], batch=batch, nblk=nbs, n_lat_blk=nbs, alpha=alpha)
    return out.reshape(batch, s_len, d)
```
